```python
import math
import jax, jax.numpy as jnp
from jax import lax
import numpy as np


D_MODEL = 1024
BATCH = 4
SEQ = 4096
DEPTH = 2

N_A = DEPTH // 2
N_B = DEPTH - N_A
N_DENSE = (DEPTH + 1) // 2
N_MOE = DEPTH // 2

POOL_WINDOWS = (2, 4, 8, 16)
N_POOL_GROUPS = len(POOL_WINDOWS)
POOL_GW = D_MODEL // N_POOL_GROUPS

N_HEADS = 8
D_NOPE = 128
D_ROPE = 64
D_V = 128
Q_LORA = 384
KV_LORA = 256
ROPE_THETA = 10000.0
Q_BLOCK = 128
ATTN_SCALE = 1.0 / math.sqrt(D_NOPE + D_ROPE)

D_FF = 3584
N_EXPERTS = 8
TOP_K = 2

EPS = 1e-6

kernel_name = "yoco_pool_mla_moe_hybrid"


def rmsnorm(x, g):
    xf = x.astype(jnp.float32)
    y = xf * lax.rsqrt(jnp.mean(xf * xf, axis=-1, keepdims=True) + EPS)
    return y.astype(x.dtype) * g


def modulate(h, shift, scale):
    return h * (1.0 + scale) + shift


def rope(x, cos, sin):
    x1, x2 = jnp.split(x, 2, axis=-1)
    return jnp.concatenate([x1 * cos - x2 * sin, x2 * cos + x1 * sin], axis=-1)


def causal_pool_diff(u, w):
    S = u.shape[1]
    uf = u.astype(jnp.float32)
    cs0 = jnp.pad(jnp.cumsum(uf, axis=1), ((0, 0), (1, 0), (0, 0)))
    upper = cs0[:, 1:]
    lower = jnp.pad(cs0[:, :S - w + 1], ((0, 0), (w - 1, 0), (0, 0)))
    count = jnp.minimum(jnp.arange(1, S + 1), w).astype(jnp.float32)[None, :, None]
    return ((upper - lower) / count - uf).astype(u.dtype)


def pool_mixer(h, w_in, w_grp, scale, w_out):
    B, S, D = h.shape
    u = h @ w_in
    groups = [causal_pool_diff(u[..., g * POOL_GW:(g + 1) * POOL_GW], POOL_WINDOWS[g])
              for g in range(N_POOL_GROUPS)]
    p = jnp.stack(groups, axis=2)
    z = jnp.einsum('bsgc,gcd->bsgd', p, w_grp).reshape(B, S, D) * scale
    return z @ w_out


def shared_kv(x, sc, kv_ada_w, kv_ada_b, kv_norm_g, w_dkv, ckv_norm_g, w_kr, w_uk, w_uv, cos, sin):
    B, S, _ = x.shape
    mod = (sc @ kv_ada_w + kv_ada_b)[:, None, :]
    shift, scale = jnp.split(mod, 2, axis=-1)
    hk = modulate(rmsnorm(x, kv_norm_g), shift, scale)
    c_kv = rmsnorm(hk @ w_dkv, ckv_norm_g)
    k_nope = (c_kv @ w_uk).reshape(B, S, N_HEADS, D_NOPE)
    v = (c_kv @ w_uv).reshape(B, S, N_HEADS, D_V)
    k_rope = rope(hk @ w_kr, cos, sin)
    return k_nope, k_rope, v


def mla_mixer(h, kv, w_dq, q_norm_g, w_uq, w_o, cos, sin):
    B, S, _ = h.shape
    k_nope, k_rope, v = kv
    cq = rmsnorm(h @ w_dq, q_norm_g)
    q = (cq @ w_uq).reshape(B, S, N_HEADS, D_NOPE + D_ROPE)
    q_nope = q[..., :D_NOPE]
    q_rope = rope(q[..., D_NOPE:], cos[:, :, None, :], sin[:, :, None, :])
    nb = S // Q_BLOCK
    qn = q_nope.reshape(B, nb, Q_BLOCK, N_HEADS, D_NOPE).transpose(1, 0, 2, 3, 4)
    qr = q_rope.reshape(B, nb, Q_BLOCK, N_HEADS, D_ROPE).transpose(1, 0, 2, 3, 4)
    key_idx = jnp.arange(S)

    def block(args):
        i, qn_b, qr_b = args
        s = (jnp.einsum('bqhd,bkhd->bhqk', qn_b, k_nope)
             + jnp.einsum('bqhd,bkd->bhqk', qr_b, k_rope)).astype(jnp.float32) * ATTN_SCALE
        q_idx = i * Q_BLOCK + jnp.arange(Q_BLOCK)
        mask = key_idx[None, :] <= q_idx[:, None]
        s = jnp.where(mask[None, None], s, -jnp.inf)
        p = jax.nn.softmax(s, axis=-1).astype(v.dtype)
        return jnp.einsum('bhqk,bkhd->bqhd', p, v)

    o = lax.map(block, (jnp.arange(nb), qn, qr))
    o = o.transpose(1, 0, 2, 3, 4).reshape(B, S, N_HEADS * D_V)
    return o @ w_o


def swiglu(h, w_gate, w_up, w_down):
    return (jax.nn.silu(h @ w_gate) * (h @ w_up)) @ w_down


def moe_swiglu(h, router_w, w_gate, w_up, w_down):
    B, S, D = h.shape
    t = h.reshape(-1, D)
    logits = (t @ router_w).astype(jnp.float32)
    top_v, top_i = lax.top_k(logits, TOP_K)
    top_w = jax.nn.softmax(top_v, axis=-1)
    gates = jnp.sum(jax.nn.one_hot(top_i, N_EXPERTS, dtype=jnp.float32) * top_w[..., None], axis=1)
    gates = gates.astype(t.dtype)
    out = jnp.zeros_like(t)
    for e in range(N_EXPERTS):
        out = out + gates[:, e:e + 1] * swiglu(t, w_gate[e], w_up[e], w_down[e])
    return out.reshape(B, S, D)


def setup_inputs(seed: int = 0) -> dict:
    key = jax.random.key(seed)
    ks = iter(jax.random.split(key, 48))
    f32 = jnp.float32
    D = D_MODEL

    def w(shape, fan_in, s=1.0):
        return jax.random.normal(next(ks), shape, f32) * (s * fan_in ** -0.5)

    def gain(shape):
        return 1.0 + 0.05 * jax.random.normal(next(ks), shape, f32)

    def bias(shape):
        return 0.01 * jax.random.normal(next(ks), shape, f32)

    x = jax.random.normal(next(ks), (BATCH, SEQ, D), f32)
    c = jax.random.normal(next(ks), (BATCH, D), f32)
    offsets = jax.random.randint(next(ks), (BATCH, 1), 0, 1024, dtype=jnp.int32)
    positions = offsets + jnp.arange(SEQ, dtype=jnp.int32)[None, :]
    return {
        "x": x,
        "c": c,
        "positions": positions,
        "ada_w": w((DEPTH, D, 6 * D), D, 0.5),
        "ada_b": bias((DEPTH, 6 * D)),
        "mix_norm_g": gain((DEPTH, D)),
        "ffn_norm_g": gain((DEPTH, D)),
        "pool_w_in": w((N_A, D, D), D),
        "pool_w_grp": w((N_A, N_POOL_GROUPS, POOL_GW, POOL_GW), POOL_GW),
        "pool_scale": 1.0 + 0.1 * jax.random.normal(next(ks), (N_A, D), f32),
        "pool_w_out": w((N_A, D, D), D),
        "kv_ada_w": w((D, 2 * D), D, 0.5),
        "kv_ada_b": bias((2 * D,)),
        "kv_norm_g": gain((D,)),
        "w_dkv": w((D, KV_LORA), D),
        "ckv_norm_g": gain((KV_LORA,)),
        "w_kr": w((D, D_ROPE), D),
        "w_uk": w((KV_LORA, N_HEADS * D_NOPE), KV_LORA),
        "w_uv": w((KV_LORA, N_HEADS * D_V), KV_LORA),
        "w_dq": w((N_B, D, Q_LORA), D),
        "q_norm_g": gain((N_B, Q_LORA)),
        "w_uq": w((N_B, Q_LORA, N_HEADS * (D_NOPE + D_ROPE)), Q_LORA),
        "w_o": w((N_B, N_HEADS * D_V, D), N_HEADS * D_V),
        "ffn_w_gate": w((N_DENSE, D, D_FF), D),
        "ffn_w_up": w((N_DENSE, D, D_FF), D),
        "ffn_w_down": w((N_DENSE, D_FF, D), D_FF),
        "router_w": w((N_MOE, D, N_EXPERTS), D),
        "moe_w_gate": w((N_MOE, N_EXPERTS, D, D_FF), D),
        "moe_w_up": w((N_MOE, N_EXPERTS, D, D_FF), D),
        "moe_w_down": w((N_MOE, N_EXPERTS, D_FF, D), D_FF),
        "final_norm_g": gain((D,)),
    }


def reference(x, c, positions, ada_w, ada_b, mix_norm_g, ffn_norm_g, pool_w_in, pool_w_grp,
              pool_scale, pool_w_out, kv_ada_w, kv_ada_b, kv_norm_g, w_dkv, ckv_norm_g, w_kr,
              w_uk, w_uv, w_dq, q_norm_g, w_uq, w_o, ffn_w_gate, ffn_w_up, ffn_w_down,
              router_w, moe_w_gate, moe_w_up, moe_w_down, final_norm_g):
    sc = jax.nn.silu(c)
    inv_freq = ROPE_THETA ** (-jnp.arange(0, D_ROPE, 2, dtype=jnp.float32) / D_ROPE)
    ang = positions.astype(jnp.float32)[..., None] * inv_freq
    cos = jnp.cos(ang).astype(x.dtype)
    sin = jnp.sin(ang).astype(x.dtype)
    kv = None
    for layer in range(DEPTH):
        mod = (sc @ ada_w[layer] + ada_b[layer])[:, None, :]
        sh1, sc1, g1, sh2, sc2, g2 = jnp.split(mod, 6, axis=-1)
        h = modulate(rmsnorm(x, mix_norm_g[layer]), sh1, sc1)
        if layer < N_A:
            y = pool_mixer(h, pool_w_in[layer], pool_w_grp[layer], pool_scale[layer], pool_w_out[layer])
        else:
            if layer == N_A:
                kv = shared_kv(x, sc, kv_ada_w, kv_ada_b, kv_norm_g, w_dkv, ckv_norm_g,
                               w_kr, w_uk, w_uv, cos, sin)
            j = layer - N_A
            y = mla_mixer(h, kv, w_dq[j], q_norm_g[j], w_uq[j], w_o[j], cos, sin)
        x = x + g1 * y
        h = modulate(rmsnorm(x, ffn_norm_g[layer]), sh2, sc2)
        if layer % 2 == 0:
            i = layer // 2
            y = swiglu(h, ffn_w_gate[i], ffn_w_up[i], ffn_w_down[i])
        else:
            i = layer // 2
            y = moe_swiglu(h, router_w[i], moe_w_gate[i], moe_w_up[i], moe_w_down[i])
        x = x + g2 * y
    return rmsnorm(x, final_norm_g)
```

```python
import functools
import math

import jax
import jax.numpy as jnp
from jax import lax
from jax.experimental import pallas as pl
from jax.experimental.pallas import tpu as pltpu

F32 = jnp.float32
BF16 = jnp.bfloat16

N_HEADS = 8
D_NOPE = 128
D_ROPE = 64
D_V = 128
HEAD_PAD = 256
ROPE_HALF = D_ROPE // 2
ROPE_THETA = 10000.0
ATTN_SCALE = 1.0 / math.sqrt(D_NOPE + D_ROPE)
POOL_WINDOWS = (2, 4, 8, 16)
POOL_HALO = 16
TOP_K = 2
EPS = 1e-6

LANES = 128
VMEM_LIMIT_BYTES = 56 * 1024 * 1024


def _cparams(sem):
    return pltpu.CompilerParams(dimension_semantics=sem, vmem_limit_bytes=VMEM_LIMIT_BYTES)


def _dot(a, b):
    return jnp.dot(a, b, preferred_element_type=F32)


def _rms_scale(x):
    return x * lax.rsqrt(jnp.mean(x * x, axis=-1, keepdims=True) + EPS)


def _silu(x):
    return x * jax.nn.sigmoid(x)


def _adaln_kernel(c_ref, w_ref, b_ref, o_ref):
    sc = _silu(c_ref[...])
    o_ref[0] = jnp.dot(sc, w_ref[0], preferred_element_type=F32,
                       precision=lax.Precision.HIGHEST) + b_ref[0]


def _adaln(c_pad, w, b, tn):
    nl, d, m = w.shape
    return pl.pallas_call(
        _adaln_kernel,
        grid=(nl, m // tn),
        in_specs=[
            pl.BlockSpec((8, d), lambda l, j: (0, 0)),
            pl.BlockSpec((1, d, tn), lambda l, j: (l, 0, j)),
            pl.BlockSpec((1, 1, tn), lambda l, j: (l, 0, j)),
        ],
        out_specs=pl.BlockSpec((1, 8, tn), lambda l, j: (l, 0, j)),
        out_shape=jax.ShapeDtypeStruct((nl, 8, m), F32),
        compiler_params=_cparams(("arbitrary", "arbitrary")),
        name="adaln",
    )(c_pad, w, b)


def _rope_table_kernel(pos_ref, freq_ref, c_ref, s1_ref, s2_ref):
    ang = pos_ref[0] * freq_ref[...]
    lane = lax.broadcasted_iota(jnp.int32, ang.shape, 1)
    cos = jnp.cos(ang)
    sin = jnp.sin(ang)
    zero = jnp.zeros_like(ang)
    c_ref[0] = jnp.where(lane < D_ROPE, cos, zero)
    s1_ref[0] = jnp.where(lane < ROPE_HALF, -sin, zero)
    s2_ref[0] = jnp.where((lane >= ROPE_HALF) & (lane < D_ROPE), sin, zero)


def _rope_tables(pos_f, freq_row, ts):
    b, s, _ = pos_f.shape
    spec = pl.BlockSpec((1, ts, LANES), lambda i, j: (i, j, 0))
    shp = jax.ShapeDtypeStruct((b, s, LANES), F32)
    return pl.pallas_call(
        _rope_table_kernel,
        grid=(b, s // ts),
        in_specs=[
            pl.BlockSpec((1, ts, 1), lambda i, j: (i, j, 0)),
            pl.BlockSpec((1, LANES), lambda i, j: (0, 0)),
        ],
        out_specs=[spec, spec, spec],
        out_shape=[shp, shp, shp],
        compiler_params=_cparams(("arbitrary", "arbitrary")),
        name="rope_tables",
    )(pos_f, freq_row)


def _apply_rope(a, c, s1, s2):
    return a * c + pltpu.roll(a, LANES - ROPE_HALF, 1) * s1 + pltpu.roll(a, ROPE_HALF, 1) * s2


def _pool_kernel(x_ref, mod_ref, g_ref, win_ref, band_ref, wgrp_ref, scale_ref, wout_ref,
                 o_ref, uext_ref, *, ts, gw):
    s = pl.program_id(1)

    @pl.when(s == 0)
    def _():
        uext_ref[0:POOL_HALO, :] = jnp.zeros((POOL_HALO, uext_ref.shape[1]), BF16)

    x = x_ref[0]
    mod = mod_ref[0]
    h = (_rms_scale(x) * g_ref[...]) * (1.0 + mod[1:2]) + mod[0:1]
    u = _dot(h.astype(BF16), win_ref[...])
    uext_ref[POOL_HALO:POOL_HALO + ts, :] = u.astype(BF16)

    t = s * ts + lax.broadcasted_iota(jnp.int32, (ts, 1), 0)
    zs = []
    for g, w in enumerate(POOL_WINDOWS):
        cols = slice(g * gw, (g + 1) * gw)
        wsum = _dot(band_ref[g], uext_ref[:, cols])
        cnt = jnp.minimum(t + 1, w).astype(F32)
        p = wsum / cnt - u[:, cols]
        zs.append(_dot(p.astype(BF16), wgrp_ref[g]))
    z = jnp.concatenate(zs, axis=-1) * scale_ref[...]
    y = _dot(z.astype(BF16), wout_ref[...])
    o_ref[0] = x + mod[2:3] * y

    uext_ref[0:POOL_HALO, :] = uext_ref[ts:ts + POOL_HALO, :]


def _pool_mixer(x, mod, g, w_in, band, w_grp, scale, w_out, ts):
    b, s, d = x.shape
    ng, gw, _ = w_grp.shape
    const2 = lambda i, j: (0, 0)
    const3 = lambda i, j: (0, 0, 0)
    return pl.pallas_call(
        functools.partial(_pool_kernel, ts=ts, gw=gw),
        grid=(b, s // ts),
        in_specs=[
            pl.BlockSpec((1, ts, d), lambda i, j: (i, j, 0)),
            pl.BlockSpec((1, 6, d), lambda i, j: (i, 0, 0)),
            pl.BlockSpec((1, d), const2),
            pl.BlockSpec((d, d), const2),
            pl.BlockSpec((ng, ts, ts + POOL_HALO), const3),
            pl.BlockSpec((ng, gw, gw), const3),
            pl.BlockSpec((1, d), const2),
            pl.BlockSpec((d, d), const2),
        ],
        out_specs=pl.BlockSpec((1, ts, d), lambda i, j: (i, j, 0)),
        out_shape=jax.ShapeDtypeStruct((b, s, d), F32),
        scratch_shapes=[pltpu.VMEM((POOL_HALO + ts, d), BF16)],
        compiler_params=_cparams(("arbitrary", "arbitrary")),
        name="pool_mixer",
    )(x, mod, g, w_in, band, w_grp, scale, w_out)


def _ffn_kernel(x_ref, mod_ref, g_ref, wg_ref, wu_ref, wd_ref, o_ref, h_ref, acc_ref):
    f = pl.program_id(1)

    @pl.when(f == 0)
    def _():
        mod = mod_ref[0]
        h = (_rms_scale(x_ref[...]) * g_ref[...]) * (1.0 + mod[4:5]) + mod[3:4]
        h_ref[...] = h.astype(BF16)
        acc_ref[...] = jnp.zeros_like(acc_ref)

    h = h_ref[...]
    gate = _dot(h, wg_ref[...].astype(BF16))
    up = _dot(h, wu_ref[...].astype(BF16))
    a = (_silu(gate) * up).astype(BF16)
    acc_ref[...] += _dot(a, wd_ref[...].astype(BF16))

    @pl.when(f == pl.num_programs(1) - 1)
    def _():
        o_ref[...] = x_ref[...] + mod_ref[0][5:6] * acc_ref[...]


def _dense_ffn(x2, mod, g, wg, wu, wd, tm, tf, rows_per_batch):
    n, d = x2.shape
    ff = wg.shape[1]
    tpb = rows_per_batch // tm
    return pl.pallas_call(
        _ffn_kernel,
        grid=(n // tm, ff // tf),
        in_specs=[
            pl.BlockSpec((tm, d), lambda i, f: (i, 0)),
            pl.BlockSpec((1, 6, d), lambda i, f: (i // tpb, 0, 0)),
            pl.BlockSpec((1, d), lambda i, f: (0, 0)),
            pl.BlockSpec((d, tf), lambda i, f: (0, f)),
            pl.BlockSpec((d, tf), lambda i, f: (0, f)),
            pl.BlockSpec((tf, d), lambda i, f: (f, 0)),
        ],
        out_specs=pl.BlockSpec((tm, d), lambda i, f: (i, 0)),
        out_shape=jax.ShapeDtypeStruct((n, d), F32),
        scratch_shapes=[pltpu.VMEM((tm, d), BF16), pltpu.VMEM((tm, d), F32)],
        compiler_params=_cparams(("arbitrary", "arbitrary")),
        name="dense_ffn",
    )(x2, mod, g, wg, wu, wd)


def _qkv_kernel(x_ref, mod_ref, kvmod_ref, gmix_ref, gkv_ref, wdq_ref, qg_ref, wqn_ref, wqr_ref,
                wdkv_ref, ckvg_ref, wkr_ref, wuk_ref, wuv_ref, c_ref, s1_ref, s2_ref,
                q_ref, k_ref, v_ref):
    x = x_ref[0]
    xn = _rms_scale(x)
    mod = mod_ref[0]
    kvmod = kvmod_ref[0]
    hq = ((xn * gmix_ref[...]) * (1.0 + mod[1:2]) + mod[0:1]).astype(BF16)
    hk = ((xn * gkv_ref[...]) * (1.0 + kvmod[1:2]) + kvmod[0:1]).astype(BF16)
    c, s1, s2 = c_ref[0], s1_ref[0], s2_ref[0]

    cq = (_rms_scale(_dot(hq, wdq_ref[...])) * qg_ref[...]).astype(BF16)
    qn = _dot(cq, wqn_ref[...]) * ATTN_SCALE
    qr = _dot(cq, wqr_ref[...]) * ATTN_SCALE

    ckv = (_rms_scale(_dot(hk, wdkv_ref[...])) * ckvg_ref[...]).astype(BF16)
    kn = _dot(ckv, wuk_ref[...])
    v_ref[0] = _dot(ckv, wuv_ref[...]).astype(BF16)
    kr = _apply_rope(_dot(hk, wkr_ref[...]), c, s1, s2).astype(BF16)

    for h in range(N_HEADS):
        lo = h * HEAD_PAD
        hs = slice(h * LANES, (h + 1) * LANES)
        q_ref[0, :, lo:lo + LANES] = qn[:, hs].astype(BF16)
        q_ref[0, :, lo + LANES:lo + HEAD_PAD] = _apply_rope(qr[:, hs], c, s1, s2).astype(BF16)
        k_ref[0, :, lo:lo + LANES] = kn[:, hs].astype(BF16)
        k_ref[0, :, lo + LANES:lo + HEAD_PAD] = kr


def _qkv_proj(x, mod, kvmod, gmix, gkv, wdq, qg, wqn, wqr, wdkv, ckvg, wkr, wuk, wuv,
              c, s1, s2, ts):
    b, s, d = x.shape
    row = lambda i, j: (i, j, 0)
    per_b = lambda i, j: (i, 0, 0)

    def full(a):
        return pl.BlockSpec(a.shape, lambda i, j: (0,) * a.ndim)

    weights = (gmix, gkv, wdq, qg, wqn, wqr, wdkv, ckvg, wkr, wuk, wuv)
    return pl.pallas_call(
        _qkv_kernel,
        grid=(b, s // ts),
        in_specs=[pl.BlockSpec((1, ts, d), row),
                  pl.BlockSpec((1, 6, d), per_b),
                  pl.BlockSpec((1, 2, d), per_b)]
                 + [full(a) for a in weights]
                 + [pl.BlockSpec((1, ts, LANES), row)] * 3,
        out_specs=[pl.BlockSpec((1, ts, N_HEADS * HEAD_PAD), row),
                   pl.BlockSpec((1, ts, N_HEADS * HEAD_PAD), row),
                   pl.BlockSpec((1, ts, N_HEADS * D_V), row)],
        out_shape=[jax.ShapeDtypeStruct((b, s, N_HEADS * HEAD_PAD), BF16),
                   jax.ShapeDtypeStruct((b, s, N_HEADS * HEAD_PAD), BF16),
                   jax.ShapeDtypeStruct((b, s, N_HEADS * D_V), BF16)],
        compiler_params=_cparams(("arbitrary", "arbitrary")),
        name="qkv_proj",
    )(x, mod, kvmod, *weights, c, s1, s2)


def _attn_kernel(q_ref, k_ref, v_ref, o_ref, *, tq):
    qi = pl.program_id(2)
    q = q_ref[0]

    def scores(j):
        k = k_ref[0, pl.ds(pl.multiple_of(j * tq, tq), tq), :]
        return lax.dot_general(q, k, (((1,), (1,)), ((), ())), preferred_element_type=F32)

    def update(j, s, carry):
        m, l, acc = carry
        v = v_ref[0, pl.ds(pl.multiple_of(j * tq, tq), tq), :]
        m_new = jnp.maximum(m, jnp.max(s, axis=-1, keepdims=True))
        alpha = jnp.exp(m - m_new)
        p = jnp.exp(s - m_new)
        l = alpha * l + jnp.sum(p, axis=-1, keepdims=True)
        acc = alpha * acc + _dot(p.astype(BF16), v)
        return m_new, l, acc

    init = (jnp.full((tq, 1), -jnp.inf, F32), jnp.zeros((tq, 1), F32), jnp.zeros((tq, D_V), F32))
    carry = lax.fori_loop(0, qi, lambda j, c: update(j, scores(j), c), init)

    row = lax.broadcasted_iota(jnp.int32, (tq, tq), 0)
    col = lax.broadcasted_iota(jnp.int32, (tq, tq), 1)
    s = jnp.where(col <= row, scores(qi), -jnp.inf)
    _, l, acc = update(qi, s, carry)
    o_ref[0] = (acc / l).astype(BF16)


def _attention(q, k, v, tq):
    b, s, _ = q.shape
    return pl.pallas_call(
        functools.partial(_attn_kernel, tq=tq),
        grid=(b, N_HEADS, s // tq),
        in_specs=[
            pl.BlockSpec((1, tq, HEAD_PAD), lambda i, h, j: (i, j, h)),
            pl.BlockSpec((1, s, HEAD_PAD), lambda i, h, j: (i, 0, h)),
            pl.BlockSpec((1, s, D_V), lambda i, h, j: (i, 0, h)),
        ],
        out_specs=pl.BlockSpec((1, tq, D_V), lambda i, h, j: (i, j, h)),
        out_shape=jax.ShapeDtypeStruct((b, s, N_HEADS * D_V), BF16),
        compiler_params=_cparams(("arbitrary", "arbitrary", "arbitrary")),
        name="flash_attn",
    )(q, k, v)


def _attn_out_kernel(o_ref, x_ref, mod_ref, wo_ref, g_ref, wr_ref, x_out_ref, h_out_ref,
                     gate_ref, *, n_exp):
    mod = mod_ref[0]
    x = x_ref[...] + mod[2:3] * _dot(o_ref[...], wo_ref[...])
    x_out_ref[...] = x
    h = (_rms_scale(x) * g_ref[...]) * (1.0 + mod[4:5]) + mod[3:4]
    h_out_ref[...] = h.astype(BF16)

    wr = wr_ref[...]
    logit = [jnp.sum(h * wr[e:e + 1], axis=-1, keepdims=True) for e in range(n_exp)]
    neg = jnp.full_like(logit[0], -jnp.inf)

    def first_argmax(vals):
        m = functools.reduce(jnp.maximum, vals)
        idx = jnp.full(m.shape, n_exp, jnp.int32)
        for e in reversed(range(n_exp)):
            idx = jnp.where(vals[e] == m, e, idx)
        return m, idx

    m1, i1 = first_argmax(logit)
    m2, i2 = first_argmax([jnp.where(i1 == e, neg, logit[e]) for e in range(n_exp)])
    ex = jnp.exp(m2 - m1)
    w1 = 1.0 / (1.0 + ex)
    w2 = ex / (1.0 + ex)
    zero = jnp.zeros_like(w1)
    for e in range(n_exp):
        gate_ref[e] = jnp.where(i1 == e, w1, zero) + jnp.where(i2 == e, w2, zero)


def _attn_out(o2, x2, mod, wo, g, wr, tm, rows_per_batch):
    n, d = x2.shape
    n_exp = wr.shape[0]
    tpb = rows_per_batch // tm
    return pl.pallas_call(
        functools.partial(_attn_out_kernel, n_exp=n_exp),
        grid=(n // tm,),
        in_specs=[
            pl.BlockSpec((tm, o2.shape[1]), lambda i: (i, 0)),
            pl.BlockSpec((tm, d), lambda i: (i, 0)),
            pl.BlockSpec((1, 6, d), lambda i: (i // tpb, 0, 0)),
            pl.BlockSpec(wo.shape, lambda i: (0, 0)),
            pl.BlockSpec((1, d), lambda i: (0, 0)),
            pl.BlockSpec(wr.shape, lambda i: (0, 0)),
        ],
        out_specs=[pl.BlockSpec((tm, d), lambda i: (i, 0)),
                   pl.BlockSpec((tm, d), lambda i: (i, 0)),
                   pl.BlockSpec((n_exp, tm, 1), lambda i: (0, i, 0))],
        out_shape=[jax.ShapeDtypeStruct((n, d), F32),
                   jax.ShapeDtypeStruct((n, d), BF16),
                   jax.ShapeDtypeStruct((n_exp, n, 1), F32)],
        compiler_params=_cparams(("arbitrary",)),
        name="attn_out_router",
    )(o2, x2, mod, wo, g, wr)


def _moe_kernel(h_ref, gate_ref, x_ref, mod_ref, gfin_ref, wg_ref, wu_ref, wd_ref, o_ref, acc_ref):
    e = pl.program_id(1)
    f = pl.program_id(2)

    @pl.when((e == 0) & (f == 0))
    def _():
        acc_ref[...] = jnp.zeros_like(acc_ref)

    h = h_ref[...]
    gate = _dot(h, wg_ref[0].astype(BF16))
    up = _dot(h, wu_ref[0].astype(BF16))
    a = ((_silu(gate) * up) * gate_ref[0]).astype(BF16)
    acc_ref[...] += _dot(a, wd_ref[0].astype(BF16))

    @pl.when((e == pl.num_programs(1) - 1) & (f == pl.num_programs(2) - 1))
    def _():
        x = x_ref[...] + mod_ref[0][5:6] * acc_ref[...]
        o_ref[...] = _rms_scale(x) * gfin_ref[...]


def _moe_dense(h, gates, x2, mod, gfin, wg, wu, wd, tm, tf, rows_per_batch):
    n, d = x2.shape
    n_exp, _, ff = wg.shape
    tpb = rows_per_batch // tm
    return pl.pallas_call(
        _moe_kernel,
        grid=(n // tm, n_exp, ff // tf),
        in_specs=[
            pl.BlockSpec((tm, d), lambda i, e, f: (i, 0)),
            pl.BlockSpec((1, tm, 1), lambda i, e, f: (e, i, 0)),
            pl.BlockSpec((tm, d), lambda i, e, f: (i, 0)),
            pl.BlockSpec((1, 6, d), lambda i, e, f: (i // tpb, 0, 0)),
            pl.BlockSpec((1, d), lambda i, e, f: (0, 0)),
            pl.BlockSpec((1, d, tf), lambda i, e, f: (e, 0, f)),
            pl.BlockSpec((1, d, tf), lambda i, e, f: (e, 0, f)),
            pl.BlockSpec((1, tf, d), lambda i, e, f: (e, f, 0)),
        ],
        out_specs=pl.BlockSpec((tm, d), lambda i, e, f: (i, 0)),
        out_shape=jax.ShapeDtypeStruct((n, d), F32),
        scratch_shapes=[pltpu.VMEM((tm, d), F32)],
        compiler_params=_cparams(("arbitrary", "arbitrary", "arbitrary")),
        name="moe_ffn",
    )(h, gates, x2, mod, gfin, wg, wu, wd)


def _pick(n, pref):
    t = min(n, pref)
    while n % t:
        t //= 2
    return t


def kernel(x, c, positions, ada_w, ada_b, mix_norm_g, ffn_norm_g, pool_w_in, pool_w_grp,
           pool_scale, pool_w_out, kv_ada_w, kv_ada_b, kv_norm_g, w_dkv, ckv_norm_g, w_kr,
           w_uk, w_uv, w_dq, q_norm_g, w_uq, w_o, ffn_w_gate, ffn_w_up, ffn_w_down,
           router_w, moe_w_gate, moe_w_up, moe_w_down, final_norm_g):
    b, s, d = x.shape
    n = b * s
    depth = ada_w.shape[0]
    assert depth == 2 and pool_w_in.shape[0] == 1 and w_dq.shape[0] == 1
    assert b <= 8 and d % LANES == 0

    ts = _pick(s, 512)
    tm = _pick(s, 1024)
    tf = _pick(ffn_w_gate.shape[-1], 512)

    c_pad = jnp.zeros((8, d), F32).at[:b].set(c)
    mods = _adaln(c_pad, ada_w, ada_b[:, None, :], _pick(6 * d, 1536))
    mods = mods[:, :b].reshape(depth, b, 6, d)
    kvmod = _adaln(c_pad, kv_ada_w[None], kv_ada_b[None, None, :], _pick(2 * d, 1024))
    kvmod = kvmod[0, :b].reshape(b, 2, d)

    r = jnp.arange(ts)[:, None]
    cidx = jnp.arange(ts + POOL_HALO)[None, :] - POOL_HALO
    band = jnp.stack([((cidx <= r) & (cidx > r - w)) for w in POOL_WINDOWS]).astype(BF16)
    x1 = _pool_mixer(x, mods[0], mix_norm_g[0:1], pool_w_in[0].astype(BF16), band,
                     pool_w_grp[0].astype(BF16), pool_scale[0:1], pool_w_out[0].astype(BF16), ts)

    x2 = _dense_ffn(x1.reshape(n, d), mods[0], ffn_norm_g[0:1], ffn_w_gate[0], ffn_w_up[0],
                    ffn_w_down[0], tm, tf, s)

    inv_freq = ROPE_THETA ** (-jnp.arange(0, D_ROPE, 2, dtype=F32) / D_ROPE)
    freq_row = jnp.concatenate([inv_freq, inv_freq, jnp.zeros((LANES - D_ROPE,), F32)])[None, :]
    rope_c, rope_s1, rope_s2 = _rope_tables(positions.astype(F32)[..., None], freq_row, ts)

    q_lora = w_dq.shape[-1]
    wuq = w_uq[0].reshape(q_lora, N_HEADS, D_NOPE + D_ROPE)
    wqn = wuq[:, :, :D_NOPE].reshape(q_lora, N_HEADS * D_NOPE).astype(BF16)
    wqr = jnp.pad(wuq[:, :, D_NOPE:], ((0, 0), (0, 0), (0, LANES - D_ROPE)))
    wqr = wqr.reshape(q_lora, N_HEADS * LANES).astype(BF16)
    wkr = jnp.pad(w_kr, ((0, 0), (0, LANES - D_ROPE))).astype(BF16)
    q, k, v = _qkv_proj(
        x2.reshape(b, s, d), mods[1], kvmod, mix_norm_g[1:2], kv_norm_g[None, :],
        w_dq[0].astype(BF16), q_norm_g[0:1], wqn, wqr, w_dkv.astype(BF16), ckv_norm_g[None, :],
        wkr, w_uk.astype(BF16), w_uv.astype(BF16), rope_c, rope_s1, rope_s2, ts)
    o = _attention(q, k, v, ts)

    x3, h3, gates = _attn_out(o.reshape(n, N_HEADS * D_V), x2, mods[1], w_o[0].astype(BF16),
                              ffn_norm_g[1:2], router_w[0].T, ts, s)
    out = _moe_dense(h3, gates, x3, mods[1], final_norm_g[None, :], moe_w_gate[0], moe_w_up[0],
                     moe_w_down[0], tm, tf, s)
    return out.reshape(b, s, d)
```

```python
import functools
import math

import jax
import jax.numpy as jnp
from jax import lax
from jax.experimental import pallas as pl
from jax.experimental.pallas import tpu as pltpu

F32 = jnp.float32
BF16 = jnp.bfloat16

N_HEADS = 8
D_NOPE = 128
D_ROPE = 64
D_V = 128
HEAD_PAD = 256
ROPE_HALF = D_ROPE // 2
ROPE_THETA = 10000.0
ATTN_SCALE = 1.0 / math.sqrt(D_NOPE + D_ROPE)
POOL_WINDOWS = (2, 4, 8, 16)
POOL_HALO = 16
EPS = 1e-6

ROUTE_SUB = 256
DISPATCH_CHUNK = 128
PIECE = 256
ROUTE_FIELDS = 8

LANES = 128
VMEM_LIMIT_BYTES = 56 * 1024 * 1024


def _cparams(sem):
    return pltpu.CompilerParams(dimension_semantics=sem, vmem_limit_bytes=VMEM_LIMIT_BYTES)


def _dot(a, b):
    return jnp.dot(a, b, preferred_element_type=F32)


def _rms_scale(x):
    return x * lax.rsqrt(jnp.mean(x * x, axis=-1, keepdims=True) + EPS)


def _silu(x):
    return x * jax.nn.sigmoid(x)


def _adaln_kernel(c_ref, w_ref, b_ref, o_ref):
    sc = _silu(c_ref[...])
    o_ref[0] = jnp.dot(sc, w_ref[0], preferred_element_type=F32,
                       precision=lax.Precision.HIGHEST) + b_ref[0]


def _adaln(c_pad, w, b, tn):
    nl, d, m = w.shape
    return pl.pallas_call(
        _adaln_kernel,
        grid=(nl, m // tn),
        in_specs=[
            pl.BlockSpec((8, d), lambda l, j: (0, 0)),
            pl.BlockSpec((1, d, tn), lambda l, j: (l, 0, j)),
            pl.BlockSpec((1, 1, tn), lambda l, j: (l, 0, j)),
        ],
        out_specs=pl.BlockSpec((1, 8, tn), lambda l, j: (l, 0, j)),
        out_shape=jax.ShapeDtypeStruct((nl, 8, m), F32),
        compiler_params=_cparams(("arbitrary", "arbitrary")),
        name="adaln",
    )(c_pad, w, b)


def _rope_table_kernel(pos_ref, freq_ref, c_ref, s1_ref, s2_ref):
    ang = pos_ref[0] * freq_ref[...]
    lane = lax.broadcasted_iota(jnp.int32, ang.shape, 1)
    cos = jnp.cos(ang)
    sin = jnp.sin(ang)
    zero = jnp.zeros_like(ang)
    c_ref[0] = jnp.where(lane < D_ROPE, cos, zero)
    s1_ref[0] = jnp.where(lane < ROPE_HALF, -sin, zero)
    s2_ref[0] = jnp.where((lane >= ROPE_HALF) & (lane < D_ROPE), sin, zero)


def _rope_tables(pos_f, freq_row, ts):
    b, s, _ = pos_f.shape
    spec = pl.BlockSpec((1, ts, LANES), lambda i, j: (i, j, 0))
    shp = jax.ShapeDtypeStruct((b, s, LANES), F32)
    return pl.pallas_call(
        _rope_table_kernel,
        grid=(b, s // ts),
        in_specs=[
            pl.BlockSpec((1, ts, 1), lambda i, j: (i, j, 0)),
            pl.BlockSpec((1, LANES), lambda i, j: (0, 0)),
        ],
        out_specs=[spec, spec, spec],
        out_shape=[shp, shp, shp],
        compiler_params=_cparams(("arbitrary", "arbitrary")),
        name="rope_tables",
    )(pos_f, freq_row)


def _apply_rope(a, c, s1, s2):
    return a * c + pltpu.roll(a, LANES - ROPE_HALF, 1) * s1 + pltpu.roll(a, ROPE_HALF, 1) * s2


def _pool_kernel(x_ref, mod_ref, g_ref, win_ref, band_ref, wgrp_ref, scale_ref, wout_ref,
                 o_ref, uext_ref, *, ts, gw):
    s = pl.program_id(1)

    @pl.when(s == 0)
    def _():
        uext_ref[0:POOL_HALO, :] = jnp.zeros((POOL_HALO, uext_ref.shape[1]), BF16)

    x = x_ref[0]
    mod = mod_ref[0]
    h = (_rms_scale(x) * g_ref[...]) * (1.0 + mod[1:2]) + mod[0:1]
    u = _dot(h.astype(BF16), win_ref[...])
    uext_ref[POOL_HALO:POOL_HALO + ts, :] = u.astype(BF16)

    t = s * ts + lax.broadcasted_iota(jnp.int32, (ts, 1), 0)
    zs = []
    for g, w in enumerate(POOL_WINDOWS):
        cols = slice(g * gw, (g + 1) * gw)
        wsum = _dot(band_ref[g], uext_ref[:, cols])
        cnt = jnp.minimum(t + 1, w).astype(F32)
        p = wsum / cnt - u[:, cols]
        zs.append(_dot(p.astype(BF16), wgrp_ref[g]))
    z = jnp.concatenate(zs, axis=-1) * scale_ref[...]
    y = _dot(z.astype(BF16), wout_ref[...])
    o_ref[0] = x + mod[2:3] * y

    uext_ref[0:POOL_HALO, :] = uext_ref[ts:ts + POOL_HALO, :]


def _pool_mixer(x, mod, g, w_in, band, w_grp, scale, w_out, ts):
    b, s, d = x.shape
    ng, gw, _ = w_grp.shape
    const2 = lambda i, j: (0, 0)
    const3 = lambda i, j: (0, 0, 0)
    return pl.pallas_call(
        functools.partial(_pool_kernel, ts=ts, gw=gw),
        grid=(b, s // ts),
        in_specs=[
            pl.BlockSpec((1, ts, d), lambda i, j: (i, j, 0)),
            pl.BlockSpec((1, 6, d), lambda i, j: (i, 0, 0)),
            pl.BlockSpec((1, d), const2),
            pl.BlockSpec((d, d), const2),
            pl.BlockSpec((ng, ts, ts + POOL_HALO), const3),
            pl.BlockSpec((ng, gw, gw), const3),
            pl.BlockSpec((1, d), const2),
            pl.BlockSpec((d, d), const2),
        ],
        out_specs=pl.BlockSpec((1, ts, d), lambda i, j: (i, j, 0)),
        out_shape=jax.ShapeDtypeStruct((b, s, d), F32),
        scratch_shapes=[pltpu.VMEM((POOL_HALO + ts, d), BF16)],
        compiler_params=_cparams(("arbitrary", "arbitrary")),
        name="pool_mixer",
    )(x, mod, g, w_in, band, w_grp, scale, w_out)


def _ffn_kernel(x_ref, mod_ref, g_ref, wg_ref, wu_ref, wd_ref, o_ref, h_ref, acc_ref):
    f = pl.program_id(1)

    @pl.when(f == 0)
    def _():
        mod = mod_ref[0]
        h = (_rms_scale(x_ref[...]) * g_ref[...]) * (1.0 + mod[4:5]) + mod[3:4]
        h_ref[...] = h.astype(BF16)
        acc_ref[...] = jnp.zeros_like(acc_ref)

    h = h_ref[...]
    gate = _dot(h, wg_ref[...].astype(BF16))
    up = _dot(h, wu_ref[...].astype(BF16))
    a = (_silu(gate) * up).astype(BF16)
    acc_ref[...] += _dot(a, wd_ref[...].astype(BF16))

    @pl.when(f == pl.num_programs(1) - 1)
    def _():
        o_ref[...] = x_ref[...] + mod_ref[0][5:6] * acc_ref[...]


def _dense_ffn(x2, mod, g, wg, wu, wd, tm, tf, rows_per_batch):
    n, d = x2.shape
    ff = wg.shape[1]
    tpb = rows_per_batch // tm
    return pl.pallas_call(
        _ffn_kernel,
        grid=(n // tm, ff // tf),
        in_specs=[
            pl.BlockSpec((tm, d), lambda i, f: (i, 0)),
            pl.BlockSpec((1, 6, d), lambda i, f: (i // tpb, 0, 0)),
            pl.BlockSpec((1, d), lambda i, f: (0, 0)),
            pl.BlockSpec((d, tf), lambda i, f: (0, f)),
            pl.BlockSpec((d, tf), lambda i, f: (0, f)),
            pl.BlockSpec((tf, d), lambda i, f: (f, 0)),
        ],
        out_specs=pl.BlockSpec((tm, d), lambda i, f: (i, 0)),
        out_shape=jax.ShapeDtypeStruct((n, d), F32),
        scratch_shapes=[pltpu.VMEM((tm, d), BF16), pltpu.VMEM((tm, d), F32)],
        compiler_params=_cparams(("arbitrary", "arbitrary")),
        name="dense_ffn",
    )(x2, mod, g, wg, wu, wd)


def _qkv_kernel(x_ref, mod_ref, kvmod_ref, gmix_ref, gkv_ref, wdq_ref, qg_ref, wqn_ref, wqr_ref,
                wdkv_ref, ckvg_ref, wkr_ref, wuk_ref, wuv_ref, c_ref, s1_ref, s2_ref,
                q_ref, k_ref, v_ref):
    x = x_ref[0]
    xn = _rms_scale(x)
    mod = mod_ref[0]
    kvmod = kvmod_ref[0]
    hq = ((xn * gmix_ref[...]) * (1.0 + mod[1:2]) + mod[0:1]).astype(BF16)
    hk = ((xn * gkv_ref[...]) * (1.0 + kvmod[1:2]) + kvmod[0:1]).astype(BF16)
    c, s1, s2 = c_ref[0], s1_ref[0], s2_ref[0]

    cq = (_rms_scale(_dot(hq, wdq_ref[...])) * qg_ref[...]).astype(BF16)
    qn = _dot(cq, wqn_ref[...]) * ATTN_SCALE
    qr = _dot(cq, wqr_ref[...]) * ATTN_SCALE

    ckv = (_rms_scale(_dot(hk, wdkv_ref[...])) * ckvg_ref[...]).astype(BF16)
    kn = _dot(ckv, wuk_ref[...])
    v_ref[0] = _dot(ckv, wuv_ref[...]).astype(BF16)
    kr = _apply_rope(_dot(hk, wkr_ref[...]), c, s1, s2).astype(BF16)

    for h in range(N_HEADS):
        lo = h * HEAD_PAD
        hs = slice(h * LANES, (h + 1) * LANES)
        q_ref[0, :, lo:lo + LANES] = qn[:, hs].astype(BF16)
        q_ref[0, :, lo + LANES:lo + HEAD_PAD] = _apply_rope(qr[:, hs], c, s1, s2).astype(BF16)
        k_ref[0, :, lo:lo + LANES] = kn[:, hs].astype(BF16)
        k_ref[0, :, lo + LANES:lo + HEAD_PAD] = kr


def _qkv_proj(x, mod, kvmod, gmix, gkv, wdq, qg, wqn, wqr, wdkv, ckvg, wkr, wuk, wuv,
              c, s1, s2, ts):
    b, s, d = x.shape
    row = lambda i, j: (i, j, 0)
    per_b = lambda i, j: (i, 0, 0)

    def full(a):
        return pl.BlockSpec(a.shape, lambda i, j: (0,) * a.ndim)

    weights = (gmix, gkv, wdq, qg, wqn, wqr, wdkv, ckvg, wkr, wuk, wuv)
    return pl.pallas_call(
        _qkv_kernel,
        grid=(b, s // ts),
        in_specs=[pl.BlockSpec((1, ts, d), row),
                  pl.BlockSpec((1, 6, d), per_b),
                  pl.BlockSpec((1, 2, d), per_b)]
                 + [full(a) for a in weights]
                 + [pl.BlockSpec((1, ts, LANES), row)] * 3,
        out_specs=[pl.BlockSpec((1, ts, N_HEADS * HEAD_PAD), row),
                   pl.BlockSpec((1, ts, N_HEADS * HEAD_PAD), row),
                   pl.BlockSpec((1, ts, N_HEADS * D_V), row)],
        out_shape=[jax.ShapeDtypeStruct((b, s, N_HEADS * HEAD_PAD), BF16),
                   jax.ShapeDtypeStruct((b, s, N_HEADS * HEAD_PAD), BF16),
                   jax.ShapeDtypeStruct((b, s, N_HEADS * D_V), BF16)],
        compiler_params=_cparams(("arbitrary", "arbitrary")),
        name="qkv_proj",
    )(x, mod, kvmod, *weights, c, s1, s2)


def _attn_kernel(q_ref, k_ref, v_ref, o_ref, *, tq):
    qi = pl.program_id(2)
    q = q_ref[0]

    def scores(j):
        k = k_ref[0, pl.ds(pl.multiple_of(j * tq, tq), tq), :]
        return lax.dot_general(q, k, (((1,), (1,)), ((), ())), preferred_element_type=F32)

    def update(j, s, carry):
        m, l, acc = carry
        v = v_ref[0, pl.ds(pl.multiple_of(j * tq, tq), tq), :]
        m_new = jnp.maximum(m, jnp.max(s, axis=-1, keepdims=True))
        alpha = jnp.exp(m - m_new)
        p = jnp.exp(s - m_new)
        l = alpha * l + jnp.sum(p, axis=-1, keepdims=True)
        acc = alpha * acc + _dot(p.astype(BF16), v)
        return m_new, l, acc

    init = (jnp.full((tq, 1), -jnp.inf, F32), jnp.zeros((tq, 1), F32), jnp.zeros((tq, D_V), F32))
    carry = lax.fori_loop(0, qi, lambda j, c: update(j, scores(j), c), init)

    row = lax.broadcasted_iota(jnp.int32, (tq, tq), 0)
    col = lax.broadcasted_iota(jnp.int32, (tq, tq), 1)
    s = jnp.where(col <= row, scores(qi), -jnp.inf)
    _, l, acc = update(qi, s, carry)
    o_ref[0] = (acc / l).astype(BF16)


def _attention(q, k, v, tq):
    b, s, _ = q.shape
    return pl.pallas_call(
        functools.partial(_attn_kernel, tq=tq),
        grid=(b, N_HEADS, s // tq),
        in_specs=[
            pl.BlockSpec((1, tq, HEAD_PAD), lambda i, h, j: (i, j, h)),
            pl.BlockSpec((1, s, HEAD_PAD), lambda i, h, j: (i, 0, h)),
            pl.BlockSpec((1, s, D_V), lambda i, h, j: (i, 0, h)),
        ],
        out_specs=pl.BlockSpec((1, tq, D_V), lambda i, h, j: (i, j, h)),
        out_shape=jax.ShapeDtypeStruct((b, s, N_HEADS * D_V), BF16),
        compiler_params=_cparams(("arbitrary", "arbitrary", "arbitrary")),
        name="flash_attn",
    )(q, k, v)


def _attn_out_kernel(o_ref, x_ref, mod_ref, wo_ref, g_ref, wr_ref, ltri_ref,
                     x_out_ref, h_out_ref, rcol_ref, rrow_ref, cnt_ref, carry_ref,
                     *, n_exp, tiles_per_block):
    i = pl.program_id(0)

    @pl.when(i % tiles_per_block == 0)
    def _():
        carry_ref[...] = jnp.zeros_like(carry_ref)

    mod = mod_ref[0]
    x = x_ref[...] + mod[2:3] * _dot(o_ref[...], wo_ref[...])
    x_out_ref[...] = x
    h = (_rms_scale(x) * g_ref[...]) * (1.0 + mod[4:5]) + mod[3:4]
    h_out_ref[...] = h.astype(BF16)

    wr = wr_ref[...]
    logit = [jnp.sum(h * wr[e:e + 1], axis=-1, keepdims=True) for e in range(n_exp)]
    neg = jnp.full_like(logit[0], -jnp.inf)

    def first_argmax(vals):
        m = functools.reduce(jnp.maximum, vals)
        idx = jnp.full(m.shape, n_exp, jnp.int32)
        for e in reversed(range(n_exp)):
            idx = jnp.where(vals[e] == m, e, idx)
        return m, idx

    m1, i1 = first_argmax(logit)
    m2, i2 = first_argmax([jnp.where(i1 == e, neg, logit[e]) for e in range(n_exp)])
    ex = jnp.exp(m2 - m1)
    w1 = 1.0 / (1.0 + ex)
    w2 = ex / (1.0 + ex)

    tm = x.shape[0]
    lane = lax.broadcasted_iota(jnp.int32, (tm, LANES), 1)
    mask = jnp.where((lane == i1) | (lane == i2), 1.0, 0.0)
    ranks = _dot(ltri_ref[...], mask.astype(BF16)) + carry_ref[0:1]
    incl = ranks + mask
    rank1 = jnp.sum(jnp.where(lane == i1, ranks, 0.0), axis=-1, keepdims=True)
    rank2 = jnp.sum(jnp.where(lane == i2, ranks, 0.0), axis=-1, keepdims=True)
    carry_ref[0:1] = incl[tm - 1:tm]
    n_sub = tm // rrow_ref.shape[2]
    sub = rrow_ref.shape[2]
    cnt_ref[0] = jnp.zeros(cnt_ref.shape[1:], F32)
    for j in range(n_sub):
        cnt_ref[0, j:j + 1] = incl[(j + 1) * sub - 1:(j + 1) * sub]

    fields = (i1.astype(F32), i2.astype(F32), rank1, rank2, w1, w2)
    rcol = jnp.zeros((tm, LANES), F32)
    for k, val in enumerate(fields):
        rcol = jnp.where(lane == k, val, rcol)
    rcol_ref[...] = rcol
    rrow = rcol.T
    for j in range(n_sub):
        rrow_ref[j] = rrow[0:ROUTE_FIELDS, j * sub:(j + 1) * sub]


def _attn_out(o2, x2, mod, wo, g, wr, ltri, tm, sub, rows_per_block):
    n, d = x2.shape
    n_exp = wr.shape[0]
    tpb = rows_per_block // tm
    n_sub = tm // sub
    return pl.pallas_call(
        functools.partial(_attn_out_kernel, n_exp=n_exp, tiles_per_block=tpb),
        grid=(n // tm,),
        in_specs=[
            pl.BlockSpec((tm, o2.shape[1]), lambda i: (i, 0)),
            pl.BlockSpec((tm, d), lambda i: (i, 0)),
            pl.BlockSpec((1, 6, d), lambda i: (i // tpb, 0, 0)),
            pl.BlockSpec(wo.shape, lambda i: (0, 0)),
            pl.BlockSpec((1, d), lambda i: (0, 0)),
            pl.BlockSpec(wr.shape, lambda i: (0, 0)),
            pl.BlockSpec((tm, tm), lambda i: (0, 0)),
        ],
        out_specs=[pl.BlockSpec((tm, d), lambda i: (i, 0)),
                   pl.BlockSpec((tm, d), lambda i: (i, 0)),
                   pl.BlockSpec((tm, LANES), lambda i: (i, 0)),
                   pl.BlockSpec((n_sub, ROUTE_FIELDS, sub), lambda i: (i, 0, 0)),
                   pl.BlockSpec((1, 8, LANES), lambda i: (i, 0, 0))],
        out_shape=[jax.ShapeDtypeStruct((n, d), F32),
                   jax.ShapeDtypeStruct((n, d), BF16),
                   jax.ShapeDtypeStruct((n, LANES), F32),
                   jax.ShapeDtypeStruct((n // sub, ROUTE_FIELDS, sub), F32),
                   jax.ShapeDtypeStruct((n // tm, 8, LANES), F32)],
        scratch_shapes=[pltpu.VMEM((8, LANES), F32)],
        compiler_params=_cparams(("arbitrary",)),
        name="attn_out_router",
    )(o2, x2, mod, wo, g, wr, ltri)


def _region_offset(idx, off_ref, base, n_exp):
    out = jnp.zeros_like(idx)
    for e in range(n_exp):
        out = jnp.where(idx == e, off_ref[base + e].astype(F32), out)
    return out


def _dispatch_kernel(off_ref, lo_ref, hi_ref, h_ref, rrow_ref, xg_ref, gs_ref,
                     *, n_exp, sub, chunks_per_step, chunks_per_block):
    blk = pl.program_id(0)
    g = pl.program_id(1)
    d = h_ref.shape[1]

    def chunk_body(j, _):
        c = g * chunks_per_step + j
        row = (c * DISPATCH_CHUNK
               + lax.broadcasted_iota(jnp.int32, (DISPATCH_CHUNK, 1), 0)).astype(F32)

        def sub_body(sb, carry):
            acc, gacc = carry
            r = rrow_ref[sb]
            pos1 = r[2:3] + _region_offset(r[0:1], off_ref, blk * n_exp, n_exp)
            pos2 = r[3:4] + _region_offset(r[1:2], off_ref, blk * n_exp, n_exp)
            m1 = pos1 == row
            m2 = pos2 == row
            onehot = jnp.where(m1 | m2, 1.0, 0.0).astype(BF16)
            tok = h_ref[pl.ds(pl.multiple_of(sb * sub, sub), sub), :]
            acc = acc + _dot(onehot, tok)
            gacc = gacc + jnp.sum(jnp.where(m1, r[4:5], 0.0) + jnp.where(m2, r[5:6], 0.0),
                                  axis=-1, keepdims=True)
            return acc, gacc

        init = (jnp.zeros((DISPATCH_CHUNK, d), F32), jnp.zeros((DISPATCH_CHUNK, 1), F32))
        t = blk * chunks_per_block + c
        acc, gacc = lax.fori_loop(lo_ref[t], hi_ref[t] + 1, sub_body, init)
        rows = pl.ds(pl.multiple_of(j * DISPATCH_CHUNK, DISPATCH_CHUNK), DISPATCH_CHUNK)
        xg_ref[rows, :] = acc.astype(BF16)
        gs_ref[rows, :] = gacc
        return 0

    lax.fori_loop(0, chunks_per_step, chunk_body, 0)


def _dispatch(off, lo, hi, h, rrow, n_blocks, rows_per_block, rcap, n_exp, sub):
    n, d = h.shape
    n_sub = rows_per_block // sub
    step_rows = _pick(rcap, 1024)
    cps = step_rows // DISPATCH_CHUNK
    cpb = rcap // DISPATCH_CHUNK
    grid_spec = pltpu.PrefetchScalarGridSpec(
        num_scalar_prefetch=3,
        grid=(n_blocks, rcap // step_rows),
        in_specs=[
            pl.BlockSpec((rows_per_block, d), lambda b, g, *_: (b, 0)),
            pl.BlockSpec((n_sub, ROUTE_FIELDS, sub), lambda b, g, *_: (b, 0, 0)),
        ],
        out_specs=[
            pl.BlockSpec((step_rows, d), lambda b, g, *_: (b * (rcap // step_rows) + g, 0)),
            pl.BlockSpec((step_rows, 1), lambda b, g, *_: (b * (rcap // step_rows) + g, 0)),
        ],
    )
    return pl.pallas_call(
        functools.partial(_dispatch_kernel, n_exp=n_exp, sub=sub, chunks_per_step=cps,
                          chunks_per_block=cpb),
        grid_spec=grid_spec,
        out_shape=[jax.ShapeDtypeStruct((n_blocks * rcap, d), BF16),
                   jax.ShapeDtypeStruct((n_blocks * rcap, 1), F32)],
        compiler_params=_cparams(("arbitrary", "arbitrary")),
        name="moe_dispatch",
    )(off, lo, hi, h, rrow)


def _expert_ffn_kernel(perm_ref, pexp_ref, pfill_ref, x_ref, gs_ref, wg_ref, wu_ref, wd_ref,
                       y_ref, *, tf):
    w = pl.program_id(0)
    fill = pfill_ref[w]
    ff = wg_ref.shape[2]

    def ffn(rows):
        x = x_ref[rows, :]
        gs = gs_ref[rows, :]
        acc = jnp.zeros((x.shape[0], y_ref.shape[1]), F32)
        for f0 in range(0, ff, tf):
            gate = _dot(x, wg_ref[0, :, f0:f0 + tf])
            up = _dot(x, wu_ref[0, :, f0:f0 + tf])
            a = ((_silu(gate) * up) * gs).astype(BF16)
            acc = acc + _dot(a, wd_ref[0, f0:f0 + tf, :])
        return acc.astype(BF16)

    half = PIECE // 2

    @pl.when(fill == 2)
    def _():
        y_ref[...] = ffn(slice(0, PIECE))

    @pl.when(fill == 1)
    def _():
        y_ref[0:half, :] = ffn(slice(0, half))
        y_ref[half:PIECE, :] = jnp.zeros((half, y_ref.shape[1]), BF16)

    @pl.when(fill == 0)
    def _():
        y_ref[...] = jnp.zeros(y_ref.shape, BF16)


def _expert_ffn(perm, pexp, pfill, xg, gs, wg, wu, wd, tf):
    rows, d = xg.shape
    n_exp, _, ff = wg.shape
    grid_spec = pltpu.PrefetchScalarGridSpec(
        num_scalar_prefetch=3,
        grid=(rows // PIECE,),
        in_specs=[
            pl.BlockSpec((PIECE, d), lambda w, perm, pexp, pfill: (perm[w], 0)),
            pl.BlockSpec((PIECE, 1), lambda w, perm, pexp, pfill: (perm[w], 0)),
            pl.BlockSpec((1, d, ff), lambda w, perm, pexp, pfill: (pexp[w], 0, 0)),
            pl.BlockSpec((1, d, ff), lambda w, perm, pexp, pfill: (pexp[w], 0, 0)),
            pl.BlockSpec((1, ff, d), lambda w, perm, pexp, pfill: (pexp[w], 0, 0)),
        ],
        out_specs=pl.BlockSpec((PIECE, d), lambda w, perm, pexp, pfill: (perm[w], 0)),
    )
    return pl.pallas_call(
        functools.partial(_expert_ffn_kernel, tf=tf),
        grid_spec=grid_spec,
        out_shape=jax.ShapeDtypeStruct((rows, d), BF16),
        compiler_params=_cparams(("arbitrary",)),
        name="expert_ffn",
    )(perm, pexp, pfill, xg, gs, wg, wu, wd)


def _combine_kernel(off_ref, lo_ref, hi_ref, yg_ref, rcol_ref, x_ref, mod_ref, gfin_ref, o_ref,
                    *, n_exp, sub, subs_per_block):
    blk = pl.program_id(0)
    t = pl.program_id(1)
    tm, d = x_ref.shape
    col = lax.broadcasted_iota(jnp.int32, (1, PIECE), 1)
    for s in range(tm // sub):
        rows = slice(s * sub, (s + 1) * sub)
        rc = rcol_ref[rows, :]
        pos1 = rc[:, 2:3] + _region_offset(rc[:, 0:1], off_ref, blk * n_exp, n_exp)
        pos2 = rc[:, 3:4] + _region_offset(rc[:, 1:2], off_ref, blk * n_exp, n_exp)
        sb = blk * subs_per_block + t * (tm // sub) + s

        def piece_body(c, acc):
            rowid = (c * PIECE + col).astype(F32)
            onehot = jnp.where((pos1 == rowid) | (pos2 == rowid), 1.0, 0.0).astype(BF16)
            return acc + _dot(onehot, yg_ref[pl.ds(pl.multiple_of(c * PIECE, PIECE), PIECE), :])

        acc = jnp.zeros((sub, d), F32)
        for e in range(n_exp):
            acc = lax.fori_loop(lo_ref[sb * n_exp + e], hi_ref[sb * n_exp + e] + 1, piece_body, acc)
        x = x_ref[rows, :] + mod_ref[0][5:6] * acc
        o_ref[rows, :] = _rms_scale(x) * gfin_ref[...]


def _combine(off, lo, hi, yg, rcol, x2, mod, gfin, n_blocks, rows_per_block, rcap, n_exp, sub, tm):
    n, d = x2.shape
    tpb = rows_per_block // tm
    grid_spec = pltpu.PrefetchScalarGridSpec(
        num_scalar_prefetch=3,
        grid=(n_blocks, tpb),
        in_specs=[
            pl.BlockSpec((rcap, d), lambda b, t, *_: (b, 0)),
            pl.BlockSpec((tm, LANES), lambda b, t, *_: (b * tpb + t, 0)),
            pl.BlockSpec((tm, d), lambda b, t, *_: (b * tpb + t, 0)),
            pl.BlockSpec((1, 6, d), lambda b, t, *_: (b, 0, 0)),
            pl.BlockSpec((1, d), lambda b, t, *_: (0, 0)),
        ],
        out_specs=pl.BlockSpec((tm, d), lambda b, t, *_: (b * tpb + t, 0)),
    )
    return pl.pallas_call(
        functools.partial(_combine_kernel, n_exp=n_exp, sub=sub,
                          subs_per_block=rows_per_block // sub),
        grid_spec=grid_spec,
        out_shape=jax.ShapeDtypeStruct((n, d), F32),
        compiler_params=_cparams(("arbitrary", "arbitrary")),
        name="moe_combine",
    )(off, lo, hi, yg, rcol, x2, mod, gfin)


def _routing_tables(cnt, n_blocks, n_sub, n_exp, rcap):
    cum_incl = cnt.astype(jnp.int32)
    cum_excl = jnp.concatenate([jnp.zeros_like(cum_incl[:, :1]), cum_incl[:, :-1]], axis=1)
    total = cum_incl[:, -1]
    psz = (total + PIECE - 1) // PIECE * PIECE
    off = jnp.cumsum(psz, axis=1) - psz
    start = off[:, None, :] + cum_excl
    end = off[:, None, :] + cum_incl

    c0 = jnp.arange(rcap // DISPATCH_CHUNK, dtype=jnp.int32)[None, :, None, None] * DISPATCH_CHUNK
    hit = ((start[:, None] < c0 + DISPATCH_CHUNK) & (end[:, None] > c0)
           & (end[:, None] > start[:, None])).any(-1)
    sub_ids = jnp.arange(n_sub, dtype=jnp.int32)
    d_lo = jnp.min(jnp.where(hit, sub_ids, n_sub), axis=-1)
    d_hi = jnp.max(jnp.where(hit, sub_ids, -1), axis=-1)
    d_lo = jnp.where(d_hi < 0, 0, d_lo)

    c_lo = start // PIECE
    c_hi = jnp.where(end > start, (end - 1) // PIECE, c_lo - 1)

    ppb = rcap // PIECE
    prow = jnp.arange(ppb, dtype=jnp.int32)[None, :, None] * PIECE
    inside = (prow >= off[:, None, :]) & (prow < (off + psz)[:, None, :])
    p_exp = jnp.argmax(inside, axis=-1).astype(jnp.int32)
    used = inside.any(-1)
    left = jnp.take_along_axis(total + off, p_exp, axis=1) - prow[..., 0]
    p_fill = jnp.where(used & (left > 0), jnp.where(left > PIECE // 2, 2, 1), 0).astype(jnp.int32)
    key = jnp.where(p_fill > 0, p_exp, n_exp).reshape(-1)
    perm = jnp.argsort(key, stable=True).astype(jnp.int32)
    p_exp_sorted = jnp.minimum(key[perm], n_exp - 1).astype(jnp.int32)
    return (off.reshape(-1).astype(jnp.int32), d_lo.reshape(-1), d_hi.reshape(-1),
            c_lo.reshape(-1).astype(jnp.int32), c_hi.reshape(-1).astype(jnp.int32),
            perm, p_exp_sorted, p_fill.reshape(-1)[perm])


def _pick(n, pref):
    t = min(n, pref)
    while n % t:
        t //= 2
    return t


def kernel(x, c, positions, ada_w, ada_b, mix_norm_g, ffn_norm_g, pool_w_in, pool_w_grp,
           pool_scale, pool_w_out, kv_ada_w, kv_ada_b, kv_norm_g, w_dkv, ckv_norm_g, w_kr,
           w_uk, w_uv, w_dq, q_norm_g, w_uq, w_o, ffn_w_gate, ffn_w_up, ffn_w_down,
           router_w, moe_w_gate, moe_w_up, moe_w_down, final_norm_g):
    b, s, d = x.shape
    n = b * s
    depth = ada_w.shape[0]
    assert depth == 2 and pool_w_in.shape[0] == 1 and w_dq.shape[0] == 1
    assert b <= 8 and d % LANES == 0

    ts = _pick(s, 512)
    tm = _pick(s, 1024)
    tf = _pick(ffn_w_gate.shape[-1], 512)

    c_pad = jnp.zeros((8, d), F32).at[:b].set(c)
    mods = _adaln(c_pad, ada_w, ada_b[:, None, :], _pick(6 * d, 1536))
    mods = mods[:, :b].reshape(depth, b, 6, d)
    kvmod = _adaln(c_pad, kv_ada_w[None], kv_ada_b[None, None, :], _pick(2 * d, 1024))
    kvmod = kvmod[0, :b].reshape(b, 2, d)

    r = jnp.arange(ts)[:, None]
    cidx = jnp.arange(ts + POOL_HALO)[None, :] - POOL_HALO
    band = jnp.stack([((cidx <= r) & (cidx > r - w)) for w in POOL_WINDOWS]).astype(BF16)
    x1 = _pool_mixer(x, mods[0], mix_norm_g[0:1], pool_w_in[0].astype(BF16), band,
                     pool_w_grp[0].astype(BF16), pool_scale[0:1], pool_w_out[0].astype(BF16), ts)

    x2 = _dense_ffn(x1.reshape(n, d), mods[0], ffn_norm_g[0:1], ffn_w_gate[0], ffn_w_up[0],
                    ffn_w_down[0], tm, tf, s)

    inv_freq = ROPE_THETA ** (-jnp.arange(0, D_ROPE, 2, dtype=F32) / D_ROPE)
    freq_row = jnp.concatenate([inv_freq, inv_freq, jnp.zeros((LANES - D_ROPE,), F32)])[None, :]
    rope_c, rope_s1, rope_s2 = _rope_tables(positions.astype(F32)[..., None], freq_row, ts)

    q_lora = w_dq.shape[-1]
    wuq = w_uq[0].reshape(q_lora, N_HEADS, D_NOPE + D_ROPE)
    wqn = wuq[:, :, :D_NOPE].reshape(q_lora, N_HEADS * D_NOPE).astype(BF16)
    wqr = jnp.pad(wuq[:, :, D_NOPE:], ((0, 0), (0, 0), (0, LANES - D_ROPE)))
    wqr = wqr.reshape(q_lora, N_HEADS * LANES).astype(BF16)
    wkr = jnp.pad(w_kr, ((0, 0), (0, LANES - D_ROPE))).astype(BF16)
    q, k, v = _qkv_proj(
        x2.reshape(b, s, d), mods[1], kvmod, mix_norm_g[1:2], kv_norm_g[None, :],
        w_dq[0].astype(BF16), q_norm_g[0:1], wqn, wqr, w_dkv.astype(BF16), ckv_norm_g[None, :],
        wkr, w_uk.astype(BF16), w_uv.astype(BF16), rope_c, rope_s1, rope_s2, ts)
    o = _attention(q, k, v, ts)

    n_exp = router_w.shape[-1]
    sub = _pick(s, ROUTE_SUB)
    n_sub = s // sub
    rcap = 2 * s + n_exp * PIECE
    ltri = (jnp.arange(ts)[None, :] < jnp.arange(ts)[:, None]).astype(BF16)
    x3, h3, rcol, rrow, cnt = _attn_out(o.reshape(n, N_HEADS * D_V), x2, mods[1],
                                        w_o[0].astype(BF16), ffn_norm_g[1:2], router_w[0].T,
                                        ltri, ts, sub, s)
    cnt = cnt[:, :ts // sub, :n_exp].reshape(b, n_sub, n_exp)
    off, d_lo, d_hi, c_lo, c_hi, perm, p_exp, p_fill = _routing_tables(cnt, b, n_sub, n_exp, rcap)
    xg, gs = _dispatch(off, d_lo, d_hi, h3, rrow, b, s, rcap, n_exp, sub)
    yg = _expert_ffn(perm, p_exp, p_fill, xg, gs, moe_w_gate[0].astype(BF16),
                     moe_w_up[0].astype(BF16), moe_w_down[0].astype(BF16), tf)
    out = _combine(off, c_lo, c_hi, yg, rcol, x3, mods[1], final_norm_g[None, :], b, s, rcap,
                   n_exp, sub, ts)
    return out.reshape(b, s, d)
```

```python
import functools
import math

import jax
import jax.numpy as jnp
from jax import lax
from jax.experimental import pallas as pl
from jax.experimental.pallas import tpu as pltpu

F32 = jnp.float32
BF16 = jnp.bfloat16

N_HEADS = 8
D_NOPE = 128
D_ROPE = 64
D_V = 128
HEAD_PAD = 256
ROPE_HALF = D_ROPE // 2
ROPE_THETA = 10000.0
ATTN_SCALE = 1.0 / math.sqrt(D_NOPE + D_ROPE)
ATTN_KEY_BLOCK = 512
POOL_WINDOWS = (2, 4, 8, 16)
POOL_HALO = 16
EPS = 1e-6

ROUTE_SUB = 256
DISPATCH_CHUNK = 128
DISPATCH_WINDOW = 4
COMBINE_WINDOW = 256
PIECE = 256
ROUTE_FIELDS = 8

LANES = 128
VMEM_LIMIT_BYTES = 56 * 1024 * 1024


def _cparams(sem):
    return pltpu.CompilerParams(dimension_semantics=sem, vmem_limit_bytes=VMEM_LIMIT_BYTES)


def _dot(a, b):
    return jnp.dot(a, b, preferred_element_type=F32)


def _rms_scale(x):
    return x * lax.rsqrt(jnp.mean(x * x, axis=-1, keepdims=True) + EPS)


def _silu(x):
    return x * jax.nn.sigmoid(x)


def _adaln_kernel(c_ref, w_ref, b_ref, o_ref):
    sc = _silu(c_ref[...])
    o_ref[0] = jnp.dot(sc, w_ref[0], preferred_element_type=F32,
                       precision=lax.Precision.HIGHEST) + b_ref[0]


def _adaln(c_pad, w, b, tn):
    nl, d, m = w.shape
    return pl.pallas_call(
        _adaln_kernel,
        grid=(nl, m // tn),
        in_specs=[
            pl.BlockSpec((8, d), lambda l, j: (0, 0)),
            pl.BlockSpec((1, d, tn), lambda l, j: (l, 0, j)),
            pl.BlockSpec((1, 1, tn), lambda l, j: (l, 0, j)),
        ],
        out_specs=pl.BlockSpec((1, 8, tn), lambda l, j: (l, 0, j)),
        out_shape=jax.ShapeDtypeStruct((nl, 8, m), F32),
        compiler_params=_cparams(("arbitrary", "arbitrary")),
        name="adaln",
    )(c_pad, w, b)


def _rope_table_kernel(pos_ref, freq_ref, tab_ref, tab_t_ref):
    ang = pos_ref[0] * freq_ref[...]
    lane = lax.broadcasted_iota(jnp.int32, ang.shape, 1)
    cos = jnp.cos(ang)
    sin = jnp.sin(ang)
    zero = jnp.zeros_like(ang)
    tabs = (jnp.where(lane < D_ROPE, cos, zero),
            jnp.where(lane < ROPE_HALF, -sin, zero),
            jnp.where((lane >= ROPE_HALF) & (lane < D_ROPE), sin, zero))
    for i, tab in enumerate(tabs):
        tab_ref[0, :, i * LANES:(i + 1) * LANES] = tab
        tab_t_ref[0, 0, i * LANES:(i + 1) * LANES, :] = tab.T


def _rope_tables(pos_f, freq_row, ts):
    b, s, _ = pos_f.shape
    return pl.pallas_call(
        _rope_table_kernel,
        grid=(b, s // ts),
        in_specs=[
            pl.BlockSpec((1, ts, 1), lambda i, j: (i, j, 0)),
            pl.BlockSpec((1, LANES), lambda i, j: (0, 0)),
        ],
        out_specs=[pl.BlockSpec((1, ts, 3 * LANES), lambda i, j: (i, j, 0)),
                   pl.BlockSpec((1, 1, 3 * LANES, ts), lambda i, j: (i, j, 0, 0))],
        out_shape=[jax.ShapeDtypeStruct((b, s, 3 * LANES), F32),
                   jax.ShapeDtypeStruct((b, s // ts, 3 * LANES, ts), F32)],
        compiler_params=_cparams(("arbitrary", "arbitrary")),
        name="rope_tables",
    )(pos_f, freq_row)


def _apply_rope(a, tab):
    c, s1, s2 = (tab[:, i * LANES:(i + 1) * LANES] for i in range(3))
    return a * c + pltpu.roll(a, LANES - ROPE_HALF, 1) * s1 + pltpu.roll(a, ROPE_HALF, 1) * s2


def _apply_rope_t(a, tab_t):
    c, s1, s2 = (tab_t[i * LANES:(i + 1) * LANES] for i in range(3))
    up = jnp.concatenate([a[ROPE_HALF:], a[:ROPE_HALF]], axis=0)
    down = jnp.concatenate([a[-ROPE_HALF:], a[:-ROPE_HALF]], axis=0)
    return a * c + up * s1 + down * s2


def _pool_kernel(x_ref, mod_ref, g_ref, win_ref, band_ref, wgrp_ref, scale_ref, wout_ref,
                 o_ref, uext_ref, *, ts, gw):
    s = pl.program_id(1)

    @pl.when(s == 0)
    def _():
        uext_ref[0:POOL_HALO, :] = jnp.zeros((POOL_HALO, uext_ref.shape[1]), BF16)

    x = x_ref[0]
    mod = mod_ref[0]
    h = (_rms_scale(x) * g_ref[...]) * (1.0 + mod[1:2]) + mod[0:1]
    u = _dot(h.astype(BF16), win_ref[...])
    uext_ref[POOL_HALO:POOL_HALO + ts, :] = u.astype(BF16)

    t = s * ts + lax.broadcasted_iota(jnp.int32, (ts, 1), 0)
    zs = []
    for g, w in enumerate(POOL_WINDOWS):
        cols = slice(g * gw, (g + 1) * gw)
        wsum = _dot(band_ref[g], uext_ref[:, cols])
        cnt = jnp.minimum(t + 1, w).astype(F32)
        p = wsum / cnt - u[:, cols]
        zs.append(_dot(p.astype(BF16), wgrp_ref[g]))
    z = jnp.concatenate(zs, axis=-1) * scale_ref[...]
    y = _dot(z.astype(BF16), wout_ref[...])
    o_ref[0] = x + mod[2:3] * y

    uext_ref[0:POOL_HALO, :] = uext_ref[ts:ts + POOL_HALO, :]


def _pool_mixer(x, mod, g, w_in, band, w_grp, scale, w_out, ts):
    b, s, d = x.shape
    ng, gw, _ = w_grp.shape
    const2 = lambda i, j: (0, 0)
    const3 = lambda i, j: (0, 0, 0)
    return pl.pallas_call(
        functools.partial(_pool_kernel, ts=ts, gw=gw),
        grid=(b, s // ts),
        in_specs=[
            pl.BlockSpec((1, ts, d), lambda i, j: (i, j, 0)),
            pl.BlockSpec((1, 6, d), lambda i, j: (i, 0, 0)),
            pl.BlockSpec((1, d), const2),
            pl.BlockSpec((d, d), const2),
            pl.BlockSpec((ng, ts, ts + POOL_HALO), const3),
            pl.BlockSpec((ng, gw, gw), const3),
            pl.BlockSpec((1, d), const2),
            pl.BlockSpec((d, d), const2),
        ],
        out_specs=pl.BlockSpec((1, ts, d), lambda i, j: (i, j, 0)),
        out_shape=jax.ShapeDtypeStruct((b, s, d), F32),
        scratch_shapes=[pltpu.VMEM((POOL_HALO + ts, d), BF16)],
        compiler_params=_cparams(("arbitrary", "arbitrary")),
        name="pool_mixer",
    )(x, mod, g, w_in, band, w_grp, scale, w_out)


def _ffn_kernel(x_ref, mod_ref, g_ref, wg_ref, wu_ref, wd_ref, o_ref, h_ref, acc_ref):
    f = pl.program_id(1)

    @pl.when(f == 0)
    def _():
        mod = mod_ref[0]
        h = (_rms_scale(x_ref[...]) * g_ref[...]) * (1.0 + mod[4:5]) + mod[3:4]
        h_ref[...] = h.astype(BF16)
        acc_ref[...] = jnp.zeros_like(acc_ref)

    h = h_ref[...]
    gate = _dot(h, wg_ref[...].astype(BF16))
    up = _dot(h, wu_ref[...].astype(BF16))
    a = (_silu(gate) * up).astype(BF16)
    acc_ref[...] += _dot(a, wd_ref[...].astype(BF16))

    @pl.when(f == pl.num_programs(1) - 1)
    def _():
        o_ref[...] = x_ref[...] + mod_ref[0][5:6] * acc_ref[...]


def _dense_ffn(x2, mod, g, wg, wu, wd, tm, tf, rows_per_batch):
    n, d = x2.shape
    ff = wg.shape[1]
    tpb = rows_per_batch // tm
    return pl.pallas_call(
        _ffn_kernel,
        grid=(n // tm, ff // tf),
        in_specs=[
            pl.BlockSpec((tm, d), lambda i, f: (i, 0)),
            pl.BlockSpec((1, 6, d), lambda i, f: (i // tpb, 0, 0)),
            pl.BlockSpec((1, d), lambda i, f: (0, 0)),
            pl.BlockSpec((d, tf), lambda i, f: (0, f)),
            pl.BlockSpec((d, tf), lambda i, f: (0, f)),
            pl.BlockSpec((tf, d), lambda i, f: (f, 0)),
        ],
        out_specs=pl.BlockSpec((tm, d), lambda i, f: (i, 0)),
        out_shape=jax.ShapeDtypeStruct((n, d), F32),
        scratch_shapes=[pltpu.VMEM((tm, d), BF16), pltpu.VMEM((tm, d), F32)],
        compiler_params=_cparams(("arbitrary", "arbitrary")),
        name="dense_ffn",
    )(x2, mod, g, wg, wu, wd)


def _dot_nt(a, b):
    return lax.dot_general(a, b, (((1,), (1,)), ((), ())), preferred_element_type=F32)


def _qkv_kernel(x_ref, mod_ref, kvmod_ref, gmix_ref, gkv_ref, wdq_ref, qg_ref, wqn_t_ref,
                wqr_t_ref, wdkv_ref, ckvg_ref, wkr_ref, wuk_ref, wuv_t_ref, tab_ref, tab_t_ref,
                qt_ref, k_ref, vt_ref):
    x = x_ref[0]
    xn = _rms_scale(x)
    mod = mod_ref[0]
    kvmod = kvmod_ref[0]
    hq = ((xn * gmix_ref[...]) * (1.0 + mod[1:2]) + mod[0:1]).astype(BF16)
    hk = ((xn * gkv_ref[...]) * (1.0 + kvmod[1:2]) + kvmod[0:1]).astype(BF16)

    cq = (_rms_scale(_dot(hq, wdq_ref[...])) * qg_ref[...]).astype(BF16)
    qn_t = _dot_nt(wqn_t_ref[...], cq) * ATTN_SCALE
    qr_t = _dot_nt(wqr_t_ref[...], cq) * ATTN_SCALE
    tab_t = tab_t_ref[0, 0]

    ckv = (_rms_scale(_dot(hk, wdkv_ref[...])) * ckvg_ref[...]).astype(BF16)
    kn = _dot(ckv, wuk_ref[...])
    vt_ref[0, 0] = _dot_nt(wuv_t_ref[...], ckv).astype(BF16)
    kr = _apply_rope(_dot(hk, wkr_ref[...]), tab_ref[0]).astype(BF16)

    for h in range(N_HEADS):
        lo = h * HEAD_PAD
        hs = slice(h * LANES, (h + 1) * LANES)
        qt_ref[0, 0, lo:lo + LANES, :] = qn_t[hs].astype(BF16)
        qt_ref[0, 0, lo + LANES:lo + HEAD_PAD, :] = _apply_rope_t(qr_t[hs], tab_t).astype(BF16)
        k_ref[0, :, lo:lo + LANES] = kn[:, hs].astype(BF16)
        k_ref[0, :, lo + LANES:lo + HEAD_PAD] = kr


def _qkv_proj(x, mod, kvmod, gmix, gkv, wdq, qg, wqn_t, wqr_t, wdkv, ckvg, wkr, wuk, wuv_t,
              tab, tab_t, ts):
    b, s, d = x.shape
    row = lambda i, j: (i, j, 0)
    tile = lambda i, j: (i, j, 0, 0)
    per_b = lambda i, j: (i, 0, 0)

    def full(a):
        return pl.BlockSpec(a.shape, lambda i, j: (0,) * a.ndim)

    weights = (gmix, gkv, wdq, qg, wqn_t, wqr_t, wdkv, ckvg, wkr, wuk, wuv_t)
    return pl.pallas_call(
        _qkv_kernel,
        grid=(b, s // ts),
        in_specs=[pl.BlockSpec((1, ts, d), row),
                  pl.BlockSpec((1, 6, d), per_b),
                  pl.BlockSpec((1, 2, d), per_b)]
                 + [full(a) for a in weights]
                 + [pl.BlockSpec((1, ts, 3 * LANES), row),
                    pl.BlockSpec((1, 1, 3 * LANES, ts), tile)],
        out_specs=[pl.BlockSpec((1, 1, N_HEADS * HEAD_PAD, ts), tile),
                   pl.BlockSpec((1, ts, N_HEADS * HEAD_PAD), row),
                   pl.BlockSpec((1, 1, N_HEADS * D_V, ts), tile)],
        out_shape=[jax.ShapeDtypeStruct((b, s // ts, N_HEADS * HEAD_PAD, ts), BF16),
                   jax.ShapeDtypeStruct((b, s, N_HEADS * HEAD_PAD), BF16),
                   jax.ShapeDtypeStruct((b, s // ts, N_HEADS * D_V, ts), BF16)],
        compiler_params=_cparams(("arbitrary", "arbitrary")),
        name="qkv_proj",
    )(x, mod, kvmod, *weights, tab, tab_t)


def _attn_kernel(qt_ref, k_ref, vt_ref, o_ref, *, tq, tk):
    nq = qt_ref.shape[1]
    kpq = tq // tk
    key = lax.broadcasted_iota(jnp.int32, (tk, tq), 0)
    qry = lax.broadcasted_iota(jnp.int32, (tk, tq), 1)
    blocks = [(qi, j) for qi in range(nq) for j in range((qi + 1) * kpq)]

    def scores_t(qi, j):
        st = _dot(k_ref[0, j * tk:(j + 1) * tk, :], qt_ref[0, qi])
        if j >= qi * kpq:
            st = jnp.where(key + (j * tk - qi * tq) <= qry, st, -jnp.inf)
        return st

    st_next = scores_t(*blocks[0])
    for n, (qi, j) in enumerate(blocks):
        st = st_next
        if n + 1 < len(blocks):
            st_next = scores_t(*blocks[n + 1])
        if j == 0:
            m = jnp.full((1, tq), -jnp.inf, F32)
            l = jnp.zeros((1, tq), F32)
            acc = jnp.zeros((D_V, tq), F32)
        m_new = jnp.maximum(m, jnp.max(st, axis=0, keepdims=True))
        alpha = jnp.exp(m - m_new)
        pt = jnp.exp(st - m_new)
        l = alpha * l + jnp.sum(pt, axis=0, keepdims=True)
        vt = vt_ref[0, j // kpq, :, (j % kpq) * tk:(j % kpq + 1) * tk]
        acc = alpha * acc + _dot(vt, pt.astype(BF16))
        m = m_new
        if j == (qi + 1) * kpq - 1:
            o_ref[0, qi * tq:(qi + 1) * tq, :] = (acc / l).T.astype(BF16)


def _attention(qt, k, vt, tq):
    b, s, _ = k.shape
    return pl.pallas_call(
        functools.partial(_attn_kernel, tq=tq, tk=min(tq, ATTN_KEY_BLOCK)),
        grid=(b, N_HEADS),
        in_specs=[
            pl.BlockSpec((1, s // tq, HEAD_PAD, tq), lambda i, h: (i, 0, h, 0)),
            pl.BlockSpec((1, s, HEAD_PAD), lambda i, h: (i, 0, h)),
            pl.BlockSpec((1, s // tq, D_V, tq), lambda i, h: (i, 0, h, 0)),
        ],
        out_specs=pl.BlockSpec((1, s, D_V), lambda i, h: (i, 0, h)),
        out_shape=jax.ShapeDtypeStruct((b, s, N_HEADS * D_V), BF16),
        compiler_params=_cparams(("arbitrary", "arbitrary")),
        name="flash_attn",
    )(qt, k, vt)


def _attn_out_kernel(o_ref, x_ref, mod_ref, wo_ref, g_ref, wr_ref, ltri_ref,
                     x_out_ref, h_out_ref, rcol_ref, rrow_ref, cnt_ref, carry_ref,
                     *, n_exp, tiles_per_block):
    i = pl.program_id(0)

    @pl.when(i % tiles_per_block == 0)
    def _():
        carry_ref[...] = jnp.zeros_like(carry_ref)

    mod = mod_ref[0]
    x = x_ref[...] + mod[2:3] * _dot(o_ref[...], wo_ref[...])
    x_out_ref[...] = x
    h = (_rms_scale(x) * g_ref[...]) * (1.0 + mod[4:5]) + mod[3:4]
    h_out_ref[...] = h.astype(BF16)

    wr = wr_ref[...]
    logit = [jnp.sum(h * wr[e:e + 1], axis=-1, keepdims=True) for e in range(n_exp)]
    neg = jnp.full_like(logit[0], -jnp.inf)

    def first_argmax(vals):
        m = functools.reduce(jnp.maximum, vals)
        idx = jnp.full(m.shape, n_exp, jnp.int32)
        for e in reversed(range(n_exp)):
            idx = jnp.where(vals[e] == m, e, idx)
        return m, idx

    m1, i1 = first_argmax(logit)
    m2, i2 = first_argmax([jnp.where(i1 == e, neg, logit[e]) for e in range(n_exp)])
    ex = jnp.exp(m2 - m1)
    w1 = 1.0 / (1.0 + ex)
    w2 = ex / (1.0 + ex)

    tm = x.shape[0]
    lane = lax.broadcasted_iota(jnp.int32, (tm, LANES), 1)
    mask = jnp.where((lane == i1) | (lane == i2), 1.0, 0.0)
    ranks = _dot(ltri_ref[...], mask.astype(BF16)) + carry_ref[0:1]
    incl = ranks + mask
    rank1 = jnp.sum(jnp.where(lane == i1, ranks, 0.0), axis=-1, keepdims=True)
    rank2 = jnp.sum(jnp.where(lane == i2, ranks, 0.0), axis=-1, keepdims=True)
    carry_ref[0:1] = incl[tm - 1:tm]
    n_sub = tm // rrow_ref.shape[2]
    sub = rrow_ref.shape[2]
    cnt_ref[0] = jnp.zeros(cnt_ref.shape[1:], F32)
    for j in range(n_sub):
        cnt_ref[0, j:j + 1] = incl[(j + 1) * sub - 1:(j + 1) * sub]

    fields = (i1.astype(F32), i2.astype(F32), rank1, rank2, w1, w2)
    rcol = jnp.zeros((tm, LANES), F32)
    for k, val in enumerate(fields):
        rcol = jnp.where(lane == k, val, rcol)
    rcol_ref[...] = rcol
    rrow = rcol.T
    for j in range(n_sub):
        rrow_ref[j] = rrow[0:ROUTE_FIELDS, j * sub:(j + 1) * sub]


def _attn_out(o2, x2, mod, wo, g, wr, ltri, tm, sub, rows_per_block):
    n, d = x2.shape
    n_exp = wr.shape[0]
    tpb = rows_per_block // tm
    n_sub = tm // sub
    return pl.pallas_call(
        functools.partial(_attn_out_kernel, n_exp=n_exp, tiles_per_block=tpb),
        grid=(n // tm,),
        in_specs=[
            pl.BlockSpec((tm, o2.shape[1]), lambda i: (i, 0)),
            pl.BlockSpec((tm, d), lambda i: (i, 0)),
            pl.BlockSpec((1, 6, d), lambda i: (i // tpb, 0, 0)),
            pl.BlockSpec(wo.shape, lambda i: (0, 0)),
            pl.BlockSpec((1, d), lambda i: (0, 0)),
            pl.BlockSpec(wr.shape, lambda i: (0, 0)),
            pl.BlockSpec((tm, tm), lambda i: (0, 0)),
        ],
        out_specs=[pl.BlockSpec((tm, d), lambda i: (i, 0)),
                   pl.BlockSpec((tm, d), lambda i: (i, 0)),
                   pl.BlockSpec((tm, LANES), lambda i: (i, 0)),
                   pl.BlockSpec((n_sub, ROUTE_FIELDS, sub), lambda i: (i, 0, 0)),
                   pl.BlockSpec((1, 8, LANES), lambda i: (i, 0, 0))],
        out_shape=[jax.ShapeDtypeStruct((n, d), F32),
                   jax.ShapeDtypeStruct((n, d), BF16),
                   jax.ShapeDtypeStruct((n, LANES), F32),
                   jax.ShapeDtypeStruct((n // sub, ROUTE_FIELDS, sub), F32),
                   jax.ShapeDtypeStruct((n // tm, 8, LANES), F32)],
        scratch_shapes=[pltpu.VMEM((8, LANES), F32)],
        compiler_params=_cparams(("arbitrary",)),
        name="attn_out_router",
    )(o2, x2, mod, wo, g, wr, ltri)


def _region_offset(idx, off_ref, base, n_exp):
    out = jnp.zeros_like(idx)
    for e in range(n_exp):
        out = jnp.where(idx == e, off_ref[base + e].astype(F32), out)
    return out


def _dispatch_kernel(off_ref, lo_ref, hi_ref, h_ref, rrow_ref, xg_ref, gs_ref,
                     *, n_exp, sub, chunks_per_step, chunks_per_block):
    blk = pl.program_id(0)
    g = pl.program_id(1)
    d = h_ref.shape[1]
    n_sub = rrow_ref.shape[0]
    win = min(DISPATCH_WINDOW, n_sub)
    iota = lax.broadcasted_iota(jnp.int32, (DISPATCH_CHUNK, 1), 0)

    def contribution(row, sb):
        r = rrow_ref[sb]
        pos1 = r[2:3] + _region_offset(r[0:1], off_ref, blk * n_exp, n_exp)
        pos2 = r[3:4] + _region_offset(r[1:2], off_ref, blk * n_exp, n_exp)
        m1 = pos1 == row
        m2 = pos2 == row
        onehot = jnp.where(m1 | m2, 1.0, 0.0).astype(BF16)
        tok = h_ref[pl.ds(pl.multiple_of(sb * sub, sub), sub), :]
        gate = jnp.sum(jnp.where(m1, r[4:5], 0.0) + jnp.where(m2, r[5:6], 0.0),
                       axis=-1, keepdims=True)
        return _dot(onehot, tok), gate

    def chunk_info(j):
        c = g * chunks_per_step + j
        t = blk * chunks_per_block + c
        row = (c * DISPATCH_CHUNK + iota).astype(F32)
        start = jnp.minimum(lo_ref[t], n_sub - win)
        return row, start, hi_ref[t]

    for j in range(chunks_per_step):
        row, start, _ = chunk_info(j)
        acc = jnp.zeros((DISPATCH_CHUNK, d), F32)
        gate = jnp.zeros((DISPATCH_CHUNK, 1), F32)
        for k in range(win):
            a, b = contribution(row, start + k)
            acc += a
            gate += b
        rows = slice(j * DISPATCH_CHUNK, (j + 1) * DISPATCH_CHUNK)
        xg_ref[rows, :] = acc.astype(BF16)
        gs_ref[rows, :] = gate

    def extra(j, _):
        row, start, last = chunk_info(j)
        rows = pl.ds(pl.multiple_of(j * DISPATCH_CHUNK, DISPATCH_CHUNK), DISPATCH_CHUNK)

        def add(sb, _):
            a, b = contribution(row, sb)
            xg_ref[rows, :] = (xg_ref[rows, :].astype(F32) + a).astype(BF16)
            gs_ref[rows, :] = gs_ref[rows, :] + b
            return 0

        return lax.fori_loop(start + win, last + 1, add, 0)

    lax.fori_loop(0, chunks_per_step, extra, 0)


def _dispatch(off, lo, hi, h, rrow, n_blocks, rows_per_block, rcap, n_exp, sub):
    n, d = h.shape
    n_sub = rows_per_block // sub
    step_rows = _pick(rcap, 1024)
    cps = step_rows // DISPATCH_CHUNK
    cpb = rcap // DISPATCH_CHUNK
    grid_spec = pltpu.PrefetchScalarGridSpec(
        num_scalar_prefetch=3,
        grid=(n_blocks, rcap // step_rows),
        in_specs=[
            pl.BlockSpec((rows_per_block, d), lambda b, g, *_: (b, 0)),
            pl.BlockSpec((n_sub, ROUTE_FIELDS, sub), lambda b, g, *_: (b, 0, 0)),
        ],
        out_specs=[
            pl.BlockSpec((step_rows, d), lambda b, g, *_: (b * (rcap // step_rows) + g, 0)),
            pl.BlockSpec((step_rows, 1), lambda b, g, *_: (b * (rcap // step_rows) + g, 0)),
        ],
    )
    return pl.pallas_call(
        functools.partial(_dispatch_kernel, n_exp=n_exp, sub=sub, chunks_per_step=cps,
                          chunks_per_block=cpb),
        grid_spec=grid_spec,
        out_shape=[jax.ShapeDtypeStruct((n_blocks * rcap, d), BF16),
                   jax.ShapeDtypeStruct((n_blocks * rcap, 1), F32)],
        compiler_params=_cparams(("arbitrary", "arbitrary")),
        name="moe_dispatch",
    )(off, lo, hi, h, rrow)


def _expert_ffn_kernel(perm_ref, pexp_ref, pfill_ref, x_ref, gs_ref, wg_ref, wu_ref, wd_ref,
                       y_ref, *, tf):
    w = pl.program_id(0)
    fill = pfill_ref[w]
    ff = wg_ref.shape[2]

    def ffn(rows):
        x = x_ref[rows, :]
        gs = gs_ref[rows, :]
        acc = jnp.zeros((x.shape[0], y_ref.shape[1]), F32)
        for f0 in range(0, ff, tf):
            gate = _dot(x, wg_ref[0, :, f0:f0 + tf])
            up = _dot(x, wu_ref[0, :, f0:f0 + tf])
            a = ((_silu(gate) * up) * gs).astype(BF16)
            acc = acc + _dot(a, wd_ref[0, f0:f0 + tf, :])
        return acc.astype(BF16)

    half = PIECE // 2

    @pl.when(fill == 2)
    def _():
        y_ref[...] = ffn(slice(0, PIECE))

    @pl.when(fill == 1)
    def _():
        y_ref[0:half, :] = ffn(slice(0, half))
        y_ref[half:PIECE, :] = jnp.zeros((half, y_ref.shape[1]), BF16)

    @pl.when(fill == 0)
    def _():
        y_ref[...] = jnp.zeros(y_ref.shape, BF16)


def _expert_ffn(perm, pexp, pfill, xg, gs, wg, wu, wd, tf):
    rows, d = xg.shape
    n_exp, _, ff = wg.shape
    grid_spec = pltpu.PrefetchScalarGridSpec(
        num_scalar_prefetch=3,
        grid=(rows // PIECE,),
        in_specs=[
            pl.BlockSpec((PIECE, d), lambda w, perm, pexp, pfill: (perm[w], 0)),
            pl.BlockSpec((PIECE, 1), lambda w, perm, pexp, pfill: (perm[w], 0)),
            pl.BlockSpec((1, d, ff), lambda w, perm, pexp, pfill: (pexp[w], 0, 0)),
            pl.BlockSpec((1, d, ff), lambda w, perm, pexp, pfill: (pexp[w], 0, 0)),
            pl.BlockSpec((1, ff, d), lambda w, perm, pexp, pfill: (pexp[w], 0, 0)),
        ],
        out_specs=pl.BlockSpec((PIECE, d), lambda w, perm, pexp, pfill: (perm[w], 0)),
    )
    return pl.pallas_call(
        functools.partial(_expert_ffn_kernel, tf=tf),
        grid_spec=grid_spec,
        out_shape=jax.ShapeDtypeStruct((rows, d), BF16),
        compiler_params=_cparams(("arbitrary",)),
        name="expert_ffn",
    )(perm, pexp, pfill, xg, gs, wg, wu, wd)


def _combine_kernel(off_ref, start_ref, end_ref, yg_ref, rcol_ref, x_ref, mod_ref, gfin_ref, o_ref,
                    acc_ref, *, n_exp, sub, subs_per_block):
    blk = pl.program_id(0)
    t = pl.program_id(1)
    tm, d = x_ref.shape
    col = lax.broadcasted_iota(jnp.int32, (1, COMBINE_WINDOW), 1)
    align = COMBINE_WINDOW // 2

    def token_info(s):
        rc = rcol_ref[s * sub:(s + 1) * sub, :]
        i1, i2 = rc[:, 0:1], rc[:, 1:2]
        pos1 = rc[:, 2:3] + _region_offset(i1, off_ref, blk * n_exp, n_exp)
        pos2 = rc[:, 3:4] + _region_offset(i2, off_ref, blk * n_exp, n_exp)
        return i1, i2, pos1, pos2

    def gathered(info, e, w0):
        i1, i2, pos1, pos2 = info
        rowid = (w0 + col).astype(F32)
        hit = ((pos1 == rowid) & (i1 == e)) | ((pos2 == rowid) & (i2 == e))
        onehot = jnp.where(hit, 1.0, 0.0).astype(BF16)
        return _dot(onehot, yg_ref[pl.ds(pl.multiple_of(w0, align), COMBINE_WINDOW), :])

    def first_window(s, e):
        idx = (blk * subs_per_block + t * (tm // sub) + s) * n_exp + e
        w0 = lax.shift_right_logical(start_ref[idx], align.bit_length() - 1) * align
        return w0, end_ref[idx]

    for s in range(tm // sub):
        info = token_info(s)
        acc = jnp.zeros((sub, d), F32)
        for e in range(n_exp):
            acc += gathered(info, e, first_window(s, e)[0])
        acc_ref[s * sub:(s + 1) * sub, :] = acc

    for s in range(tm // sub):
        info = token_info(s)
        for e in range(n_exp):
            w0, end = first_window(s, e)
            n_more = lax.shift_right_logical(
                jnp.maximum(end - w0 - 1, 0), COMBINE_WINDOW.bit_length() - 1)

            def add(k, _, info=info, e=e, w0=w0, s=s):
                rows = slice(s * sub, (s + 1) * sub)
                acc_ref[rows, :] = acc_ref[rows, :] + gathered(info, e, w0 + k * COMBINE_WINDOW)
                return 0

            lax.fori_loop(1, n_more + 1, add, 0)

    x = x_ref[...] + mod_ref[0][5:6] * acc_ref[...]
    o_ref[...] = _rms_scale(x) * gfin_ref[...]


def _combine(off, lo, hi, yg, rcol, x2, mod, gfin, n_blocks, rows_per_block, rcap, n_exp, sub, tm):
    n, d = x2.shape
    tpb = rows_per_block // tm
    grid_spec = pltpu.PrefetchScalarGridSpec(
        num_scalar_prefetch=3,
        grid=(n_blocks, tpb),
        in_specs=[
            pl.BlockSpec((rcap, d), lambda b, t, *_: (b, 0)),
            pl.BlockSpec((tm, LANES), lambda b, t, *_: (b * tpb + t, 0)),
            pl.BlockSpec((tm, d), lambda b, t, *_: (b * tpb + t, 0)),
            pl.BlockSpec((1, 6, d), lambda b, t, *_: (b, 0, 0)),
            pl.BlockSpec((1, d), lambda b, t, *_: (0, 0)),
        ],
        out_specs=pl.BlockSpec((tm, d), lambda b, t, *_: (b * tpb + t, 0)),
        scratch_shapes=[pltpu.VMEM((tm, d), F32)],
    )
    return pl.pallas_call(
        functools.partial(_combine_kernel, n_exp=n_exp, sub=sub,
                          subs_per_block=rows_per_block // sub),
        grid_spec=grid_spec,
        out_shape=jax.ShapeDtypeStruct((n, d), F32),
        compiler_params=_cparams(("arbitrary", "arbitrary")),
        name="moe_combine",
    )(off, lo, hi, yg, rcol, x2, mod, gfin)


def _routing_tables(cnt, n_blocks, n_sub, n_exp, rcap):
    cum_incl = cnt.astype(jnp.int32)
    cum_excl = jnp.concatenate([jnp.zeros_like(cum_incl[:, :1]), cum_incl[:, :-1]], axis=1)
    total = cum_incl[:, -1]
    psz = (total + PIECE - 1) // PIECE * PIECE
    off = jnp.cumsum(psz, axis=1) - psz
    start = off[:, None, :] + cum_excl
    end = off[:, None, :] + cum_incl

    c0 = jnp.arange(rcap // DISPATCH_CHUNK, dtype=jnp.int32)[None, :, None, None] * DISPATCH_CHUNK
    hit = ((start[:, None] < c0 + DISPATCH_CHUNK) & (end[:, None] > c0)
           & (end[:, None] > start[:, None])).any(-1)
    sub_ids = jnp.arange(n_sub, dtype=jnp.int32)
    d_lo = jnp.min(jnp.where(hit, sub_ids, n_sub), axis=-1)
    d_hi = jnp.max(jnp.where(hit, sub_ids, -1), axis=-1)
    d_lo = jnp.where(d_hi < 0, 0, d_lo)


    ppb = rcap // PIECE
    prow = jnp.arange(ppb, dtype=jnp.int32)[None, :, None] * PIECE
    inside = (prow >= off[:, None, :]) & (prow < (off + psz)[:, None, :])
    p_exp = jnp.argmax(inside, axis=-1).astype(jnp.int32)
    used = inside.any(-1)
    left = jnp.take_along_axis(total + off, p_exp, axis=1) - prow[..., 0]
    p_fill = jnp.where(used & (left > 0), jnp.where(left > PIECE // 2, 2, 1), 0).astype(jnp.int32)
    key = jnp.where(p_fill > 0, p_exp, n_exp).reshape(-1)
    perm = jnp.argsort(key, stable=True).astype(jnp.int32)
    p_exp_sorted = jnp.minimum(key[perm], n_exp - 1).astype(jnp.int32)
    return (off.reshape(-1).astype(jnp.int32), d_lo.reshape(-1), d_hi.reshape(-1),
            start.reshape(-1).astype(jnp.int32), end.reshape(-1).astype(jnp.int32),
            perm, p_exp_sorted, p_fill.reshape(-1)[perm])


def _pick(n, pref):
    t = min(n, pref)
    while n % t:
        t //= 2
    return t


def kernel(x, c, positions, ada_w, ada_b, mix_norm_g, ffn_norm_g, pool_w_in, pool_w_grp,
           pool_scale, pool_w_out, kv_ada_w, kv_ada_b, kv_norm_g, w_dkv, ckv_norm_g, w_kr,
           w_uk, w_uv, w_dq, q_norm_g, w_uq, w_o, ffn_w_gate, ffn_w_up, ffn_w_down,
           router_w, moe_w_gate, moe_w_up, moe_w_down, final_norm_g):
    b, s, d = x.shape
    n = b * s
    depth = ada_w.shape[0]
    assert depth == 2 and pool_w_in.shape[0] == 1 and w_dq.shape[0] == 1
    assert b <= 8 and d % LANES == 0

    ts = _pick(s, 512)
    tm = _pick(s, 1024)
    tf = _pick(ffn_w_gate.shape[-1], 512)

    c_pad = jnp.zeros((8, d), F32).at[:b].set(c)
    mods = _adaln(c_pad, ada_w, ada_b[:, None, :], _pick(6 * d, 1536))
    mods = mods[:, :b].reshape(depth, b, 6, d)
    kvmod = _adaln(c_pad, kv_ada_w[None], kv_ada_b[None, None, :], _pick(2 * d, 1024))
    kvmod = kvmod[0, :b].reshape(b, 2, d)

    r = jnp.arange(ts)[:, None]
    cidx = jnp.arange(ts + POOL_HALO)[None, :] - POOL_HALO
    band = jnp.stack([((cidx <= r) & (cidx > r - w)) for w in POOL_WINDOWS]).astype(BF16)
    x1 = _pool_mixer(x, mods[0], mix_norm_g[0:1], pool_w_in[0].astype(BF16), band,
                     pool_w_grp[0].astype(BF16), pool_scale[0:1], pool_w_out[0].astype(BF16), ts)

    x2 = _dense_ffn(x1.reshape(n, d), mods[0], ffn_norm_g[0:1], ffn_w_gate[0], ffn_w_up[0],
                    ffn_w_down[0], tm, tf, s)

    inv_freq = ROPE_THETA ** (-jnp.arange(0, D_ROPE, 2, dtype=F32) / D_ROPE)
    freq_row = jnp.concatenate([inv_freq, inv_freq, jnp.zeros((LANES - D_ROPE,), F32)])[None, :]
    rope_tab, rope_tab_t = _rope_tables(positions.astype(F32)[..., None], freq_row, ts)

    q_lora = w_dq.shape[-1]
    wuq = w_uq[0].reshape(q_lora, N_HEADS, D_NOPE + D_ROPE)
    wqn_t = wuq[:, :, :D_NOPE].reshape(q_lora, N_HEADS * D_NOPE).T.astype(BF16)
    wqr = jnp.pad(wuq[:, :, D_NOPE:], ((0, 0), (0, 0), (0, LANES - D_ROPE)))
    wqr_t = wqr.reshape(q_lora, N_HEADS * LANES).T.astype(BF16)
    wkr = jnp.pad(w_kr, ((0, 0), (0, LANES - D_ROPE))).astype(BF16)
    qt, k, vt = _qkv_proj(
        x2.reshape(b, s, d), mods[1], kvmod, mix_norm_g[1:2], kv_norm_g[None, :],
        w_dq[0].astype(BF16), q_norm_g[0:1], wqn_t, wqr_t, w_dkv.astype(BF16),
        ckv_norm_g[None, :], wkr, w_uk.astype(BF16), w_uv.T.astype(BF16), rope_tab, rope_tab_t, ts)
    o = _attention(qt, k, vt, ts)

    n_exp = router_w.shape[-1]
    sub = _pick(s, ROUTE_SUB)
    n_sub = s // sub
    rcap = 2 * s + n_exp * PIECE
    ltri = (jnp.arange(ts)[None, :] < jnp.arange(ts)[:, None]).astype(BF16)
    x3, h3, rcol, rrow, cnt = _attn_out(o.reshape(n, N_HEADS * D_V), x2, mods[1],
                                        w_o[0].astype(BF16), ffn_norm_g[1:2], router_w[0].T,
                                        ltri, ts, sub, s)
    cnt = cnt[:, :ts // sub, :n_exp].reshape(b, n_sub, n_exp)
    off, d_lo, d_hi, c_lo, c_hi, perm, p_exp, p_fill = _routing_tables(cnt, b, n_sub, n_exp, rcap)
    xg, gs = _dispatch(off, d_lo, d_hi, h3, rrow, b, s, rcap, n_exp, sub)
    yg = _expert_ffn(perm, p_exp, p_fill, xg, gs, moe_w_gate[0].astype(BF16),
                     moe_w_up[0].astype(BF16), moe_w_down[0].astype(BF16), tf)
    out = _combine(off, c_lo, c_hi, yg, rcol, x3, mods[1], final_norm_g[None, :], b, s, rcap,
                   n_exp, sub, ts)
    return out.reshape(b, s, d)
```

```python
import functools
import math

import jax
import jax.numpy as jnp
from jax import lax
from jax.experimental import pallas as pl
from jax.experimental.pallas import tpu as pltpu

F32 = jnp.float32
BF16 = jnp.bfloat16

N_HEADS = 8
D_NOPE = 128
D_ROPE = 64
D_V = 128
HEAD_PAD = 256
ROPE_HALF = D_ROPE // 2
ROPE_THETA = 10000.0
ATTN_SCALE = 1.0 / math.sqrt(D_NOPE + D_ROPE)
Q_SCALE = ATTN_SCALE * math.log2(math.e)
ATTN_QUERY_TILE = 512
ATTN_KEY_BLOCK = 512
POOL_WINDOWS = (2, 4, 8, 16)
POOL_HALO = 16
EPS = 1e-6

ROUTE_SUB = 256
DISPATCH_CHUNK = 128
DISPATCH_WINDOW = 4
COMBINE_WINDOW = 256
PIECE = 256
ROUTE_FIELDS = 8

LANES = 128
VMEM_LIMIT_BYTES = 56 * 1024 * 1024


def _cparams(sem):
    return pltpu.CompilerParams(dimension_semantics=sem, vmem_limit_bytes=VMEM_LIMIT_BYTES)


def _dot(a, b):
    return jnp.dot(a, b, preferred_element_type=F32)


def _rms_scale(x):
    return x * lax.rsqrt(jnp.mean(x * x, axis=-1, keepdims=True) + EPS)


def _silu(x):
    return x * jax.nn.sigmoid(x)


def _adaln_kernel(c_ref, w_ref, b_ref, o_ref):
    sc = _silu(c_ref[...])
    o_ref[0] = jnp.dot(sc, w_ref[0], preferred_element_type=F32,
                       precision=lax.Precision.HIGHEST) + b_ref[0]


def _adaln(c_pad, w, b, tn):
    nl, d, m = w.shape
    return pl.pallas_call(
        _adaln_kernel,
        grid=(nl, m // tn),
        in_specs=[
            pl.BlockSpec((8, d), lambda l, j: (0, 0)),
            pl.BlockSpec((1, d, tn), lambda l, j: (l, 0, j)),
            pl.BlockSpec((1, 1, tn), lambda l, j: (l, 0, j)),
        ],
        out_specs=pl.BlockSpec((1, 8, tn), lambda l, j: (l, 0, j)),
        out_shape=jax.ShapeDtypeStruct((nl, 8, m), F32),
        compiler_params=_cparams(("arbitrary", "arbitrary")),
        name="adaln",
    )(c_pad, w, b)


def _rope_table_kernel(pos_ref, freq_ref, tab_ref, tab_t_ref):
    ang = pos_ref[0] * freq_ref[...]
    lane = lax.broadcasted_iota(jnp.int32, ang.shape, 1)
    cos = jnp.cos(ang)
    sin = jnp.sin(ang)
    zero = jnp.zeros_like(ang)
    tabs = (jnp.where(lane < D_ROPE, cos, zero),
            jnp.where(lane < ROPE_HALF, -sin, zero),
            jnp.where((lane >= ROPE_HALF) & (lane < D_ROPE), sin, zero))
    for i, tab in enumerate(tabs):
        tab_ref[0, :, i * LANES:(i + 1) * LANES] = tab
        tab_t_ref[0, 0, i * LANES:(i + 1) * LANES, :] = tab.T


def _rope_tables(pos_f, freq_row, ts):
    b, s, _ = pos_f.shape
    return pl.pallas_call(
        _rope_table_kernel,
        grid=(b, s // ts),
        in_specs=[
            pl.BlockSpec((1, ts, 1), lambda i, j: (i, j, 0)),
            pl.BlockSpec((1, LANES), lambda i, j: (0, 0)),
        ],
        out_specs=[pl.BlockSpec((1, ts, 3 * LANES), lambda i, j: (i, j, 0)),
                   pl.BlockSpec((1, 1, 3 * LANES, ts), lambda i, j: (i, j, 0, 0))],
        out_shape=[jax.ShapeDtypeStruct((b, s, 3 * LANES), F32),
                   jax.ShapeDtypeStruct((b, s // ts, 3 * LANES, ts), F32)],
        compiler_params=_cparams(("arbitrary", "arbitrary")),
        name="rope_tables",
    )(pos_f, freq_row)


def _apply_rope(a, tab):
    c, s1, s2 = (tab[:, i * LANES:(i + 1) * LANES] for i in range(3))
    return a * c + pltpu.roll(a, LANES - ROPE_HALF, 1) * s1 + pltpu.roll(a, ROPE_HALF, 1) * s2


def _apply_rope_t(a, tab_t):
    c, s1, s2 = (tab_t[i * LANES:(i + 1) * LANES] for i in range(3))
    up = jnp.concatenate([a[ROPE_HALF:], a[:ROPE_HALF]], axis=0)
    down = jnp.concatenate([a[-ROPE_HALF:], a[:-ROPE_HALF]], axis=0)
    return a * c + up * s1 + down * s2


def _pool_kernel(x_ref, mod_ref, g_ref, win_ref, band_ref, wgrp_ref, scale_ref, wout_ref,
                 o_ref, uext_ref, *, ts, gw):
    s = pl.program_id(1)

    @pl.when(s == 0)
    def _():
        uext_ref[0:POOL_HALO, :] = jnp.zeros((POOL_HALO, uext_ref.shape[1]), BF16)

    x = x_ref[0]
    mod = mod_ref[0]
    h = (_rms_scale(x) * g_ref[...]) * (1.0 + mod[1:2]) + mod[0:1]
    u = _dot(h.astype(BF16), win_ref[...])
    uext_ref[POOL_HALO:POOL_HALO + ts, :] = u.astype(BF16)

    t = s * ts + lax.broadcasted_iota(jnp.int32, (ts, 1), 0)
    zs = []
    for g, w in enumerate(POOL_WINDOWS):
        cols = slice(g * gw, (g + 1) * gw)
        wsum = _dot(band_ref[g], uext_ref[:, cols])
        cnt = jnp.minimum(t + 1, w).astype(F32)
        p = wsum / cnt - u[:, cols]
        zs.append(_dot(p.astype(BF16), wgrp_ref[g]))
    z = jnp.concatenate(zs, axis=-1) * scale_ref[...]
    y = _dot(z.astype(BF16), wout_ref[...])
    o_ref[0] = x + mod[2:3] * y

    uext_ref[0:POOL_HALO, :] = uext_ref[ts:ts + POOL_HALO, :]


def _pool_mixer(x, mod, g, w_in, band, w_grp, scale, w_out, ts):
    b, s, d = x.shape
    ng, gw, _ = w_grp.shape
    const2 = lambda i, j: (0, 0)
    const3 = lambda i, j: (0, 0, 0)
    return pl.pallas_call(
        functools.partial(_pool_kernel, ts=ts, gw=gw),
        grid=(b, s // ts),
        in_specs=[
            pl.BlockSpec((1, ts, d), lambda i, j: (i, j, 0)),
            pl.BlockSpec((1, 6, d), lambda i, j: (i, 0, 0)),
            pl.BlockSpec((1, d), const2),
            pl.BlockSpec((d, d), const2),
            pl.BlockSpec((ng, ts, ts + POOL_HALO), const3),
            pl.BlockSpec((ng, gw, gw), const3),
            pl.BlockSpec((1, d), const2),
            pl.BlockSpec((d, d), const2),
        ],
        out_specs=pl.BlockSpec((1, ts, d), lambda i, j: (i, j, 0)),
        out_shape=jax.ShapeDtypeStruct((b, s, d), F32),
        scratch_shapes=[pltpu.VMEM((POOL_HALO + ts, d), BF16)],
        compiler_params=_cparams(("arbitrary", "arbitrary")),
        name="pool_mixer",
    )(x, mod, g, w_in, band, w_grp, scale, w_out)


def _ffn_kernel(x_ref, mod_ref, g_ref, wg_ref, wu_ref, wd_ref, eg_ref, eu_ref, ed_ref,
                o_ref, eg_out_ref, eu_out_ref, ed_out_ref, h_ref, acc_ref):
    f = pl.program_id(1)

    eg_out_ref[...] = eg_ref[...].astype(BF16)
    eu_out_ref[...] = eu_ref[...].astype(BF16)
    ed_out_ref[...] = ed_ref[...].astype(BF16)

    @pl.when(f == 0)
    def _():
        mod = mod_ref[0]
        h = (_rms_scale(x_ref[...]) * g_ref[...]) * (1.0 + mod[4:5]) + mod[3:4]
        h_ref[...] = h.astype(BF16)
        acc_ref[...] = jnp.zeros_like(acc_ref)

    h = h_ref[...]
    gate = _dot(h, wg_ref[...].astype(BF16))
    up = _dot(h, wu_ref[...].astype(BF16))
    a = (_silu(gate) * up).astype(BF16)
    acc_ref[...] += _dot(a, wd_ref[...].astype(BF16))

    @pl.when(f == pl.num_programs(1) - 1)
    def _():
        o_ref[...] = x_ref[...] + mod_ref[0][5:6] * acc_ref[...]


def _dense_ffn(x2, mod, g, wg, wu, wd, eg, eu, ed, tm, tf, rows_per_batch):
    n, d = x2.shape
    ff = wg.shape[1]
    tpb = rows_per_batch // tm
    n_f = ff // tf
    n_exp, _, eff = eg.shape
    steps = (n // tm) * n_f
    assert steps >= n_exp, "too few grid steps to carry the expert-weight cast"
    k = max(c for c in range(1, steps // n_exp + 1) if eff % (c * LANES) == 0)
    cw = eff // k

    def chunk(i, f):
        c = jnp.minimum(i * n_f + f, n_exp * k - 1)
        return c // k, c % k

    gate_up = pl.BlockSpec((1, d, cw), lambda i, f: (chunk(i, f)[0], 0, chunk(i, f)[1]))
    down = pl.BlockSpec((1, cw, d), lambda i, f: (chunk(i, f)[0], chunk(i, f)[1], 0))
    return pl.pallas_call(
        _ffn_kernel,
        grid=(n // tm, n_f),
        in_specs=[
            pl.BlockSpec((tm, d), lambda i, f: (i, 0)),
            pl.BlockSpec((1, 6, d), lambda i, f: (i // tpb, 0, 0)),
            pl.BlockSpec((1, d), lambda i, f: (0, 0)),
            pl.BlockSpec((d, tf), lambda i, f: (0, f)),
            pl.BlockSpec((d, tf), lambda i, f: (0, f)),
            pl.BlockSpec((tf, d), lambda i, f: (f, 0)),
            gate_up, gate_up, down,
        ],
        out_specs=[pl.BlockSpec((tm, d), lambda i, f: (i, 0)), gate_up, gate_up, down],
        out_shape=[jax.ShapeDtypeStruct((n, d), F32),
                   jax.ShapeDtypeStruct(eg.shape, BF16),
                   jax.ShapeDtypeStruct(eu.shape, BF16),
                   jax.ShapeDtypeStruct(ed.shape, BF16)],
        scratch_shapes=[pltpu.VMEM((tm, d), BF16), pltpu.VMEM((tm, d), F32)],
        compiler_params=_cparams(("arbitrary", "arbitrary")),
        name="dense_ffn",
    )(x2, mod, g, wg, wu, wd, eg, eu, ed)


def _dot_nt(a, b):
    return lax.dot_general(a, b, (((1,), (1,)), ((), ())), preferred_element_type=F32)


def _qkv_kernel(x_ref, mod_ref, kvmod_ref, gmix_ref, gkv_ref, wdq_ref, qg_ref, wqn_t_ref,
                wqr_t_ref, wdkv_ref, ckvg_ref, wkr_ref, wuk_ref, wuv_t_ref, tab_ref, tab_t_ref,
                qt_ref, k_ref, vt_ref):
    x = x_ref[0]
    xn = _rms_scale(x)
    mod = mod_ref[0]
    kvmod = kvmod_ref[0]
    hq = ((xn * gmix_ref[...]) * (1.0 + mod[1:2]) + mod[0:1]).astype(BF16)
    hk = ((xn * gkv_ref[...]) * (1.0 + kvmod[1:2]) + kvmod[0:1]).astype(BF16)

    cq = (_rms_scale(_dot(hq, wdq_ref[...])) * qg_ref[...]).astype(BF16)
    qn_t = _dot_nt(wqn_t_ref[...], cq) * Q_SCALE
    qr_t = _dot_nt(wqr_t_ref[...], cq) * Q_SCALE
    tab_t = tab_t_ref[0, 0]

    ckv = (_rms_scale(_dot(hk, wdkv_ref[...])) * ckvg_ref[...]).astype(BF16)
    kn = _dot(ckv, wuk_ref[...])
    vt_ref[0, 0] = _dot_nt(wuv_t_ref[...], ckv).astype(BF16)
    kr = _apply_rope(_dot(hk, wkr_ref[...]), tab_ref[0]).astype(BF16)

    for h in range(N_HEADS):
        lo = h * HEAD_PAD
        hs = slice(h * LANES, (h + 1) * LANES)
        qt_ref[0, 0, lo:lo + LANES, :] = qn_t[hs].astype(BF16)
        qt_ref[0, 0, lo + LANES:lo + HEAD_PAD, :] = _apply_rope_t(qr_t[hs], tab_t).astype(BF16)
        k_ref[0, :, lo:lo + LANES] = kn[:, hs].astype(BF16)
        k_ref[0, :, lo + LANES:lo + HEAD_PAD] = kr


def _qkv_proj(x, mod, kvmod, gmix, gkv, wdq, qg, wqn_t, wqr_t, wdkv, ckvg, wkr, wuk, wuv_t,
              tab, tab_t, ts):
    b, s, d = x.shape
    row = lambda i, j: (i, j, 0)
    tile = lambda i, j: (i, j, 0, 0)
    per_b = lambda i, j: (i, 0, 0)

    def full(a):
        return pl.BlockSpec(a.shape, lambda i, j: (0,) * a.ndim)

    weights = (gmix, gkv, wdq, qg, wqn_t, wqr_t, wdkv, ckvg, wkr, wuk, wuv_t)
    return pl.pallas_call(
        _qkv_kernel,
        grid=(b, s // ts),
        in_specs=[pl.BlockSpec((1, ts, d), row),
                  pl.BlockSpec((1, 6, d), per_b),
                  pl.BlockSpec((1, 2, d), per_b)]
                 + [full(a) for a in weights]
                 + [pl.BlockSpec((1, ts, 3 * LANES), row),
                    pl.BlockSpec((1, 1, 3 * LANES, ts), tile)],
        out_specs=[pl.BlockSpec((1, 1, N_HEADS * HEAD_PAD, ts), tile),
                   pl.BlockSpec((1, ts, N_HEADS * HEAD_PAD), row),
                   pl.BlockSpec((1, 1, N_HEADS * D_V, ts), tile)],
        out_shape=[jax.ShapeDtypeStruct((b, s // ts, N_HEADS * HEAD_PAD, ts), BF16),
                   jax.ShapeDtypeStruct((b, s, N_HEADS * HEAD_PAD), BF16),
                   jax.ShapeDtypeStruct((b, s // ts, N_HEADS * D_V, ts), BF16)],
        compiler_params=_cparams(("arbitrary", "arbitrary")),
        name="qkv_proj",
    )(x, mod, kvmod, *weights, tab, tab_t)


def _attn_kernel(qt_ref, k_ref, vt_ref, o_ref, *, tq, tk):
    ts = qt_ref.shape[3]
    s = k_ref.shape[1]
    key = lax.broadcasted_iota(jnp.int32, (tk, tq), 0)
    qry = lax.broadcasted_iota(jnp.int32, (tk, tq), 1)
    last_block = lambda qi: ((qi + 1) * tq - 1) // tk
    blocks = [(qi, j) for qi in range(s // tq) for j in range(last_block(qi) + 1)]

    def tile(ref, start, size):
        return ref[0, start // ts, :, start % ts:start % ts + size]

    def scores_t(qi, j):
        st = _dot(k_ref[0, j * tk:(j + 1) * tk, :], tile(qt_ref, qi * tq, tq))
        if (j + 1) * tk - 1 > qi * tq:
            st = jnp.where(key + (j * tk - qi * tq) <= qry, st, -jnp.inf)
        return st

    st_next = scores_t(*blocks[0])
    for n, (qi, j) in enumerate(blocks):
        st = st_next
        if n + 1 < len(blocks):
            st_next = scores_t(*blocks[n + 1])
        if j == 0:
            m = jnp.full((1, tq), -jnp.inf, F32)
            l = jnp.zeros((1, tq), F32)
            acc = jnp.zeros((D_V, tq), F32)
        m_new = jnp.maximum(m, jnp.max(st, axis=0, keepdims=True))
        alpha = jnp.exp2(m - m_new)
        pt = jnp.exp2(st - m_new)
        l = alpha * l + jnp.sum(pt, axis=0, keepdims=True)
        acc = alpha * acc + _dot(tile(vt_ref, j * tk, tk), pt.astype(BF16))
        m = m_new
        if j == last_block(qi):
            o_ref[0, qi * tq:(qi + 1) * tq, :] = (acc / l).T.astype(BF16)


def _attention(qt, k, vt, ts):
    b, s, _ = k.shape
    return pl.pallas_call(
        functools.partial(_attn_kernel, tq=min(ts, ATTN_QUERY_TILE), tk=min(ts, ATTN_KEY_BLOCK)),
        grid=(b, N_HEADS),
        in_specs=[
            pl.BlockSpec((1, s // ts, HEAD_PAD, ts), lambda i, h: (i, 0, h, 0)),
            pl.BlockSpec((1, s, HEAD_PAD), lambda i, h: (i, 0, h)),
            pl.BlockSpec((1, s // ts, D_V, ts), lambda i, h: (i, 0, h, 0)),
        ],
        out_specs=pl.BlockSpec((1, s, D_V), lambda i, h: (i, 0, h)),
        out_shape=jax.ShapeDtypeStruct((b, s, N_HEADS * D_V), BF16),
        compiler_params=_cparams(("arbitrary", "arbitrary")),
        name="flash_attn",
    )(qt, k, vt)


def _attn_out_kernel(o_ref, x_ref, mod_ref, wo_ref, g_ref, wr_ref, ltri_ref,
                     x_out_ref, h_out_ref, rcol_ref, rrow_ref, cnt_ref, carry_ref,
                     *, n_exp, tiles_per_block):
    i = pl.program_id(0)

    @pl.when(i % tiles_per_block == 0)
    def _():
        carry_ref[...] = jnp.zeros_like(carry_ref)

    mod = mod_ref[0]
    x = x_ref[...] + mod[2:3] * _dot(o_ref[...], wo_ref[...])
    x_out_ref[...] = x
    h = (_rms_scale(x) * g_ref[...]) * (1.0 + mod[4:5]) + mod[3:4]
    h_out_ref[...] = h.astype(BF16)

    wr = wr_ref[...]
    logit = [jnp.sum(h * wr[e:e + 1], axis=-1, keepdims=True) for e in range(n_exp)]
    neg = jnp.full_like(logit[0], -jnp.inf)

    def first_argmax(vals):
        m = functools.reduce(jnp.maximum, vals)
        idx = jnp.full(m.shape, n_exp, jnp.int32)
        for e in reversed(range(n_exp)):
            idx = jnp.where(vals[e] == m, e, idx)
        return m, idx

    m1, i1 = first_argmax(logit)
    m2, i2 = first_argmax([jnp.where(i1 == e, neg, logit[e]) for e in range(n_exp)])
    ex = jnp.exp(m2 - m1)
    w1 = 1.0 / (1.0 + ex)
    w2 = ex / (1.0 + ex)

    tm = x.shape[0]
    lane = lax.broadcasted_iota(jnp.int32, (tm, LANES), 1)
    mask = jnp.where((lane == i1) | (lane == i2), 1.0, 0.0)
    ranks = _dot(ltri_ref[...], mask.astype(BF16)) + carry_ref[0:1]
    incl = ranks + mask
    rank1 = jnp.sum(jnp.where(lane == i1, ranks, 0.0), axis=-1, keepdims=True)
    rank2 = jnp.sum(jnp.where(lane == i2, ranks, 0.0), axis=-1, keepdims=True)
    carry_ref[0:1] = incl[tm - 1:tm]
    n_sub = tm // rrow_ref.shape[2]
    sub = rrow_ref.shape[2]
    cnt_ref[0] = jnp.zeros(cnt_ref.shape[1:], F32)
    for j in range(n_sub):
        cnt_ref[0, j:j + 1] = incl[(j + 1) * sub - 1:(j + 1) * sub]

    fields = (i1.astype(F32), i2.astype(F32), rank1, rank2, w1, w2)
    rcol = jnp.zeros((tm, LANES), F32)
    for k, val in enumerate(fields):
        rcol = jnp.where(lane == k, val, rcol)
    rcol_ref[...] = rcol
    rrow = rcol.T
    for j in range(n_sub):
        rrow_ref[j] = rrow[0:ROUTE_FIELDS, j * sub:(j + 1) * sub]


def _attn_out(o2, x2, mod, wo, g, wr, ltri, tm, sub, rows_per_block):
    n, d = x2.shape
    n_exp = wr.shape[0]
    tpb = rows_per_block // tm
    n_sub = tm // sub
    return pl.pallas_call(
        functools.partial(_attn_out_kernel, n_exp=n_exp, tiles_per_block=tpb),
        grid=(n // tm,),
        in_specs=[
            pl.BlockSpec((tm, o2.shape[1]), lambda i: (i, 0)),
            pl.BlockSpec((tm, d), lambda i: (i, 0)),
            pl.BlockSpec((1, 6, d), lambda i: (i // tpb, 0, 0)),
            pl.BlockSpec(wo.shape, lambda i: (0, 0)),
            pl.BlockSpec((1, d), lambda i: (0, 0)),
            pl.BlockSpec(wr.shape, lambda i: (0, 0)),
            pl.BlockSpec((tm, tm), lambda i: (0, 0)),
        ],
        out_specs=[pl.BlockSpec((tm, d), lambda i: (i, 0)),
                   pl.BlockSpec((tm, d), lambda i: (i, 0)),
                   pl.BlockSpec((tm, LANES), lambda i: (i, 0)),
                   pl.BlockSpec((n_sub, ROUTE_FIELDS, sub), lambda i: (i, 0, 0)),
                   pl.BlockSpec((1, 8, LANES), lambda i: (i, 0, 0))],
        out_shape=[jax.ShapeDtypeStruct((n, d), F32),
                   jax.ShapeDtypeStruct((n, d), BF16),
                   jax.ShapeDtypeStruct((n, LANES), F32),
                   jax.ShapeDtypeStruct((n // sub, ROUTE_FIELDS, sub), F32),
                   jax.ShapeDtypeStruct((n // tm, 8, LANES), F32)],
        scratch_shapes=[pltpu.VMEM((8, LANES), F32)],
        compiler_params=_cparams(("arbitrary",)),
        name="attn_out_router",
    )(o2, x2, mod, wo, g, wr, ltri)


def _dispatch_kernel(off_ref, lo_ref, hi_ref, cexp_ref, h_ref, rrow_ref, xg_ref, gs_ref,
                     *, n_exp, sub, chunks_per_step, chunks_per_block):
    blk = pl.program_id(0)
    g = pl.program_id(1)
    d = h_ref.shape[1]
    n_sub = rrow_ref.shape[0]
    win = min(DISPATCH_WINDOW, n_sub)
    iota = lax.broadcasted_iota(jnp.int32, (DISPATCH_CHUNK, 1), 0)

    def contribution(info, sb):
        row, exp_id, exp_off = info
        r = rrow_ref[sb]
        first = r[0:1] == exp_id
        second = r[1:2] == exp_id
        pos = jnp.where(first, r[2:3] + exp_off, jnp.where(second, r[3:4] + exp_off, -1.0))
        hit = pos == row
        onehot = jnp.where(hit, 1.0, 0.0).astype(BF16)
        tok = h_ref[pl.ds(pl.multiple_of(sb * sub, sub), sub), :]
        gate = jnp.sum(jnp.where(hit, jnp.where(first, r[4:5], r[5:6]), 0.0),
                       axis=-1, keepdims=True)
        return _dot(onehot, tok), gate

    def chunk_info(j):
        c = g * chunks_per_step + j
        t = blk * chunks_per_block + c
        row = (c * DISPATCH_CHUNK + iota).astype(F32)
        exp_id = cexp_ref[t]
        exp_off = off_ref[blk * n_exp + exp_id].astype(F32)
        start = jnp.minimum(lo_ref[t], n_sub - win)
        return (row, exp_id.astype(F32), exp_off), start, hi_ref[t]

    for j in range(chunks_per_step):
        info, start, _ = chunk_info(j)
        acc = jnp.zeros((DISPATCH_CHUNK, d), F32)
        gate = jnp.zeros((DISPATCH_CHUNK, 1), F32)
        for k in range(win):
            a, b = contribution(info, start + k)
            acc += a
            gate += b
        rows = slice(j * DISPATCH_CHUNK, (j + 1) * DISPATCH_CHUNK)
        xg_ref[rows, :] = acc.astype(BF16)
        gs_ref[rows, :] = gate

    def extra(j, _):
        info, start, last = chunk_info(j)
        rows = pl.ds(pl.multiple_of(j * DISPATCH_CHUNK, DISPATCH_CHUNK), DISPATCH_CHUNK)

        def add(sb, _):
            a, b = contribution(info, sb)
            xg_ref[rows, :] = (xg_ref[rows, :].astype(F32) + a).astype(BF16)
            gs_ref[rows, :] = gs_ref[rows, :] + b
            return 0

        return lax.fori_loop(start + win, last + 1, add, 0)

    lax.fori_loop(0, chunks_per_step, extra, 0)


def _dispatch(off, lo, hi, cexp, h, rrow, n_blocks, rows_per_block, rcap, n_exp, sub):
    n, d = h.shape
    n_sub = rows_per_block // sub
    step_rows = _pick(rcap, 1024)
    cps = step_rows // DISPATCH_CHUNK
    cpb = rcap // DISPATCH_CHUNK
    grid_spec = pltpu.PrefetchScalarGridSpec(
        num_scalar_prefetch=4,
        grid=(n_blocks, rcap // step_rows),
        in_specs=[
            pl.BlockSpec((rows_per_block, d), lambda b, g, *_: (b, 0)),
            pl.BlockSpec((n_sub, ROUTE_FIELDS, sub), lambda b, g, *_: (b, 0, 0)),
        ],
        out_specs=[
            pl.BlockSpec((step_rows, d), lambda b, g, *_: (b * (rcap // step_rows) + g, 0)),
            pl.BlockSpec((step_rows, 1), lambda b, g, *_: (b * (rcap // step_rows) + g, 0)),
        ],
    )
    return pl.pallas_call(
        functools.partial(_dispatch_kernel, n_exp=n_exp, sub=sub, chunks_per_step=cps,
                          chunks_per_block=cpb),
        grid_spec=grid_spec,
        out_shape=[jax.ShapeDtypeStruct((n_blocks * rcap, d), BF16),
                   jax.ShapeDtypeStruct((n_blocks * rcap, 1), F32)],
        compiler_params=_cparams(("arbitrary", "arbitrary")),
        name="moe_dispatch",
    )(off, lo, hi, cexp, h, rrow)


def _expert_ffn_kernel(perm_ref, pexp_ref, pfill_ref, x_ref, gs_ref, wg_ref, wu_ref, wd_ref,
                       y_ref, *, tf):
    w = pl.program_id(0)
    fill = pfill_ref[w]
    ff = wg_ref.shape[2]

    def ffn(rows):
        x = x_ref[rows, :]
        gs = gs_ref[rows, :]
        acc = jnp.zeros((x.shape[0], y_ref.shape[1]), F32)
        for f0 in range(0, ff, tf):
            gate = _dot(x, wg_ref[0, :, f0:f0 + tf])
            up = _dot(x, wu_ref[0, :, f0:f0 + tf])
            a = ((_silu(gate) * up) * gs).astype(BF16)
            acc = acc + _dot(a, wd_ref[0, f0:f0 + tf, :])
        return acc.astype(BF16)

    half = PIECE // 2

    @pl.when(fill == 2)
    def _():
        y_ref[...] = ffn(slice(0, PIECE))

    @pl.when(fill == 1)
    def _():
        y_ref[0:half, :] = ffn(slice(0, half))
        y_ref[half:PIECE, :] = jnp.zeros((half, y_ref.shape[1]), BF16)

    @pl.when(fill == 0)
    def _():
        y_ref[...] = jnp.zeros(y_ref.shape, BF16)


def _expert_ffn(perm, pexp, pfill, xg, gs, wg, wu, wd, tf):
    rows, d = xg.shape
    n_exp, _, ff = wg.shape
    grid_spec = pltpu.PrefetchScalarGridSpec(
        num_scalar_prefetch=3,
        grid=(rows // PIECE,),
        in_specs=[
            pl.BlockSpec((PIECE, d), lambda w, perm, pexp, pfill: (perm[w], 0)),
            pl.BlockSpec((PIECE, 1), lambda w, perm, pexp, pfill: (perm[w], 0)),
            pl.BlockSpec((1, d, ff), lambda w, perm, pexp, pfill: (pexp[w], 0, 0)),
            pl.BlockSpec((1, d, ff), lambda w, perm, pexp, pfill: (pexp[w], 0, 0)),
            pl.BlockSpec((1, ff, d), lambda w, perm, pexp, pfill: (pexp[w], 0, 0)),
        ],
        out_specs=pl.BlockSpec((PIECE, d), lambda w, perm, pexp, pfill: (perm[w], 0)),
    )
    return pl.pallas_call(
        functools.partial(_expert_ffn_kernel, tf=tf),
        grid_spec=grid_spec,
        out_shape=jax.ShapeDtypeStruct((rows, d), BF16),
        compiler_params=_cparams(("arbitrary",)),
        name="expert_ffn",
    )(perm, pexp, pfill, xg, gs, wg, wu, wd)


def _combine_kernel(off_ref, start_ref, end_ref, yg_ref, rcol_ref, x_ref, mod_ref, gfin_ref, o_ref,
                    acc_ref, *, n_exp, sub, subs_per_block):
    blk = pl.program_id(0)
    t = pl.program_id(1)
    tm, d = x_ref.shape
    col = lax.broadcasted_iota(jnp.int32, (1, COMBINE_WINDOW), 1)
    align = COMBINE_WINDOW // 2

    def token_info(s):
        rc = rcol_ref[s * sub:(s + 1) * sub, :]
        return rc[:, 0:1], rc[:, 1:2], rc[:, 2:3], rc[:, 3:4]

    def gathered(info, e, w0):
        i1, i2, rank1, rank2 = info
        exp_off = off_ref[blk * n_exp + e].astype(F32)
        pos = jnp.where(i1 == e, rank1 + exp_off, jnp.where(i2 == e, rank2 + exp_off, -1.0))
        onehot = jnp.where(pos == (w0 + col).astype(F32), 1.0, 0.0).astype(BF16)
        return _dot(onehot, yg_ref[pl.ds(pl.multiple_of(w0, align), COMBINE_WINDOW), :])

    def first_window(s, e):
        idx = (blk * subs_per_block + t * (tm // sub) + s) * n_exp + e
        w0 = lax.shift_right_logical(start_ref[idx], align.bit_length() - 1) * align
        return w0, end_ref[idx]

    for s in range(tm // sub):
        info = token_info(s)
        acc = jnp.zeros((sub, d), F32)
        for e in range(n_exp):
            acc += gathered(info, e, first_window(s, e)[0])
        acc_ref[s * sub:(s + 1) * sub, :] = acc

    for s in range(tm // sub):
        info = token_info(s)
        for e in range(n_exp):
            w0, end = first_window(s, e)
            n_more = lax.shift_right_logical(
                jnp.maximum(end - w0 - 1, 0), COMBINE_WINDOW.bit_length() - 1)

            def add(k, _, info=info, e=e, w0=w0, s=s):
                rows = slice(s * sub, (s + 1) * sub)
                acc_ref[rows, :] = acc_ref[rows, :] + gathered(info, e, w0 + k * COMBINE_WINDOW)
                return 0

            lax.fori_loop(1, n_more + 1, add, 0)

    x = x_ref[...] + mod_ref[0][5:6] * acc_ref[...]
    o_ref[...] = _rms_scale(x) * gfin_ref[...]


def _combine(off, lo, hi, yg, rcol, x2, mod, gfin, n_blocks, rows_per_block, rcap, n_exp, sub, tm):
    n, d = x2.shape
    tpb = rows_per_block // tm
    grid_spec = pltpu.PrefetchScalarGridSpec(
        num_scalar_prefetch=3,
        grid=(n_blocks, tpb),
        in_specs=[
            pl.BlockSpec((rcap, d), lambda b, t, *_: (b, 0)),
            pl.BlockSpec((tm, LANES), lambda b, t, *_: (b * tpb + t, 0)),
            pl.BlockSpec((tm, d), lambda b, t, *_: (b * tpb + t, 0)),
            pl.BlockSpec((1, 6, d), lambda b, t, *_: (b, 0, 0)),
            pl.BlockSpec((1, d), lambda b, t, *_: (0, 0)),
        ],
        out_specs=pl.BlockSpec((tm, d), lambda b, t, *_: (b * tpb + t, 0)),
        scratch_shapes=[pltpu.VMEM((tm, d), F32)],
    )
    return pl.pallas_call(
        functools.partial(_combine_kernel, n_exp=n_exp, sub=sub,
                          subs_per_block=rows_per_block // sub),
        grid_spec=grid_spec,
        out_shape=jax.ShapeDtypeStruct((n, d), F32),
        compiler_params=_cparams(("arbitrary", "arbitrary")),
        name="moe_combine",
    )(off, lo, hi, yg, rcol, x2, mod, gfin)


def _routing_tables(cnt, n_blocks, n_sub, n_exp, rcap):
    cum_incl = cnt.astype(jnp.int32)
    cum_excl = jnp.concatenate([jnp.zeros_like(cum_incl[:, :1]), cum_incl[:, :-1]], axis=1)
    total = cum_incl[:, -1]
    psz = (total + PIECE - 1) // PIECE * PIECE
    off = jnp.cumsum(psz, axis=1) - psz
    start = off[:, None, :] + cum_excl
    end = off[:, None, :] + cum_incl

    c0 = jnp.arange(rcap // DISPATCH_CHUNK, dtype=jnp.int32)[None, :, None, None] * DISPATCH_CHUNK
    hit = ((start[:, None] < c0 + DISPATCH_CHUNK) & (end[:, None] > c0)
           & (end[:, None] > start[:, None])).any(-1)
    sub_ids = jnp.arange(n_sub, dtype=jnp.int32)
    d_lo = jnp.min(jnp.where(hit, sub_ids, n_sub), axis=-1)
    d_hi = jnp.max(jnp.where(hit, sub_ids, -1), axis=-1)
    d_lo = jnp.where(d_hi < 0, 0, d_lo)


    ppb = rcap // PIECE
    prow = jnp.arange(ppb, dtype=jnp.int32)[None, :, None] * PIECE
    inside = (prow >= off[:, None, :]) & (prow < (off + psz)[:, None, :])
    p_exp = jnp.argmax(inside, axis=-1).astype(jnp.int32)
    used = inside.any(-1)
    left = jnp.take_along_axis(total + off, p_exp, axis=1) - prow[..., 0]
    p_fill = jnp.where(used & (left > 0), jnp.where(left > PIECE // 2, 2, 1), 0).astype(jnp.int32)
    key = jnp.where(p_fill > 0, p_exp, n_exp).reshape(-1)
    perm = jnp.argsort(key, stable=True).astype(jnp.int32)
    p_exp_sorted = jnp.minimum(key[perm], n_exp - 1).astype(jnp.int32)
    d_exp = jnp.repeat(p_exp, PIECE // DISPATCH_CHUNK, axis=1).reshape(-1)
    return (off.reshape(-1).astype(jnp.int32), d_lo.reshape(-1), d_hi.reshape(-1), d_exp,
            start.reshape(-1).astype(jnp.int32), end.reshape(-1).astype(jnp.int32),
            perm, p_exp_sorted, p_fill.reshape(-1)[perm])


def _pick(n, pref):
    t = min(n, pref)
    while n % t:
        t //= 2
    return t


def kernel(x, c, positions, ada_w, ada_b, mix_norm_g, ffn_norm_g, pool_w_in, pool_w_grp,
           pool_scale, pool_w_out, kv_ada_w, kv_ada_b, kv_norm_g, w_dkv, ckv_norm_g, w_kr,
           w_uk, w_uv, w_dq, q_norm_g, w_uq, w_o, ffn_w_gate, ffn_w_up, ffn_w_down,
           router_w, moe_w_gate, moe_w_up, moe_w_down, final_norm_g):
    b, s, d = x.shape
    n = b * s
    depth = ada_w.shape[0]
    assert depth == 2 and pool_w_in.shape[0] == 1 and w_dq.shape[0] == 1
    assert b <= 8 and d % LANES == 0

    ts = _pick(s, 512)
    tm = _pick(s, 1024)
    tf = _pick(ffn_w_gate.shape[-1], 512)

    c_pad = jnp.zeros((8, d), F32).at[:b].set(c)
    mods = _adaln(c_pad, ada_w, ada_b[:, None, :], _pick(6 * d, 1536))
    mods = mods[:, :b].reshape(depth, b, 6, d)
    kvmod = _adaln(c_pad, kv_ada_w[None], kv_ada_b[None, None, :], _pick(2 * d, 1024))
    kvmod = kvmod[0, :b].reshape(b, 2, d)

    r = jnp.arange(ts)[:, None]
    cidx = jnp.arange(ts + POOL_HALO)[None, :] - POOL_HALO
    band = jnp.stack([((cidx <= r) & (cidx > r - w)) for w in POOL_WINDOWS]).astype(BF16)
    x1 = _pool_mixer(x, mods[0], mix_norm_g[0:1], pool_w_in[0].astype(BF16), band,
                     pool_w_grp[0].astype(BF16), pool_scale[0:1], pool_w_out[0].astype(BF16), ts)

    x2, ewg, ewu, ewd = _dense_ffn(x1.reshape(n, d), mods[0], ffn_norm_g[0:1], ffn_w_gate[0],
                                   ffn_w_up[0], ffn_w_down[0], moe_w_gate[0], moe_w_up[0],
                                   moe_w_down[0], tm, tf, s)

    inv_freq = ROPE_THETA ** (-jnp.arange(0, D_ROPE, 2, dtype=F32) / D_ROPE)
    freq_row = jnp.concatenate([inv_freq, inv_freq, jnp.zeros((LANES - D_ROPE,), F32)])[None, :]
    rope_tab, rope_tab_t = _rope_tables(positions.astype(F32)[..., None], freq_row, ts)

    q_lora = w_dq.shape[-1]
    wuq = w_uq[0].reshape(q_lora, N_HEADS, D_NOPE + D_ROPE)
    wqn_t = wuq[:, :, :D_NOPE].reshape(q_lora, N_HEADS * D_NOPE).T.astype(BF16)
    wqr = jnp.pad(wuq[:, :, D_NOPE:], ((0, 0), (0, 0), (0, LANES - D_ROPE)))
    wqr_t = wqr.reshape(q_lora, N_HEADS * LANES).T.astype(BF16)
    wkr = jnp.pad(w_kr, ((0, 0), (0, LANES - D_ROPE))).astype(BF16)
    qt, k, vt = _qkv_proj(
        x2.reshape(b, s, d), mods[1], kvmod, mix_norm_g[1:2], kv_norm_g[None, :],
        w_dq[0].astype(BF16), q_norm_g[0:1], wqn_t, wqr_t, w_dkv.astype(BF16),
        ckv_norm_g[None, :], wkr, w_uk.astype(BF16), w_uv.T.astype(BF16), rope_tab, rope_tab_t, ts)
    o = _attention(qt, k, vt, ts)

    n_exp = router_w.shape[-1]
    sub = _pick(s, ROUTE_SUB)
    n_sub = s // sub
    rcap = 2 * s + n_exp * PIECE
    ltri = (jnp.arange(ts)[None, :] < jnp.arange(ts)[:, None]).astype(BF16)
    x3, h3, rcol, rrow, cnt = _attn_out(o.reshape(n, N_HEADS * D_V), x2, mods[1],
                                        w_o[0].astype(BF16), ffn_norm_g[1:2], router_w[0].T,
                                        ltri, ts, sub, s)
    cnt = cnt[:, :ts // sub, :n_exp].reshape(b, n_sub, n_exp)
    off, d_lo, d_hi, d_exp, c_lo, c_hi, perm, p_exp, p_fill = _routing_tables(cnt, b, n_sub, n_exp, rcap)
    xg, gs = _dispatch(off, d_lo, d_hi, d_exp, h3, rrow, b, s, rcap, n_exp, sub)
    yg = _expert_ffn(perm, p_exp, p_fill, xg, gs, ewg, ewu, ewd, tf)
    out = _combine(off, c_lo, c_hi, yg, rcol, x3, mods[1], final_norm_g[None, :], b, s, rcap,
                   n_exp, sub, ts)
    return out.reshape(b, s, d)
```

```python
import functools
import math

import jax
import jax.numpy as jnp
from jax import lax
from jax.experimental import pallas as pl
from jax.experimental.pallas import tpu as pltpu

F32 = jnp.float32
BF16 = jnp.bfloat16

N_HEADS = 8
D_NOPE = 128
D_ROPE = 64
D_V = 128
HEAD_PAD = 256
ROPE_HALF = D_ROPE // 2
ROPE_THETA = 10000.0
ATTN_SCALE = 1.0 / math.sqrt(D_NOPE + D_ROPE)
Q_SCALE = ATTN_SCALE * math.log2(math.e)
ATTN_QUERY_TILE = 512
ATTN_KEY_BLOCK = 512
POOL_WINDOWS = (2, 4, 8, 16)
POOL_HALO = 16
POOL_ROW_BLOCK = 128
EPS = 1e-6

ROUTE_SUB = 256
DISPATCH_CHUNK = 128
DISPATCH_WINDOW = 4
COMBINE_WINDOW = 256
PIECE = 256
ROUTE_FIELDS = 8

LANES = 128
VMEM_LIMIT_BYTES = 56 * 1024 * 1024


def _cparams(sem):
    return pltpu.CompilerParams(dimension_semantics=sem, vmem_limit_bytes=VMEM_LIMIT_BYTES)


def _dot(a, b):
    return jnp.dot(a, b, preferred_element_type=F32)


def _rms_scale(x):
    return x * lax.rsqrt(jnp.mean(x * x, axis=-1, keepdims=True) + EPS)


def _silu(x):
    return x * jax.nn.sigmoid(x)


def _adaln_kernel(c_ref, w_ref, b_ref, o_ref):
    sc = _silu(c_ref[...])
    o_ref[0] = jnp.dot(sc, w_ref[0], preferred_element_type=F32,
                       precision=lax.Precision.HIGHEST) + b_ref[0]


def _adaln(c_pad, w, b, tn):
    nl, d, m = w.shape
    return pl.pallas_call(
        _adaln_kernel,
        grid=(nl, m // tn),
        in_specs=[
            pl.BlockSpec((8, d), lambda l, j: (0, 0)),
            pl.BlockSpec((1, d, tn), lambda l, j: (l, 0, j)),
            pl.BlockSpec((1, 1, tn), lambda l, j: (l, 0, j)),
        ],
        out_specs=pl.BlockSpec((1, 8, tn), lambda l, j: (l, 0, j)),
        out_shape=jax.ShapeDtypeStruct((nl, 8, m), F32),
        compiler_params=_cparams(("arbitrary", "arbitrary")),
        name="adaln",
    )(c_pad, w, b)


def _rope_table_side_job(pos_ref, freq_ref, tab_ref, tab_t_ref):
    ang = pos_ref[0] * freq_ref[...]
    lane = lax.broadcasted_iota(jnp.int32, ang.shape, 1)
    cos = jnp.cos(ang)
    sin = jnp.sin(ang)
    zero = jnp.zeros_like(ang)
    tabs = (jnp.where(lane < D_ROPE, cos, zero),
            jnp.where(lane < ROPE_HALF, -sin, zero),
            jnp.where((lane >= ROPE_HALF) & (lane < D_ROPE), sin, zero))
    for i, tab in enumerate(tabs):
        tab_ref[0, :, i * LANES:(i + 1) * LANES] = tab
        tab_t_ref[0, 0, i * LANES:(i + 1) * LANES, :] = tab.T


def _apply_rope(a, tab):
    c, s1, s2 = (tab[:, i * LANES:(i + 1) * LANES] for i in range(3))
    return a * c + pltpu.roll(a, LANES - ROPE_HALF, 1) * s1 + pltpu.roll(a, ROPE_HALF, 1) * s2


def _apply_rope_t(a, tab_t):
    c, s1, s2 = (tab_t[i * LANES:(i + 1) * LANES] for i in range(3))
    up = jnp.concatenate([a[ROPE_HALF:], a[:ROPE_HALF]], axis=0)
    down = jnp.concatenate([a[-ROPE_HALF:], a[:-ROPE_HALF]], axis=0)
    return a * c + up * s1 + down * s2


def _cast_specs(n_exp, d, ff, steps, step_of):
    assert steps >= n_exp, "too few grid steps to carry the weight cast"
    k = max(c for c in range(1, steps // n_exp + 1) if ff % (c * LANES) == 0)
    cw = ff // k

    def chunk(*idx):
        c = jnp.minimum(step_of(*idx), n_exp * k - 1)
        return c // k, c % k

    gate_up = pl.BlockSpec((1, d, cw), lambda *idx: (chunk(*idx)[0], 0, chunk(*idx)[1]))
    down = pl.BlockSpec((1, cw, d), lambda *idx: (chunk(*idx)[0], chunk(*idx)[1], 0))
    return gate_up, down


def _cast_side_job(src_refs, dst_refs):
    for src, dst in zip(src_refs, dst_refs):
        dst[...] = src[...].astype(BF16)


def _pool_kernel(x_ref, mod_ref, g_ref, win_ref, band_ref, wgrp_ref, scale_ref, wout_ref,
                 fg_ref, fu_ref, fd_ref, pos_ref, freq_ref,
                 o_ref, fg_out_ref, fu_out_ref, fd_out_ref, tab_ref, tab_t_ref, uext_ref,
                 *, ts, gw):
    s = pl.program_id(1)

    @pl.when(s == 0)
    def _():
        uext_ref[0:POOL_HALO, :] = jnp.zeros((POOL_HALO, uext_ref.shape[1]), BF16)

    x = x_ref[0]
    mod = mod_ref[0]
    h = (_rms_scale(x) * g_ref[...]) * (1.0 + mod[1:2]) + mod[0:1]
    u = _dot(h.astype(BF16), win_ref[...])
    uext_ref[POOL_HALO:POOL_HALO + ts, :] = u.astype(BF16)

    t = s * ts + lax.broadcasted_iota(jnp.int32, (ts, 1), 0)
    zs = []
    for g, w in enumerate(POOL_WINDOWS):
        cols = slice(g * gw, (g + 1) * gw)
        rb = band_ref.shape[1]
        wsum = jnp.concatenate(
            [_dot(band_ref[g], uext_ref[r0:r0 + rb + POOL_HALO, cols]) for r0 in range(0, ts, rb)],
            axis=0)
        cnt = jnp.minimum(t + 1, w).astype(F32)
        p = wsum / cnt - u[:, cols]
        zs.append(_dot(p.astype(BF16), wgrp_ref[g]))
    z = jnp.concatenate(zs, axis=-1) * scale_ref[...]
    y = _dot(z.astype(BF16), wout_ref[...])
    o_ref[0] = x + mod[2:3] * y

    uext_ref[0:POOL_HALO, :] = uext_ref[ts:ts + POOL_HALO, :]

    _cast_side_job((fg_ref, fu_ref, fd_ref), (fg_out_ref, fu_out_ref, fd_out_ref))
    _rope_table_side_job(pos_ref, freq_ref, tab_ref, tab_t_ref)


def _pool_mixer(x, mod, g, w_in, band, w_grp, scale, w_out, fg, fu, fd, pos_f, freq_row, ts):
    b, s, d = x.shape
    ng, gw, _ = w_grp.shape
    n_s = s // ts
    gate_up, down = _cast_specs(1, d, fg.shape[2], b * n_s, lambda i, j: i * n_s + j)
    const2 = lambda i, j: (0, 0)
    const3 = lambda i, j: (0, 0, 0)
    return pl.pallas_call(
        functools.partial(_pool_kernel, ts=ts, gw=gw),
        grid=(b, s // ts),
        in_specs=[
            pl.BlockSpec((1, ts, d), lambda i, j: (i, j, 0)),
            pl.BlockSpec((1, 6, d), lambda i, j: (i, 0, 0)),
            pl.BlockSpec((1, d), const2),
            pl.BlockSpec((d, d), const2),
            pl.BlockSpec(band.shape, const3),
            pl.BlockSpec((ng, gw, gw), const3),
            pl.BlockSpec((1, d), const2),
            pl.BlockSpec((d, d), const2),
            gate_up, gate_up, down,
            pl.BlockSpec((1, ts, 1), lambda i, j: (i, j, 0)),
            pl.BlockSpec((1, LANES), const2),
        ],
        out_specs=[pl.BlockSpec((1, ts, d), lambda i, j: (i, j, 0)), gate_up, gate_up, down,
                   pl.BlockSpec((1, ts, 3 * LANES), lambda i, j: (i, j, 0)),
                   pl.BlockSpec((1, 1, 3 * LANES, ts), lambda i, j: (i, j, 0, 0))],
        out_shape=[jax.ShapeDtypeStruct((b, s, d), F32),
                   jax.ShapeDtypeStruct(fg.shape, BF16),
                   jax.ShapeDtypeStruct(fu.shape, BF16),
                   jax.ShapeDtypeStruct(fd.shape, BF16),
                   jax.ShapeDtypeStruct((b, s, 3 * LANES), F32),
                   jax.ShapeDtypeStruct((b, s // ts, 3 * LANES, ts), F32)],
        scratch_shapes=[pltpu.VMEM((POOL_HALO + ts, d), BF16)],
        compiler_params=_cparams(("arbitrary", "arbitrary")),
        name="pool_mixer",
    )(x, mod, g, w_in, band, w_grp, scale, w_out, fg, fu, fd, pos_f, freq_row)


def _ffn_kernel(x_ref, mod_ref, g_ref, wg_ref, wu_ref, wd_ref, eg_ref, eu_ref, ed_ref,
                o_ref, eg_out_ref, eu_out_ref, ed_out_ref, h_ref, acc_ref):
    f = pl.program_id(1)

    @pl.when(f == 0)
    def _():
        mod = mod_ref[0]
        h = (_rms_scale(x_ref[...]) * g_ref[...]) * (1.0 + mod[4:5]) + mod[3:4]
        h_ref[...] = h.astype(BF16)
        acc_ref[...] = jnp.zeros_like(acc_ref)

    h = h_ref[...]
    gate = _dot(h, wg_ref[0])
    up = _dot(h, wu_ref[0])
    a = (_silu(gate) * up).astype(BF16)
    acc_ref[...] += _dot(a, wd_ref[0])
    _cast_side_job((eg_ref, eu_ref, ed_ref), (eg_out_ref, eu_out_ref, ed_out_ref))

    @pl.when(f == pl.num_programs(1) - 1)
    def _():
        o_ref[...] = x_ref[...] + mod_ref[0][5:6] * acc_ref[...]


def _dense_ffn(x2, mod, g, wg, wu, wd, eg, eu, ed, tm, tf, rows_per_batch):
    n, d = x2.shape
    ff = wg.shape[2]
    tpb = rows_per_batch // tm
    n_f = ff // tf
    n_exp, _, eff = eg.shape
    gate_up, down = _cast_specs(n_exp, d, eff, (n // tm) * n_f, lambda i, f: i * n_f + f)
    return pl.pallas_call(
        _ffn_kernel,
        grid=(n // tm, n_f),
        in_specs=[
            pl.BlockSpec((tm, d), lambda i, f: (i, 0)),
            pl.BlockSpec((1, 6, d), lambda i, f: (i // tpb, 0, 0)),
            pl.BlockSpec((1, d), lambda i, f: (0, 0)),
            pl.BlockSpec((1, d, tf), lambda i, f: (0, 0, f)),
            pl.BlockSpec((1, d, tf), lambda i, f: (0, 0, f)),
            pl.BlockSpec((1, tf, d), lambda i, f: (0, f, 0)),
            gate_up, gate_up, down,
        ],
        out_specs=[pl.BlockSpec((tm, d), lambda i, f: (i, 0)), gate_up, gate_up, down],
        out_shape=[jax.ShapeDtypeStruct((n, d), F32),
                   jax.ShapeDtypeStruct(eg.shape, BF16),
                   jax.ShapeDtypeStruct(eu.shape, BF16),
                   jax.ShapeDtypeStruct(ed.shape, BF16)],
        scratch_shapes=[pltpu.VMEM((tm, d), BF16), pltpu.VMEM((tm, d), F32)],
        compiler_params=_cparams(("arbitrary", "arbitrary")),
        name="dense_ffn",
    )(x2, mod, g, wg, wu, wd, eg, eu, ed)


def _dot_nt(a, b):
    return lax.dot_general(a, b, (((1,), (1,)), ((), ())), preferred_element_type=F32)


def _qkv_kernel(x_ref, mod_ref, kvmod_ref, gmix_ref, gkv_ref, wdq_ref, qg_ref, wqn_t_ref,
                wqr_t_ref, wdkv_ref, ckvg_ref, wkr_ref, wuk_ref, wuv_t_ref, tab_ref, tab_t_ref,
                qt_ref, k_ref, vt_ref):
    x = x_ref[0]
    xn = _rms_scale(x)
    mod = mod_ref[0]
    kvmod = kvmod_ref[0]
    hq = ((xn * gmix_ref[...]) * (1.0 + mod[1:2]) + mod[0:1]).astype(BF16)
    hk = ((xn * gkv_ref[...]) * (1.0 + kvmod[1:2]) + kvmod[0:1]).astype(BF16)

    cq = (_rms_scale(_dot(hq, wdq_ref[...])) * qg_ref[...]).astype(BF16)
    qn_t = _dot_nt(wqn_t_ref[...], cq) * Q_SCALE
    qr_t = _dot_nt(wqr_t_ref[...], cq) * Q_SCALE
    tab_t = tab_t_ref[0, 0]

    ckv = (_rms_scale(_dot(hk, wdkv_ref[...])) * ckvg_ref[...]).astype(BF16)
    kn = _dot(ckv, wuk_ref[...])
    vt_ref[0, 0] = _dot_nt(wuv_t_ref[...], ckv).astype(BF16)
    kr = _apply_rope(_dot(hk, wkr_ref[...]), tab_ref[0]).astype(BF16)

    for h in range(N_HEADS):
        lo = h * HEAD_PAD
        hs = slice(h * LANES, (h + 1) * LANES)
        qt_ref[0, 0, lo:lo + LANES, :] = qn_t[hs].astype(BF16)
        qt_ref[0, 0, lo + LANES:lo + HEAD_PAD, :] = _apply_rope_t(qr_t[hs], tab_t).astype(BF16)
        k_ref[0, :, lo:lo + LANES] = kn[:, hs].astype(BF16)
        k_ref[0, :, lo + LANES:lo + HEAD_PAD] = kr


def _qkv_proj(x, mod, kvmod, gmix, gkv, wdq, qg, wqn_t, wqr_t, wdkv, ckvg, wkr, wuk, wuv_t,
              tab, tab_t, ts):
    b, s, d = x.shape
    row = lambda i, j: (i, j, 0)
    tile = lambda i, j: (i, j, 0, 0)
    per_b = lambda i, j: (i, 0, 0)

    def full(a):
        return pl.BlockSpec(a.shape, lambda i, j: (0,) * a.ndim)

    weights = (gmix, gkv, wdq, qg, wqn_t, wqr_t, wdkv, ckvg, wkr, wuk, wuv_t)
    return pl.pallas_call(
        _qkv_kernel,
        grid=(b, s // ts),
        in_specs=[pl.BlockSpec((1, ts, d), row),
                  pl.BlockSpec((1, 6, d), per_b),
                  pl.BlockSpec((1, 2, d), per_b)]
                 + [full(a) for a in weights]
                 + [pl.BlockSpec((1, ts, 3 * LANES), row),
                    pl.BlockSpec((1, 1, 3 * LANES, ts), tile)],
        out_specs=[pl.BlockSpec((1, 1, N_HEADS * HEAD_PAD, ts), tile),
                   pl.BlockSpec((1, ts, N_HEADS * HEAD_PAD), row),
                   pl.BlockSpec((1, 1, N_HEADS * D_V, ts), tile)],
        out_shape=[jax.ShapeDtypeStruct((b, s // ts, N_HEADS * HEAD_PAD, ts), BF16),
                   jax.ShapeDtypeStruct((b, s, N_HEADS * HEAD_PAD), BF16),
                   jax.ShapeDtypeStruct((b, s // ts, N_HEADS * D_V, ts), BF16)],
        compiler_params=_cparams(("arbitrary", "arbitrary")),
        name="qkv_proj",
    )(x, mod, kvmod, *weights, tab, tab_t)


def _attn_kernel(qt_ref, k_ref, vt_ref, o_ref, *, tq, tk):
    ts = qt_ref.shape[3]
    s = k_ref.shape[1]
    key = lax.broadcasted_iota(jnp.int32, (tk, tq), 0)
    qry = lax.broadcasted_iota(jnp.int32, (tk, tq), 1)
    last_block = lambda qi: ((qi + 1) * tq - 1) // tk
    blocks = [(qi, j) for qi in range(s // tq) for j in range(last_block(qi) + 1)]

    def tile(ref, start, size):
        return ref[0, start // ts, :, start % ts:start % ts + size]

    def scores_t(qi, j):
        st = _dot(k_ref[0, j * tk:(j + 1) * tk, :], tile(qt_ref, qi * tq, tq))
        if (j + 1) * tk - 1 > qi * tq:
            st = jnp.where(key + (j * tk - qi * tq) <= qry, st, -jnp.inf)
        return st

    st_next = scores_t(*blocks[0])
    for n, (qi, j) in enumerate(blocks):
        st = st_next
        if n + 1 < len(blocks):
            st_next = scores_t(*blocks[n + 1])
        if j == 0:
            m = jnp.full((1, tq), -jnp.inf, F32)
            l = jnp.zeros((1, tq), F32)
            acc = jnp.zeros((D_V, tq), F32)
        m_new = jnp.maximum(m, jnp.max(st, axis=0, keepdims=True))
        alpha = jnp.exp2(m - m_new)
        pt = jnp.exp2(st - m_new)
        l = alpha * l + jnp.sum(pt, axis=0, keepdims=True)
        acc = alpha * acc + _dot(tile(vt_ref, j * tk, tk), pt.astype(BF16))
        m = m_new
        if j == last_block(qi):
            o_ref[0, qi * tq:(qi + 1) * tq, :] = (acc / l).T.astype(BF16)


def _attention(qt, k, vt, ts):
    b, s, _ = k.shape
    return pl.pallas_call(
        functools.partial(_attn_kernel, tq=min(ts, ATTN_QUERY_TILE), tk=min(ts, ATTN_KEY_BLOCK)),
        grid=(b, N_HEADS),
        in_specs=[
            pl.BlockSpec((1, s // ts, HEAD_PAD, ts), lambda i, h: (i, 0, h, 0)),
            pl.BlockSpec((1, s, HEAD_PAD), lambda i, h: (i, 0, h)),
            pl.BlockSpec((1, s // ts, D_V, ts), lambda i, h: (i, 0, h, 0)),
        ],
        out_specs=pl.BlockSpec((1, s, D_V), lambda i, h: (i, 0, h)),
        out_shape=jax.ShapeDtypeStruct((b, s, N_HEADS * D_V), BF16),
        compiler_params=_cparams(("arbitrary", "arbitrary")),
        name="flash_attn",
    )(qt, k, vt)


def _attn_out_kernel(o_ref, x_ref, mod_ref, wo_ref, g_ref, wr_ref, ltri_ref,
                     x_out_ref, h_out_ref, rcol_ref, rrow_ref, cnt_ref, carry_ref,
                     *, n_exp, tiles_per_block):
    i = pl.program_id(0)

    @pl.when(i % tiles_per_block == 0)
    def _():
        carry_ref[...] = jnp.zeros_like(carry_ref)

    mod = mod_ref[0]
    x = x_ref[...] + mod[2:3] * _dot(o_ref[...], wo_ref[...])
    x_out_ref[...] = x
    h = (_rms_scale(x) * g_ref[...]) * (1.0 + mod[4:5]) + mod[3:4]
    h_out_ref[...] = h.astype(BF16)

    wr = wr_ref[...]
    logit = [jnp.sum(h * wr[e:e + 1], axis=-1, keepdims=True) for e in range(n_exp)]
    neg = jnp.full_like(logit[0], -jnp.inf)

    def first_argmax(vals):
        m = functools.reduce(jnp.maximum, vals)
        idx = jnp.full(m.shape, n_exp, jnp.int32)
        for e in reversed(range(n_exp)):
            idx = jnp.where(vals[e] == m, e, idx)
        return m, idx

    m1, i1 = first_argmax(logit)
    m2, i2 = first_argmax([jnp.where(i1 == e, neg, logit[e]) for e in range(n_exp)])
    ex = jnp.exp(m2 - m1)
    w1 = 1.0 / (1.0 + ex)
    w2 = ex / (1.0 + ex)

    tm = x.shape[0]
    lane = lax.broadcasted_iota(jnp.int32, (tm, LANES), 1)
    mask = jnp.where((lane == i1) | (lane == i2), 1.0, 0.0)
    ranks = _dot(ltri_ref[...], mask.astype(BF16)) + carry_ref[0:1]
    incl = ranks + mask
    rank1 = jnp.sum(jnp.where(lane == i1, ranks, 0.0), axis=-1, keepdims=True)
    rank2 = jnp.sum(jnp.where(lane == i2, ranks, 0.0), axis=-1, keepdims=True)
    carry_ref[0:1] = incl[tm - 1:tm]
    n_sub = tm // rrow_ref.shape[2]
    sub = rrow_ref.shape[2]
    cnt_ref[0] = jnp.zeros(cnt_ref.shape[1:], F32)
    for j in range(n_sub):
        cnt_ref[0, j:j + 1] = incl[(j + 1) * sub - 1:(j + 1) * sub]

    fields = (i1.astype(F32), i2.astype(F32), rank1, rank2, w1, w2)
    rcol = jnp.zeros((tm, LANES), F32)
    for k, val in enumerate(fields):
        rcol = jnp.where(lane == k, val, rcol)
    rcol_ref[...] = rcol
    rrow = rcol.T
    for j in range(n_sub):
        rrow_ref[j] = rrow[0:ROUTE_FIELDS, j * sub:(j + 1) * sub]


def _attn_out(o2, x2, mod, wo, g, wr, ltri, tm, sub, rows_per_block):
    n, d = x2.shape
    n_exp = wr.shape[0]
    tpb = rows_per_block // tm
    n_sub = tm // sub
    return pl.pallas_call(
        functools.partial(_attn_out_kernel, n_exp=n_exp, tiles_per_block=tpb),
        grid=(n // tm,),
        in_specs=[
            pl.BlockSpec((tm, o2.shape[1]), lambda i: (i, 0)),
            pl.BlockSpec((tm, d), lambda i: (i, 0)),
            pl.BlockSpec((1, 6, d), lambda i: (i // tpb, 0, 0)),
            pl.BlockSpec(wo.shape, lambda i: (0, 0)),
            pl.BlockSpec((1, d), lambda i: (0, 0)),
            pl.BlockSpec(wr.shape, lambda i: (0, 0)),
            pl.BlockSpec((tm, tm), lambda i: (0, 0)),
        ],
        out_specs=[pl.BlockSpec((tm, d), lambda i: (i, 0)),
                   pl.BlockSpec((tm, d), lambda i: (i, 0)),
                   pl.BlockSpec((tm, LANES), lambda i: (i, 0)),
                   pl.BlockSpec((n_sub, ROUTE_FIELDS, sub), lambda i: (i, 0, 0)),
                   pl.BlockSpec((1, 8, LANES), lambda i: (i, 0, 0))],
        out_shape=[jax.ShapeDtypeStruct((n, d), F32),
                   jax.ShapeDtypeStruct((n, d), BF16),
                   jax.ShapeDtypeStruct((n, LANES), F32),
                   jax.ShapeDtypeStruct((n // sub, ROUTE_FIELDS, sub), F32),
                   jax.ShapeDtypeStruct((n // tm, 8, LANES), F32)],
        scratch_shapes=[pltpu.VMEM((8, LANES), F32)],
        compiler_params=_cparams(("arbitrary",)),
        name="attn_out_router",
    )(o2, x2, mod, wo, g, wr, ltri)


def _dispatch_kernel(off_ref, lo_ref, hi_ref, cexp_ref, h_ref, rrow_ref, xg_ref, gs_ref,
                     *, n_exp, sub, chunks_per_step, chunks_per_block):
    blk = pl.program_id(0)
    g = pl.program_id(1)
    d = h_ref.shape[1]
    n_sub = rrow_ref.shape[0]
    win = min(DISPATCH_WINDOW, n_sub)
    iota = lax.broadcasted_iota(jnp.int32, (DISPATCH_CHUNK, 1), 0)

    def contribution(info, sb):
        row, exp_id, exp_off = info
        r = rrow_ref[sb]
        first = r[0:1] == exp_id
        second = r[1:2] == exp_id
        pos = jnp.where(first, r[2:3] + exp_off, jnp.where(second, r[3:4] + exp_off, -1.0))
        hit = pos == row
        onehot = jnp.where(hit, 1.0, 0.0).astype(BF16)
        tok = h_ref[pl.ds(pl.multiple_of(sb * sub, sub), sub), :]
        gate = jnp.sum(jnp.where(hit, jnp.where(first, r[4:5], r[5:6]), 0.0),
                       axis=-1, keepdims=True)
        return _dot(onehot, tok), gate

    def chunk_info(j):
        c = g * chunks_per_step + j
        t = blk * chunks_per_block + c
        row = (c * DISPATCH_CHUNK + iota).astype(F32)
        exp_id = cexp_ref[t]
        exp_off = off_ref[blk * n_exp + exp_id].astype(F32)
        start = jnp.minimum(lo_ref[t], n_sub - win)
        return (row, exp_id.astype(F32), exp_off), start, hi_ref[t]

    for j in range(chunks_per_step):
        info, start, _ = chunk_info(j)
        acc = jnp.zeros((DISPATCH_CHUNK, d), F32)
        gate = jnp.zeros((DISPATCH_CHUNK, 1), F32)
        for k in range(win):
            a, b = contribution(info, start + k)
            acc += a
            gate += b
        rows = slice(j * DISPATCH_CHUNK, (j + 1) * DISPATCH_CHUNK)
        xg_ref[rows, :] = acc.astype(BF16)
        gs_ref[rows, :] = gate

    def extra(j, _):
        info, start, last = chunk_info(j)
        rows = pl.ds(pl.multiple_of(j * DISPATCH_CHUNK, DISPATCH_CHUNK), DISPATCH_CHUNK)

        def add(sb, _):
            a, b = contribution(info, sb)
            xg_ref[rows, :] = (xg_ref[rows, :].astype(F32) + a).astype(BF16)
            gs_ref[rows, :] = gs_ref[rows, :] + b
            return 0

        return lax.fori_loop(start + win, last + 1, add, 0)

    lax.fori_loop(0, chunks_per_step, extra, 0)


def _dispatch(off, lo, hi, cexp, h, rrow, n_blocks, rows_per_block, rcap, n_exp, sub):
    n, d = h.shape
    n_sub = rows_per_block // sub
    step_rows = _pick(rcap, 1024)
    cps = step_rows // DISPATCH_CHUNK
    cpb = rcap // DISPATCH_CHUNK
    grid_spec = pltpu.PrefetchScalarGridSpec(
        num_scalar_prefetch=4,
        grid=(n_blocks, rcap // step_rows),
        in_specs=[
            pl.BlockSpec((rows_per_block, d), lambda b, g, *_: (b, 0)),
            pl.BlockSpec((n_sub, ROUTE_FIELDS, sub), lambda b, g, *_: (b, 0, 0)),
        ],
        out_specs=[
            pl.BlockSpec((step_rows, d), lambda b, g, *_: (b * (rcap // step_rows) + g, 0)),
            pl.BlockSpec((step_rows, 1), lambda b, g, *_: (b * (rcap // step_rows) + g, 0)),
        ],
    )
    return pl.pallas_call(
        functools.partial(_dispatch_kernel, n_exp=n_exp, sub=sub, chunks_per_step=cps,
                          chunks_per_block=cpb),
        grid_spec=grid_spec,
        out_shape=[jax.ShapeDtypeStruct((n_blocks * rcap, d), BF16),
                   jax.ShapeDtypeStruct((n_blocks * rcap, 1), F32)],
        compiler_params=_cparams(("arbitrary", "arbitrary")),
        name="moe_dispatch",
    )(off, lo, hi, cexp, h, rrow)


def _expert_ffn_kernel(perm_ref, pexp_ref, pfill_ref, x_ref, gs_ref, wg_ref, wu_ref, wd_ref,
                       y_ref, *, tf):
    w = pl.program_id(0)
    fill = pfill_ref[w]
    ff = wg_ref.shape[2]

    def ffn(rows):
        x = x_ref[rows, :]
        gs = gs_ref[rows, :]
        acc = jnp.zeros((x.shape[0], y_ref.shape[1]), F32)
        for f0 in range(0, ff, tf):
            gate = _dot(x, wg_ref[0, :, f0:f0 + tf])
            up = _dot(x, wu_ref[0, :, f0:f0 + tf])
            a = ((_silu(gate) * up) * gs).astype(BF16)
            acc = acc + _dot(a, wd_ref[0, f0:f0 + tf, :])
        return acc.astype(BF16)

    half = PIECE // 2

    @pl.when(fill == 2)
    def _():
        y_ref[...] = ffn(slice(0, PIECE))

    @pl.when(fill == 1)
    def _():
        y_ref[0:half, :] = ffn(slice(0, half))
        y_ref[half:PIECE, :] = jnp.zeros((half, y_ref.shape[1]), BF16)

    @pl.when(fill == 0)
    def _():
        y_ref[...] = jnp.zeros(y_ref.shape, BF16)


def _expert_ffn(perm, pexp, pfill, xg, gs, wg, wu, wd, tf):
    rows, d = xg.shape
    n_exp, _, ff = wg.shape
    grid_spec = pltpu.PrefetchScalarGridSpec(
        num_scalar_prefetch=3,
        grid=(rows // PIECE,),
        in_specs=[
            pl.BlockSpec((PIECE, d), lambda w, perm, pexp, pfill: (perm[w], 0)),
            pl.BlockSpec((PIECE, 1), lambda w, perm, pexp, pfill: (perm[w], 0)),
            pl.BlockSpec((1, d, ff), lambda w, perm, pexp, pfill: (pexp[w], 0, 0)),
            pl.BlockSpec((1, d, ff), lambda w, perm, pexp, pfill: (pexp[w], 0, 0)),
            pl.BlockSpec((1, ff, d), lambda w, perm, pexp, pfill: (pexp[w], 0, 0)),
        ],
        out_specs=pl.BlockSpec((PIECE, d), lambda w, perm, pexp, pfill: (perm[w], 0)),
    )
    return pl.pallas_call(
        functools.partial(_expert_ffn_kernel, tf=tf),
        grid_spec=grid_spec,
        out_shape=jax.ShapeDtypeStruct((rows, d), BF16),
        compiler_params=_cparams(("arbitrary",)),
        name="expert_ffn",
    )(perm, pexp, pfill, xg, gs, wg, wu, wd)


def _combine_kernel(off_ref, start_ref, end_ref, yg_ref, rcol_ref, x_ref, mod_ref, gfin_ref, o_ref,
                    acc_ref, *, n_exp, sub, subs_per_block):
    blk = pl.program_id(0)
    t = pl.program_id(1)
    tm, d = x_ref.shape
    col = lax.broadcasted_iota(jnp.int32, (1, COMBINE_WINDOW), 1)
    align = COMBINE_WINDOW // 2

    def token_info(s):
        rc = rcol_ref[s * sub:(s + 1) * sub, :]
        return rc[:, 0:1], rc[:, 1:2], rc[:, 2:3], rc[:, 3:4]

    def gathered(info, e, w0):
        i1, i2, rank1, rank2 = info
        exp_off = off_ref[blk * n_exp + e].astype(F32)
        pos = jnp.where(i1 == e, rank1 + exp_off, jnp.where(i2 == e, rank2 + exp_off, -1.0))
        onehot = jnp.where(pos == (w0 + col).astype(F32), 1.0, 0.0).astype(BF16)
        return _dot(onehot, yg_ref[pl.ds(pl.multiple_of(w0, align), COMBINE_WINDOW), :])

    def first_window(s, e):
        idx = (blk * subs_per_block + t * (tm // sub) + s) * n_exp + e
        w0 = lax.shift_right_logical(start_ref[idx], align.bit_length() - 1) * align
        return w0, end_ref[idx]

    for s in range(tm // sub):
        info = token_info(s)
        acc = jnp.zeros((sub, d), F32)
        for e in range(n_exp):
            acc += gathered(info, e, first_window(s, e)[0])
        acc_ref[s * sub:(s + 1) * sub, :] = acc

    for s in range(tm // sub):
        info = token_info(s)
        for e in range(n_exp):
            w0, end = first_window(s, e)
            n_more = lax.shift_right_logical(
                jnp.maximum(end - w0 - 1, 0), COMBINE_WINDOW.bit_length() - 1)

            def add(k, _, info=info, e=e, w0=w0, s=s):
                rows = slice(s * sub, (s + 1) * sub)
                acc_ref[rows, :] = acc_ref[rows, :] + gathered(info, e, w0 + k * COMBINE_WINDOW)
                return 0

            lax.fori_loop(1, n_more + 1, add, 0)

    x = x_ref[...] + mod_ref[0][5:6] * acc_ref[...]
    o_ref[...] = _rms_scale(x) * gfin_ref[...]


def _combine(off, lo, hi, yg, rcol, x2, mod, gfin, n_blocks, rows_per_block, rcap, n_exp, sub, tm):
    n, d = x2.shape
    tpb = rows_per_block // tm
    grid_spec = pltpu.PrefetchScalarGridSpec(
        num_scalar_prefetch=3,
        grid=(n_blocks, tpb),
        in_specs=[
            pl.BlockSpec((rcap, d), lambda b, t, *_: (b, 0)),
            pl.BlockSpec((tm, LANES), lambda b, t, *_: (b * tpb + t, 0)),
            pl.BlockSpec((tm, d), lambda b, t, *_: (b * tpb + t, 0)),
            pl.BlockSpec((1, 6, d), lambda b, t, *_: (b, 0, 0)),
            pl.BlockSpec((1, d), lambda b, t, *_: (0, 0)),
        ],
        out_specs=pl.BlockSpec((tm, d), lambda b, t, *_: (b * tpb + t, 0)),
        scratch_shapes=[pltpu.VMEM((tm, d), F32)],
    )
    return pl.pallas_call(
        functools.partial(_combine_kernel, n_exp=n_exp, sub=sub,
                          subs_per_block=rows_per_block // sub),
        grid_spec=grid_spec,
        out_shape=jax.ShapeDtypeStruct((n, d), F32),
        compiler_params=_cparams(("arbitrary", "arbitrary")),
        name="moe_combine",
    )(off, lo, hi, yg, rcol, x2, mod, gfin)


def _routing_tables(cnt, n_blocks, n_sub, n_exp, rcap):
    cum_incl = cnt.astype(jnp.int32)
    cum_excl = jnp.concatenate([jnp.zeros_like(cum_incl[:, :1]), cum_incl[:, :-1]], axis=1)
    total = cum_incl[:, -1]
    psz = (total + PIECE - 1) // PIECE * PIECE
    off = jnp.cumsum(psz, axis=1) - psz
    start = off[:, None, :] + cum_excl
    end = off[:, None, :] + cum_incl

    c0 = jnp.arange(rcap // DISPATCH_CHUNK, dtype=jnp.int32)[None, :, None, None] * DISPATCH_CHUNK
    hit = ((start[:, None] < c0 + DISPATCH_CHUNK) & (end[:, None] > c0)
           & (end[:, None] > start[:, None])).any(-1)
    sub_ids = jnp.arange(n_sub, dtype=jnp.int32)
    d_lo = jnp.min(jnp.where(hit, sub_ids, n_sub), axis=-1)
    d_hi = jnp.max(jnp.where(hit, sub_ids, -1), axis=-1)
    d_lo = jnp.where(d_hi < 0, 0, d_lo)


    ppb = rcap // PIECE
    prow = jnp.arange(ppb, dtype=jnp.int32)[None, :, None] * PIECE
    inside = (prow >= off[:, None, :]) & (prow < (off + psz)[:, None, :])
    p_exp = jnp.argmax(inside, axis=-1).astype(jnp.int32)
    used = inside.any(-1)
    left = jnp.take_along_axis(total + off, p_exp, axis=1) - prow[..., 0]
    p_fill = jnp.where(used & (left > 0), jnp.where(left > PIECE // 2, 2, 1), 0).astype(jnp.int32)
    key = jnp.where(p_fill > 0, p_exp, n_exp).reshape(-1)
    perm = jnp.argsort(key, stable=True).astype(jnp.int32)
    p_exp_sorted = jnp.minimum(key[perm], n_exp - 1).astype(jnp.int32)
    d_exp = jnp.repeat(p_exp, PIECE // DISPATCH_CHUNK, axis=1).reshape(-1)
    return (off.reshape(-1).astype(jnp.int32), d_lo.reshape(-1), d_hi.reshape(-1), d_exp,
            start.reshape(-1).astype(jnp.int32), end.reshape(-1).astype(jnp.int32),
            perm, p_exp_sorted, p_fill.reshape(-1)[perm])


def _pick(n, pref):
    t = min(n, pref)
    while n % t:
        t //= 2
    return t


def kernel(x, c, positions, ada_w, ada_b, mix_norm_g, ffn_norm_g, pool_w_in, pool_w_grp,
           pool_scale, pool_w_out, kv_ada_w, kv_ada_b, kv_norm_g, w_dkv, ckv_norm_g, w_kr,
           w_uk, w_uv, w_dq, q_norm_g, w_uq, w_o, ffn_w_gate, ffn_w_up, ffn_w_down,
           router_w, moe_w_gate, moe_w_up, moe_w_down, final_norm_g):
    b, s, d = x.shape
    n = b * s
    depth = ada_w.shape[0]
    assert depth == 2 and pool_w_in.shape[0] == 1 and w_dq.shape[0] == 1
    assert b <= 8 and d % LANES == 0

    ts = _pick(s, 512)
    tm = _pick(s, 1024)
    tf = _pick(ffn_w_gate.shape[-1], 512)

    c_pad = jnp.zeros((8, d), F32).at[:b].set(c)
    mods = _adaln(c_pad, ada_w, ada_b[:, None, :], _pick(6 * d, 1536))
    mods = mods[:, :b].reshape(depth, b, 6, d)
    kvmod = _adaln(c_pad, kv_ada_w[None], kv_ada_b[None, None, :], _pick(2 * d, 1024))
    kvmod = kvmod[0, :b].reshape(b, 2, d)

    r = jnp.arange(min(ts, POOL_ROW_BLOCK))[:, None]
    cidx = jnp.arange(min(ts, POOL_ROW_BLOCK) + POOL_HALO)[None, :] - POOL_HALO
    band = jnp.stack([((cidx <= r) & (cidx > r - w)) for w in POOL_WINDOWS]).astype(BF16)
    inv_freq = ROPE_THETA ** (-jnp.arange(0, D_ROPE, 2, dtype=F32) / D_ROPE)
    freq_row = jnp.concatenate([inv_freq, inv_freq, jnp.zeros((LANES - D_ROPE,), F32)])[None, :]
    x1, fwg, fwu, fwd, rope_tab, rope_tab_t = _pool_mixer(
        x, mods[0], mix_norm_g[0:1], pool_w_in[0].astype(BF16), band, pool_w_grp[0].astype(BF16),
        pool_scale[0:1], pool_w_out[0].astype(BF16), ffn_w_gate, ffn_w_up, ffn_w_down,
        positions.astype(F32)[..., None], freq_row, ts)

    x2, ewg, ewu, ewd = _dense_ffn(x1.reshape(n, d), mods[0], ffn_norm_g[0:1], fwg, fwu, fwd,
                                   moe_w_gate[0], moe_w_up[0], moe_w_down[0], tm, tf, s)


    q_lora = w_dq.shape[-1]
    wuq = w_uq[0].reshape(q_lora, N_HEADS, D_NOPE + D_ROPE)
    wqn_t = wuq[:, :, :D_NOPE].reshape(q_lora, N_HEADS * D_NOPE).T.astype(BF16)
    wqr = jnp.pad(wuq[:, :, D_NOPE:], ((0, 0), (0, 0), (0, LANES - D_ROPE)))
    wqr_t = wqr.reshape(q_lora, N_HEADS * LANES).T.astype(BF16)
    wkr = jnp.pad(w_kr, ((0, 0), (0, LANES - D_ROPE))).astype(BF16)
    qt, k, vt = _qkv_proj(
        x2.reshape(b, s, d), mods[1], kvmod, mix_norm_g[1:2], kv_norm_g[None, :],
        w_dq[0].astype(BF16), q_norm_g[0:1], wqn_t, wqr_t, w_dkv.astype(BF16),
        ckv_norm_g[None, :], wkr, w_uk.astype(BF16), w_uv.T.astype(BF16), rope_tab, rope_tab_t, ts)
    o = _attention(qt, k, vt, ts)

    n_exp = router_w.shape[-1]
    sub = _pick(s, ROUTE_SUB)
    n_sub = s // sub
    rcap = 2 * s + n_exp * PIECE
    ltri = (jnp.arange(ts)[None, :] < jnp.arange(ts)[:, None]).astype(BF16)
    x3, h3, rcol, rrow, cnt = _attn_out(o.reshape(n, N_HEADS * D_V), x2, mods[1],
                                        w_o[0].astype(BF16), ffn_norm_g[1:2], router_w[0].T,
                                        ltri, ts, sub, s)
    cnt = cnt[:, :ts // sub, :n_exp].reshape(b, n_sub, n_exp)
    off, d_lo, d_hi, d_exp, c_lo, c_hi, perm, p_exp, p_fill = _routing_tables(cnt, b, n_sub, n_exp, rcap)
    xg, gs = _dispatch(off, d_lo, d_hi, d_exp, h3, rrow, b, s, rcap, n_exp, sub)
    yg = _expert_ffn(perm, p_exp, p_fill, xg, gs, ewg, ewu, ewd, tf)
    out = _combine(off, c_lo, c_hi, yg, rcol, x3, mods[1], final_norm_g[None, :], b, s, rcap,
                   n_exp, sub, ts)
    return out.reshape(b, s, d)
```

```python
import functools
import math

import jax
import jax.numpy as jnp
from jax import lax
from jax.experimental import pallas as pl
from jax.experimental.pallas import tpu as pltpu

F32 = jnp.float32
BF16 = jnp.bfloat16

N_HEADS = 8
D_NOPE = 128
D_ROPE = 64
D_V = 128
HEAD_PAD = 256
ROPE_HALF = D_ROPE // 2
ROPE_THETA = 10000.0
ATTN_SCALE = 1.0 / math.sqrt(D_NOPE + D_ROPE)
Q_SCALE = ATTN_SCALE * math.log2(math.e)
ATTN_QUERY_TILE = 512
ATTN_KEY_BLOCK = 512
ATTN_LOOKAHEAD = 2
POOL_WINDOWS = (2, 4, 8, 16)
POOL_HALO = 16
POOL_ROW_BLOCK = 128
EPS = 1e-6

ROUTE_SUB = 256
DISPATCH_CHUNK = 128
DISPATCH_WINDOW = 4
COMBINE_WINDOW = 256
PIECE = 256
ROUTE_FIELDS = 8

LANES = 128
VMEM_LIMIT_BYTES = 56 * 1024 * 1024


def _cparams(sem):
    return pltpu.CompilerParams(dimension_semantics=sem, vmem_limit_bytes=VMEM_LIMIT_BYTES)


def _dot(a, b):
    return jnp.dot(a, b, preferred_element_type=F32)


def _rms_scale(x):
    return x * lax.rsqrt(jnp.mean(x * x, axis=-1, keepdims=True) + EPS)


def _silu(x):
    return x * jax.nn.sigmoid(x)


def _adaln_kernel(c_ref, w_ref, b_ref, o_ref):
    sc = _silu(c_ref[...])
    o_ref[0] = jnp.dot(sc, w_ref[0], preferred_element_type=F32,
                       precision=lax.Precision.HIGHEST) + b_ref[0]


def _adaln(c_pad, w, b, tn):
    nl, d, m = w.shape
    return pl.pallas_call(
        _adaln_kernel,
        grid=(nl, m // tn),
        in_specs=[
            pl.BlockSpec((8, d), lambda l, j: (0, 0)),
            pl.BlockSpec((1, d, tn), lambda l, j: (l, 0, j)),
            pl.BlockSpec((1, 1, tn), lambda l, j: (l, 0, j)),
        ],
        out_specs=pl.BlockSpec((1, 8, tn), lambda l, j: (l, 0, j)),
        out_shape=jax.ShapeDtypeStruct((nl, 8, m), F32),
        compiler_params=_cparams(("arbitrary", "arbitrary")),
        name="adaln",
    )(c_pad, w, b)


def _rope_table_side_job(pos_ref, freq_ref, tab_ref, tab_t_ref):
    ang = pos_ref[0] * freq_ref[...]
    lane = lax.broadcasted_iota(jnp.int32, ang.shape, 1)
    cos = jnp.cos(ang)
    sin = jnp.sin(ang)
    zero = jnp.zeros_like(ang)
    tabs = (jnp.where(lane < D_ROPE, cos, zero),
            jnp.where(lane < ROPE_HALF, -sin, zero),
            jnp.where((lane >= ROPE_HALF) & (lane < D_ROPE), sin, zero))
    for i, tab in enumerate(tabs):
        tab_ref[0, :, i * LANES:(i + 1) * LANES] = tab
        tab_t_ref[0, 0, i * LANES:(i + 1) * LANES, :] = tab.T


def _apply_rope(a, tab):
    c, s1, s2 = (tab[:, i * LANES:(i + 1) * LANES] for i in range(3))
    return a * c + pltpu.roll(a, LANES - ROPE_HALF, 1) * s1 + pltpu.roll(a, ROPE_HALF, 1) * s2


def _apply_rope_t(a, tab_t):
    c, s1, s2 = (tab_t[i * LANES:(i + 1) * LANES] for i in range(3))
    up = jnp.concatenate([a[ROPE_HALF:], a[:ROPE_HALF]], axis=0)
    down = jnp.concatenate([a[-ROPE_HALF:], a[:-ROPE_HALF]], axis=0)
    return a * c + up * s1 + down * s2


def _cast_specs(n_exp, d, ff, steps, step_of):
    assert steps >= n_exp, "too few grid steps to carry the weight cast"
    k = max(c for c in range(1, steps // n_exp + 1) if ff % (c * LANES) == 0)
    cw = ff // k

    def chunk(*idx):
        c = jnp.minimum(step_of(*idx), n_exp * k - 1)
        return c // k, c % k

    gate_up = pl.BlockSpec((1, d, cw), lambda *idx: (chunk(*idx)[0], 0, chunk(*idx)[1]))
    down = pl.BlockSpec((1, cw, d), lambda *idx: (chunk(*idx)[0], chunk(*idx)[1], 0))
    return gate_up, down


def _cast_side_job(src_refs, dst_refs):
    for src, dst in zip(src_refs, dst_refs):
        dst[...] = src[...].astype(BF16)


def _pool_kernel(x_ref, mod_ref, g_ref, win_ref, band_ref, wgrp_ref, scale_ref, wout_ref,
                 fg_ref, fu_ref, fd_ref, pos_ref, freq_ref,
                 o_ref, fg_out_ref, fu_out_ref, fd_out_ref, tab_ref, tab_t_ref, uext_ref,
                 *, ts, gw):
    s = pl.program_id(1)

    @pl.when(s == 0)
    def _():
        uext_ref[0:POOL_HALO, :] = jnp.zeros((POOL_HALO, uext_ref.shape[1]), BF16)

    x = x_ref[0]
    mod = mod_ref[0]
    h = (_rms_scale(x) * g_ref[...]) * (1.0 + mod[1:2]) + mod[0:1]
    u = _dot(h.astype(BF16), win_ref[...])
    uext_ref[POOL_HALO:POOL_HALO + ts, :] = u.astype(BF16)

    t = s * ts + lax.broadcasted_iota(jnp.int32, (ts, 1), 0)
    zs = []
    for g, w in enumerate(POOL_WINDOWS):
        cols = slice(g * gw, (g + 1) * gw)
        rb = band_ref.shape[1]
        wsum = jnp.concatenate(
            [_dot(band_ref[g], uext_ref[r0:r0 + rb + POOL_HALO, cols]) for r0 in range(0, ts, rb)],
            axis=0)
        cnt = jnp.minimum(t + 1, w).astype(F32)
        p = wsum / cnt - u[:, cols]
        zs.append(_dot(p.astype(BF16), wgrp_ref[g]))
    z = jnp.concatenate(zs, axis=-1) * scale_ref[...]
    y = _dot(z.astype(BF16), wout_ref[...])
    o_ref[0] = x + mod[2:3] * y

    uext_ref[0:POOL_HALO, :] = uext_ref[ts:ts + POOL_HALO, :]

    _cast_side_job((fg_ref, fu_ref, fd_ref), (fg_out_ref, fu_out_ref, fd_out_ref))
    _rope_table_side_job(pos_ref, freq_ref, tab_ref, tab_t_ref)


def _pool_mixer(x, mod, g, w_in, band, w_grp, scale, w_out, fg, fu, fd, pos_f, freq_row, ts):
    b, s, d = x.shape
    ng, gw, _ = w_grp.shape
    n_s = s // ts
    gate_up, down = _cast_specs(1, d, fg.shape[2], b * n_s, lambda i, j: i * n_s + j)
    const2 = lambda i, j: (0, 0)
    const3 = lambda i, j: (0, 0, 0)
    return pl.pallas_call(
        functools.partial(_pool_kernel, ts=ts, gw=gw),
        grid=(b, s // ts),
        in_specs=[
            pl.BlockSpec((1, ts, d), lambda i, j: (i, j, 0)),
            pl.BlockSpec((1, 6, d), lambda i, j: (i, 0, 0)),
            pl.BlockSpec((1, d), const2),
            pl.BlockSpec((d, d), const2),
            pl.BlockSpec(band.shape, const3),
            pl.BlockSpec((ng, gw, gw), const3),
            pl.BlockSpec((1, d), const2),
            pl.BlockSpec((d, d), const2),
            gate_up, gate_up, down,
            pl.BlockSpec((1, ts, 1), lambda i, j: (i, j, 0)),
            pl.BlockSpec((1, LANES), const2),
        ],
        out_specs=[pl.BlockSpec((1, ts, d), lambda i, j: (i, j, 0)), gate_up, gate_up, down,
                   pl.BlockSpec((1, ts, 3 * LANES), lambda i, j: (i, j, 0)),
                   pl.BlockSpec((1, 1, 3 * LANES, ts), lambda i, j: (i, j, 0, 0))],
        out_shape=[jax.ShapeDtypeStruct((b, s, d), F32),
                   jax.ShapeDtypeStruct(fg.shape, BF16),
                   jax.ShapeDtypeStruct(fu.shape, BF16),
                   jax.ShapeDtypeStruct(fd.shape, BF16),
                   jax.ShapeDtypeStruct((b, s, 3 * LANES), F32),
                   jax.ShapeDtypeStruct((b, s // ts, 3 * LANES, ts), F32)],
        scratch_shapes=[pltpu.VMEM((POOL_HALO + ts, d), BF16)],
        compiler_params=_cparams(("arbitrary", "arbitrary")),
        name="pool_mixer",
    )(x, mod, g, w_in, band, w_grp, scale, w_out, fg, fu, fd, pos_f, freq_row)


def _ffn_kernel(x_ref, mod_ref, g_ref, wg_ref, wu_ref, wd_ref, eg_ref, eu_ref, ed_ref,
                o_ref, eg_out_ref, eu_out_ref, ed_out_ref, h_ref, acc_ref):
    f = pl.program_id(1)

    @pl.when(f == 0)
    def _():
        mod = mod_ref[0]
        h = (_rms_scale(x_ref[...]) * g_ref[...]) * (1.0 + mod[4:5]) + mod[3:4]
        h_ref[...] = h.astype(BF16)
        acc_ref[...] = jnp.zeros_like(acc_ref)

    h = h_ref[...]
    gate = _dot(h, wg_ref[0])
    up = _dot(h, wu_ref[0])
    a = (_silu(gate) * up).astype(BF16)
    acc_ref[...] += _dot(a, wd_ref[0])
    _cast_side_job((eg_ref, eu_ref, ed_ref), (eg_out_ref, eu_out_ref, ed_out_ref))

    @pl.when(f == pl.num_programs(1) - 1)
    def _():
        o_ref[...] = x_ref[...] + mod_ref[0][5:6] * acc_ref[...]


def _dense_ffn(x2, mod, g, wg, wu, wd, eg, eu, ed, tm, tf, rows_per_batch):
    n, d = x2.shape
    ff = wg.shape[2]
    tpb = rows_per_batch // tm
    n_f = ff // tf
    n_exp, _, eff = eg.shape
    gate_up, down = _cast_specs(n_exp, d, eff, (n // tm) * n_f, lambda i, f: i * n_f + f)
    return pl.pallas_call(
        _ffn_kernel,
        grid=(n // tm, n_f),
        in_specs=[
            pl.BlockSpec((tm, d), lambda i, f: (i, 0)),
            pl.BlockSpec((1, 6, d), lambda i, f: (i // tpb, 0, 0)),
            pl.BlockSpec((1, d), lambda i, f: (0, 0)),
            pl.BlockSpec((1, d, tf), lambda i, f: (0, 0, f)),
            pl.BlockSpec((1, d, tf), lambda i, f: (0, 0, f)),
            pl.BlockSpec((1, tf, d), lambda i, f: (0, f, 0)),
            gate_up, gate_up, down,
        ],
        out_specs=[pl.BlockSpec((tm, d), lambda i, f: (i, 0)), gate_up, gate_up, down],
        out_shape=[jax.ShapeDtypeStruct((n, d), F32),
                   jax.ShapeDtypeStruct(eg.shape, BF16),
                   jax.ShapeDtypeStruct(eu.shape, BF16),
                   jax.ShapeDtypeStruct(ed.shape, BF16)],
        scratch_shapes=[pltpu.VMEM((tm, d), BF16), pltpu.VMEM((tm, d), F32)],
        compiler_params=_cparams(("arbitrary", "arbitrary")),
        name="dense_ffn",
    )(x2, mod, g, wg, wu, wd, eg, eu, ed)


def _dot_nt(a, b):
    return lax.dot_general(a, b, (((1,), (1,)), ((), ())), preferred_element_type=F32)


def _qkv_kernel(x_ref, mod_ref, kvmod_ref, gmix_ref, gkv_ref, wdq_ref, qg_ref, wqn_t_ref,
                wqr_t_ref, wdkv_ref, ckvg_ref, wkr_ref, wuk_ref, wuv_t_ref, tab_ref, tab_t_ref,
                qt_ref, k_ref, vt_ref):
    x = x_ref[0]
    xn = _rms_scale(x)
    mod = mod_ref[0]
    kvmod = kvmod_ref[0]
    hq = ((xn * gmix_ref[...]) * (1.0 + mod[1:2]) + mod[0:1]).astype(BF16)
    hk = ((xn * gkv_ref[...]) * (1.0 + kvmod[1:2]) + kvmod[0:1]).astype(BF16)

    cq = (_rms_scale(_dot(hq, wdq_ref[...])) * qg_ref[...]).astype(BF16)
    qn_t = _dot_nt(wqn_t_ref[...], cq) * Q_SCALE
    qr_t = _dot_nt(wqr_t_ref[...], cq) * Q_SCALE
    tab_t = tab_t_ref[0, 0]

    ckv = (_rms_scale(_dot(hk, wdkv_ref[...])) * ckvg_ref[...]).astype(BF16)
    kn = _dot(ckv, wuk_ref[...])
    vt_ref[0, 0] = _dot_nt(wuv_t_ref[...], ckv).astype(BF16)
    kr = _apply_rope(_dot(hk, wkr_ref[...]), tab_ref[0]).astype(BF16)

    for h in range(N_HEADS):
        lo = h * HEAD_PAD
        hs = slice(h * LANES, (h + 1) * LANES)
        qt_ref[0, 0, lo:lo + LANES, :] = qn_t[hs].astype(BF16)
        qt_ref[0, 0, lo + LANES:lo + HEAD_PAD, :] = _apply_rope_t(qr_t[hs], tab_t).astype(BF16)
        k_ref[0, :, lo:lo + LANES] = kn[:, hs].astype(BF16)
        k_ref[0, :, lo + LANES:lo + HEAD_PAD] = kr


def _qkv_proj(x, mod, kvmod, gmix, gkv, wdq, qg, wqn_t, wqr_t, wdkv, ckvg, wkr, wuk, wuv_t,
              tab, tab_t, ts):
    b, s, d = x.shape
    row = lambda i, j: (i, j, 0)
    tile = lambda i, j: (i, j, 0, 0)
    per_b = lambda i, j: (i, 0, 0)

    def full(a):
        return pl.BlockSpec(a.shape, lambda i, j: (0,) * a.ndim)

    weights = (gmix, gkv, wdq, qg, wqn_t, wqr_t, wdkv, ckvg, wkr, wuk, wuv_t)
    return pl.pallas_call(
        _qkv_kernel,
        grid=(b, s // ts),
        in_specs=[pl.BlockSpec((1, ts, d), row),
                  pl.BlockSpec((1, 6, d), per_b),
                  pl.BlockSpec((1, 2, d), per_b)]
                 + [full(a) for a in weights]
                 + [pl.BlockSpec((1, ts, 3 * LANES), row),
                    pl.BlockSpec((1, 1, 3 * LANES, ts), tile)],
        out_specs=[pl.BlockSpec((1, 1, N_HEADS * HEAD_PAD, ts), tile),
                   pl.BlockSpec((1, ts, N_HEADS * HEAD_PAD), row),
                   pl.BlockSpec((1, 1, N_HEADS * D_V, ts), tile)],
        out_shape=[jax.ShapeDtypeStruct((b, s // ts, N_HEADS * HEAD_PAD, ts), BF16),
                   jax.ShapeDtypeStruct((b, s, N_HEADS * HEAD_PAD), BF16),
                   jax.ShapeDtypeStruct((b, s // ts, N_HEADS * D_V, ts), BF16)],
        compiler_params=_cparams(("arbitrary", "arbitrary")),
        name="qkv_proj",
    )(x, mod, kvmod, *weights, tab, tab_t)


def _attn_kernel(qt_ref, k_ref, vt_ref, o_ref, *, tq, tk):
    ts = qt_ref.shape[3]
    s = k_ref.shape[1]
    key = lax.broadcasted_iota(jnp.int32, (tk, tq), 0)
    qry = lax.broadcasted_iota(jnp.int32, (tk, tq), 1)
    last_block = lambda qi: ((qi + 1) * tq - 1) // tk
    blocks = [(qi, j) for qi in range(s // tq) for j in range(last_block(qi) + 1)]

    def tile(ref, start, size):
        return ref[0, start // ts, :, start % ts:start % ts + size]

    def scores_t(qi, j):
        st = _dot(k_ref[0, j * tk:(j + 1) * tk, :], tile(qt_ref, qi * tq, tq))
        if (j + 1) * tk - 1 > qi * tq:
            st = jnp.where(key + (j * tk - qi * tq) <= qry, st, -jnp.inf)
        return st

    ahead = [scores_t(*blk) for blk in blocks[:ATTN_LOOKAHEAD]]
    for n, (qi, j) in enumerate(blocks):
        st = ahead.pop(0)
        if n + ATTN_LOOKAHEAD < len(blocks):
            ahead.append(scores_t(*blocks[n + ATTN_LOOKAHEAD]))
        if j == 0:
            m = jnp.full((1, tq), -jnp.inf, F32)
            l = jnp.zeros((1, tq), F32)
            acc = jnp.zeros((D_V, tq), F32)
        m_new = jnp.maximum(m, jnp.max(st, axis=0, keepdims=True))
        alpha = jnp.exp2(m - m_new)
        pt = jnp.exp2(st - m_new)
        l = alpha * l + jnp.sum(pt, axis=0, keepdims=True)
        acc = alpha * acc + _dot(tile(vt_ref, j * tk, tk), pt.astype(BF16))
        m = m_new
        if j == last_block(qi):
            o_ref[0, qi * tq:(qi + 1) * tq, :] = (acc / l).T.astype(BF16)


def _attention(qt, k, vt, ts):
    b, s, _ = k.shape
    return pl.pallas_call(
        functools.partial(_attn_kernel, tq=min(ts, ATTN_QUERY_TILE), tk=min(ts, ATTN_KEY_BLOCK)),
        grid=(b, N_HEADS),
        in_specs=[
            pl.BlockSpec((1, s // ts, HEAD_PAD, ts), lambda i, h: (i, 0, h, 0)),
            pl.BlockSpec((1, s, HEAD_PAD), lambda i, h: (i, 0, h)),
            pl.BlockSpec((1, s // ts, D_V, ts), lambda i, h: (i, 0, h, 0)),
        ],
        out_specs=pl.BlockSpec((1, s, D_V), lambda i, h: (i, 0, h)),
        out_shape=jax.ShapeDtypeStruct((b, s, N_HEADS * D_V), BF16),
        compiler_params=_cparams(("arbitrary", "arbitrary")),
        name="flash_attn",
    )(qt, k, vt)


def _attn_out_kernel(o_ref, x_ref, mod_ref, wo_ref, g_ref, wr_ref, ltri_ref,
                     x_out_ref, h_out_ref, rcol_ref, rrow_ref, cnt_ref, carry_ref,
                     *, n_exp, tiles_per_block):
    i = pl.program_id(0)

    @pl.when(i % tiles_per_block == 0)
    def _():
        carry_ref[...] = jnp.zeros_like(carry_ref)

    mod = mod_ref[0]
    x = x_ref[...] + mod[2:3] * _dot(o_ref[...], wo_ref[...])
    x_out_ref[...] = x
    h = (_rms_scale(x) * g_ref[...]) * (1.0 + mod[4:5]) + mod[3:4]
    h_out_ref[...] = h.astype(BF16)

    wr = wr_ref[...]
    logit = [jnp.sum(h * wr[e:e + 1], axis=-1, keepdims=True) for e in range(n_exp)]
    neg = jnp.full_like(logit[0], -jnp.inf)

    def first_argmax(vals):
        m = functools.reduce(jnp.maximum, vals)
        idx = jnp.full(m.shape, n_exp, jnp.int32)
        for e in reversed(range(n_exp)):
            idx = jnp.where(vals[e] == m, e, idx)
        return m, idx

    m1, i1 = first_argmax(logit)
    m2, i2 = first_argmax([jnp.where(i1 == e, neg, logit[e]) for e in range(n_exp)])
    ex = jnp.exp(m2 - m1)
    w1 = 1.0 / (1.0 + ex)
    w2 = ex / (1.0 + ex)

    tm = x.shape[0]
    lane = lax.broadcasted_iota(jnp.int32, (tm, LANES), 1)
    mask = jnp.where((lane == i1) | (lane == i2), 1.0, 0.0)
    ranks = _dot(ltri_ref[...], mask.astype(BF16)) + carry_ref[0:1]
    incl = ranks + mask
    rank1 = jnp.sum(jnp.where(lane == i1, ranks, 0.0), axis=-1, keepdims=True)
    rank2 = jnp.sum(jnp.where(lane == i2, ranks, 0.0), axis=-1, keepdims=True)
    carry_ref[0:1] = incl[tm - 1:tm]
    n_sub = tm // rrow_ref.shape[2]
    sub = rrow_ref.shape[2]
    cnt_ref[0] = jnp.zeros(cnt_ref.shape[1:], F32)
    for j in range(n_sub):
        cnt_ref[0, j:j + 1] = incl[(j + 1) * sub - 1:(j + 1) * sub]

    fields = (i1.astype(F32), i2.astype(F32), rank1, rank2, w1, w2)
    rcol = jnp.zeros((tm, LANES), F32)
    for k, val in enumerate(fields):
        rcol = jnp.where(lane == k, val, rcol)
    rcol_ref[...] = rcol
    rrow = rcol.T
    for j in range(n_sub):
        rrow_ref[j] = rrow[0:ROUTE_FIELDS, j * sub:(j + 1) * sub]


def _attn_out(o2, x2, mod, wo, g, wr, ltri, tm, sub, rows_per_block):
    n, d = x2.shape
    n_exp = wr.shape[0]
    tpb = rows_per_block // tm
    n_sub = tm // sub
    return pl.pallas_call(
        functools.partial(_attn_out_kernel, n_exp=n_exp, tiles_per_block=tpb),
        grid=(n // tm,),
        in_specs=[
            pl.BlockSpec((tm, o2.shape[1]), lambda i: (i, 0)),
            pl.BlockSpec((tm, d), lambda i: (i, 0)),
            pl.BlockSpec((1, 6, d), lambda i: (i // tpb, 0, 0)),
            pl.BlockSpec(wo.shape, lambda i: (0, 0)),
            pl.BlockSpec((1, d), lambda i: (0, 0)),
            pl.BlockSpec(wr.shape, lambda i: (0, 0)),
            pl.BlockSpec((tm, tm), lambda i: (0, 0)),
        ],
        out_specs=[pl.BlockSpec((tm, d), lambda i: (i, 0)),
                   pl.BlockSpec((tm, d), lambda i: (i, 0)),
                   pl.BlockSpec((tm, LANES), lambda i: (i, 0)),
                   pl.BlockSpec((n_sub, ROUTE_FIELDS, sub), lambda i: (i, 0, 0)),
                   pl.BlockSpec((1, 8, LANES), lambda i: (i, 0, 0))],
        out_shape=[jax.ShapeDtypeStruct((n, d), F32),
                   jax.ShapeDtypeStruct((n, d), BF16),
                   jax.ShapeDtypeStruct((n, LANES), F32),
                   jax.ShapeDtypeStruct((n // sub, ROUTE_FIELDS, sub), F32),
                   jax.ShapeDtypeStruct((n // tm, 8, LANES), F32)],
        scratch_shapes=[pltpu.VMEM((8, LANES), F32)],
        compiler_params=_cparams(("arbitrary",)),
        name="attn_out_router",
    )(o2, x2, mod, wo, g, wr, ltri)


def _dispatch_kernel(off_ref, lo_ref, hi_ref, cexp_ref, h_ref, rrow_ref, xg_ref, gs_ref,
                     *, n_exp, sub, chunks_per_step, chunks_per_block):
    blk = pl.program_id(0)
    g = pl.program_id(1)
    d = h_ref.shape[1]
    n_sub = rrow_ref.shape[0]
    win = min(DISPATCH_WINDOW, n_sub)
    iota = lax.broadcasted_iota(jnp.int32, (DISPATCH_CHUNK, 1), 0)

    def contribution(info, sb):
        row, exp_id, exp_off = info
        r = rrow_ref[sb]
        first = r[0:1] == exp_id
        second = r[1:2] == exp_id
        pos = jnp.where(first, r[2:3] + exp_off, jnp.where(second, r[3:4] + exp_off, -1.0))
        hit = pos == row
        onehot = jnp.where(hit, 1.0, 0.0).astype(BF16)
        tok = h_ref[pl.ds(pl.multiple_of(sb * sub, sub), sub), :]
        gate = jnp.sum(jnp.where(hit, jnp.where(first, r[4:5], r[5:6]), 0.0),
                       axis=-1, keepdims=True)
        return _dot(onehot, tok), gate

    def chunk_info(j):
        c = g * chunks_per_step + j
        t = blk * chunks_per_block + c
        row = (c * DISPATCH_CHUNK + iota).astype(F32)
        exp_id = cexp_ref[t]
        exp_off = off_ref[blk * n_exp + exp_id].astype(F32)
        start = jnp.minimum(lo_ref[t], n_sub - win)
        return (row, exp_id.astype(F32), exp_off), start, hi_ref[t]

    for j in range(chunks_per_step):
        info, start, _ = chunk_info(j)
        acc = jnp.zeros((DISPATCH_CHUNK, d), F32)
        gate = jnp.zeros((DISPATCH_CHUNK, 1), F32)
        for k in range(win):
            a, b = contribution(info, start + k)
            acc += a
            gate += b
        rows = slice(j * DISPATCH_CHUNK, (j + 1) * DISPATCH_CHUNK)
        xg_ref[rows, :] = acc.astype(BF16)
        gs_ref[rows, :] = gate

    def extra(j, _):
        info, start, last = chunk_info(j)
        rows = pl.ds(pl.multiple_of(j * DISPATCH_CHUNK, DISPATCH_CHUNK), DISPATCH_CHUNK)

        def add(sb, _):
            a, b = contribution(info, sb)
            xg_ref[rows, :] = (xg_ref[rows, :].astype(F32) + a).astype(BF16)
            gs_ref[rows, :] = gs_ref[rows, :] + b
            return 0

        return lax.fori_loop(start + win, last + 1, add, 0)

    lax.fori_loop(0, chunks_per_step, extra, 0)


def _dispatch(off, lo, hi, cexp, h, rrow, n_blocks, rows_per_block, rcap, n_exp, sub):
    n, d = h.shape
    n_sub = rows_per_block // sub
    step_rows = _pick(rcap, 1024)
    cps = step_rows // DISPATCH_CHUNK
    cpb = rcap // DISPATCH_CHUNK
    grid_spec = pltpu.PrefetchScalarGridSpec(
        num_scalar_prefetch=4,
        grid=(n_blocks, rcap // step_rows),
        in_specs=[
            pl.BlockSpec((rows_per_block, d), lambda b, g, *_: (b, 0)),
            pl.BlockSpec((n_sub, ROUTE_FIELDS, sub), lambda b, g, *_: (b, 0, 0)),
        ],
        out_specs=[
            pl.BlockSpec((step_rows, d), lambda b, g, *_: (b * (rcap // step_rows) + g, 0)),
            pl.BlockSpec((step_rows, 1), lambda b, g, *_: (b * (rcap // step_rows) + g, 0)),
        ],
    )
    return pl.pallas_call(
        functools.partial(_dispatch_kernel, n_exp=n_exp, sub=sub, chunks_per_step=cps,
                          chunks_per_block=cpb),
        grid_spec=grid_spec,
        out_shape=[jax.ShapeDtypeStruct((n_blocks * rcap, d), BF16),
                   jax.ShapeDtypeStruct((n_blocks * rcap, 1), F32)],
        compiler_params=_cparams(("arbitrary", "arbitrary")),
        name="moe_dispatch",
    )(off, lo, hi, cexp, h, rrow)


def _expert_ffn_kernel(perm_ref, pexp_ref, pfill_ref, x_ref, gs_ref, wg_ref, wu_ref, wd_ref,
                       y_ref, *, tf):
    w = pl.program_id(0)
    fill = pfill_ref[w]
    ff = wg_ref.shape[2]

    def ffn(rows):
        x = x_ref[rows, :]
        gs = gs_ref[rows, :]
        acc = jnp.zeros((x.shape[0], y_ref.shape[1]), F32)
        for f0 in range(0, ff, tf):
            gate = _dot(x, wg_ref[0, :, f0:f0 + tf])
            up = _dot(x, wu_ref[0, :, f0:f0 + tf])
            a = ((_silu(gate) * up) * gs).astype(BF16)
            acc = acc + _dot(a, wd_ref[0, f0:f0 + tf, :])
        return acc.astype(BF16)

    half = PIECE // 2

    @pl.when(fill == 2)
    def _():
        y_ref[...] = ffn(slice(0, PIECE))

    @pl.when(fill == 1)
    def _():
        y_ref[0:half, :] = ffn(slice(0, half))
        y_ref[half:PIECE, :] = jnp.zeros((half, y_ref.shape[1]), BF16)

    @pl.when(fill == 0)
    def _():
        y_ref[...] = jnp.zeros(y_ref.shape, BF16)


def _expert_ffn(perm, pexp, pfill, xg, gs, wg, wu, wd, tf):
    rows, d = xg.shape
    n_exp, _, ff = wg.shape
    grid_spec = pltpu.PrefetchScalarGridSpec(
        num_scalar_prefetch=3,
        grid=(rows // PIECE,),
        in_specs=[
            pl.BlockSpec((PIECE, d), lambda w, perm, pexp, pfill: (perm[w], 0)),
            pl.BlockSpec((PIECE, 1), lambda w, perm, pexp, pfill: (perm[w], 0)),
            pl.BlockSpec((1, d, ff), lambda w, perm, pexp, pfill: (pexp[w], 0, 0)),
            pl.BlockSpec((1, d, ff), lambda w, perm, pexp, pfill: (pexp[w], 0, 0)),
            pl.BlockSpec((1, ff, d), lambda w, perm, pexp, pfill: (pexp[w], 0, 0)),
        ],
        out_specs=pl.BlockSpec((PIECE, d), lambda w, perm, pexp, pfill: (perm[w], 0)),
    )
    return pl.pallas_call(
        functools.partial(_expert_ffn_kernel, tf=tf),
        grid_spec=grid_spec,
        out_shape=jax.ShapeDtypeStruct((rows, d), BF16),
        compiler_params=_cparams(("arbitrary",)),
        name="expert_ffn",
    )(perm, pexp, pfill, xg, gs, wg, wu, wd)


def _combine_kernel(off_ref, start_ref, end_ref, yg_ref, rcol_ref, x_ref, mod_ref, gfin_ref, o_ref,
                    acc_ref, *, n_exp, sub, subs_per_block):
    blk = pl.program_id(0)
    t = pl.program_id(1)
    tm, d = x_ref.shape
    col = lax.broadcasted_iota(jnp.int32, (1, COMBINE_WINDOW), 1)
    align = COMBINE_WINDOW // 2

    def token_info(s):
        rc = rcol_ref[s * sub:(s + 1) * sub, :]
        return tuple(jnp.broadcast_to(rc[:, k:k + 1], (sub, COMBINE_WINDOW)) for k in range(4))

    def gathered(info, e, w0):
        i1, i2, rank1, rank2 = info
        rank = jnp.where(i1 == e, rank1, jnp.where(i2 == e, rank2, -float(COMBINE_WINDOW)))
        in_region = (w0 - off_ref[blk * n_exp + e] + col).astype(F32)
        onehot = jnp.where(rank == in_region, 1.0, 0.0).astype(BF16)
        return _dot(onehot, yg_ref[pl.ds(pl.multiple_of(w0, align), COMBINE_WINDOW), :])

    def first_window(s, e):
        idx = (blk * subs_per_block + t * (tm // sub) + s) * n_exp + e
        w0 = lax.shift_right_logical(start_ref[idx], align.bit_length() - 1) * align
        return w0, end_ref[idx]

    for s in range(tm // sub):
        info = token_info(s)
        acc = jnp.zeros((sub, d), F32)
        for e in range(n_exp):
            acc += gathered(info, e, first_window(s, e)[0])
        acc_ref[s * sub:(s + 1) * sub, :] = acc

    for s in range(tm // sub):
        info = token_info(s)
        for e in range(n_exp):
            w0, end = first_window(s, e)
            n_more = lax.shift_right_logical(
                jnp.maximum(end - w0 - 1, 0), COMBINE_WINDOW.bit_length() - 1)

            def add(k, _, info=info, e=e, w0=w0, s=s):
                rows = slice(s * sub, (s + 1) * sub)
                acc_ref[rows, :] = acc_ref[rows, :] + gathered(info, e, w0 + k * COMBINE_WINDOW)
                return 0

            lax.fori_loop(1, n_more + 1, add, 0)

    x = x_ref[...] + mod_ref[0][5:6] * acc_ref[...]
    o_ref[...] = _rms_scale(x) * gfin_ref[...]


def _combine(off, lo, hi, yg, rcol, x2, mod, gfin, n_blocks, rows_per_block, rcap, n_exp, sub, tm):
    n, d = x2.shape
    tpb = rows_per_block // tm
    grid_spec = pltpu.PrefetchScalarGridSpec(
        num_scalar_prefetch=3,
        grid=(n_blocks, tpb),
        in_specs=[
            pl.BlockSpec((rcap, d), lambda b, t, *_: (b, 0)),
            pl.BlockSpec((tm, LANES), lambda b, t, *_: (b * tpb + t, 0)),
            pl.BlockSpec((tm, d), lambda b, t, *_: (b * tpb + t, 0)),
            pl.BlockSpec((1, 6, d), lambda b, t, *_: (b, 0, 0)),
            pl.BlockSpec((1, d), lambda b, t, *_: (0, 0)),
        ],
        out_specs=pl.BlockSpec((tm, d), lambda b, t, *_: (b * tpb + t, 0)),
        scratch_shapes=[pltpu.VMEM((tm, d), F32)],
    )
    return pl.pallas_call(
        functools.partial(_combine_kernel, n_exp=n_exp, sub=sub,
                          subs_per_block=rows_per_block // sub),
        grid_spec=grid_spec,
        out_shape=jax.ShapeDtypeStruct((n, d), F32),
        compiler_params=_cparams(("arbitrary", "arbitrary")),
        name="moe_combine",
    )(off, lo, hi, yg, rcol, x2, mod, gfin)


def _routing_tables(cnt, n_blocks, n_sub, n_exp, rcap):
    cum_incl = cnt.astype(jnp.int32)
    cum_excl = jnp.concatenate([jnp.zeros_like(cum_incl[:, :1]), cum_incl[:, :-1]], axis=1)
    total = cum_incl[:, -1]
    psz = (total + PIECE - 1) // PIECE * PIECE
    off = jnp.cumsum(psz, axis=1) - psz
    start = off[:, None, :] + cum_excl
    end = off[:, None, :] + cum_incl

    c0 = jnp.arange(rcap // DISPATCH_CHUNK, dtype=jnp.int32)[None, :, None, None] * DISPATCH_CHUNK
    hit = ((start[:, None] < c0 + DISPATCH_CHUNK) & (end[:, None] > c0)
           & (end[:, None] > start[:, None])).any(-1)
    sub_ids = jnp.arange(n_sub, dtype=jnp.int32)
    d_lo = jnp.min(jnp.where(hit, sub_ids, n_sub), axis=-1)
    d_hi = jnp.max(jnp.where(hit, sub_ids, -1), axis=-1)
    d_lo = jnp.where(d_hi < 0, 0, d_lo)


    ppb = rcap // PIECE
    prow = jnp.arange(ppb, dtype=jnp.int32)[None, :, None] * PIECE
    inside = (prow >= off[:, None, :]) & (prow < (off + psz)[:, None, :])
    p_exp = jnp.argmax(inside, axis=-1).astype(jnp.int32)
    used = inside.any(-1)
    left = jnp.take_along_axis(total + off, p_exp, axis=1) - prow[..., 0]
    p_fill = jnp.where(used & (left > 0), jnp.where(left > PIECE // 2, 2, 1), 0).astype(jnp.int32)
    key = jnp.where(p_fill > 0, p_exp, n_exp).reshape(-1)
    perm = jnp.argsort(key, stable=True).astype(jnp.int32)
    p_exp_sorted = jnp.minimum(key[perm], n_exp - 1).astype(jnp.int32)
    d_exp = jnp.repeat(p_exp, PIECE // DISPATCH_CHUNK, axis=1).reshape(-1)
    return (off.reshape(-1).astype(jnp.int32), d_lo.reshape(-1), d_hi.reshape(-1), d_exp,
            start.reshape(-1).astype(jnp.int32), end.reshape(-1).astype(jnp.int32),
            perm, p_exp_sorted, p_fill.reshape(-1)[perm])


def _pick(n, pref):
    t = min(n, pref)
    while n % t:
        t //= 2
    return t


def kernel(x, c, positions, ada_w, ada_b, mix_norm_g, ffn_norm_g, pool_w_in, pool_w_grp,
           pool_scale, pool_w_out, kv_ada_w, kv_ada_b, kv_norm_g, w_dkv, ckv_norm_g, w_kr,
           w_uk, w_uv, w_dq, q_norm_g, w_uq, w_o, ffn_w_gate, ffn_w_up, ffn_w_down,
           router_w, moe_w_gate, moe_w_up, moe_w_down, final_norm_g):
    b, s, d = x.shape
    n = b * s
    depth = ada_w.shape[0]
    assert depth == 2 and pool_w_in.shape[0] == 1 and w_dq.shape[0] == 1
    assert b <= 8 and d % LANES == 0

    ts = _pick(s, 512)
    tm = _pick(s, 1024)
    tf = _pick(ffn_w_gate.shape[-1], 512)

    c_pad = jnp.zeros((8, d), F32).at[:b].set(c)
    mods = _adaln(c_pad, ada_w, ada_b[:, None, :], _pick(6 * d, 1536))
    mods = mods[:, :b].reshape(depth, b, 6, d)
    kvmod = _adaln(c_pad, kv_ada_w[None], kv_ada_b[None, None, :], _pick(2 * d, 1024))
    kvmod = kvmod[0, :b].reshape(b, 2, d)

    r = jnp.arange(min(ts, POOL_ROW_BLOCK))[:, None]
    cidx = jnp.arange(min(ts, POOL_ROW_BLOCK) + POOL_HALO)[None, :] - POOL_HALO
    band = jnp.stack([((cidx <= r) & (cidx > r - w)) for w in POOL_WINDOWS]).astype(BF16)
    inv_freq = ROPE_THETA ** (-jnp.arange(0, D_ROPE, 2, dtype=F32) / D_ROPE)
    freq_row = jnp.concatenate([inv_freq, inv_freq, jnp.zeros((LANES - D_ROPE,), F32)])[None, :]
    x1, fwg, fwu, fwd, rope_tab, rope_tab_t = _pool_mixer(
        x, mods[0], mix_norm_g[0:1], pool_w_in[0].astype(BF16), band, pool_w_grp[0].astype(BF16),
        pool_scale[0:1], pool_w_out[0].astype(BF16), ffn_w_gate, ffn_w_up, ffn_w_down,
        positions.astype(F32)[..., None], freq_row, ts)

    x2, ewg, ewu, ewd = _dense_ffn(x1.reshape(n, d), mods[0], ffn_norm_g[0:1], fwg, fwu, fwd,
                                   moe_w_gate[0], moe_w_up[0], moe_w_down[0], tm, tf, s)


    q_lora = w_dq.shape[-1]
    wuq = w_uq[0].reshape(q_lora, N_HEADS, D_NOPE + D_ROPE)
    wqn_t = wuq[:, :, :D_NOPE].reshape(q_lora, N_HEADS * D_NOPE).T.astype(BF16)
    wqr = jnp.pad(wuq[:, :, D_NOPE:], ((0, 0), (0, 0), (0, LANES - D_ROPE)))
    wqr_t = wqr.reshape(q_lora, N_HEADS * LANES).T.astype(BF16)
    wkr = jnp.pad(w_kr, ((0, 0), (0, LANES - D_ROPE))).astype(BF16)
    qt, k, vt = _qkv_proj(
        x2.reshape(b, s, d), mods[1], kvmod, mix_norm_g[1:2], kv_norm_g[None, :],
        w_dq[0].astype(BF16), q_norm_g[0:1], wqn_t, wqr_t, w_dkv.astype(BF16),
        ckv_norm_g[None, :], wkr, w_uk.astype(BF16), w_uv.T.astype(BF16), rope_tab, rope_tab_t, ts)
    o = _attention(qt, k, vt, ts)

    n_exp = router_w.shape[-1]
    sub = _pick(s, ROUTE_SUB)
    n_sub = s // sub
    rcap = 2 * s + n_exp * PIECE
    ltri = (jnp.arange(ts)[None, :] < jnp.arange(ts)[:, None]).astype(BF16)
    x3, h3, rcol, rrow, cnt = _attn_out(o.reshape(n, N_HEADS * D_V), x2, mods[1],
                                        w_o[0].astype(BF16), ffn_norm_g[1:2], router_w[0].T,
                                        ltri, ts, sub, s)
    cnt = cnt[:, :ts // sub, :n_exp].reshape(b, n_sub, n_exp)
    off, d_lo, d_hi, d_exp, c_lo, c_hi, perm, p_exp, p_fill = _routing_tables(cnt, b, n_sub, n_exp, rcap)
    xg, gs = _dispatch(off, d_lo, d_hi, d_exp, h3, rrow, b, s, rcap, n_exp, sub)
    yg = _expert_ffn(perm, p_exp, p_fill, xg, gs, ewg, ewu, ewd, tf)
    out = _combine(off, c_lo, c_hi, yg, rcol, x3, mods[1], final_norm_g[None, :], b, s, rcap,
                   n_exp, sub, ts)
    return out.reshape(b, s, d)
```

```python
import functools
import math

import jax
import jax.numpy as jnp
from jax import lax
from jax.experimental import pallas as pl
from jax.experimental.pallas import tpu as pltpu

F32 = jnp.float32
BF16 = jnp.bfloat16

N_HEADS = 8
D_NOPE = 128
D_ROPE = 64
D_V = 128
HEAD_PAD = 256
ROPE_HALF = D_ROPE // 2
ROPE_THETA = 10000.0
ATTN_SCALE = 1.0 / math.sqrt(D_NOPE + D_ROPE)
Q_SCALE = ATTN_SCALE * math.log2(math.e)
ATTN_QUERY_TILE = 512
ATTN_KEY_BLOCK = 512
ATTN_LOOKAHEAD = 2
POOL_WINDOWS = (2, 4, 8, 16)
POOL_HALO = 16
POOL_ROW_BLOCK = 128
EPS = 1e-6

ROUTE_SUB = 256
DISPATCH_CHUNK = 128
DISPATCH_WINDOW = 4
COMBINE_WINDOW = 256
PIECE = 256
ROUTE_FIELDS = 8

LANES = 128
VMEM_LIMIT_BYTES = 56 * 1024 * 1024


def _cparams(sem):
    return pltpu.CompilerParams(dimension_semantics=sem, vmem_limit_bytes=VMEM_LIMIT_BYTES)


def _dot(a, b):
    return jnp.dot(a, b, preferred_element_type=F32)


def _rms_scale(x):
    return x * lax.rsqrt(jnp.mean(x * x, axis=-1, keepdims=True) + EPS)


def _silu(x):
    return x * jax.nn.sigmoid(x)


def _adaln_kernel(c_ref, w_ref, b_ref, o_ref):
    sc = _silu(c_ref[...])
    o_ref[0] = jnp.dot(sc, w_ref[0], preferred_element_type=F32,
                       precision=lax.Precision.HIGHEST) + b_ref[0]


def _adaln(c_pad, w, b, tn):
    nl, d, m = w.shape
    return pl.pallas_call(
        _adaln_kernel,
        grid=(nl, m // tn),
        in_specs=[
            pl.BlockSpec((8, d), lambda l, j: (0, 0)),
            pl.BlockSpec((1, d, tn), lambda l, j: (l, 0, j)),
            pl.BlockSpec((1, 1, tn), lambda l, j: (l, 0, j)),
        ],
        out_specs=pl.BlockSpec((1, 8, tn), lambda l, j: (l, 0, j)),
        out_shape=jax.ShapeDtypeStruct((nl, 8, m), F32),
        compiler_params=_cparams(("arbitrary", "arbitrary")),
        name="adaln",
    )(c_pad, w, b)


def _rope_table_side_job(pos_ref, freq_ref, tab_ref, tab_t_ref):
    ang = pos_ref[0] * freq_ref[...]
    lane = lax.broadcasted_iota(jnp.int32, ang.shape, 1)
    cos = jnp.cos(ang)
    sin = jnp.sin(ang)
    zero = jnp.zeros_like(ang)
    tabs = (jnp.where(lane < D_ROPE, cos, zero),
            jnp.where(lane < ROPE_HALF, -sin, zero),
            jnp.where((lane >= ROPE_HALF) & (lane < D_ROPE), sin, zero))
    for i, tab in enumerate(tabs):
        tab_ref[0, :, i * LANES:(i + 1) * LANES] = tab
        tab_t_ref[0, 0, i * LANES:(i + 1) * LANES, :] = tab.T


def _apply_rope(a, tab):
    c, s1, s2 = (tab[:, i * LANES:(i + 1) * LANES] for i in range(3))
    return a * c + pltpu.roll(a, LANES - ROPE_HALF, 1) * s1 + pltpu.roll(a, ROPE_HALF, 1) * s2


def _apply_rope_t(a, tab_t):
    c, s1, s2 = (tab_t[i * LANES:(i + 1) * LANES] for i in range(3))
    up = jnp.concatenate([a[ROPE_HALF:], a[:ROPE_HALF]], axis=0)
    down = jnp.concatenate([a[-ROPE_HALF:], a[:-ROPE_HALF]], axis=0)
    return a * c + up * s1 + down * s2


def _cast_specs(n_exp, d, ff, steps, step_of):
    assert steps >= n_exp, "too few grid steps to carry the weight cast"
    k = max(c for c in range(1, steps // n_exp + 1) if ff % (c * LANES) == 0)
    cw = ff // k

    def chunk(*idx):
        c = jnp.minimum(step_of(*idx), n_exp * k - 1)
        return c // k, c % k

    gate_up = pl.BlockSpec((1, d, cw), lambda *idx: (chunk(*idx)[0], 0, chunk(*idx)[1]))
    down = pl.BlockSpec((1, cw, d), lambda *idx: (chunk(*idx)[0], chunk(*idx)[1], 0))
    return gate_up, down


def _cast_side_job(src_refs, dst_refs):
    for src, dst in zip(src_refs, dst_refs):
        dst[...] = src[...].astype(BF16)


def _pool_kernel(x_ref, mod_ref, g_ref, win_ref, band_ref, wgrp_ref, scale_ref, wout_ref,
                 fg_ref, fu_ref, fd_ref, pos_ref, freq_ref,
                 o_ref, fg_out_ref, fu_out_ref, fd_out_ref, tab_ref, tab_t_ref, uext_ref,
                 *, ts, gw):
    s = pl.program_id(1)

    @pl.when(s == 0)
    def _():
        uext_ref[0:POOL_HALO, :] = jnp.zeros((POOL_HALO, uext_ref.shape[1]), BF16)

    x = x_ref[0]
    mod = mod_ref[0]
    h = (_rms_scale(x) * g_ref[...]) * (1.0 + mod[1:2]) + mod[0:1]
    u = _dot(h.astype(BF16), win_ref[...])
    uext_ref[POOL_HALO:POOL_HALO + ts, :] = u.astype(BF16)

    t = s * ts + lax.broadcasted_iota(jnp.int32, (ts, 1), 0)
    zs = []
    for g, w in enumerate(POOL_WINDOWS):
        cols = slice(g * gw, (g + 1) * gw)
        rb = band_ref.shape[1]
        wsum = jnp.concatenate(
            [_dot(band_ref[g], uext_ref[r0:r0 + rb + POOL_HALO, cols]) for r0 in range(0, ts, rb)],
            axis=0)
        cnt = jnp.minimum(t + 1, w).astype(F32)
        p = wsum / cnt - u[:, cols]
        zs.append(_dot(p.astype(BF16), wgrp_ref[g]))
    z = jnp.concatenate(zs, axis=-1) * scale_ref[...]
    y = _dot(z.astype(BF16), wout_ref[...])
    o_ref[0] = x + mod[2:3] * y

    uext_ref[0:POOL_HALO, :] = uext_ref[ts:ts + POOL_HALO, :]

    _cast_side_job((fg_ref, fu_ref, fd_ref), (fg_out_ref, fu_out_ref, fd_out_ref))
    _rope_table_side_job(pos_ref, freq_ref, tab_ref, tab_t_ref)


def _pool_mixer(x, mod, g, w_in, band, w_grp, scale, w_out, fg, fu, fd, pos_f, freq_row, ts):
    b, s, d = x.shape
    ng, gw, _ = w_grp.shape
    n_s = s // ts
    gate_up, down = _cast_specs(1, d, fg.shape[2], b * n_s, lambda i, j: i * n_s + j)
    const2 = lambda i, j: (0, 0)
    const3 = lambda i, j: (0, 0, 0)
    return pl.pallas_call(
        functools.partial(_pool_kernel, ts=ts, gw=gw),
        grid=(b, s // ts),
        in_specs=[
            pl.BlockSpec((1, ts, d), lambda i, j: (i, j, 0)),
            pl.BlockSpec((1, 6, d), lambda i, j: (i, 0, 0)),
            pl.BlockSpec((1, d), const2),
            pl.BlockSpec((d, d), const2),
            pl.BlockSpec(band.shape, const3),
            pl.BlockSpec((ng, gw, gw), const3),
            pl.BlockSpec((1, d), const2),
            pl.BlockSpec((d, d), const2),
            gate_up, gate_up, down,
            pl.BlockSpec((1, ts, 1), lambda i, j: (i, j, 0)),
            pl.BlockSpec((1, LANES), const2),
        ],
        out_specs=[pl.BlockSpec((1, ts, d), lambda i, j: (i, j, 0)), gate_up, gate_up, down,
                   pl.BlockSpec((1, ts, 3 * LANES), lambda i, j: (i, j, 0)),
                   pl.BlockSpec((1, 1, 3 * LANES, ts), lambda i, j: (i, j, 0, 0))],
        out_shape=[jax.ShapeDtypeStruct((b, s, d), F32),
                   jax.ShapeDtypeStruct(fg.shape, BF16),
                   jax.ShapeDtypeStruct(fu.shape, BF16),
                   jax.ShapeDtypeStruct(fd.shape, BF16),
                   jax.ShapeDtypeStruct((b, s, 3 * LANES), F32),
                   jax.ShapeDtypeStruct((b, s // ts, 3 * LANES, ts), F32)],
        scratch_shapes=[pltpu.VMEM((POOL_HALO + ts, d), BF16)],
        compiler_params=_cparams(("arbitrary", "arbitrary")),
        name="pool_mixer",
    )(x, mod, g, w_in, band, w_grp, scale, w_out, fg, fu, fd, pos_f, freq_row)


def _ffn_kernel(x_ref, mod_ref, g_ref, wg_ref, wu_ref, wd_ref, eg_ref, eu_ref, ed_ref,
                o_ref, eg_out_ref, eu_out_ref, ed_out_ref, h_ref, acc_ref):
    f = pl.program_id(1)

    @pl.when(f == 0)
    def _():
        mod = mod_ref[0]
        h = (_rms_scale(x_ref[...]) * g_ref[...]) * (1.0 + mod[4:5]) + mod[3:4]
        h_ref[...] = h.astype(BF16)
        acc_ref[...] = jnp.zeros_like(acc_ref)

    h = h_ref[...]
    gate = _dot(h, wg_ref[0])
    up = _dot(h, wu_ref[0])
    a = (_silu(gate) * up).astype(BF16)
    acc_ref[...] += _dot(a, wd_ref[0])
    _cast_side_job((eg_ref, eu_ref, ed_ref), (eg_out_ref, eu_out_ref, ed_out_ref))

    @pl.when(f == pl.num_programs(1) - 1)
    def _():
        o_ref[...] = x_ref[...] + mod_ref[0][5:6] * acc_ref[...]


def _dense_ffn(x2, mod, g, wg, wu, wd, eg, eu, ed, tm, tf, rows_per_batch):
    n, d = x2.shape
    ff = wg.shape[2]
    tpb = rows_per_batch // tm
    n_f = ff // tf
    n_exp, _, eff = eg.shape
    gate_up, down = _cast_specs(n_exp, d, eff, (n // tm) * n_f, lambda i, f: i * n_f + f)
    return pl.pallas_call(
        _ffn_kernel,
        grid=(n // tm, n_f),
        in_specs=[
            pl.BlockSpec((tm, d), lambda i, f: (i, 0)),
            pl.BlockSpec((1, 6, d), lambda i, f: (i // tpb, 0, 0)),
            pl.BlockSpec((1, d), lambda i, f: (0, 0)),
            pl.BlockSpec((1, d, tf), lambda i, f: (0, 0, f)),
            pl.BlockSpec((1, d, tf), lambda i, f: (0, 0, f)),
            pl.BlockSpec((1, tf, d), lambda i, f: (0, f, 0)),
            gate_up, gate_up, down,
        ],
        out_specs=[pl.BlockSpec((tm, d), lambda i, f: (i, 0)), gate_up, gate_up, down],
        out_shape=[jax.ShapeDtypeStruct((n, d), F32),
                   jax.ShapeDtypeStruct(eg.shape, BF16),
                   jax.ShapeDtypeStruct(eu.shape, BF16),
                   jax.ShapeDtypeStruct(ed.shape, BF16)],
        scratch_shapes=[pltpu.VMEM((tm, d), BF16), pltpu.VMEM((tm, d), F32)],
        compiler_params=_cparams(("arbitrary", "arbitrary")),
        name="dense_ffn",
    )(x2, mod, g, wg, wu, wd, eg, eu, ed)


def _dot_nt(a, b):
    return lax.dot_general(a, b, (((1,), (1,)), ((), ())), preferred_element_type=F32)


def _qkv_kernel(x_ref, mod_ref, kvmod_ref, gmix_ref, gkv_ref, wdq_ref, qg_ref, wqn_t_ref,
                wqr_t_ref, wdkv_ref, ckvg_ref, wkr_ref, wuk_ref, wuv_t_ref, tab_ref, tab_t_ref,
                qt_ref, k_ref, vt_ref):
    x = x_ref[0]
    xn = _rms_scale(x)
    mod = mod_ref[0]
    kvmod = kvmod_ref[0]
    hq = ((xn * gmix_ref[...]) * (1.0 + mod[1:2]) + mod[0:1]).astype(BF16)
    hk = ((xn * gkv_ref[...]) * (1.0 + kvmod[1:2]) + kvmod[0:1]).astype(BF16)

    cq = (_rms_scale(_dot(hq, wdq_ref[...])) * qg_ref[...]).astype(BF16)
    qn_t = _dot_nt(wqn_t_ref[...], cq) * Q_SCALE
    qr_t = _dot_nt(wqr_t_ref[...], cq) * Q_SCALE
    tab_t = tab_t_ref[0, 0]

    ckv = (_rms_scale(_dot(hk, wdkv_ref[...])) * ckvg_ref[...]).astype(BF16)
    kn = _dot(ckv, wuk_ref[...])
    vt_ref[0, 0] = _dot_nt(wuv_t_ref[...], ckv).astype(BF16)
    kr = _apply_rope(_dot(hk, wkr_ref[...]), tab_ref[0]).astype(BF16)

    for h in range(N_HEADS):
        lo = h * HEAD_PAD
        hs = slice(h * LANES, (h + 1) * LANES)
        qt_ref[0, 0, lo:lo + LANES, :] = qn_t[hs].astype(BF16)
        qt_ref[0, 0, lo + LANES:lo + HEAD_PAD, :] = _apply_rope_t(qr_t[hs], tab_t).astype(BF16)
        k_ref[0, :, lo:lo + LANES] = kn[:, hs].astype(BF16)
        k_ref[0, :, lo + LANES:lo + HEAD_PAD] = kr


def _qkv_proj(x, mod, kvmod, gmix, gkv, wdq, qg, wqn_t, wqr_t, wdkv, ckvg, wkr, wuk, wuv_t,
              tab, tab_t, ts):
    b, s, d = x.shape
    row = lambda i, j: (i, j, 0)
    tile = lambda i, j: (i, j, 0, 0)
    per_b = lambda i, j: (i, 0, 0)

    def full(a):
        return pl.BlockSpec(a.shape, lambda i, j: (0,) * a.ndim)

    weights = (gmix, gkv, wdq, qg, wqn_t, wqr_t, wdkv, ckvg, wkr, wuk, wuv_t)
    return pl.pallas_call(
        _qkv_kernel,
        grid=(b, s // ts),
        in_specs=[pl.BlockSpec((1, ts, d), row),
                  pl.BlockSpec((1, 6, d), per_b),
                  pl.BlockSpec((1, 2, d), per_b)]
                 + [full(a) for a in weights]
                 + [pl.BlockSpec((1, ts, 3 * LANES), row),
                    pl.BlockSpec((1, 1, 3 * LANES, ts), tile)],
        out_specs=[pl.BlockSpec((1, 1, N_HEADS * HEAD_PAD, ts), tile),
                   pl.BlockSpec((1, ts, N_HEADS * HEAD_PAD), row),
                   pl.BlockSpec((1, 1, N_HEADS * D_V, ts), tile)],
        out_shape=[jax.ShapeDtypeStruct((b, s // ts, N_HEADS * HEAD_PAD, ts), BF16),
                   jax.ShapeDtypeStruct((b, s, N_HEADS * HEAD_PAD), BF16),
                   jax.ShapeDtypeStruct((b, s // ts, N_HEADS * D_V, ts), BF16)],
        compiler_params=_cparams(("arbitrary", "arbitrary")),
        name="qkv_proj",
    )(x, mod, kvmod, *weights, tab, tab_t)


def _attn_kernel(qt_ref, k_ref, vt_ref, o_ref, *, tq, tk):
    ts = qt_ref.shape[3]
    s = k_ref.shape[1]
    key = lax.broadcasted_iota(jnp.int32, (tk, tq), 0)
    qry = lax.broadcasted_iota(jnp.int32, (tk, tq), 1)
    last_block = lambda qi: ((qi + 1) * tq - 1) // tk
    blocks = [(qi, j) for qi in range(s // tq) for j in range(last_block(qi) + 1)]

    def tile(ref, start, size):
        return ref[0, start // ts, :, start % ts:start % ts + size]

    def scores_t(qi, j):
        st = _dot(k_ref[0, j * tk:(j + 1) * tk, :], tile(qt_ref, qi * tq, tq))
        if (j + 1) * tk - 1 > qi * tq:
            st = jnp.where(key + (j * tk - qi * tq) <= qry, st, -jnp.inf)
        return st

    ahead = [scores_t(*blk) for blk in blocks[:ATTN_LOOKAHEAD]]
    for n, (qi, j) in enumerate(blocks):
        st = ahead.pop(0)
        if n + ATTN_LOOKAHEAD < len(blocks):
            ahead.append(scores_t(*blocks[n + ATTN_LOOKAHEAD]))
        if j == 0:
            m = jnp.full((1, tq), -jnp.inf, F32)
            l = jnp.zeros((1, tq), F32)
            acc = jnp.zeros((D_V, tq), F32)
        m_new = jnp.maximum(m, jnp.max(st, axis=0, keepdims=True))
        alpha = jnp.exp2(m - m_new)
        pt = jnp.exp2(st - m_new)
        l = alpha * l + jnp.sum(pt, axis=0, keepdims=True)
        acc = alpha * acc + _dot(tile(vt_ref, j * tk, tk), pt.astype(BF16))
        m = m_new
        if j == last_block(qi):
            o_ref[0, qi * tq:(qi + 1) * tq, :] = (acc / l).T.astype(BF16)


def _attention(qt, k, vt, ts):
    b, s, _ = k.shape
    return pl.pallas_call(
        functools.partial(_attn_kernel, tq=min(ts, ATTN_QUERY_TILE), tk=min(ts, ATTN_KEY_BLOCK)),
        grid=(b, N_HEADS),
        in_specs=[
            pl.BlockSpec((1, s // ts, HEAD_PAD, ts), lambda i, h: (i, 0, h, 0)),
            pl.BlockSpec((1, s, HEAD_PAD), lambda i, h: (i, 0, h)),
            pl.BlockSpec((1, s // ts, D_V, ts), lambda i, h: (i, 0, h, 0)),
        ],
        out_specs=pl.BlockSpec((1, s, D_V), lambda i, h: (i, 0, h)),
        out_shape=jax.ShapeDtypeStruct((b, s, N_HEADS * D_V), BF16),
        compiler_params=_cparams(("arbitrary", "arbitrary")),
        name="flash_attn",
    )(qt, k, vt)


def _attn_out_kernel(o_ref, x_ref, mod_ref, wo_ref, g_ref, wr_ref, ltri_ref,
                     x_out_ref, h_out_ref, rcol_ref, rrow_ref, cnt_ref, carry_ref,
                     *, n_exp, tiles_per_block):
    i = pl.program_id(0)

    @pl.when(i % tiles_per_block == 0)
    def _():
        carry_ref[...] = jnp.zeros_like(carry_ref)

    mod = mod_ref[0]
    x = x_ref[...] + mod[2:3] * _dot(o_ref[...], wo_ref[...])
    x_out_ref[...] = x
    h = (_rms_scale(x) * g_ref[...]) * (1.0 + mod[4:5]) + mod[3:4]
    h_hi = h.astype(BF16)
    h_out_ref[...] = h_hi

    tm = x.shape[0]
    h_lo = (h - h_hi.astype(F32)).astype(BF16)
    parts = _dot(jnp.concatenate([h_hi, h_lo], axis=0), wr_ref[...])
    logits = (parts[:tm, :LANES] + parts[:tm, LANES:]) + (parts[tm:, :LANES] + parts[tm:, LANES:])
    lane = lax.broadcasted_iota(jnp.int32, (tm, LANES), 1)
    lane_f = lane.astype(F32)

    def first_argmax(vals):
        m = jnp.max(vals, axis=-1, keepdims=True)
        idx = jnp.min(jnp.where(vals == m, lane_f, float(LANES)), axis=-1, keepdims=True)
        return m, idx.astype(jnp.int32)

    logits = jnp.where(lane < n_exp, logits, -jnp.inf)
    m1, i1 = first_argmax(logits)
    m2, i2 = first_argmax(jnp.where(lane == i1, -jnp.inf, logits))
    ex = jnp.exp(m2 - m1)
    w1 = 1.0 / (1.0 + ex)
    w2 = ex / (1.0 + ex)

    mask = jnp.where((lane == i1) | (lane == i2), 1.0, 0.0)
    ranks = _dot(ltri_ref[...], mask.astype(BF16)) + carry_ref[0:1]
    incl = ranks + mask
    rank1 = jnp.sum(jnp.where(lane == i1, ranks, 0.0), axis=-1, keepdims=True)
    rank2 = jnp.sum(jnp.where(lane == i2, ranks, 0.0), axis=-1, keepdims=True)
    carry_ref[0:1] = incl[tm - 1:tm]
    n_sub = tm // rrow_ref.shape[2]
    sub = rrow_ref.shape[2]
    cnt_ref[0] = jnp.zeros(cnt_ref.shape[1:], F32)
    for j in range(n_sub):
        cnt_ref[0, j:j + 1] = incl[(j + 1) * sub - 1:(j + 1) * sub]

    fields = (i1.astype(F32), i2.astype(F32), rank1, rank2, w1, w2)
    rcol = jnp.zeros((tm, LANES), F32)
    for k, val in enumerate(fields):
        rcol = jnp.where(lane == k, val, rcol)
    rcol_ref[...] = rcol
    rrow = rcol.T
    for j in range(n_sub):
        rrow_ref[j] = rrow[0:ROUTE_FIELDS, j * sub:(j + 1) * sub]


def _attn_out(o2, x2, mod, wo, g, wr, n_exp, ltri, tm, sub, rows_per_block):
    n, d = x2.shape
    tpb = rows_per_block // tm
    n_sub = tm // sub
    return pl.pallas_call(
        functools.partial(_attn_out_kernel, n_exp=n_exp, tiles_per_block=tpb),
        grid=(n // tm,),
        in_specs=[
            pl.BlockSpec((tm, o2.shape[1]), lambda i: (i, 0)),
            pl.BlockSpec((tm, d), lambda i: (i, 0)),
            pl.BlockSpec((1, 6, d), lambda i: (i // tpb, 0, 0)),
            pl.BlockSpec(wo.shape, lambda i: (0, 0)),
            pl.BlockSpec((1, d), lambda i: (0, 0)),
            pl.BlockSpec(wr.shape, lambda i: (0, 0)),
            pl.BlockSpec((tm, tm), lambda i: (0, 0)),
        ],
        out_specs=[pl.BlockSpec((tm, d), lambda i: (i, 0)),
                   pl.BlockSpec((tm, d), lambda i: (i, 0)),
                   pl.BlockSpec((tm, LANES), lambda i: (i, 0)),
                   pl.BlockSpec((n_sub, ROUTE_FIELDS, sub), lambda i: (i, 0, 0)),
                   pl.BlockSpec((1, 8, LANES), lambda i: (i, 0, 0))],
        out_shape=[jax.ShapeDtypeStruct((n, d), F32),
                   jax.ShapeDtypeStruct((n, d), BF16),
                   jax.ShapeDtypeStruct((n, LANES), F32),
                   jax.ShapeDtypeStruct((n // sub, ROUTE_FIELDS, sub), F32),
                   jax.ShapeDtypeStruct((n // tm, 8, LANES), F32)],
        scratch_shapes=[pltpu.VMEM((8, LANES), F32)],
        compiler_params=_cparams(("arbitrary",)),
        name="attn_out_router",
    )(o2, x2, mod, wo, g, wr, ltri)


def _dispatch_kernel(off_ref, lo_ref, hi_ref, cexp_ref, h_ref, rrow_ref, xg_ref, gs_ref,
                     *, n_exp, sub, chunks_per_step, chunks_per_block):
    blk = pl.program_id(0)
    g = pl.program_id(1)
    d = h_ref.shape[1]
    n_sub = rrow_ref.shape[0]
    win = min(DISPATCH_WINDOW, n_sub)
    iota = lax.broadcasted_iota(jnp.int32, (DISPATCH_CHUNK, 1), 0)

    def contribution(info, sb):
        row, exp_id, exp_off = info
        r = rrow_ref[sb]
        first = r[0:1] == exp_id
        second = r[1:2] == exp_id
        pos = jnp.where(first, r[2:3] + exp_off, jnp.where(second, r[3:4] + exp_off, -1.0))
        hit = pos == row
        onehot = jnp.where(hit, 1.0, 0.0).astype(BF16)
        tok = h_ref[pl.ds(pl.multiple_of(sb * sub, sub), sub), :]
        gate = jnp.sum(jnp.where(hit, jnp.where(first, r[4:5], r[5:6]), 0.0),
                       axis=-1, keepdims=True)
        return _dot(onehot, tok), gate

    def chunk_info(j):
        c = g * chunks_per_step + j
        t = blk * chunks_per_block + c
        row = (c * DISPATCH_CHUNK + iota).astype(F32)
        exp_id = cexp_ref[t]
        exp_off = off_ref[blk * n_exp + exp_id].astype(F32)
        start = jnp.minimum(lo_ref[t], n_sub - win)
        return (row, exp_id.astype(F32), exp_off), start, hi_ref[t]

    for j in range(chunks_per_step):
        info, start, _ = chunk_info(j)
        acc = jnp.zeros((DISPATCH_CHUNK, d), F32)
        gate = jnp.zeros((DISPATCH_CHUNK, 1), F32)
        for k in range(win):
            a, b = contribution(info, start + k)
            acc += a
            gate += b
        rows = slice(j * DISPATCH_CHUNK, (j + 1) * DISPATCH_CHUNK)
        xg_ref[rows, :] = acc.astype(BF16)
        gs_ref[rows, :] = gate

    def extra(j, _):
        info, start, last = chunk_info(j)
        rows = pl.ds(pl.multiple_of(j * DISPATCH_CHUNK, DISPATCH_CHUNK), DISPATCH_CHUNK)

        def add(sb, _):
            a, b = contribution(info, sb)
            xg_ref[rows, :] = (xg_ref[rows, :].astype(F32) + a).astype(BF16)
            gs_ref[rows, :] = gs_ref[rows, :] + b
            return 0

        return lax.fori_loop(start + win, last + 1, add, 0)

    lax.fori_loop(0, chunks_per_step, extra, 0)


def _dispatch(off, lo, hi, cexp, h, rrow, n_blocks, rows_per_block, rcap, n_exp, sub):
    n, d = h.shape
    n_sub = rows_per_block // sub
    step_rows = _pick(rcap, 1024)
    cps = step_rows // DISPATCH_CHUNK
    cpb = rcap // DISPATCH_CHUNK
    grid_spec = pltpu.PrefetchScalarGridSpec(
        num_scalar_prefetch=4,
        grid=(n_blocks, rcap // step_rows),
        in_specs=[
            pl.BlockSpec((rows_per_block, d), lambda b, g, *_: (b, 0)),
            pl.BlockSpec((n_sub, ROUTE_FIELDS, sub), lambda b, g, *_: (b, 0, 0)),
        ],
        out_specs=[
            pl.BlockSpec((step_rows, d), lambda b, g, *_: (b * (rcap // step_rows) + g, 0)),
            pl.BlockSpec((step_rows, 1), lambda b, g, *_: (b * (rcap // step_rows) + g, 0)),
        ],
    )
    return pl.pallas_call(
        functools.partial(_dispatch_kernel, n_exp=n_exp, sub=sub, chunks_per_step=cps,
                          chunks_per_block=cpb),
        grid_spec=grid_spec,
        out_shape=[jax.ShapeDtypeStruct((n_blocks * rcap, d), BF16),
                   jax.ShapeDtypeStruct((n_blocks * rcap, 1), F32)],
        compiler_params=_cparams(("arbitrary", "arbitrary")),
        name="moe_dispatch",
    )(off, lo, hi, cexp, h, rrow)


def _expert_ffn_kernel(perm_ref, pexp_ref, pfill_ref, x_ref, gs_ref, wg_ref, wu_ref, wd_ref,
                       y_ref, *, tf):
    w = pl.program_id(0)
    fill = pfill_ref[w]
    ff = wg_ref.shape[2]

    def ffn(rows):
        x = x_ref[rows, :]
        gs = gs_ref[rows, :]
        acc = jnp.zeros((x.shape[0], y_ref.shape[1]), F32)
        for f0 in range(0, ff, tf):
            gate = _dot(x, wg_ref[0, :, f0:f0 + tf])
            up = _dot(x, wu_ref[0, :, f0:f0 + tf])
            a = ((_silu(gate) * up) * gs).astype(BF16)
            acc = acc + _dot(a, wd_ref[0, f0:f0 + tf, :])
        return acc.astype(BF16)

    half = PIECE // 2

    @pl.when(fill == 2)
    def _():
        y_ref[...] = ffn(slice(0, PIECE))

    @pl.when(fill == 1)
    def _():
        y_ref[0:half, :] = ffn(slice(0, half))
        y_ref[half:PIECE, :] = jnp.zeros((half, y_ref.shape[1]), BF16)

    @pl.when(fill == 0)
    def _():
        y_ref[...] = jnp.zeros(y_ref.shape, BF16)


def _expert_ffn(perm, pexp, pfill, xg, gs, wg, wu, wd, tf):
    rows, d = xg.shape
    n_exp, _, ff = wg.shape
    grid_spec = pltpu.PrefetchScalarGridSpec(
        num_scalar_prefetch=3,
        grid=(rows // PIECE,),
        in_specs=[
            pl.BlockSpec((PIECE, d), lambda w, perm, pexp, pfill: (perm[w], 0)),
            pl.BlockSpec((PIECE, 1), lambda w, perm, pexp, pfill: (perm[w], 0)),
            pl.BlockSpec((1, d, ff), lambda w, perm, pexp, pfill: (pexp[w], 0, 0)),
            pl.BlockSpec((1, d, ff), lambda w, perm, pexp, pfill: (pexp[w], 0, 0)),
            pl.BlockSpec((1, ff, d), lambda w, perm, pexp, pfill: (pexp[w], 0, 0)),
        ],
        out_specs=pl.BlockSpec((PIECE, d), lambda w, perm, pexp, pfill: (perm[w], 0)),
    )
    return pl.pallas_call(
        functools.partial(_expert_ffn_kernel, tf=tf),
        grid_spec=grid_spec,
        out_shape=jax.ShapeDtypeStruct((rows, d), BF16),
        compiler_params=_cparams(("arbitrary",)),
        name="expert_ffn",
    )(perm, pexp, pfill, xg, gs, wg, wu, wd)


def _combine_kernel(off_ref, start_ref, end_ref, yg_ref, rcol_ref, x_ref, mod_ref, gfin_ref, o_ref,
                    acc_ref, *, n_exp, sub, subs_per_block):
    blk = pl.program_id(0)
    t = pl.program_id(1)
    tm, d = x_ref.shape
    col = lax.broadcasted_iota(jnp.int32, (1, COMBINE_WINDOW), 1)
    align = COMBINE_WINDOW // 2

    def token_info(s):
        rc = rcol_ref[s * sub:(s + 1) * sub, :]
        return tuple(jnp.broadcast_to(rc[:, k:k + 1], (sub, COMBINE_WINDOW)) for k in range(4))

    def gathered(info, e, w0):
        i1, i2, rank1, rank2 = info
        rank = jnp.where(i1 == e, rank1, jnp.where(i2 == e, rank2, -float(COMBINE_WINDOW)))
        in_region = (w0 - off_ref[blk * n_exp + e] + col).astype(F32)
        onehot = jnp.where(rank == in_region, 1.0, 0.0).astype(BF16)
        return _dot(onehot, yg_ref[pl.ds(pl.multiple_of(w0, align), COMBINE_WINDOW), :])

    def first_window(s, e):
        idx = (blk * subs_per_block + t * (tm // sub) + s) * n_exp + e
        w0 = lax.shift_right_logical(start_ref[idx], align.bit_length() - 1) * align
        return w0, end_ref[idx]

    for s in range(tm // sub):
        info = token_info(s)
        acc = jnp.zeros((sub, d), F32)
        for e in range(n_exp):
            acc += gathered(info, e, first_window(s, e)[0])
        acc_ref[s * sub:(s + 1) * sub, :] = acc

    for s in range(tm // sub):
        info = token_info(s)
        for e in range(n_exp):
            w0, end = first_window(s, e)
            n_more = lax.shift_right_logical(
                jnp.maximum(end - w0 - 1, 0), COMBINE_WINDOW.bit_length() - 1)

            def add(k, _, info=info, e=e, w0=w0, s=s):
                rows = slice(s * sub, (s + 1) * sub)
                acc_ref[rows, :] = acc_ref[rows, :] + gathered(info, e, w0 + k * COMBINE_WINDOW)
                return 0

            lax.fori_loop(1, n_more + 1, add, 0)

    x = x_ref[...] + mod_ref[0][5:6] * acc_ref[...]
    o_ref[...] = _rms_scale(x) * gfin_ref[...]


def _combine(off, lo, hi, yg, rcol, x2, mod, gfin, n_blocks, rows_per_block, rcap, n_exp, sub, tm):
    n, d = x2.shape
    tpb = rows_per_block // tm
    grid_spec = pltpu.PrefetchScalarGridSpec(
        num_scalar_prefetch=3,
        grid=(n_blocks, tpb),
        in_specs=[
            pl.BlockSpec((rcap, d), lambda b, t, *_: (b, 0)),
            pl.BlockSpec((tm, LANES), lambda b, t, *_: (b * tpb + t, 0)),
            pl.BlockSpec((tm, d), lambda b, t, *_: (b * tpb + t, 0)),
            pl.BlockSpec((1, 6, d), lambda b, t, *_: (b, 0, 0)),
            pl.BlockSpec((1, d), lambda b, t, *_: (0, 0)),
        ],
        out_specs=pl.BlockSpec((tm, d), lambda b, t, *_: (b * tpb + t, 0)),
        scratch_shapes=[pltpu.VMEM((tm, d), F32)],
    )
    return pl.pallas_call(
        functools.partial(_combine_kernel, n_exp=n_exp, sub=sub,
                          subs_per_block=rows_per_block // sub),
        grid_spec=grid_spec,
        out_shape=jax.ShapeDtypeStruct((n, d), F32),
        compiler_params=_cparams(("arbitrary", "arbitrary")),
        name="moe_combine",
    )(off, lo, hi, yg, rcol, x2, mod, gfin)


def _routing_tables(cnt, n_blocks, n_sub, n_exp, rcap):
    cum_incl = cnt.astype(jnp.int32)
    cum_excl = jnp.concatenate([jnp.zeros_like(cum_incl[:, :1]), cum_incl[:, :-1]], axis=1)
    total = cum_incl[:, -1]
    psz = (total + PIECE - 1) // PIECE * PIECE
    off = jnp.cumsum(psz, axis=1) - psz
    start = off[:, None, :] + cum_excl
    end = off[:, None, :] + cum_incl

    c0 = jnp.arange(rcap // DISPATCH_CHUNK, dtype=jnp.int32)[None, :, None, None] * DISPATCH_CHUNK
    hit = ((start[:, None] < c0 + DISPATCH_CHUNK) & (end[:, None] > c0)
           & (end[:, None] > start[:, None])).any(-1)
    sub_ids = jnp.arange(n_sub, dtype=jnp.int32)
    d_lo = jnp.min(jnp.where(hit, sub_ids, n_sub), axis=-1)
    d_hi = jnp.max(jnp.where(hit, sub_ids, -1), axis=-1)
    d_lo = jnp.where(d_hi < 0, 0, d_lo)


    ppb = rcap // PIECE
    prow = jnp.arange(ppb, dtype=jnp.int32)[None, :, None] * PIECE
    inside = (prow >= off[:, None, :]) & (prow < (off + psz)[:, None, :])
    p_exp = jnp.argmax(inside, axis=-1).astype(jnp.int32)
    used = inside.any(-1)
    left = jnp.take_along_axis(total + off, p_exp, axis=1) - prow[..., 0]
    p_fill = jnp.where(used & (left > 0), jnp.where(left > PIECE // 2, 2, 1), 0).astype(jnp.int32)
    key = jnp.where(p_fill > 0, p_exp, n_exp).reshape(-1)
    perm = jnp.argsort(key, stable=True).astype(jnp.int32)
    p_exp_sorted = jnp.minimum(key[perm], n_exp - 1).astype(jnp.int32)
    d_exp = jnp.repeat(p_exp, PIECE // DISPATCH_CHUNK, axis=1).reshape(-1)
    return (off.reshape(-1).astype(jnp.int32), d_lo.reshape(-1), d_hi.reshape(-1), d_exp,
            start.reshape(-1).astype(jnp.int32), end.reshape(-1).astype(jnp.int32),
            perm, p_exp_sorted, p_fill.reshape(-1)[perm])


def _pick(n, pref):
    t = min(n, pref)
    while n % t:
        t //= 2
    return t


def kernel(x, c, positions, ada_w, ada_b, mix_norm_g, ffn_norm_g, pool_w_in, pool_w_grp,
           pool_scale, pool_w_out, kv_ada_w, kv_ada_b, kv_norm_g, w_dkv, ckv_norm_g, w_kr,
           w_uk, w_uv, w_dq, q_norm_g, w_uq, w_o, ffn_w_gate, ffn_w_up, ffn_w_down,
           router_w, moe_w_gate, moe_w_up, moe_w_down, final_norm_g):
    b, s, d = x.shape
    n = b * s
    depth = ada_w.shape[0]
    assert depth == 2 and pool_w_in.shape[0] == 1 and w_dq.shape[0] == 1
    assert b <= 8 and d % LANES == 0

    ts = _pick(s, 512)
    tm = _pick(s, 1024)
    tf = _pick(ffn_w_gate.shape[-1], 512)

    c_pad = jnp.zeros((8, d), F32).at[:b].set(c)
    mods = _adaln(c_pad, ada_w, ada_b[:, None, :], _pick(6 * d, 1536))
    mods = mods[:, :b].reshape(depth, b, 6, d)
    kvmod = _adaln(c_pad, kv_ada_w[None], kv_ada_b[None, None, :], _pick(2 * d, 1024))
    kvmod = kvmod[0, :b].reshape(b, 2, d)

    r = jnp.arange(min(ts, POOL_ROW_BLOCK))[:, None]
    cidx = jnp.arange(min(ts, POOL_ROW_BLOCK) + POOL_HALO)[None, :] - POOL_HALO
    band = jnp.stack([((cidx <= r) & (cidx > r - w)) for w in POOL_WINDOWS]).astype(BF16)
    inv_freq = ROPE_THETA ** (-jnp.arange(0, D_ROPE, 2, dtype=F32) / D_ROPE)
    freq_row = jnp.concatenate([inv_freq, inv_freq, jnp.zeros((LANES - D_ROPE,), F32)])[None, :]
    x1, fwg, fwu, fwd, rope_tab, rope_tab_t = _pool_mixer(
        x, mods[0], mix_norm_g[0:1], pool_w_in[0].astype(BF16), band, pool_w_grp[0].astype(BF16),
        pool_scale[0:1], pool_w_out[0].astype(BF16), ffn_w_gate, ffn_w_up, ffn_w_down,
        positions.astype(F32)[..., None], freq_row, ts)

    x2, ewg, ewu, ewd = _dense_ffn(x1.reshape(n, d), mods[0], ffn_norm_g[0:1], fwg, fwu, fwd,
                                   moe_w_gate[0], moe_w_up[0], moe_w_down[0], tm, tf, s)


    q_lora = w_dq.shape[-1]
    wuq = w_uq[0].reshape(q_lora, N_HEADS, D_NOPE + D_ROPE)
    wqn_t = wuq[:, :, :D_NOPE].reshape(q_lora, N_HEADS * D_NOPE).T.astype(BF16)
    wqr = jnp.pad(wuq[:, :, D_NOPE:], ((0, 0), (0, 0), (0, LANES - D_ROPE)))
    wqr_t = wqr.reshape(q_lora, N_HEADS * LANES).T.astype(BF16)
    wkr = jnp.pad(w_kr, ((0, 0), (0, LANES - D_ROPE))).astype(BF16)
    qt, k, vt = _qkv_proj(
        x2.reshape(b, s, d), mods[1], kvmod, mix_norm_g[1:2], kv_norm_g[None, :],
        w_dq[0].astype(BF16), q_norm_g[0:1], wqn_t, wqr_t, w_dkv.astype(BF16),
        ckv_norm_g[None, :], wkr, w_uk.astype(BF16), w_uv.T.astype(BF16), rope_tab, rope_tab_t, ts)
    o = _attention(qt, k, vt, ts)

    n_exp = router_w.shape[-1]
    sub = _pick(s, ROUTE_SUB)
    n_sub = s // sub
    rcap = 2 * s + n_exp * PIECE
    ltri = (jnp.arange(ts)[None, :] < jnp.arange(ts)[:, None]).astype(BF16)
    assert n_exp <= LANES
    wr = jnp.pad(router_w[0], ((0, 0), (0, LANES - n_exp)))
    wr_hi = wr.astype(BF16)
    wr_split = jnp.concatenate([wr_hi, (wr - wr_hi.astype(F32)).astype(BF16)], axis=1)
    x3, h3, rcol, rrow, cnt = _attn_out(o.reshape(n, N_HEADS * D_V), x2, mods[1],
                                        w_o[0].astype(BF16), ffn_norm_g[1:2], wr_split, n_exp,
                                        ltri, ts, sub, s)
    cnt = cnt[:, :ts // sub, :n_exp].reshape(b, n_sub, n_exp)
    off, d_lo, d_hi, d_exp, c_lo, c_hi, perm, p_exp, p_fill = _routing_tables(cnt, b, n_sub, n_exp, rcap)
    xg, gs = _dispatch(off, d_lo, d_hi, d_exp, h3, rrow, b, s, rcap, n_exp, sub)
    yg = _expert_ffn(perm, p_exp, p_fill, xg, gs, ewg, ewu, ewd, tf)
    out = _combine(off, c_lo, c_hi, yg, rcol, x3, mods[1], final_norm_g[None, :], b, s, rcap,
                   n_exp, sub, ts)
    return out.reshape(b, s, d)
```

```python
import functools
import math

import jax
import jax.numpy as jnp
from jax import lax
from jax.experimental import pallas as pl
from jax.experimental.pallas import tpu as pltpu

F32 = jnp.float32
BF16 = jnp.bfloat16

N_HEADS = 8
D_NOPE = 128
D_ROPE = 64
D_V = 128
HEAD_PAD = 256
ROPE_HALF = D_ROPE // 2
ROPE_THETA = 10000.0
ATTN_SCALE = 1.0 / math.sqrt(D_NOPE + D_ROPE)
Q_SCALE = ATTN_SCALE * math.log2(math.e)
ATTN_QUERY_TILE = 512
ATTN_KEY_BLOCK = 512
ATTN_LOOKAHEAD = 2
POOL_WINDOWS = (2, 4, 8, 16)
POOL_HALO = 16
POOL_ROW_BLOCK = 128
EPS = 1e-6

ROUTE_SUB = 256
DISPATCH_CHUNK = 128
DISPATCH_WINDOW = 4
COMBINE_WINDOW = 256
PIECE = 256
ROUTE_FIELDS = 8

LANES = 128
VMEM_LIMIT_BYTES = 56 * 1024 * 1024


def _cparams(sem):
    return pltpu.CompilerParams(dimension_semantics=sem, vmem_limit_bytes=VMEM_LIMIT_BYTES)


def _dot(a, b):
    return jnp.dot(a, b, preferred_element_type=F32)


def _rms_scale(x):
    return x * lax.rsqrt(jnp.mean(x * x, axis=-1, keepdims=True) + EPS)


def _silu(x):
    return x * jax.nn.sigmoid(x)


def _adaln_kernel(c_ref, w_ref, b_ref, o_ref):
    sc = _silu(c_ref[...])
    o_ref[0] = jnp.dot(sc, w_ref[0], preferred_element_type=F32,
                       precision=lax.Precision.HIGHEST) + b_ref[0]


def _adaln(c_pad, w, b, tn):
    nl, d, m = w.shape
    return pl.pallas_call(
        _adaln_kernel,
        grid=(nl, m // tn),
        in_specs=[
            pl.BlockSpec((8, d), lambda l, j: (0, 0)),
            pl.BlockSpec((1, d, tn), lambda l, j: (l, 0, j)),
            pl.BlockSpec((1, 1, tn), lambda l, j: (l, 0, j)),
        ],
        out_specs=pl.BlockSpec((1, 8, tn), lambda l, j: (l, 0, j)),
        out_shape=jax.ShapeDtypeStruct((nl, 8, m), F32),
        compiler_params=_cparams(("arbitrary", "arbitrary")),
        name="adaln",
    )(c_pad, w, b)


def _rope_table_side_job(pos_ref, freq_ref, tab_ref, tab_t_ref):
    ang = pos_ref[0] * freq_ref[...]
    lane = lax.broadcasted_iota(jnp.int32, ang.shape, 1)
    cos = jnp.cos(ang)
    sin = jnp.sin(ang)
    zero = jnp.zeros_like(ang)
    tabs = (jnp.where(lane < D_ROPE, cos, zero),
            jnp.where(lane < ROPE_HALF, -sin, zero),
            jnp.where((lane >= ROPE_HALF) & (lane < D_ROPE), sin, zero))
    for i, tab in enumerate(tabs):
        tab_ref[0, :, i * LANES:(i + 1) * LANES] = tab
        tab_t_ref[0, 0, i * LANES:(i + 1) * LANES, :] = tab.T


def _apply_rope(a, tab):
    c, s1, s2 = (tab[:, i * LANES:(i + 1) * LANES] for i in range(3))
    return a * c + pltpu.roll(a, LANES - ROPE_HALF, 1) * s1 + pltpu.roll(a, ROPE_HALF, 1) * s2


def _cast_specs(n_exp, d, ff, steps, step_of):
    assert steps >= n_exp, "too few grid steps to carry the weight cast"
    k = max(c for c in range(1, steps // n_exp + 1) if ff % (c * LANES) == 0)
    cw = ff // k

    def chunk(*idx):
        c = jnp.minimum(step_of(*idx), n_exp * k - 1)
        return c // k, c % k

    gate_up = pl.BlockSpec((1, d, cw), lambda *idx: (chunk(*idx)[0], 0, chunk(*idx)[1]))
    down = pl.BlockSpec((1, cw, d), lambda *idx: (chunk(*idx)[0], chunk(*idx)[1], 0))
    return gate_up, down


def _cast_side_job(src_refs, dst_refs):
    for src, dst in zip(src_refs, dst_refs):
        dst[...] = src[...].astype(BF16)


def _pool_kernel(x_ref, mod_ref, g_ref, win_ref, band_ref, wgrp_ref, scale_ref, wout_ref,
                 fg_ref, fu_ref, fd_ref, pos_ref, freq_ref, gffn_ref,
                 o_ref, fg_out_ref, fu_out_ref, fd_out_ref, tab_ref, tab_t_ref, h_out_ref, uext_ref,
                 *, ts, gw):
    s = pl.program_id(1)

    @pl.when(s == 0)
    def _():
        uext_ref[0:POOL_HALO, :] = jnp.zeros((POOL_HALO, uext_ref.shape[1]), BF16)

    x = x_ref[0]
    mod = mod_ref[0]
    h = (_rms_scale(x) * g_ref[...]) * (1.0 + mod[1:2]) + mod[0:1]
    u = _dot(h.astype(BF16), win_ref[...])
    uext_ref[POOL_HALO:POOL_HALO + ts, :] = u.astype(BF16)

    t = s * ts + lax.broadcasted_iota(jnp.int32, (ts, 1), 0)
    zs = []
    for g, w in enumerate(POOL_WINDOWS):
        cols = slice(g * gw, (g + 1) * gw)
        rb = band_ref.shape[1]
        wsum = jnp.concatenate(
            [_dot(band_ref[g], uext_ref[r0:r0 + rb + POOL_HALO, cols]) for r0 in range(0, ts, rb)],
            axis=0)
        cnt = jnp.minimum(t + 1, w).astype(F32)
        p = wsum / cnt - u[:, cols]
        zs.append(_dot(p.astype(BF16), wgrp_ref[g]))
    z = jnp.concatenate(zs, axis=-1) * scale_ref[...]
    y = _dot(z.astype(BF16), wout_ref[...])
    x_out = x + mod[2:3] * y
    o_ref[0] = x_out
    h_out_ref[0] = ((_rms_scale(x_out) * gffn_ref[...]) * (1.0 + mod[4:5]) + mod[3:4]).astype(BF16)

    uext_ref[0:POOL_HALO, :] = uext_ref[ts:ts + POOL_HALO, :]

    _cast_side_job((fg_ref, fu_ref, fd_ref), (fg_out_ref, fu_out_ref, fd_out_ref))
    _rope_table_side_job(pos_ref, freq_ref, tab_ref, tab_t_ref)


def _pool_mixer(x, mod, g, w_in, band, w_grp, scale, w_out, fg, fu, fd, pos_f, freq_row, g_ffn,
                ts):
    b, s, d = x.shape
    ng, gw, _ = w_grp.shape
    n_s = s // ts
    gate_up, down = _cast_specs(1, d, fg.shape[2], b * n_s, lambda i, j: i * n_s + j)
    const2 = lambda i, j: (0, 0)
    const3 = lambda i, j: (0, 0, 0)
    return pl.pallas_call(
        functools.partial(_pool_kernel, ts=ts, gw=gw),
        grid=(b, s // ts),
        in_specs=[
            pl.BlockSpec((1, ts, d), lambda i, j: (i, j, 0)),
            pl.BlockSpec((1, 6, d), lambda i, j: (i, 0, 0)),
            pl.BlockSpec((1, d), const2),
            pl.BlockSpec((d, d), const2),
            pl.BlockSpec(band.shape, const3),
            pl.BlockSpec((ng, gw, gw), const3),
            pl.BlockSpec((1, d), const2),
            pl.BlockSpec((d, d), const2),
            gate_up, gate_up, down,
            pl.BlockSpec((1, ts, 1), lambda i, j: (i, j, 0)),
            pl.BlockSpec((1, LANES), const2),
            pl.BlockSpec((1, d), const2),
        ],
        out_specs=[pl.BlockSpec((1, ts, d), lambda i, j: (i, j, 0)), gate_up, gate_up, down,
                   pl.BlockSpec((1, ts, 3 * LANES), lambda i, j: (i, j, 0)),
                   pl.BlockSpec((1, 1, 3 * LANES, ts), lambda i, j: (i, j, 0, 0)),
                   pl.BlockSpec((1, ts, d), lambda i, j: (i, j, 0))],
        out_shape=[jax.ShapeDtypeStruct((b, s, d), F32),
                   jax.ShapeDtypeStruct(fg.shape, BF16),
                   jax.ShapeDtypeStruct(fu.shape, BF16),
                   jax.ShapeDtypeStruct(fd.shape, BF16),
                   jax.ShapeDtypeStruct((b, s, 3 * LANES), F32),
                   jax.ShapeDtypeStruct((b, s // ts, 3 * LANES, ts), F32),
                   jax.ShapeDtypeStruct((b, s, d), BF16)],
        scratch_shapes=[pltpu.VMEM((POOL_HALO + ts, d), BF16)],
        compiler_params=_cparams(("arbitrary", "arbitrary")),
        name="pool_mixer",
    )(x, mod, g, w_in, band, w_grp, scale, w_out, fg, fu, fd, pos_f, freq_row, g_ffn)


def _ffn_kernel(x_ref, h_ref, mod_ref, wg_ref, wu_ref, wd_ref, eg_ref, eu_ref, ed_ref,
                o_ref, eg_out_ref, eu_out_ref, ed_out_ref, acc_ref):
    f = pl.program_id(1)

    @pl.when(f == 0)
    def _():
        acc_ref[...] = jnp.zeros_like(acc_ref)

    h = h_ref[...]
    gate = _dot(h, wg_ref[0])
    up = _dot(h, wu_ref[0])
    a = (_silu(gate) * up).astype(BF16)
    acc_ref[...] += _dot(a, wd_ref[0])
    _cast_side_job((eg_ref, eu_ref, ed_ref), (eg_out_ref, eu_out_ref, ed_out_ref))

    @pl.when(f == pl.num_programs(1) - 1)
    def _():
        o_ref[...] = x_ref[...] + mod_ref[0][5:6] * acc_ref[...]


def _dense_ffn(x2, h2, mod, wg, wu, wd, eg, eu, ed, tm, tf, rows_per_batch):
    n, d = x2.shape
    ff = wg.shape[2]
    tpb = rows_per_batch // tm
    n_f = ff // tf
    n_exp, _, eff = eg.shape
    gate_up, down = _cast_specs(n_exp, d, eff, (n // tm) * n_f, lambda i, f: i * n_f + f)
    return pl.pallas_call(
        _ffn_kernel,
        grid=(n // tm, n_f),
        in_specs=[
            pl.BlockSpec((tm, d), lambda i, f: (i, 0)),
            pl.BlockSpec((tm, d), lambda i, f: (i, 0)),
            pl.BlockSpec((1, 6, d), lambda i, f: (i // tpb, 0, 0)),
            pl.BlockSpec((1, d, tf), lambda i, f: (0, 0, f)),
            pl.BlockSpec((1, d, tf), lambda i, f: (0, 0, f)),
            pl.BlockSpec((1, tf, d), lambda i, f: (0, f, 0)),
            gate_up, gate_up, down,
        ],
        out_specs=[pl.BlockSpec((tm, d), lambda i, f: (i, 0)), gate_up, gate_up, down],
        out_shape=[jax.ShapeDtypeStruct((n, d), F32),
                   jax.ShapeDtypeStruct(eg.shape, BF16),
                   jax.ShapeDtypeStruct(eu.shape, BF16),
                   jax.ShapeDtypeStruct(ed.shape, BF16)],
        scratch_shapes=[pltpu.VMEM((tm, d), F32)],
        compiler_params=_cparams(("arbitrary", "arbitrary")),
        name="dense_ffn",
    )(x2, h2, mod, wg, wu, wd, eg, eu, ed)


def _dot_nt(a, b):
    return lax.dot_general(a, b, (((1,), (1,)), ((), ())), preferred_element_type=F32)


def _qkv_kernel(x_ref, mod_ref, kvmod_ref, gmix_ref, gkv_ref, wdq_ref, qg_ref, wqn_t_ref,
                wqr_t_ref, wdkv_ref, ckvg_ref, wkr_ref, wuk_ref, wuv_t_ref, tab_ref, tab_t_ref,
                qt_ref, k_ref, vt_ref):
    x = x_ref[0]
    xn = _rms_scale(x)
    mod = mod_ref[0]
    kvmod = kvmod_ref[0]
    hq = ((xn * gmix_ref[...]) * (1.0 + mod[1:2]) + mod[0:1]).astype(BF16)
    hk = ((xn * gkv_ref[...]) * (1.0 + kvmod[1:2]) + kvmod[0:1]).astype(BF16)

    cq = (_rms_scale(_dot(hq, wdq_ref[...])) * qg_ref[...]).astype(BF16)
    qn_t = _dot_nt(wqn_t_ref[...], cq) * Q_SCALE
    qr_t = _dot_nt(wqr_t_ref[...], cq) * Q_SCALE
    cos_t = tab_t_ref[0, 0, 0:ROPE_HALF, :]
    sin_t = tab_t_ref[0, 0, 2 * LANES + ROPE_HALF:2 * LANES + D_ROPE, :]
    zero_rows = jnp.zeros((HEAD_PAD - LANES - D_ROPE, x.shape[0]), BF16)

    ckv = (_rms_scale(_dot(hk, wdkv_ref[...])) * ckvg_ref[...]).astype(BF16)
    kn = _dot(ckv, wuk_ref[...])
    vt_ref[0, 0] = _dot_nt(wuv_t_ref[...], ckv).astype(BF16)
    kr = _apply_rope(_dot(hk, wkr_ref[...]), tab_ref[0]).astype(BF16)

    for h in range(N_HEADS):
        lo = h * HEAD_PAD
        hs = slice(h * LANES, (h + 1) * LANES)
        qt_ref[0, 0, lo:lo + LANES, :] = qn_t[hs].astype(BF16)
        x1 = qr_t[h * D_ROPE:h * D_ROPE + ROPE_HALF]
        x2 = qr_t[h * D_ROPE + ROPE_HALF:(h + 1) * D_ROPE]
        ro = lo + LANES
        qt_ref[0, 0, ro:ro + ROPE_HALF, :] = (x1 * cos_t - x2 * sin_t).astype(BF16)
        qt_ref[0, 0, ro + ROPE_HALF:ro + D_ROPE, :] = (x2 * cos_t + x1 * sin_t).astype(BF16)
        qt_ref[0, 0, ro + D_ROPE:lo + HEAD_PAD, :] = zero_rows
        k_ref[0, :, lo:lo + LANES] = kn[:, hs].astype(BF16)
        k_ref[0, :, lo + LANES:lo + HEAD_PAD] = kr


def _qkv_proj(x, mod, kvmod, gmix, gkv, wdq, qg, wqn_t, wqr_t, wdkv, ckvg, wkr, wuk, wuv_t,
              tab, tab_t, ts):
    b, s, d = x.shape
    row = lambda i, j: (i, j, 0)
    tile = lambda i, j: (i, j, 0, 0)
    per_b = lambda i, j: (i, 0, 0)

    def full(a):
        return pl.BlockSpec(a.shape, lambda i, j: (0,) * a.ndim)

    weights = (gmix, gkv, wdq, qg, wqn_t, wqr_t, wdkv, ckvg, wkr, wuk, wuv_t)
    return pl.pallas_call(
        _qkv_kernel,
        grid=(b, s // ts),
        in_specs=[pl.BlockSpec((1, ts, d), row),
                  pl.BlockSpec((1, 6, d), per_b),
                  pl.BlockSpec((1, 2, d), per_b)]
                 + [full(a) for a in weights]
                 + [pl.BlockSpec((1, ts, 3 * LANES), row),
                    pl.BlockSpec((1, 1, 3 * LANES, ts), tile)],
        out_specs=[pl.BlockSpec((1, 1, N_HEADS * HEAD_PAD, ts), tile),
                   pl.BlockSpec((1, ts, N_HEADS * HEAD_PAD), row),
                   pl.BlockSpec((1, 1, N_HEADS * D_V, ts), tile)],
        out_shape=[jax.ShapeDtypeStruct((b, s // ts, N_HEADS * HEAD_PAD, ts), BF16),
                   jax.ShapeDtypeStruct((b, s, N_HEADS * HEAD_PAD), BF16),
                   jax.ShapeDtypeStruct((b, s // ts, N_HEADS * D_V, ts), BF16)],
        compiler_params=_cparams(("arbitrary", "arbitrary")),
        name="qkv_proj",
    )(x, mod, kvmod, *weights, tab, tab_t)


def _attn_kernel(qt_ref, k_ref, vt_ref, o_ref, *, tq, tk):
    ts = qt_ref.shape[3]
    s = k_ref.shape[1]
    key = lax.broadcasted_iota(jnp.int32, (tk, tq), 0)
    qry = lax.broadcasted_iota(jnp.int32, (tk, tq), 1)
    last_block = lambda qi: ((qi + 1) * tq - 1) // tk
    blocks = [(qi, j) for qi in range(s // tq) for j in range(last_block(qi) + 1)]

    def tile(ref, start, size):
        return ref[0, start // ts, :, start % ts:start % ts + size]

    def scores_t(qi, j):
        st = _dot(k_ref[0, j * tk:(j + 1) * tk, :], tile(qt_ref, qi * tq, tq))
        if (j + 1) * tk - 1 > qi * tq:
            st = jnp.where(key + (j * tk - qi * tq) <= qry, st, -jnp.inf)
        return st

    ahead = [scores_t(*blk) for blk in blocks[:ATTN_LOOKAHEAD]]
    for n, (qi, j) in enumerate(blocks):
        st = ahead.pop(0)
        if n + ATTN_LOOKAHEAD < len(blocks):
            ahead.append(scores_t(*blocks[n + ATTN_LOOKAHEAD]))
        if j == 0:
            m = jnp.full((1, tq), -jnp.inf, F32)
            l = jnp.zeros((1, tq), F32)
            acc = jnp.zeros((D_V, tq), F32)
        m_new = jnp.maximum(m, jnp.max(st, axis=0, keepdims=True))
        alpha = jnp.exp2(m - m_new)
        pt = jnp.exp2(st - m_new)
        l = alpha * l + jnp.sum(pt, axis=0, keepdims=True)
        acc = alpha * acc + _dot(tile(vt_ref, j * tk, tk), pt.astype(BF16))
        m = m_new
        if j == last_block(qi):
            o_ref[0, qi * tq:(qi + 1) * tq, :] = (acc / l).T.astype(BF16)


def _attention(qt, k, vt, ts):
    b, s, _ = k.shape
    return pl.pallas_call(
        functools.partial(_attn_kernel, tq=min(ts, ATTN_QUERY_TILE), tk=min(ts, ATTN_KEY_BLOCK)),
        grid=(b, N_HEADS),
        in_specs=[
            pl.BlockSpec((1, s // ts, HEAD_PAD, ts), lambda i, h: (i, 0, h, 0)),
            pl.BlockSpec((1, s, HEAD_PAD), lambda i, h: (i, 0, h)),
            pl.BlockSpec((1, s // ts, D_V, ts), lambda i, h: (i, 0, h, 0)),
        ],
        out_specs=pl.BlockSpec((1, s, D_V), lambda i, h: (i, 0, h)),
        out_shape=jax.ShapeDtypeStruct((b, s, N_HEADS * D_V), BF16),
        compiler_params=_cparams(("arbitrary", "arbitrary")),
        name="flash_attn",
    )(qt, k, vt)


def _attn_out_kernel(o_ref, x_ref, mod_ref, wo_ref, g_ref, wr_ref, ltri_ref,
                     x_out_ref, h_out_ref, rcol_ref, rrow_ref, cnt_ref, carry_ref,
                     *, n_exp, tiles_per_block):
    i = pl.program_id(0)

    @pl.when(i % tiles_per_block == 0)
    def _():
        carry_ref[...] = jnp.zeros_like(carry_ref)

    mod = mod_ref[0]
    x = x_ref[...] + mod[2:3] * _dot(o_ref[...], wo_ref[...])
    x_out_ref[...] = x
    h = (_rms_scale(x) * g_ref[...]) * (1.0 + mod[4:5]) + mod[3:4]
    h_hi = h.astype(BF16)
    h_out_ref[...] = h_hi

    tm = x.shape[0]
    h_lo = (h - h_hi.astype(F32)).astype(BF16)
    parts = _dot(jnp.concatenate([h_hi, h_lo], axis=0), wr_ref[...])
    logits = (parts[:tm, :LANES] + parts[:tm, LANES:]) + (parts[tm:, :LANES] + parts[tm:, LANES:])
    lane = lax.broadcasted_iota(jnp.int32, (tm, LANES), 1)
    lane_f = lane.astype(F32)

    def first_argmax(vals):
        m = jnp.max(vals, axis=-1, keepdims=True)
        idx = jnp.min(jnp.where(vals == m, lane_f, float(LANES)), axis=-1, keepdims=True)
        return m, idx.astype(jnp.int32)

    logits = jnp.where(lane < n_exp, logits, -jnp.inf)
    m1, i1 = first_argmax(logits)
    m2, i2 = first_argmax(jnp.where(lane == i1, -jnp.inf, logits))
    ex = jnp.exp(m2 - m1)
    w1 = 1.0 / (1.0 + ex)
    w2 = ex / (1.0 + ex)

    mask = jnp.where((lane == i1) | (lane == i2), 1.0, 0.0)
    ranks = _dot(ltri_ref[...], mask.astype(BF16)) + carry_ref[0:1]
    incl = ranks + mask
    rank1 = jnp.sum(jnp.where(lane == i1, ranks, 0.0), axis=-1, keepdims=True)
    rank2 = jnp.sum(jnp.where(lane == i2, ranks, 0.0), axis=-1, keepdims=True)
    carry_ref[0:1] = incl[tm - 1:tm]
    n_sub = tm // rrow_ref.shape[2]
    sub = rrow_ref.shape[2]
    cnt_ref[0] = jnp.zeros(cnt_ref.shape[1:], F32)
    for j in range(n_sub):
        cnt_ref[0, j:j + 1] = incl[(j + 1) * sub - 1:(j + 1) * sub]

    fields = (i1.astype(F32), i2.astype(F32), rank1, rank2, w1, w2)
    rcol = jnp.zeros((tm, LANES), F32)
    for k, val in enumerate(fields):
        rcol = jnp.where(lane == k, val, rcol)
    rcol_ref[...] = rcol
    rrow = rcol.T
    for j in range(n_sub):
        rrow_ref[j] = rrow[0:ROUTE_FIELDS, j * sub:(j + 1) * sub]


def _attn_out(o2, x2, mod, wo, g, wr, n_exp, ltri, tm, sub, rows_per_block):
    n, d = x2.shape
    tpb = rows_per_block // tm
    n_sub = tm // sub
    return pl.pallas_call(
        functools.partial(_attn_out_kernel, n_exp=n_exp, tiles_per_block=tpb),
        grid=(n // tm,),
        in_specs=[
            pl.BlockSpec((tm, o2.shape[1]), lambda i: (i, 0)),
            pl.BlockSpec((tm, d), lambda i: (i, 0)),
            pl.BlockSpec((1, 6, d), lambda i: (i // tpb, 0, 0)),
            pl.BlockSpec(wo.shape, lambda i: (0, 0)),
            pl.BlockSpec((1, d), lambda i: (0, 0)),
            pl.BlockSpec(wr.shape, lambda i: (0, 0)),
            pl.BlockSpec((tm, tm), lambda i: (0, 0)),
        ],
        out_specs=[pl.BlockSpec((tm, d), lambda i: (i, 0)),
                   pl.BlockSpec((tm, d), lambda i: (i, 0)),
                   pl.BlockSpec((tm, LANES), lambda i: (i, 0)),
                   pl.BlockSpec((n_sub, ROUTE_FIELDS, sub), lambda i: (i, 0, 0)),
                   pl.BlockSpec((1, 8, LANES), lambda i: (i, 0, 0))],
        out_shape=[jax.ShapeDtypeStruct((n, d), F32),
                   jax.ShapeDtypeStruct((n, d), BF16),
                   jax.ShapeDtypeStruct((n, LANES), F32),
                   jax.ShapeDtypeStruct((n // sub, ROUTE_FIELDS, sub), F32),
                   jax.ShapeDtypeStruct((n // tm, 8, LANES), F32)],
        scratch_shapes=[pltpu.VMEM((8, LANES), F32)],
        compiler_params=_cparams(("arbitrary",)),
        name="attn_out_router",
    )(o2, x2, mod, wo, g, wr, ltri)


def _dispatch_kernel(off_ref, lo_ref, hi_ref, cexp_ref, h_ref, rrow_ref, xg_ref, gs_ref,
                     *, n_exp, sub, chunks_per_step, chunks_per_block):
    blk = pl.program_id(0)
    g = pl.program_id(1)
    d = h_ref.shape[1]
    n_sub = rrow_ref.shape[0]
    win = min(DISPATCH_WINDOW, n_sub)
    iota = lax.broadcasted_iota(jnp.int32, (DISPATCH_CHUNK, 1), 0)

    def contribution(info, sb):
        row, exp_id, exp_off = info
        r = rrow_ref[sb]
        first = r[0:1] == exp_id
        second = r[1:2] == exp_id
        pos = jnp.where(first, r[2:3] + exp_off, jnp.where(second, r[3:4] + exp_off, -1.0))
        hit = pos == row
        onehot = jnp.where(hit, 1.0, 0.0).astype(BF16)
        tok = h_ref[pl.ds(pl.multiple_of(sb * sub, sub), sub), :]
        gate = jnp.sum(jnp.where(hit, jnp.where(first, r[4:5], r[5:6]), 0.0),
                       axis=-1, keepdims=True)
        return _dot(onehot, tok), gate

    def chunk_info(j):
        c = g * chunks_per_step + j
        t = blk * chunks_per_block + c
        row = (c * DISPATCH_CHUNK + iota).astype(F32)
        exp_id = cexp_ref[t]
        exp_off = off_ref[blk * n_exp + exp_id].astype(F32)
        start = jnp.minimum(lo_ref[t], n_sub - win)
        return (row, exp_id.astype(F32), exp_off), start, hi_ref[t]

    for j in range(chunks_per_step):
        info, start, _ = chunk_info(j)
        acc = jnp.zeros((DISPATCH_CHUNK, d), F32)
        gate = jnp.zeros((DISPATCH_CHUNK, 1), F32)
        for k in range(win):
            a, b = contribution(info, start + k)
            acc += a
            gate += b
        rows = slice(j * DISPATCH_CHUNK, (j + 1) * DISPATCH_CHUNK)
        xg_ref[rows, :] = acc.astype(BF16)
        gs_ref[rows, :] = gate

    def extra(j, _):
        info, start, last = chunk_info(j)
        rows = pl.ds(pl.multiple_of(j * DISPATCH_CHUNK, DISPATCH_CHUNK), DISPATCH_CHUNK)

        def add(sb, _):
            a, b = contribution(info, sb)
            xg_ref[rows, :] = (xg_ref[rows, :].astype(F32) + a).astype(BF16)
            gs_ref[rows, :] = gs_ref[rows, :] + b
            return 0

        return lax.fori_loop(start + win, last + 1, add, 0)

    lax.fori_loop(0, chunks_per_step, extra, 0)


def _dispatch(off, lo, hi, cexp, h, rrow, n_blocks, rows_per_block, rcap, n_exp, sub):
    n, d = h.shape
    n_sub = rows_per_block // sub
    step_rows = _pick(rcap, 1024)
    cps = step_rows // DISPATCH_CHUNK
    cpb = rcap // DISPATCH_CHUNK
    grid_spec = pltpu.PrefetchScalarGridSpec(
        num_scalar_prefetch=4,
        grid=(n_blocks, rcap // step_rows),
        in_specs=[
            pl.BlockSpec((rows_per_block, d), lambda b, g, *_: (b, 0)),
            pl.BlockSpec((n_sub, ROUTE_FIELDS, sub), lambda b, g, *_: (b, 0, 0)),
        ],
        out_specs=[
            pl.BlockSpec((step_rows, d), lambda b, g, *_: (b * (rcap // step_rows) + g, 0)),
            pl.BlockSpec((step_rows, 1), lambda b, g, *_: (b * (rcap // step_rows) + g, 0)),
        ],
    )
    return pl.pallas_call(
        functools.partial(_dispatch_kernel, n_exp=n_exp, sub=sub, chunks_per_step=cps,
                          chunks_per_block=cpb),
        grid_spec=grid_spec,
        out_shape=[jax.ShapeDtypeStruct((n_blocks * rcap, d), BF16),
                   jax.ShapeDtypeStruct((n_blocks * rcap, 1), F32)],
        compiler_params=_cparams(("arbitrary", "arbitrary")),
        name="moe_dispatch",
    )(off, lo, hi, cexp, h, rrow)


def _expert_ffn_kernel(perm_ref, pexp_ref, pfill_ref, x_ref, gs_ref, wg_ref, wu_ref, wd_ref,
                       y_ref, *, tf):
    w = pl.program_id(0)
    fill = pfill_ref[w]
    ff = wg_ref.shape[2]

    def ffn(rows):
        x = x_ref[rows, :]
        gs = gs_ref[rows, :]
        acc = jnp.zeros((x.shape[0], y_ref.shape[1]), F32)
        for f0 in range(0, ff, tf):
            gate = _dot(x, wg_ref[0, :, f0:f0 + tf])
            up = _dot(x, wu_ref[0, :, f0:f0 + tf])
            a = ((_silu(gate) * up) * gs).astype(BF16)
            acc = acc + _dot(a, wd_ref[0, f0:f0 + tf, :])
        return acc.astype(BF16)

    half = PIECE // 2

    @pl.when(fill == 2)
    def _():
        y_ref[...] = ffn(slice(0, PIECE))

    @pl.when(fill == 1)
    def _():
        y_ref[0:half, :] = ffn(slice(0, half))
        y_ref[half:PIECE, :] = jnp.zeros((half, y_ref.shape[1]), BF16)

    @pl.when(fill == 0)
    def _():
        y_ref[...] = jnp.zeros(y_ref.shape, BF16)


def _expert_ffn(perm, pexp, pfill, xg, gs, wg, wu, wd, tf):
    rows, d = xg.shape
    n_exp, _, ff = wg.shape
    grid_spec = pltpu.PrefetchScalarGridSpec(
        num_scalar_prefetch=3,
        grid=(rows // PIECE,),
        in_specs=[
            pl.BlockSpec((PIECE, d), lambda w, perm, pexp, pfill: (perm[w], 0)),
            pl.BlockSpec((PIECE, 1), lambda w, perm, pexp, pfill: (perm[w], 0)),
            pl.BlockSpec((1, d, ff), lambda w, perm, pexp, pfill: (pexp[w], 0, 0)),
            pl.BlockSpec((1, d, ff), lambda w, perm, pexp, pfill: (pexp[w], 0, 0)),
            pl.BlockSpec((1, ff, d), lambda w, perm, pexp, pfill: (pexp[w], 0, 0)),
        ],
        out_specs=pl.BlockSpec((PIECE, d), lambda w, perm, pexp, pfill: (perm[w], 0)),
    )
    return pl.pallas_call(
        functools.partial(_expert_ffn_kernel, tf=tf),
        grid_spec=grid_spec,
        out_shape=jax.ShapeDtypeStruct((rows, d), BF16),
        compiler_params=_cparams(("arbitrary",)),
        name="expert_ffn",
    )(perm, pexp, pfill, xg, gs, wg, wu, wd)


def _combine_kernel(off_ref, start_ref, end_ref, yg_ref, rcol_ref, x_ref, mod_ref, gfin_ref, o_ref,
                    acc_ref, *, n_exp, sub, subs_per_block):
    blk = pl.program_id(0)
    t = pl.program_id(1)
    tm, d = x_ref.shape
    col = lax.broadcasted_iota(jnp.int32, (1, COMBINE_WINDOW), 1)
    align = COMBINE_WINDOW // 2

    def token_info(s):
        rc = rcol_ref[s * sub:(s + 1) * sub, :]
        return tuple(jnp.broadcast_to(rc[:, k:k + 1], (sub, COMBINE_WINDOW)) for k in range(4))

    def gathered(info, e, w0):
        i1, i2, rank1, rank2 = info
        rank = jnp.where(i1 == e, rank1, jnp.where(i2 == e, rank2, -float(COMBINE_WINDOW)))
        in_region = (w0 - off_ref[blk * n_exp + e] + col).astype(F32)
        onehot = jnp.where(rank == in_region, 1.0, 0.0).astype(BF16)
        return _dot(onehot, yg_ref[pl.ds(pl.multiple_of(w0, align), COMBINE_WINDOW), :])

    def first_window(s, e):
        idx = (blk * subs_per_block + t * (tm // sub) + s) * n_exp + e
        w0 = lax.shift_right_logical(start_ref[idx], align.bit_length() - 1) * align
        return w0, end_ref[idx]

    for s in range(tm // sub):
        info = token_info(s)
        acc = jnp.zeros((sub, d), F32)
        for e in range(n_exp):
            acc += gathered(info, e, first_window(s, e)[0])
        acc_ref[s * sub:(s + 1) * sub, :] = acc

    for s in range(tm // sub):
        info = token_info(s)
        for e in range(n_exp):
            w0, end = first_window(s, e)
            n_more = lax.shift_right_logical(
                jnp.maximum(end - w0 - 1, 0), COMBINE_WINDOW.bit_length() - 1)

            def add(k, _, info=info, e=e, w0=w0, s=s):
                rows = slice(s * sub, (s + 1) * sub)
                acc_ref[rows, :] = acc_ref[rows, :] + gathered(info, e, w0 + k * COMBINE_WINDOW)
                return 0

            lax.fori_loop(1, n_more + 1, add, 0)

    x = x_ref[...] + mod_ref[0][5:6] * acc_ref[...]
    o_ref[...] = _rms_scale(x) * gfin_ref[...]


def _combine(off, lo, hi, yg, rcol, x2, mod, gfin, n_blocks, rows_per_block, rcap, n_exp, sub, tm):
    n, d = x2.shape
    tpb = rows_per_block // tm
    grid_spec = pltpu.PrefetchScalarGridSpec(
        num_scalar_prefetch=3,
        grid=(n_blocks, tpb),
        in_specs=[
            pl.BlockSpec((rcap, d), lambda b, t, *_: (b, 0)),
            pl.BlockSpec((tm, LANES), lambda b, t, *_: (b * tpb + t, 0)),
            pl.BlockSpec((tm, d), lambda b, t, *_: (b * tpb + t, 0)),
            pl.BlockSpec((1, 6, d), lambda b, t, *_: (b, 0, 0)),
            pl.BlockSpec((1, d), lambda b, t, *_: (0, 0)),
        ],
        out_specs=pl.BlockSpec((tm, d), lambda b, t, *_: (b * tpb + t, 0)),
        scratch_shapes=[pltpu.VMEM((tm, d), F32)],
    )
    return pl.pallas_call(
        functools.partial(_combine_kernel, n_exp=n_exp, sub=sub,
                          subs_per_block=rows_per_block // sub),
        grid_spec=grid_spec,
        out_shape=jax.ShapeDtypeStruct((n, d), F32),
        compiler_params=_cparams(("arbitrary", "arbitrary")),
        name="moe_combine",
    )(off, lo, hi, yg, rcol, x2, mod, gfin)


def _routing_tables(cnt, n_blocks, n_sub, n_exp, rcap):
    cum_incl = cnt.astype(jnp.int32)
    cum_excl = jnp.concatenate([jnp.zeros_like(cum_incl[:, :1]), cum_incl[:, :-1]], axis=1)
    total = cum_incl[:, -1]
    psz = (total + PIECE - 1) // PIECE * PIECE
    off = jnp.cumsum(psz, axis=1) - psz
    start = off[:, None, :] + cum_excl
    end = off[:, None, :] + cum_incl

    c0 = jnp.arange(rcap // DISPATCH_CHUNK, dtype=jnp.int32)[None, :, None, None] * DISPATCH_CHUNK
    hit = ((start[:, None] < c0 + DISPATCH_CHUNK) & (end[:, None] > c0)
           & (end[:, None] > start[:, None])).any(-1)
    sub_ids = jnp.arange(n_sub, dtype=jnp.int32)
    d_lo = jnp.min(jnp.where(hit, sub_ids, n_sub), axis=-1)
    d_hi = jnp.max(jnp.where(hit, sub_ids, -1), axis=-1)
    d_lo = jnp.where(d_hi < 0, 0, d_lo)


    ppb = rcap // PIECE
    prow = jnp.arange(ppb, dtype=jnp.int32)[None, :, None] * PIECE
    inside = (prow >= off[:, None, :]) & (prow < (off + psz)[:, None, :])
    p_exp = jnp.argmax(inside, axis=-1).astype(jnp.int32)
    used = inside.any(-1)
    left = jnp.take_along_axis(total + off, p_exp, axis=1) - prow[..., 0]
    p_fill = jnp.where(used & (left > 0), jnp.where(left > PIECE // 2, 2, 1), 0).astype(jnp.int32)
    key = jnp.where(p_fill > 0, p_exp, n_exp).reshape(-1)
    perm = jnp.argsort(key, stable=True).astype(jnp.int32)
    p_exp_sorted = jnp.minimum(key[perm], n_exp - 1).astype(jnp.int32)
    d_exp = jnp.repeat(p_exp, PIECE // DISPATCH_CHUNK, axis=1).reshape(-1)
    return (off.reshape(-1).astype(jnp.int32), d_lo.reshape(-1), d_hi.reshape(-1), d_exp,
            start.reshape(-1).astype(jnp.int32), end.reshape(-1).astype(jnp.int32),
            perm, p_exp_sorted, p_fill.reshape(-1)[perm])


def _pick(n, pref):
    t = min(n, pref)
    while n % t:
        t //= 2
    return t


def kernel(x, c, positions, ada_w, ada_b, mix_norm_g, ffn_norm_g, pool_w_in, pool_w_grp,
           pool_scale, pool_w_out, kv_ada_w, kv_ada_b, kv_norm_g, w_dkv, ckv_norm_g, w_kr,
           w_uk, w_uv, w_dq, q_norm_g, w_uq, w_o, ffn_w_gate, ffn_w_up, ffn_w_down,
           router_w, moe_w_gate, moe_w_up, moe_w_down, final_norm_g):
    b, s, d = x.shape
    n = b * s
    depth = ada_w.shape[0]
    assert depth == 2 and pool_w_in.shape[0] == 1 and w_dq.shape[0] == 1
    assert b <= 8 and d % LANES == 0

    ts = _pick(s, 512)
    tm = _pick(s, 1024)
    tf = _pick(ffn_w_gate.shape[-1], 512)

    c_pad = jnp.zeros((8, d), F32).at[:b].set(c)
    mods = _adaln(c_pad, ada_w, ada_b[:, None, :], _pick(6 * d, 1536))
    mods = mods[:, :b].reshape(depth, b, 6, d)
    kvmod = _adaln(c_pad, kv_ada_w[None], kv_ada_b[None, None, :], _pick(2 * d, 1024))
    kvmod = kvmod[0, :b].reshape(b, 2, d)

    r = jnp.arange(min(ts, POOL_ROW_BLOCK))[:, None]
    cidx = jnp.arange(min(ts, POOL_ROW_BLOCK) + POOL_HALO)[None, :] - POOL_HALO
    band = jnp.stack([((cidx <= r) & (cidx > r - w)) for w in POOL_WINDOWS]).astype(BF16)
    inv_freq = ROPE_THETA ** (-jnp.arange(0, D_ROPE, 2, dtype=F32) / D_ROPE)
    freq_row = jnp.concatenate([inv_freq, inv_freq, jnp.zeros((LANES - D_ROPE,), F32)])[None, :]
    x1, fwg, fwu, fwd, rope_tab, rope_tab_t, h1 = _pool_mixer(
        x, mods[0], mix_norm_g[0:1], pool_w_in[0].astype(BF16), band, pool_w_grp[0].astype(BF16),
        pool_scale[0:1], pool_w_out[0].astype(BF16), ffn_w_gate, ffn_w_up, ffn_w_down,
        positions.astype(F32)[..., None], freq_row, ffn_norm_g[0:1], ts)

    x2, ewg, ewu, ewd = _dense_ffn(x1.reshape(n, d), h1.reshape(n, d), mods[0], fwg, fwu, fwd,
                                   moe_w_gate[0], moe_w_up[0], moe_w_down[0], tm, tf, s)


    q_lora = w_dq.shape[-1]
    wuq = w_uq[0].reshape(q_lora, N_HEADS, D_NOPE + D_ROPE)
    wqn_t = wuq[:, :, :D_NOPE].reshape(q_lora, N_HEADS * D_NOPE).T.astype(BF16)
    wqr_t = wuq[:, :, D_NOPE:].reshape(q_lora, N_HEADS * D_ROPE).T.astype(BF16)
    wkr = jnp.pad(w_kr, ((0, 0), (0, LANES - D_ROPE))).astype(BF16)
    qt, k, vt = _qkv_proj(
        x2.reshape(b, s, d), mods[1], kvmod, mix_norm_g[1:2], kv_norm_g[None, :],
        w_dq[0].astype(BF16), q_norm_g[0:1], wqn_t, wqr_t, w_dkv.astype(BF16),
        ckv_norm_g[None, :], wkr, w_uk.astype(BF16), w_uv.T.astype(BF16), rope_tab, rope_tab_t, ts)
    o = _attention(qt, k, vt, ts)

    n_exp = router_w.shape[-1]
    sub = _pick(s, ROUTE_SUB)
    n_sub = s // sub
    rcap = 2 * s + n_exp * PIECE
    ltri = (jnp.arange(ts)[None, :] < jnp.arange(ts)[:, None]).astype(BF16)
    assert n_exp <= LANES
    wr = jnp.pad(router_w[0], ((0, 0), (0, LANES - n_exp)))
    wr_hi = wr.astype(BF16)
    wr_split = jnp.concatenate([wr_hi, (wr - wr_hi.astype(F32)).astype(BF16)], axis=1)
    x3, h3, rcol, rrow, cnt = _attn_out(o.reshape(n, N_HEADS * D_V), x2, mods[1],
                                        w_o[0].astype(BF16), ffn_norm_g[1:2], wr_split, n_exp,
                                        ltri, ts, sub, s)
    cnt = cnt[:, :ts // sub, :n_exp].reshape(b, n_sub, n_exp)
    off, d_lo, d_hi, d_exp, c_lo, c_hi, perm, p_exp, p_fill = _routing_tables(cnt, b, n_sub, n_exp, rcap)
    xg, gs = _dispatch(off, d_lo, d_hi, d_exp, h3, rrow, b, s, rcap, n_exp, sub)
    yg = _expert_ffn(perm, p_exp, p_fill, xg, gs, ewg, ewu, ewd, tf)
    out = _combine(off, c_lo, c_hi, yg, rcol, x3, mods[1], final_norm_g[None, :], b, s, rcap,
                   n_exp, sub, ts)
    return out.reshape(b, s, d)
```

```python
import functools
import math

import jax
import jax.numpy as jnp
from jax import lax
from jax.experimental import pallas as pl
from jax.experimental.pallas import tpu as pltpu

F32 = jnp.float32
BF16 = jnp.bfloat16

N_HEADS = 8
D_NOPE = 128
D_ROPE = 64
D_V = 128
HEAD_PAD = 256
ROPE_HALF = D_ROPE // 2
ROPE_THETA = 10000.0
ATTN_SCALE = 1.0 / math.sqrt(D_NOPE + D_ROPE)
Q_SCALE = ATTN_SCALE * math.log2(math.e)
ATTN_QUERY_TILE = 512
ATTN_KEY_BLOCK = 512
ATTN_LOOKAHEAD = 2
POOL_WINDOWS = (2, 4, 8, 16)
POOL_HALO = 16
POOL_ROW_BLOCK = 128
EPS = 1e-6

ROUTE_SUB = 256
DISPATCH_CHUNK = 128
DISPATCH_WINDOW = 4
COMBINE_WINDOW = 256
PIECE = 256
ROUTE_FIELDS = 8

LANES = 128
VMEM_LIMIT_BYTES = 56 * 1024 * 1024


def _cparams(sem):
    return pltpu.CompilerParams(dimension_semantics=sem, vmem_limit_bytes=VMEM_LIMIT_BYTES)


def _dot(a, b):
    return jnp.dot(a, b, preferred_element_type=F32)


def _rms_scale(x):
    return x * lax.rsqrt(jnp.mean(x * x, axis=-1, keepdims=True) + EPS)


def _silu(x):
    return x * jax.nn.sigmoid(x)


def _adaln_kernel(c_ref, w_ref, b_ref, o_ref):
    sc = _silu(c_ref[...])
    o_ref[0] = jnp.dot(sc, w_ref[0], preferred_element_type=F32,
                       precision=lax.Precision.HIGHEST) + b_ref[0]


def _adaln(c_pad, w, b, tn):
    nl, d, m = w.shape
    return pl.pallas_call(
        _adaln_kernel,
        grid=(nl, m // tn),
        in_specs=[
            pl.BlockSpec((8, d), lambda l, j: (0, 0)),
            pl.BlockSpec((1, d, tn), lambda l, j: (l, 0, j)),
            pl.BlockSpec((1, 1, tn), lambda l, j: (l, 0, j)),
        ],
        out_specs=pl.BlockSpec((1, 8, tn), lambda l, j: (l, 0, j)),
        out_shape=jax.ShapeDtypeStruct((nl, 8, m), F32),
        compiler_params=_cparams(("arbitrary", "arbitrary")),
        name="adaln",
    )(c_pad, w, b)


def _rope_table_side_job(pos_ref, freq_ref, tab_ref, tab_t_ref):
    ang = pos_ref[0] * freq_ref[...]
    lane = lax.broadcasted_iota(jnp.int32, ang.shape, 1)
    cos = jnp.cos(ang)
    sin = jnp.sin(ang)
    zero = jnp.zeros_like(ang)
    tabs = (jnp.where(lane < D_ROPE, cos, zero),
            jnp.where(lane < ROPE_HALF, -sin, zero),
            jnp.where((lane >= ROPE_HALF) & (lane < D_ROPE), sin, zero))
    for i, tab in enumerate(tabs):
        tab_ref[0, :, i * LANES:(i + 1) * LANES] = tab
        tab_t_ref[0, 0, i * LANES:(i + 1) * LANES, :] = tab.T


def _apply_rope(a, tab):
    c, s1, s2 = (tab[:, i * LANES:(i + 1) * LANES] for i in range(3))
    return a * c + pltpu.roll(a, LANES - ROPE_HALF, 1) * s1 + pltpu.roll(a, ROPE_HALF, 1) * s2


def _cast_specs(n_exp, d, ff, steps, step_of):
    assert steps >= n_exp, "too few grid steps to carry the weight cast"
    k = max(c for c in range(1, steps // n_exp + 1) if ff % (c * LANES) == 0)
    cw = ff // k

    def chunk(*idx):
        c = jnp.minimum(step_of(*idx), n_exp * k - 1)
        return c // k, c % k

    gate_up = pl.BlockSpec((1, d, cw), lambda *idx: (chunk(*idx)[0], 0, chunk(*idx)[1]))
    down = pl.BlockSpec((1, cw, d), lambda *idx: (chunk(*idx)[0], chunk(*idx)[1], 0))
    return gate_up, down


def _cast_side_job(src_refs, dst_refs):
    for src, dst in zip(src_refs, dst_refs):
        dst[...] = src[...].astype(BF16)


def _pool_kernel(x_ref, mod_ref, g_ref, win_ref, band_ref, wgrp_ref, scale_ref, wout_ref,
                 fg_ref, fu_ref, fd_ref, pos_ref, freq_ref, gffn_ref,
                 o_ref, fg_out_ref, fu_out_ref, fd_out_ref, tab_ref, tab_t_ref, h_out_ref, uext_ref,
                 *, ts, gw):
    s = pl.program_id(1)

    @pl.when(s == 0)
    def _():
        uext_ref[0:POOL_HALO, :] = jnp.zeros((POOL_HALO, uext_ref.shape[1]), BF16)

    x = x_ref[0]
    mod = mod_ref[0]
    h = (_rms_scale(x) * g_ref[...]) * (1.0 + mod[1:2]) + mod[0:1]
    u = _dot(h.astype(BF16), win_ref[...])
    uext_ref[POOL_HALO:POOL_HALO + ts, :] = u.astype(BF16)

    t = s * ts + lax.broadcasted_iota(jnp.int32, (ts, 1), 0)
    zs = []
    for g, w in enumerate(POOL_WINDOWS):
        cols = slice(g * gw, (g + 1) * gw)
        rb = band_ref.shape[1]
        wsum = jnp.concatenate(
            [_dot(band_ref[g], uext_ref[r0:r0 + rb + POOL_HALO, cols]) for r0 in range(0, ts, rb)],
            axis=0)
        cnt = jnp.minimum(t + 1, w).astype(F32)
        p = wsum / cnt - u[:, cols]
        zs.append(_dot(p.astype(BF16), wgrp_ref[g]))
    z = jnp.concatenate(zs, axis=-1) * scale_ref[...]
    y = _dot(z.astype(BF16), wout_ref[...])
    x_out = x + mod[2:3] * y
    o_ref[0] = x_out
    h_out_ref[0] = ((_rms_scale(x_out) * gffn_ref[...]) * (1.0 + mod[4:5]) + mod[3:4]).astype(BF16)

    uext_ref[0:POOL_HALO, :] = uext_ref[ts:ts + POOL_HALO, :]

    _cast_side_job((fg_ref, fu_ref, fd_ref), (fg_out_ref, fu_out_ref, fd_out_ref))
    _rope_table_side_job(pos_ref, freq_ref, tab_ref, tab_t_ref)


def _pool_mixer(x, mod, g, w_in, band, w_grp, scale, w_out, fg, fu, fd, pos_f, freq_row, g_ffn,
                ts):
    b, s, d = x.shape
    ng, gw, _ = w_grp.shape
    n_s = s // ts
    gate_up, down = _cast_specs(1, d, fg.shape[2], b * n_s, lambda i, j: i * n_s + j)
    const2 = lambda i, j: (0, 0)
    const3 = lambda i, j: (0, 0, 0)
    return pl.pallas_call(
        functools.partial(_pool_kernel, ts=ts, gw=gw),
        grid=(b, s // ts),
        in_specs=[
            pl.BlockSpec((1, ts, d), lambda i, j: (i, j, 0)),
            pl.BlockSpec((1, 6, d), lambda i, j: (i, 0, 0)),
            pl.BlockSpec((1, d), const2),
            pl.BlockSpec((d, d), const2),
            pl.BlockSpec(band.shape, const3),
            pl.BlockSpec((ng, gw, gw), const3),
            pl.BlockSpec((1, d), const2),
            pl.BlockSpec((d, d), const2),
            gate_up, gate_up, down,
            pl.BlockSpec((1, ts, 1), lambda i, j: (i, j, 0)),
            pl.BlockSpec((1, LANES), const2),
            pl.BlockSpec((1, d), const2),
        ],
        out_specs=[pl.BlockSpec((1, ts, d), lambda i, j: (i, j, 0)), gate_up, gate_up, down,
                   pl.BlockSpec((1, ts, 3 * LANES), lambda i, j: (i, j, 0)),
                   pl.BlockSpec((1, 1, 3 * LANES, ts), lambda i, j: (i, j, 0, 0)),
                   pl.BlockSpec((1, ts, d), lambda i, j: (i, j, 0))],
        out_shape=[jax.ShapeDtypeStruct((b, s, d), F32),
                   jax.ShapeDtypeStruct(fg.shape, BF16),
                   jax.ShapeDtypeStruct(fu.shape, BF16),
                   jax.ShapeDtypeStruct(fd.shape, BF16),
                   jax.ShapeDtypeStruct((b, s, 3 * LANES), F32),
                   jax.ShapeDtypeStruct((b, s // ts, 3 * LANES, ts), F32),
                   jax.ShapeDtypeStruct((b, s, d), BF16)],
        scratch_shapes=[pltpu.VMEM((POOL_HALO + ts, d), BF16)],
        compiler_params=_cparams(("arbitrary", "arbitrary")),
        name="pool_mixer",
    )(x, mod, g, w_in, band, w_grp, scale, w_out, fg, fu, fd, pos_f, freq_row, g_ffn)


def _ffn_kernel(x_ref, h_ref, mod_ref, wg_ref, wu_ref, wd_ref, o_ref, *, tf):
    h = h_ref[...]
    ff = wg_ref.shape[2]
    acc = jnp.zeros(x_ref.shape, F32)
    for f0 in range(0, ff, tf):
        gate = _dot(h, wg_ref[0, :, f0:f0 + tf])
        up = _dot(h, wu_ref[0, :, f0:f0 + tf])
        a = (_silu(gate) * up).astype(BF16)
        acc = acc + _dot(a, wd_ref[0, f0:f0 + tf, :])
    o_ref[...] = x_ref[...] + mod_ref[0][5:6] * acc


def _dense_ffn(x2, h2, mod, wg, wu, wd, tm, tf, rows_per_batch):
    n, d = x2.shape
    tpb = rows_per_batch // tm
    resident = lambda a: pl.BlockSpec(a.shape, lambda i: (0, 0, 0), pipeline_mode=pl.Buffered(1))
    return pl.pallas_call(
        functools.partial(_ffn_kernel, tf=tf),
        grid=(n // tm,),
        in_specs=[
            pl.BlockSpec((tm, d), lambda i: (i, 0)),
            pl.BlockSpec((tm, d), lambda i: (i, 0)),
            pl.BlockSpec((1, 6, d), lambda i: (i // tpb, 0, 0)),
            resident(wg), resident(wu), resident(wd),
        ],
        out_specs=pl.BlockSpec((tm, d), lambda i: (i, 0)),
        out_shape=jax.ShapeDtypeStruct((n, d), F32),
        compiler_params=_cparams(("arbitrary",)),
        name="dense_ffn",
    )(x2, h2, mod, wg, wu, wd)


def _dot_nt(a, b):
    return lax.dot_general(a, b, (((1,), (1,)), ((), ())), preferred_element_type=F32)


def _qkv_kernel(x_ref, mod_ref, kvmod_ref, gmix_ref, gkv_ref, wdq_ref, qg_ref, wqn_t_ref,
                wqr_t_ref, wdkv_ref, ckvg_ref, wkr_ref, wuk_ref, wuv_t_ref, tab_ref, tab_t_ref,
                qt_ref, k_ref, vt_ref):
    x = x_ref[0]
    xn = _rms_scale(x)
    mod = mod_ref[0]
    kvmod = kvmod_ref[0]
    hq = ((xn * gmix_ref[...]) * (1.0 + mod[1:2]) + mod[0:1]).astype(BF16)
    hk = ((xn * gkv_ref[...]) * (1.0 + kvmod[1:2]) + kvmod[0:1]).astype(BF16)

    cq = (_rms_scale(_dot(hq, wdq_ref[...])) * qg_ref[...]).astype(BF16)
    qn_t = _dot_nt(wqn_t_ref[...], cq) * Q_SCALE
    qr_t = _dot_nt(wqr_t_ref[...], cq) * Q_SCALE
    cos_t = tab_t_ref[0, 0, 0:ROPE_HALF, :]
    sin_t = tab_t_ref[0, 0, 2 * LANES + ROPE_HALF:2 * LANES + D_ROPE, :]
    zero_rows = jnp.zeros((HEAD_PAD - LANES - D_ROPE, x.shape[0]), BF16)

    ckv = (_rms_scale(_dot(hk, wdkv_ref[...])) * ckvg_ref[...]).astype(BF16)
    kn = _dot(ckv, wuk_ref[...])
    vt_ref[0, 0] = _dot_nt(wuv_t_ref[...], ckv).astype(BF16)
    kr = _apply_rope(_dot(hk, wkr_ref[...]), tab_ref[0]).astype(BF16)

    for h in range(N_HEADS):
        lo = h * HEAD_PAD
        hs = slice(h * LANES, (h + 1) * LANES)
        qt_ref[0, 0, lo:lo + LANES, :] = qn_t[hs].astype(BF16)
        x1 = qr_t[h * D_ROPE:h * D_ROPE + ROPE_HALF]
        x2 = qr_t[h * D_ROPE + ROPE_HALF:(h + 1) * D_ROPE]
        ro = lo + LANES
        qt_ref[0, 0, ro:ro + ROPE_HALF, :] = (x1 * cos_t - x2 * sin_t).astype(BF16)
        qt_ref[0, 0, ro + ROPE_HALF:ro + D_ROPE, :] = (x2 * cos_t + x1 * sin_t).astype(BF16)
        qt_ref[0, 0, ro + D_ROPE:lo + HEAD_PAD, :] = zero_rows
        k_ref[0, :, lo:lo + LANES] = kn[:, hs].astype(BF16)
        k_ref[0, :, lo + LANES:lo + HEAD_PAD] = kr


def _qkv_proj(x, mod, kvmod, gmix, gkv, wdq, qg, wqn_t, wqr_t, wdkv, ckvg, wkr, wuk, wuv_t,
              tab, tab_t, ts):
    b, s, d = x.shape
    row = lambda i, j: (i, j, 0)
    tile = lambda i, j: (i, j, 0, 0)
    per_b = lambda i, j: (i, 0, 0)

    def full(a):
        return pl.BlockSpec(a.shape, lambda i, j: (0,) * a.ndim)

    weights = (gmix, gkv, wdq, qg, wqn_t, wqr_t, wdkv, ckvg, wkr, wuk, wuv_t)
    return pl.pallas_call(
        _qkv_kernel,
        grid=(b, s // ts),
        in_specs=[pl.BlockSpec((1, ts, d), row),
                  pl.BlockSpec((1, 6, d), per_b),
                  pl.BlockSpec((1, 2, d), per_b)]
                 + [full(a) for a in weights]
                 + [pl.BlockSpec((1, ts, 3 * LANES), row),
                    pl.BlockSpec((1, 1, 3 * LANES, ts), tile)],
        out_specs=[pl.BlockSpec((1, 1, N_HEADS * HEAD_PAD, ts), tile),
                   pl.BlockSpec((1, ts, N_HEADS * HEAD_PAD), row),
                   pl.BlockSpec((1, 1, N_HEADS * D_V, ts), tile)],
        out_shape=[jax.ShapeDtypeStruct((b, s // ts, N_HEADS * HEAD_PAD, ts), BF16),
                   jax.ShapeDtypeStruct((b, s, N_HEADS * HEAD_PAD), BF16),
                   jax.ShapeDtypeStruct((b, s // ts, N_HEADS * D_V, ts), BF16)],
        compiler_params=_cparams(("arbitrary", "arbitrary")),
        name="qkv_proj",
    )(x, mod, kvmod, *weights, tab, tab_t)


def _attn_kernel(qt_ref, k_ref, vt_ref, eg_ref, eu_ref, ed_ref,
                 o_ref, eg_out_ref, eu_out_ref, ed_out_ref, *, tq, tk):
    ts = qt_ref.shape[3]
    s = k_ref.shape[1]
    key = lax.broadcasted_iota(jnp.int32, (tk, tq), 0)
    qry = lax.broadcasted_iota(jnp.int32, (tk, tq), 1)
    last_block = lambda qi: ((qi + 1) * tq - 1) // tk
    blocks = [(qi, j) for qi in range(s // tq) for j in range(last_block(qi) + 1)]

    def tile(ref, start, size):
        return ref[0, start // ts, :, start % ts:start % ts + size]

    def scores_t(qi, j):
        st = _dot(k_ref[0, j * tk:(j + 1) * tk, :], tile(qt_ref, qi * tq, tq))
        if (j + 1) * tk - 1 > qi * tq:
            st = jnp.where(key + (j * tk - qi * tq) <= qry, st, -jnp.inf)
        return st

    ahead = [scores_t(*blk) for blk in blocks[:ATTN_LOOKAHEAD]]
    for n, (qi, j) in enumerate(blocks):
        st = ahead.pop(0)
        if n + ATTN_LOOKAHEAD < len(blocks):
            ahead.append(scores_t(*blocks[n + ATTN_LOOKAHEAD]))
        if j == 0:
            m = jnp.full((1, tq), -jnp.inf, F32)
            l = jnp.zeros((1, tq), F32)
            acc = jnp.zeros((D_V, tq), F32)
        m_new = jnp.maximum(m, jnp.max(st, axis=0, keepdims=True))
        alpha = jnp.exp2(m - m_new)
        pt = jnp.exp2(st - m_new)
        l = alpha * l + jnp.sum(pt, axis=0, keepdims=True)
        acc = alpha * acc + _dot(tile(vt_ref, j * tk, tk), pt.astype(BF16))
        m = m_new
        if j == last_block(qi):
            o_ref[0, qi * tq:(qi + 1) * tq, :] = (acc / l).T.astype(BF16)

    _cast_side_job((eg_ref, eu_ref, ed_ref), (eg_out_ref, eu_out_ref, ed_out_ref))


def _attention(qt, k, vt, eg, eu, ed, ts):
    b, s, _ = k.shape
    n_exp, d, eff = eg.shape
    gate_up, down = _cast_specs(n_exp, d, eff, b * N_HEADS, lambda i, h: i * N_HEADS + h)
    return pl.pallas_call(
        functools.partial(_attn_kernel, tq=min(ts, ATTN_QUERY_TILE), tk=min(ts, ATTN_KEY_BLOCK)),
        grid=(b, N_HEADS),
        in_specs=[
            pl.BlockSpec((1, s // ts, HEAD_PAD, ts), lambda i, h: (i, 0, h, 0)),
            pl.BlockSpec((1, s, HEAD_PAD), lambda i, h: (i, 0, h)),
            pl.BlockSpec((1, s // ts, D_V, ts), lambda i, h: (i, 0, h, 0)),
            gate_up, gate_up, down,
        ],
        out_specs=[pl.BlockSpec((1, s, D_V), lambda i, h: (i, 0, h)), gate_up, gate_up, down],
        out_shape=[jax.ShapeDtypeStruct((b, s, N_HEADS * D_V), BF16),
                   jax.ShapeDtypeStruct(eg.shape, BF16),
                   jax.ShapeDtypeStruct(eu.shape, BF16),
                   jax.ShapeDtypeStruct(ed.shape, BF16)],
        compiler_params=_cparams(("arbitrary", "arbitrary")),
        name="flash_attn",
    )(qt, k, vt, eg, eu, ed)


def _attn_out_kernel(o_ref, x_ref, mod_ref, wo_ref, g_ref, wr_ref, ltri_ref,
                     x_out_ref, h_out_ref, rcol_ref, rrow_ref, cnt_ref, carry_ref,
                     *, n_exp, tiles_per_block):
    i = pl.program_id(0)

    @pl.when(i % tiles_per_block == 0)
    def _():
        carry_ref[...] = jnp.zeros_like(carry_ref)

    mod = mod_ref[0]
    x = x_ref[...] + mod[2:3] * _dot(o_ref[...], wo_ref[...])
    x_out_ref[...] = x
    h = (_rms_scale(x) * g_ref[...]) * (1.0 + mod[4:5]) + mod[3:4]
    h_hi = h.astype(BF16)
    h_out_ref[...] = h_hi

    tm = x.shape[0]
    h_lo = (h - h_hi.astype(F32)).astype(BF16)
    parts = _dot(jnp.concatenate([h_hi, h_lo], axis=0), wr_ref[...])
    logits = (parts[:tm, :LANES] + parts[:tm, LANES:]) + (parts[tm:, :LANES] + parts[tm:, LANES:])
    lane = lax.broadcasted_iota(jnp.int32, (tm, LANES), 1)
    lane_f = lane.astype(F32)

    def first_argmax(vals):
        m = jnp.max(vals, axis=-1, keepdims=True)
        idx = jnp.min(jnp.where(vals == m, lane_f, float(LANES)), axis=-1, keepdims=True)
        return m, idx.astype(jnp.int32)

    logits = jnp.where(lane < n_exp, logits, -jnp.inf)
    m1, i1 = first_argmax(logits)
    m2, i2 = first_argmax(jnp.where(lane == i1, -jnp.inf, logits))
    ex = jnp.exp(m2 - m1)
    w1 = 1.0 / (1.0 + ex)
    w2 = ex / (1.0 + ex)

    mask = jnp.where((lane == i1) | (lane == i2), 1.0, 0.0)
    ranks = _dot(ltri_ref[...], mask.astype(BF16)) + carry_ref[0:1]
    incl = ranks + mask
    rank1 = jnp.sum(jnp.where(lane == i1, ranks, 0.0), axis=-1, keepdims=True)
    rank2 = jnp.sum(jnp.where(lane == i2, ranks, 0.0), axis=-1, keepdims=True)
    carry_ref[0:1] = incl[tm - 1:tm]
    n_sub = tm // rrow_ref.shape[2]
    sub = rrow_ref.shape[2]
    cnt_ref[0] = jnp.zeros(cnt_ref.shape[1:], F32)
    for j in range(n_sub):
        cnt_ref[0, j:j + 1] = incl[(j + 1) * sub - 1:(j + 1) * sub]

    fields = (i1.astype(F32), i2.astype(F32), rank1, rank2, w1, w2)
    rcol = jnp.zeros((tm, LANES), F32)
    for k, val in enumerate(fields):
        rcol = jnp.where(lane == k, val, rcol)
    rcol_ref[...] = rcol
    rrow = rcol.T
    for j in range(n_sub):
        rrow_ref[j] = rrow[0:ROUTE_FIELDS, j * sub:(j + 1) * sub]


def _attn_out(o2, x2, mod, wo, g, wr, n_exp, ltri, tm, sub, rows_per_block):
    n, d = x2.shape
    tpb = rows_per_block // tm
    n_sub = tm // sub
    return pl.pallas_call(
        functools.partial(_attn_out_kernel, n_exp=n_exp, tiles_per_block=tpb),
        grid=(n // tm,),
        in_specs=[
            pl.BlockSpec((tm, o2.shape[1]), lambda i: (i, 0)),
            pl.BlockSpec((tm, d), lambda i: (i, 0)),
            pl.BlockSpec((1, 6, d), lambda i: (i // tpb, 0, 0)),
            pl.BlockSpec(wo.shape, lambda i: (0, 0)),
            pl.BlockSpec((1, d), lambda i: (0, 0)),
            pl.BlockSpec(wr.shape, lambda i: (0, 0)),
            pl.BlockSpec((tm, tm), lambda i: (0, 0)),
        ],
        out_specs=[pl.BlockSpec((tm, d), lambda i: (i, 0)),
                   pl.BlockSpec((tm, d), lambda i: (i, 0)),
                   pl.BlockSpec((tm, LANES), lambda i: (i, 0)),
                   pl.BlockSpec((n_sub, ROUTE_FIELDS, sub), lambda i: (i, 0, 0)),
                   pl.BlockSpec((1, 8, LANES), lambda i: (i, 0, 0))],
        out_shape=[jax.ShapeDtypeStruct((n, d), F32),
                   jax.ShapeDtypeStruct((n, d), BF16),
                   jax.ShapeDtypeStruct((n, LANES), F32),
                   jax.ShapeDtypeStruct((n // sub, ROUTE_FIELDS, sub), F32),
                   jax.ShapeDtypeStruct((n // tm, 8, LANES), F32)],
        scratch_shapes=[pltpu.VMEM((8, LANES), F32)],
        compiler_params=_cparams(("arbitrary",)),
        name="attn_out_router",
    )(o2, x2, mod, wo, g, wr, ltri)


def _dispatch_kernel(off_ref, lo_ref, hi_ref, cexp_ref, h_ref, rrow_ref, xg_ref, gs_ref,
                     *, n_exp, sub, chunks_per_step, chunks_per_block):
    blk = pl.program_id(0)
    g = pl.program_id(1)
    d = h_ref.shape[1]
    n_sub = rrow_ref.shape[0]
    win = min(DISPATCH_WINDOW, n_sub)
    iota = lax.broadcasted_iota(jnp.int32, (DISPATCH_CHUNK, 1), 0)

    def contribution(info, sb):
        row, exp_id, exp_off = info
        r = rrow_ref[sb]
        first = r[0:1] == exp_id
        second = r[1:2] == exp_id
        pos = jnp.where(first, r[2:3] + exp_off, jnp.where(second, r[3:4] + exp_off, -1.0))
        hit = pos == row
        onehot = jnp.where(hit, 1.0, 0.0).astype(BF16)
        tok = h_ref[pl.ds(pl.multiple_of(sb * sub, sub), sub), :]
        gate = jnp.sum(jnp.where(hit, jnp.where(first, r[4:5], r[5:6]), 0.0),
                       axis=-1, keepdims=True)
        return _dot(onehot, tok), gate

    def chunk_info(j):
        c = g * chunks_per_step + j
        t = blk * chunks_per_block + c
        row = (c * DISPATCH_CHUNK + iota).astype(F32)
        exp_id = cexp_ref[t]
        exp_off = off_ref[blk * n_exp + exp_id].astype(F32)
        start = jnp.minimum(lo_ref[t], n_sub - win)
        return (row, exp_id.astype(F32), exp_off), start, hi_ref[t]

    for j in range(chunks_per_step):
        info, start, _ = chunk_info(j)
        acc = jnp.zeros((DISPATCH_CHUNK, d), F32)
        gate = jnp.zeros((DISPATCH_CHUNK, 1), F32)
        for k in range(win):
            a, b = contribution(info, start + k)
            acc += a
            gate += b
        rows = slice(j * DISPATCH_CHUNK, (j + 1) * DISPATCH_CHUNK)
        xg_ref[rows, :] = acc.astype(BF16)
        gs_ref[rows, :] = gate

    def extra(j, _):
        info, start, last = chunk_info(j)
        rows = pl.ds(pl.multiple_of(j * DISPATCH_CHUNK, DISPATCH_CHUNK), DISPATCH_CHUNK)

        def add(sb, _):
            a, b = contribution(info, sb)
            xg_ref[rows, :] = (xg_ref[rows, :].astype(F32) + a).astype(BF16)
            gs_ref[rows, :] = gs_ref[rows, :] + b
            return 0

        return lax.fori_loop(start + win, last + 1, add, 0)

    lax.fori_loop(0, chunks_per_step, extra, 0)


def _dispatch(off, lo, hi, cexp, h, rrow, n_blocks, rows_per_block, rcap, n_exp, sub):
    n, d = h.shape
    n_sub = rows_per_block // sub
    step_rows = _pick(rcap, 1024)
    cps = step_rows // DISPATCH_CHUNK
    cpb = rcap // DISPATCH_CHUNK
    grid_spec = pltpu.PrefetchScalarGridSpec(
        num_scalar_prefetch=4,
        grid=(n_blocks, rcap // step_rows),
        in_specs=[
            pl.BlockSpec((rows_per_block, d), lambda b, g, *_: (b, 0)),
            pl.BlockSpec((n_sub, ROUTE_FIELDS, sub), lambda b, g, *_: (b, 0, 0)),
        ],
        out_specs=[
            pl.BlockSpec((step_rows, d), lambda b, g, *_: (b * (rcap // step_rows) + g, 0)),
            pl.BlockSpec((step_rows, 1), lambda b, g, *_: (b * (rcap // step_rows) + g, 0)),
        ],
    )
    return pl.pallas_call(
        functools.partial(_dispatch_kernel, n_exp=n_exp, sub=sub, chunks_per_step=cps,
                          chunks_per_block=cpb),
        grid_spec=grid_spec,
        out_shape=[jax.ShapeDtypeStruct((n_blocks * rcap, d), BF16),
                   jax.ShapeDtypeStruct((n_blocks * rcap, 1), F32)],
        compiler_params=_cparams(("arbitrary", "arbitrary")),
        name="moe_dispatch",
    )(off, lo, hi, cexp, h, rrow)


def _expert_ffn_kernel(perm_ref, pexp_ref, pfill_ref, x_ref, gs_ref, wg_ref, wu_ref, wd_ref,
                       y_ref, *, tf):
    w = pl.program_id(0)
    fill = pfill_ref[w]
    ff = wg_ref.shape[2]

    def ffn(rows):
        x = x_ref[rows, :]
        gs = gs_ref[rows, :]
        acc = jnp.zeros((x.shape[0], y_ref.shape[1]), F32)
        for f0 in range(0, ff, tf):
            gate = _dot(x, wg_ref[0, :, f0:f0 + tf])
            up = _dot(x, wu_ref[0, :, f0:f0 + tf])
            a = ((_silu(gate) * up) * gs).astype(BF16)
            acc = acc + _dot(a, wd_ref[0, f0:f0 + tf, :])
        return acc.astype(BF16)

    half = PIECE // 2

    @pl.when(fill == 2)
    def _():
        y_ref[...] = ffn(slice(0, PIECE))

    @pl.when(fill == 1)
    def _():
        y_ref[0:half, :] = ffn(slice(0, half))
        y_ref[half:PIECE, :] = jnp.zeros((half, y_ref.shape[1]), BF16)

    @pl.when(fill == 0)
    def _():
        y_ref[...] = jnp.zeros(y_ref.shape, BF16)


def _expert_ffn(perm, pexp, pfill, xg, gs, wg, wu, wd, tf):
    rows, d = xg.shape
    n_exp, _, ff = wg.shape
    grid_spec = pltpu.PrefetchScalarGridSpec(
        num_scalar_prefetch=3,
        grid=(rows // PIECE,),
        in_specs=[
            pl.BlockSpec((PIECE, d), lambda w, perm, pexp, pfill: (perm[w], 0)),
            pl.BlockSpec((PIECE, 1), lambda w, perm, pexp, pfill: (perm[w], 0)),
            pl.BlockSpec((1, d, ff), lambda w, perm, pexp, pfill: (pexp[w], 0, 0)),
            pl.BlockSpec((1, d, ff), lambda w, perm, pexp, pfill: (pexp[w], 0, 0)),
            pl.BlockSpec((1, ff, d), lambda w, perm, pexp, pfill: (pexp[w], 0, 0)),
        ],
        out_specs=pl.BlockSpec((PIECE, d), lambda w, perm, pexp, pfill: (perm[w], 0)),
    )
    return pl.pallas_call(
        functools.partial(_expert_ffn_kernel, tf=tf),
        grid_spec=grid_spec,
        out_shape=jax.ShapeDtypeStruct((rows, d), BF16),
        compiler_params=_cparams(("arbitrary",)),
        name="expert_ffn",
    )(perm, pexp, pfill, xg, gs, wg, wu, wd)


def _combine_kernel(off_ref, start_ref, end_ref, yg_ref, rcol_ref, x_ref, mod_ref, gfin_ref, o_ref,
                    acc_ref, *, n_exp, sub, subs_per_block):
    blk = pl.program_id(0)
    t = pl.program_id(1)
    tm, d = x_ref.shape
    col = lax.broadcasted_iota(jnp.int32, (1, COMBINE_WINDOW), 1)
    align = COMBINE_WINDOW // 2

    def token_info(s):
        rc = rcol_ref[s * sub:(s + 1) * sub, :]
        return tuple(jnp.broadcast_to(rc[:, k:k + 1], (sub, COMBINE_WINDOW)) for k in range(4))

    def gathered(info, e, w0):
        i1, i2, rank1, rank2 = info
        rank = jnp.where(i1 == e, rank1, jnp.where(i2 == e, rank2, -float(COMBINE_WINDOW)))
        in_region = (w0 - off_ref[blk * n_exp + e] + col).astype(F32)
        onehot = jnp.where(rank == in_region, 1.0, 0.0).astype(BF16)
        return _dot(onehot, yg_ref[pl.ds(pl.multiple_of(w0, align), COMBINE_WINDOW), :])

    def first_window(s, e):
        idx = (blk * subs_per_block + t * (tm // sub) + s) * n_exp + e
        w0 = lax.shift_right_logical(start_ref[idx], align.bit_length() - 1) * align
        return w0, end_ref[idx]

    for s in range(tm // sub):
        info = token_info(s)
        acc = jnp.zeros((sub, d), F32)
        for e in range(n_exp):
            acc += gathered(info, e, first_window(s, e)[0])
        acc_ref[s * sub:(s + 1) * sub, :] = acc

    for s in range(tm // sub):
        info = token_info(s)
        for e in range(n_exp):
            w0, end = first_window(s, e)
            n_more = lax.shift_right_logical(
                jnp.maximum(end - w0 - 1, 0), COMBINE_WINDOW.bit_length() - 1)

            def add(k, _, info=info, e=e, w0=w0, s=s):
                rows = slice(s * sub, (s + 1) * sub)
                acc_ref[rows, :] = acc_ref[rows, :] + gathered(info, e, w0 + k * COMBINE_WINDOW)
                return 0

            lax.fori_loop(1, n_more + 1, add, 0)

    x = x_ref[...] + mod_ref[0][5:6] * acc_ref[...]
    o_ref[...] = _rms_scale(x) * gfin_ref[...]


def _combine(off, lo, hi, yg, rcol, x2, mod, gfin, n_blocks, rows_per_block, rcap, n_exp, sub, tm):
    n, d = x2.shape
    tpb = rows_per_block // tm
    grid_spec = pltpu.PrefetchScalarGridSpec(
        num_scalar_prefetch=3,
        grid=(n_blocks, tpb),
        in_specs=[
            pl.BlockSpec((rcap, d), lambda b, t, *_: (b, 0)),
            pl.BlockSpec((tm, LANES), lambda b, t, *_: (b * tpb + t, 0)),
            pl.BlockSpec((tm, d), lambda b, t, *_: (b * tpb + t, 0)),
            pl.BlockSpec((1, 6, d), lambda b, t, *_: (b, 0, 0)),
            pl.BlockSpec((1, d), lambda b, t, *_: (0, 0)),
        ],
        out_specs=pl.BlockSpec((tm, d), lambda b, t, *_: (b * tpb + t, 0)),
        scratch_shapes=[pltpu.VMEM((tm, d), F32)],
    )
    return pl.pallas_call(
        functools.partial(_combine_kernel, n_exp=n_exp, sub=sub,
                          subs_per_block=rows_per_block // sub),
        grid_spec=grid_spec,
        out_shape=jax.ShapeDtypeStruct((n, d), F32),
        compiler_params=_cparams(("arbitrary", "arbitrary")),
        name="moe_combine",
    )(off, lo, hi, yg, rcol, x2, mod, gfin)


def _routing_tables(cnt, n_blocks, n_sub, n_exp, rcap):
    cum_incl = cnt.astype(jnp.int32)
    cum_excl = jnp.concatenate([jnp.zeros_like(cum_incl[:, :1]), cum_incl[:, :-1]], axis=1)
    total = cum_incl[:, -1]
    psz = (total + PIECE - 1) // PIECE * PIECE
    off = jnp.cumsum(psz, axis=1) - psz
    start = off[:, None, :] + cum_excl
    end = off[:, None, :] + cum_incl

    c0 = jnp.arange(rcap // DISPATCH_CHUNK, dtype=jnp.int32)[None, :, None, None] * DISPATCH_CHUNK
    hit = ((start[:, None] < c0 + DISPATCH_CHUNK) & (end[:, None] > c0)
           & (end[:, None] > start[:, None])).any(-1)
    sub_ids = jnp.arange(n_sub, dtype=jnp.int32)
    d_lo = jnp.min(jnp.where(hit, sub_ids, n_sub), axis=-1)
    d_hi = jnp.max(jnp.where(hit, sub_ids, -1), axis=-1)
    d_lo = jnp.where(d_hi < 0, 0, d_lo)


    ppb = rcap // PIECE
    prow = jnp.arange(ppb, dtype=jnp.int32)[None, :, None] * PIECE
    inside = (prow >= off[:, None, :]) & (prow < (off + psz)[:, None, :])
    p_exp = jnp.argmax(inside, axis=-1).astype(jnp.int32)
    used = inside.any(-1)
    left = jnp.take_along_axis(total + off, p_exp, axis=1) - prow[..., 0]
    p_fill = jnp.where(used & (left > 0), jnp.where(left > PIECE // 2, 2, 1), 0).astype(jnp.int32)
    key = jnp.where(p_fill > 0, p_exp, n_exp).reshape(-1)
    perm = jnp.argsort(key, stable=True).astype(jnp.int32)
    p_exp_sorted = jnp.minimum(key[perm], n_exp - 1).astype(jnp.int32)
    d_exp = jnp.repeat(p_exp, PIECE // DISPATCH_CHUNK, axis=1).reshape(-1)
    return (off.reshape(-1).astype(jnp.int32), d_lo.reshape(-1), d_hi.reshape(-1), d_exp,
            start.reshape(-1).astype(jnp.int32), end.reshape(-1).astype(jnp.int32),
            perm, p_exp_sorted, p_fill.reshape(-1)[perm])


def _pick(n, pref):
    t = min(n, pref)
    while n % t:
        t //= 2
    return t


def kernel(x, c, positions, ada_w, ada_b, mix_norm_g, ffn_norm_g, pool_w_in, pool_w_grp,
           pool_scale, pool_w_out, kv_ada_w, kv_ada_b, kv_norm_g, w_dkv, ckv_norm_g, w_kr,
           w_uk, w_uv, w_dq, q_norm_g, w_uq, w_o, ffn_w_gate, ffn_w_up, ffn_w_down,
           router_w, moe_w_gate, moe_w_up, moe_w_down, final_norm_g):
    b, s, d = x.shape
    n = b * s
    depth = ada_w.shape[0]
    assert depth == 2 and pool_w_in.shape[0] == 1 and w_dq.shape[0] == 1
    assert b <= 8 and d % LANES == 0

    ts = _pick(s, 512)
    tm = _pick(s, 512)
    tf = _pick(ffn_w_gate.shape[-1], 512)

    c_pad = jnp.zeros((8, d), F32).at[:b].set(c)
    mods = _adaln(c_pad, ada_w, ada_b[:, None, :], _pick(6 * d, 1536))
    mods = mods[:, :b].reshape(depth, b, 6, d)
    kvmod = _adaln(c_pad, kv_ada_w[None], kv_ada_b[None, None, :], _pick(2 * d, 1024))
    kvmod = kvmod[0, :b].reshape(b, 2, d)

    r = jnp.arange(min(ts, POOL_ROW_BLOCK))[:, None]
    cidx = jnp.arange(min(ts, POOL_ROW_BLOCK) + POOL_HALO)[None, :] - POOL_HALO
    band = jnp.stack([((cidx <= r) & (cidx > r - w)) for w in POOL_WINDOWS]).astype(BF16)
    inv_freq = ROPE_THETA ** (-jnp.arange(0, D_ROPE, 2, dtype=F32) / D_ROPE)
    freq_row = jnp.concatenate([inv_freq, inv_freq, jnp.zeros((LANES - D_ROPE,), F32)])[None, :]
    x1, fwg, fwu, fwd, rope_tab, rope_tab_t, h1 = _pool_mixer(
        x, mods[0], mix_norm_g[0:1], pool_w_in[0].astype(BF16), band, pool_w_grp[0].astype(BF16),
        pool_scale[0:1], pool_w_out[0].astype(BF16), ffn_w_gate, ffn_w_up, ffn_w_down,
        positions.astype(F32)[..., None], freq_row, ffn_norm_g[0:1], ts)

    x2 = _dense_ffn(x1.reshape(n, d), h1.reshape(n, d), mods[0], fwg, fwu, fwd, tm, tf, s)


    q_lora = w_dq.shape[-1]
    wuq = w_uq[0].reshape(q_lora, N_HEADS, D_NOPE + D_ROPE)
    wqn_t = wuq[:, :, :D_NOPE].reshape(q_lora, N_HEADS * D_NOPE).T.astype(BF16)
    wqr_t = wuq[:, :, D_NOPE:].reshape(q_lora, N_HEADS * D_ROPE).T.astype(BF16)
    wkr = jnp.pad(w_kr, ((0, 0), (0, LANES - D_ROPE))).astype(BF16)
    qt, k, vt = _qkv_proj(
        x2.reshape(b, s, d), mods[1], kvmod, mix_norm_g[1:2], kv_norm_g[None, :],
        w_dq[0].astype(BF16), q_norm_g[0:1], wqn_t, wqr_t, w_dkv.astype(BF16),
        ckv_norm_g[None, :], wkr, w_uk.astype(BF16), w_uv.T.astype(BF16), rope_tab, rope_tab_t, ts)
    o, ewg, ewu, ewd = _attention(qt, k, vt, moe_w_gate[0], moe_w_up[0], moe_w_down[0], ts)

    n_exp = router_w.shape[-1]
    sub = _pick(s, ROUTE_SUB)
    n_sub = s // sub
    rcap = 2 * s + n_exp * PIECE
    ltri = (jnp.arange(ts)[None, :] < jnp.arange(ts)[:, None]).astype(BF16)
    assert n_exp <= LANES
    wr = jnp.pad(router_w[0], ((0, 0), (0, LANES - n_exp)))
    wr_hi = wr.astype(BF16)
    wr_split = jnp.concatenate([wr_hi, (wr - wr_hi.astype(F32)).astype(BF16)], axis=1)
    x3, h3, rcol, rrow, cnt = _attn_out(o.reshape(n, N_HEADS * D_V), x2, mods[1],
                                        w_o[0].astype(BF16), ffn_norm_g[1:2], wr_split, n_exp,
                                        ltri, ts, sub, s)
    cnt = cnt[:, :ts // sub, :n_exp].reshape(b, n_sub, n_exp)
    off, d_lo, d_hi, d_exp, c_lo, c_hi, perm, p_exp, p_fill = _routing_tables(cnt, b, n_sub, n_exp, rcap)
    xg, gs = _dispatch(off, d_lo, d_hi, d_exp, h3, rrow, b, s, rcap, n_exp, sub)
    yg = _expert_ffn(perm, p_exp, p_fill, xg, gs, ewg, ewu, ewd, tf)
    out = _combine(off, c_lo, c_hi, yg, rcol, x3, mods[1], final_norm_g[None, :], b, s, rcap,
                   n_exp, sub, ts)
    return out.reshape(b, s, d)
```

```python
import functools
import math

import jax
import jax.numpy as jnp
from jax import lax
from jax.experimental import pallas as pl
from jax.experimental.pallas import tpu as pltpu

F32 = jnp.float32
BF16 = jnp.bfloat16

N_HEADS = 8
D_NOPE = 128
D_ROPE = 64
D_V = 128
HEAD_PAD = 256
ROPE_HALF = D_ROPE // 2
ROPE_THETA = 10000.0
ATTN_SCALE = 1.0 / math.sqrt(D_NOPE + D_ROPE)
Q_SCALE = ATTN_SCALE * math.log2(math.e)
ATTN_QUERY_TILE = 512
ATTN_KEY_BLOCK = 512
ATTN_LOOKAHEAD = 2
POOL_WINDOWS = (2, 4, 8, 16)
POOL_HALO = 16
POOL_ROW_BLOCK = 128
EPS = 1e-6

ROUTE_SUB = 256
DISPATCH_CHUNK = 128
DISPATCH_WINDOW = 4
COMBINE_WINDOW = 256
PIECE = 512
PIECE_QUANTUM = 128
ROUTE_FIELDS = 8

LANES = 128
VMEM_LIMIT_BYTES = 56 * 1024 * 1024


def _cparams(sem):
    return pltpu.CompilerParams(dimension_semantics=sem, vmem_limit_bytes=VMEM_LIMIT_BYTES)


def _dot(a, b):
    return jnp.dot(a, b, preferred_element_type=F32)


def _rms_scale(x):
    return x * lax.rsqrt(jnp.mean(x * x, axis=-1, keepdims=True) + EPS)


def _silu(x):
    return x * jax.nn.sigmoid(x)


def _adaln_kernel(c_ref, w_ref, b_ref, o_ref):
    sc = _silu(c_ref[...])
    o_ref[0] = jnp.dot(sc, w_ref[0], preferred_element_type=F32,
                       precision=lax.Precision.HIGHEST) + b_ref[0]


def _adaln(c_pad, w, b, tn):
    nl, d, m = w.shape
    return pl.pallas_call(
        _adaln_kernel,
        grid=(nl, m // tn),
        in_specs=[
            pl.BlockSpec((8, d), lambda l, j: (0, 0)),
            pl.BlockSpec((1, d, tn), lambda l, j: (l, 0, j)),
            pl.BlockSpec((1, 1, tn), lambda l, j: (l, 0, j)),
        ],
        out_specs=pl.BlockSpec((1, 8, tn), lambda l, j: (l, 0, j)),
        out_shape=jax.ShapeDtypeStruct((nl, 8, m), F32),
        compiler_params=_cparams(("arbitrary", "arbitrary")),
        name="adaln",
    )(c_pad, w, b)


def _rope_table_side_job(pos_ref, freq_ref, tab_ref, tab_t_ref):
    ang = pos_ref[0] * freq_ref[...]
    lane = lax.broadcasted_iota(jnp.int32, ang.shape, 1)
    cos = jnp.cos(ang)
    sin = jnp.sin(ang)
    zero = jnp.zeros_like(ang)
    tabs = (jnp.where(lane < D_ROPE, cos, zero),
            jnp.where(lane < ROPE_HALF, -sin, zero),
            jnp.where((lane >= ROPE_HALF) & (lane < D_ROPE), sin, zero))
    for i, tab in enumerate(tabs):
        tab_ref[0, :, i * LANES:(i + 1) * LANES] = tab
        tab_t_ref[0, 0, i * LANES:(i + 1) * LANES, :] = tab.T


def _apply_rope(a, tab):
    c, s1, s2 = (tab[:, i * LANES:(i + 1) * LANES] for i in range(3))
    return a * c + pltpu.roll(a, LANES - ROPE_HALF, 1) * s1 + pltpu.roll(a, ROPE_HALF, 1) * s2


def _cast_specs(n_exp, d, ff, steps, step_of):
    assert steps >= n_exp, "too few grid steps to carry the weight cast"
    k = max(c for c in range(1, steps // n_exp + 1) if ff % (c * LANES) == 0)
    cw = ff // k

    def chunk(*idx):
        c = jnp.minimum(step_of(*idx), n_exp * k - 1)
        return c // k, c % k

    gate_up = pl.BlockSpec((1, d, cw), lambda *idx: (chunk(*idx)[0], 0, chunk(*idx)[1]))
    down = pl.BlockSpec((1, cw, d), lambda *idx: (chunk(*idx)[0], chunk(*idx)[1], 0))
    return gate_up, down


def _cast_side_job(src_refs, dst_refs):
    for src, dst in zip(src_refs, dst_refs):
        dst[...] = src[...].astype(BF16)


def _pool_kernel(x_ref, mod_ref, g_ref, win_ref, band_ref, wgrp_ref, scale_ref, wout_ref,
                 fg_ref, fu_ref, fd_ref, pos_ref, freq_ref, gffn_ref,
                 o_ref, fg_out_ref, fu_out_ref, fd_out_ref, tab_ref, tab_t_ref, h_out_ref, uext_ref,
                 *, ts, gw):
    s = pl.program_id(1)

    @pl.when(s == 0)
    def _():
        uext_ref[0:POOL_HALO, :] = jnp.zeros((POOL_HALO, uext_ref.shape[1]), BF16)

    x = x_ref[0]
    mod = mod_ref[0]
    h = (_rms_scale(x) * g_ref[...]) * (1.0 + mod[1:2]) + mod[0:1]
    u = _dot(h.astype(BF16), win_ref[...])
    uext_ref[POOL_HALO:POOL_HALO + ts, :] = u.astype(BF16)

    t = s * ts + lax.broadcasted_iota(jnp.int32, (ts, 1), 0)
    zs = []
    for g, w in enumerate(POOL_WINDOWS):
        cols = slice(g * gw, (g + 1) * gw)
        rb = band_ref.shape[1]
        wsum = jnp.concatenate(
            [_dot(band_ref[g], uext_ref[r0:r0 + rb + POOL_HALO, cols]) for r0 in range(0, ts, rb)],
            axis=0)
        cnt = jnp.minimum(t + 1, w).astype(F32)
        p = wsum / cnt - u[:, cols]
        zs.append(_dot(p.astype(BF16), wgrp_ref[g]))
    z = jnp.concatenate(zs, axis=-1) * scale_ref[...]
    y = _dot(z.astype(BF16), wout_ref[...])
    x_out = x + mod[2:3] * y
    o_ref[0] = x_out
    h_out_ref[0] = ((_rms_scale(x_out) * gffn_ref[...]) * (1.0 + mod[4:5]) + mod[3:4]).astype(BF16)

    uext_ref[0:POOL_HALO, :] = uext_ref[ts:ts + POOL_HALO, :]

    _cast_side_job((fg_ref, fu_ref, fd_ref), (fg_out_ref, fu_out_ref, fd_out_ref))
    _rope_table_side_job(pos_ref, freq_ref, tab_ref, tab_t_ref)


def _pool_mixer(x, mod, g, w_in, band, w_grp, scale, w_out, fg, fu, fd, pos_f, freq_row, g_ffn,
                ts):
    b, s, d = x.shape
    ng, gw, _ = w_grp.shape
    n_s = s // ts
    gate_up, down = _cast_specs(1, d, fg.shape[2], b * n_s, lambda i, j: i * n_s + j)
    const2 = lambda i, j: (0, 0)
    const3 = lambda i, j: (0, 0, 0)
    return pl.pallas_call(
        functools.partial(_pool_kernel, ts=ts, gw=gw),
        grid=(b, s // ts),
        in_specs=[
            pl.BlockSpec((1, ts, d), lambda i, j: (i, j, 0)),
            pl.BlockSpec((1, 6, d), lambda i, j: (i, 0, 0)),
            pl.BlockSpec((1, d), const2),
            pl.BlockSpec((d, d), const2),
            pl.BlockSpec(band.shape, const3),
            pl.BlockSpec((ng, gw, gw), const3),
            pl.BlockSpec((1, d), const2),
            pl.BlockSpec((d, d), const2),
            gate_up, gate_up, down,
            pl.BlockSpec((1, ts, 1), lambda i, j: (i, j, 0)),
            pl.BlockSpec((1, LANES), const2),
            pl.BlockSpec((1, d), const2),
        ],
        out_specs=[pl.BlockSpec((1, ts, d), lambda i, j: (i, j, 0)), gate_up, gate_up, down,
                   pl.BlockSpec((1, ts, 3 * LANES), lambda i, j: (i, j, 0)),
                   pl.BlockSpec((1, 1, 3 * LANES, ts), lambda i, j: (i, j, 0, 0)),
                   pl.BlockSpec((1, ts, d), lambda i, j: (i, j, 0))],
        out_shape=[jax.ShapeDtypeStruct((b, s, d), F32),
                   jax.ShapeDtypeStruct(fg.shape, BF16),
                   jax.ShapeDtypeStruct(fu.shape, BF16),
                   jax.ShapeDtypeStruct(fd.shape, BF16),
                   jax.ShapeDtypeStruct((b, s, 3 * LANES), F32),
                   jax.ShapeDtypeStruct((b, s // ts, 3 * LANES, ts), F32),
                   jax.ShapeDtypeStruct((b, s, d), BF16)],
        scratch_shapes=[pltpu.VMEM((POOL_HALO + ts, d), BF16)],
        compiler_params=_cparams(("arbitrary", "arbitrary")),
        name="pool_mixer",
    )(x, mod, g, w_in, band, w_grp, scale, w_out, fg, fu, fd, pos_f, freq_row, g_ffn)


def _ffn_kernel(x_ref, h_ref, mod_ref, wg_ref, wu_ref, wd_ref, o_ref, *, tf):
    h = h_ref[...]
    ff = wg_ref.shape[2]
    acc = jnp.zeros(x_ref.shape, F32)
    for f0 in range(0, ff, tf):
        gate = _dot(h, wg_ref[0, :, f0:f0 + tf])
        up = _dot(h, wu_ref[0, :, f0:f0 + tf])
        a = (_silu(gate) * up).astype(BF16)
        acc = acc + _dot(a, wd_ref[0, f0:f0 + tf, :])
    o_ref[...] = x_ref[...] + mod_ref[0][5:6] * acc


def _dense_ffn(x2, h2, mod, wg, wu, wd, tm, tf, rows_per_batch):
    n, d = x2.shape
    tpb = rows_per_batch // tm
    resident = lambda a: pl.BlockSpec(a.shape, lambda i: (0, 0, 0), pipeline_mode=pl.Buffered(1))
    return pl.pallas_call(
        functools.partial(_ffn_kernel, tf=tf),
        grid=(n // tm,),
        in_specs=[
            pl.BlockSpec((tm, d), lambda i: (i, 0)),
            pl.BlockSpec((tm, d), lambda i: (i, 0)),
            pl.BlockSpec((1, 6, d), lambda i: (i // tpb, 0, 0)),
            resident(wg), resident(wu), resident(wd),
        ],
        out_specs=pl.BlockSpec((tm, d), lambda i: (i, 0)),
        out_shape=jax.ShapeDtypeStruct((n, d), F32),
        compiler_params=_cparams(("arbitrary",)),
        name="dense_ffn",
    )(x2, h2, mod, wg, wu, wd)


def _dot_nt(a, b):
    return lax.dot_general(a, b, (((1,), (1,)), ((), ())), preferred_element_type=F32)


def _qkv_kernel(x_ref, mod_ref, kvmod_ref, gmix_ref, gkv_ref, wdq_ref, qg_ref, wqn_t_ref,
                wqr_t_ref, wdkv_ref, ckvg_ref, wkr_ref, wuk_ref, wuv_t_ref, tab_ref, tab_t_ref,
                qt_ref, k_ref, vt_ref):
    x = x_ref[0]
    xn = _rms_scale(x)
    mod = mod_ref[0]
    kvmod = kvmod_ref[0]
    hq = ((xn * gmix_ref[...]) * (1.0 + mod[1:2]) + mod[0:1]).astype(BF16)
    hk = ((xn * gkv_ref[...]) * (1.0 + kvmod[1:2]) + kvmod[0:1]).astype(BF16)

    cq = (_rms_scale(_dot(hq, wdq_ref[...])) * qg_ref[...]).astype(BF16)
    qn_t = _dot_nt(wqn_t_ref[...], cq) * Q_SCALE
    qr_t = _dot_nt(wqr_t_ref[...], cq) * Q_SCALE
    cos_t = tab_t_ref[0, 0, 0:ROPE_HALF, :]
    sin_t = tab_t_ref[0, 0, 2 * LANES + ROPE_HALF:2 * LANES + D_ROPE, :]
    zero_rows = jnp.zeros((HEAD_PAD - LANES - D_ROPE, x.shape[0]), BF16)

    ckv = (_rms_scale(_dot(hk, wdkv_ref[...])) * ckvg_ref[...]).astype(BF16)
    kn = _dot(ckv, wuk_ref[...])
    vt_ref[0, 0] = _dot_nt(wuv_t_ref[...], ckv).astype(BF16)
    kr = _apply_rope(_dot(hk, wkr_ref[...]), tab_ref[0]).astype(BF16)

    for h in range(N_HEADS):
        lo = h * HEAD_PAD
        hs = slice(h * LANES, (h + 1) * LANES)
        qt_ref[0, 0, lo:lo + LANES, :] = qn_t[hs].astype(BF16)
        x1 = qr_t[h * D_ROPE:h * D_ROPE + ROPE_HALF]
        x2 = qr_t[h * D_ROPE + ROPE_HALF:(h + 1) * D_ROPE]
        ro = lo + LANES
        qt_ref[0, 0, ro:ro + ROPE_HALF, :] = (x1 * cos_t - x2 * sin_t).astype(BF16)
        qt_ref[0, 0, ro + ROPE_HALF:ro + D_ROPE, :] = (x2 * cos_t + x1 * sin_t).astype(BF16)
        qt_ref[0, 0, ro + D_ROPE:lo + HEAD_PAD, :] = zero_rows
        k_ref[0, :, lo:lo + LANES] = kn[:, hs].astype(BF16)
        k_ref[0, :, lo + LANES:lo + HEAD_PAD] = kr


def _qkv_proj(x, mod, kvmod, gmix, gkv, wdq, qg, wqn_t, wqr_t, wdkv, ckvg, wkr, wuk, wuv_t,
              tab, tab_t, ts):
    b, s, d = x.shape
    row = lambda i, j: (i, j, 0)
    tile = lambda i, j: (i, j, 0, 0)
    per_b = lambda i, j: (i, 0, 0)

    def full(a):
        return pl.BlockSpec(a.shape, lambda i, j: (0,) * a.ndim)

    weights = (gmix, gkv, wdq, qg, wqn_t, wqr_t, wdkv, ckvg, wkr, wuk, wuv_t)
    return pl.pallas_call(
        _qkv_kernel,
        grid=(b, s // ts),
        in_specs=[pl.BlockSpec((1, ts, d), row),
                  pl.BlockSpec((1, 6, d), per_b),
                  pl.BlockSpec((1, 2, d), per_b)]
                 + [full(a) for a in weights]
                 + [pl.BlockSpec((1, ts, 3 * LANES), row),
                    pl.BlockSpec((1, 1, 3 * LANES, ts), tile)],
        out_specs=[pl.BlockSpec((1, 1, N_HEADS * HEAD_PAD, ts), tile),
                   pl.BlockSpec((1, ts, N_HEADS * HEAD_PAD), row),
                   pl.BlockSpec((1, 1, N_HEADS * D_V, ts), tile)],
        out_shape=[jax.ShapeDtypeStruct((b, s // ts, N_HEADS * HEAD_PAD, ts), BF16),
                   jax.ShapeDtypeStruct((b, s, N_HEADS * HEAD_PAD), BF16),
                   jax.ShapeDtypeStruct((b, s // ts, N_HEADS * D_V, ts), BF16)],
        compiler_params=_cparams(("arbitrary", "arbitrary")),
        name="qkv_proj",
    )(x, mod, kvmod, *weights, tab, tab_t)


def _attn_kernel(qt_ref, k_ref, vt_ref, eg_ref, eu_ref, ed_ref,
                 o_ref, eg_out_ref, eu_out_ref, ed_out_ref, *, tq, tk):
    ts = qt_ref.shape[3]
    s = k_ref.shape[1]
    key = lax.broadcasted_iota(jnp.int32, (tk, tq), 0)
    qry = lax.broadcasted_iota(jnp.int32, (tk, tq), 1)
    last_block = lambda qi: ((qi + 1) * tq - 1) // tk
    blocks = [(qi, j) for qi in range(s // tq) for j in range(last_block(qi) + 1)]

    def tile(ref, start, size):
        return ref[0, start // ts, :, start % ts:start % ts + size]

    def scores_t(qi, j):
        st = _dot(k_ref[0, j * tk:(j + 1) * tk, :], tile(qt_ref, qi * tq, tq))
        if (j + 1) * tk - 1 > qi * tq:
            st = jnp.where(key + (j * tk - qi * tq) <= qry, st, -jnp.inf)
        return st

    ahead = [scores_t(*blk) for blk in blocks[:ATTN_LOOKAHEAD]]
    for n, (qi, j) in enumerate(blocks):
        st = ahead.pop(0)
        if n + ATTN_LOOKAHEAD < len(blocks):
            ahead.append(scores_t(*blocks[n + ATTN_LOOKAHEAD]))
        if j == 0:
            m = jnp.full((1, tq), -jnp.inf, F32)
            l = jnp.zeros((1, tq), F32)
            acc = jnp.zeros((D_V, tq), F32)
        m_new = jnp.maximum(m, jnp.max(st, axis=0, keepdims=True))
        alpha = jnp.exp2(m - m_new)
        pt = jnp.exp2(st - m_new)
        l = alpha * l + jnp.sum(pt, axis=0, keepdims=True)
        acc = alpha * acc + _dot(tile(vt_ref, j * tk, tk), pt.astype(BF16))
        m = m_new
        if j == last_block(qi):
            o_ref[0, qi * tq:(qi + 1) * tq, :] = (acc / l).T.astype(BF16)

    _cast_side_job((eg_ref, eu_ref, ed_ref), (eg_out_ref, eu_out_ref, ed_out_ref))


def _attention(qt, k, vt, eg, eu, ed, ts):
    b, s, _ = k.shape
    n_exp, d, eff = eg.shape
    gate_up, down = _cast_specs(n_exp, d, eff, b * N_HEADS, lambda i, h: i * N_HEADS + h)
    return pl.pallas_call(
        functools.partial(_attn_kernel, tq=min(ts, ATTN_QUERY_TILE), tk=min(ts, ATTN_KEY_BLOCK)),
        grid=(b, N_HEADS),
        in_specs=[
            pl.BlockSpec((1, s // ts, HEAD_PAD, ts), lambda i, h: (i, 0, h, 0)),
            pl.BlockSpec((1, s, HEAD_PAD), lambda i, h: (i, 0, h)),
            pl.BlockSpec((1, s // ts, D_V, ts), lambda i, h: (i, 0, h, 0)),
            gate_up, gate_up, down,
        ],
        out_specs=[pl.BlockSpec((1, s, D_V), lambda i, h: (i, 0, h)), gate_up, gate_up, down],
        out_shape=[jax.ShapeDtypeStruct((b, s, N_HEADS * D_V), BF16),
                   jax.ShapeDtypeStruct(eg.shape, BF16),
                   jax.ShapeDtypeStruct(eu.shape, BF16),
                   jax.ShapeDtypeStruct(ed.shape, BF16)],
        compiler_params=_cparams(("arbitrary", "arbitrary")),
        name="flash_attn",
    )(qt, k, vt, eg, eu, ed)


def _attn_out_kernel(o_ref, x_ref, mod_ref, wo_ref, g_ref, wr_ref, ltri_ref,
                     x_out_ref, h_out_ref, rcol_ref, rrow_ref, cnt_ref, carry_ref,
                     *, n_exp, tiles_per_block):
    i = pl.program_id(0)

    @pl.when(i % tiles_per_block == 0)
    def _():
        carry_ref[...] = jnp.zeros_like(carry_ref)

    mod = mod_ref[0]
    x = x_ref[...] + mod[2:3] * _dot(o_ref[...], wo_ref[...])
    x_out_ref[...] = x
    h = (_rms_scale(x) * g_ref[...]) * (1.0 + mod[4:5]) + mod[3:4]
    h_hi = h.astype(BF16)
    h_out_ref[...] = h_hi

    tm = x.shape[0]
    h_lo = (h - h_hi.astype(F32)).astype(BF16)
    parts = _dot(jnp.concatenate([h_hi, h_lo], axis=0), wr_ref[...])
    logits = (parts[:tm, :LANES] + parts[:tm, LANES:]) + (parts[tm:, :LANES] + parts[tm:, LANES:])
    lane = lax.broadcasted_iota(jnp.int32, (tm, LANES), 1)
    lane_f = lane.astype(F32)

    def first_argmax(vals):
        m = jnp.max(vals, axis=-1, keepdims=True)
        idx = jnp.min(jnp.where(vals == m, lane_f, float(LANES)), axis=-1, keepdims=True)
        return m, idx.astype(jnp.int32)

    logits = jnp.where(lane < n_exp, logits, -jnp.inf)
    m1, i1 = first_argmax(logits)
    m2, i2 = first_argmax(jnp.where(lane == i1, -jnp.inf, logits))
    ex = jnp.exp(m2 - m1)
    w1 = 1.0 / (1.0 + ex)
    w2 = ex / (1.0 + ex)

    mask = jnp.where((lane == i1) | (lane == i2), 1.0, 0.0)
    ranks = _dot(ltri_ref[...], mask.astype(BF16)) + carry_ref[0:1]
    incl = ranks + mask
    rank1 = jnp.sum(jnp.where(lane == i1, ranks, 0.0), axis=-1, keepdims=True)
    rank2 = jnp.sum(jnp.where(lane == i2, ranks, 0.0), axis=-1, keepdims=True)
    carry_ref[0:1] = incl[tm - 1:tm]
    n_sub = tm // rrow_ref.shape[2]
    sub = rrow_ref.shape[2]
    cnt_ref[0] = jnp.zeros(cnt_ref.shape[1:], F32)
    for j in range(n_sub):
        cnt_ref[0, j:j + 1] = incl[(j + 1) * sub - 1:(j + 1) * sub]

    fields = (i1.astype(F32), i2.astype(F32), rank1, rank2, w1, w2)
    rcol = jnp.zeros((tm, LANES), F32)
    for k, val in enumerate(fields):
        rcol = jnp.where(lane == k, val, rcol)
    rcol_ref[...] = rcol
    rrow = rcol.T
    for j in range(n_sub):
        rrow_ref[j] = rrow[0:ROUTE_FIELDS, j * sub:(j + 1) * sub]


def _attn_out(o2, x2, mod, wo, g, wr, n_exp, ltri, tm, sub, rows_per_block):
    n, d = x2.shape
    tpb = rows_per_block // tm
    n_sub = tm // sub
    return pl.pallas_call(
        functools.partial(_attn_out_kernel, n_exp=n_exp, tiles_per_block=tpb),
        grid=(n // tm,),
        in_specs=[
            pl.BlockSpec((tm, o2.shape[1]), lambda i: (i, 0)),
            pl.BlockSpec((tm, d), lambda i: (i, 0)),
            pl.BlockSpec((1, 6, d), lambda i: (i // tpb, 0, 0)),
            pl.BlockSpec(wo.shape, lambda i: (0, 0)),
            pl.BlockSpec((1, d), lambda i: (0, 0)),
            pl.BlockSpec(wr.shape, lambda i: (0, 0)),
            pl.BlockSpec((tm, tm), lambda i: (0, 0)),
        ],
        out_specs=[pl.BlockSpec((tm, d), lambda i: (i, 0)),
                   pl.BlockSpec((tm, d), lambda i: (i, 0)),
                   pl.BlockSpec((tm, LANES), lambda i: (i, 0)),
                   pl.BlockSpec((n_sub, ROUTE_FIELDS, sub), lambda i: (i, 0, 0)),
                   pl.BlockSpec((1, 8, LANES), lambda i: (i, 0, 0))],
        out_shape=[jax.ShapeDtypeStruct((n, d), F32),
                   jax.ShapeDtypeStruct((n, d), BF16),
                   jax.ShapeDtypeStruct((n, LANES), F32),
                   jax.ShapeDtypeStruct((n // sub, ROUTE_FIELDS, sub), F32),
                   jax.ShapeDtypeStruct((n // tm, 8, LANES), F32)],
        scratch_shapes=[pltpu.VMEM((8, LANES), F32)],
        compiler_params=_cparams(("arbitrary",)),
        name="attn_out_router",
    )(o2, x2, mod, wo, g, wr, ltri)


def _dispatch_kernel(off_ref, lo_ref, hi_ref, cexp_ref, h_ref, rrow_ref, xg_ref, gs_ref,
                     *, n_exp, sub, chunks_per_step, chunks_per_block):
    blk = pl.program_id(0)
    g = pl.program_id(1)
    d = h_ref.shape[1]
    n_sub = rrow_ref.shape[0]
    win = min(DISPATCH_WINDOW, n_sub)
    iota = lax.broadcasted_iota(jnp.int32, (DISPATCH_CHUNK, 1), 0)

    def contribution(info, sb):
        row, exp_id, exp_off = info
        r = rrow_ref[sb]
        first = r[0:1] == exp_id
        second = r[1:2] == exp_id
        pos = jnp.where(first, r[2:3] + exp_off, jnp.where(second, r[3:4] + exp_off, -1.0))
        hit = pos == row
        onehot = jnp.where(hit, 1.0, 0.0).astype(BF16)
        tok = h_ref[pl.ds(pl.multiple_of(sb * sub, sub), sub), :]
        gate = jnp.sum(jnp.where(hit, jnp.where(first, r[4:5], r[5:6]), 0.0),
                       axis=-1, keepdims=True)
        return _dot(onehot, tok), gate

    def chunk_info(j):
        c = g * chunks_per_step + j
        t = blk * chunks_per_block + c
        row = (c * DISPATCH_CHUNK + iota).astype(F32)
        exp_id = cexp_ref[t]
        exp_off = off_ref[blk * n_exp + exp_id].astype(F32)
        start = jnp.minimum(lo_ref[t], n_sub - win)
        return (row, exp_id.astype(F32), exp_off), start, hi_ref[t]

    first_chunk = blk * chunks_per_block + g * chunks_per_step
    used = functools.reduce(jnp.logical_or,
                            [hi_ref[first_chunk + j] >= 0 for j in range(chunks_per_step)])

    @pl.when(jnp.logical_not(used))
    def _():
        xg_ref[...] = jnp.zeros(xg_ref.shape, BF16)
        gs_ref[...] = jnp.zeros(gs_ref.shape, F32)

    @pl.when(used)
    def _():
        for j in range(chunks_per_step):
            info, start, _ = chunk_info(j)
            acc = jnp.zeros((DISPATCH_CHUNK, d), F32)
            gate = jnp.zeros((DISPATCH_CHUNK, 1), F32)
            for k in range(win):
                a, b = contribution(info, start + k)
                acc += a
                gate += b
            rows = slice(j * DISPATCH_CHUNK, (j + 1) * DISPATCH_CHUNK)
            xg_ref[rows, :] = acc.astype(BF16)
            gs_ref[rows, :] = gate

        def extra(j, _):
            info, start, last = chunk_info(j)
            rows = pl.ds(pl.multiple_of(j * DISPATCH_CHUNK, DISPATCH_CHUNK), DISPATCH_CHUNK)

            def add(sb, _):
                a, b = contribution(info, sb)
                xg_ref[rows, :] = (xg_ref[rows, :].astype(F32) + a).astype(BF16)
                gs_ref[rows, :] = gs_ref[rows, :] + b
                return 0

            return lax.fori_loop(start + win, last + 1, add, 0)

        lax.fori_loop(0, chunks_per_step, extra, 0)


def _dispatch(off, lo, hi, cexp, h, rrow, n_blocks, rows_per_block, rcap, n_exp, sub):
    n, d = h.shape
    n_sub = rows_per_block // sub
    step_rows = _pick(rcap, 1024)
    cps = step_rows // DISPATCH_CHUNK
    cpb = rcap // DISPATCH_CHUNK
    grid_spec = pltpu.PrefetchScalarGridSpec(
        num_scalar_prefetch=4,
        grid=(n_blocks, rcap // step_rows),
        in_specs=[
            pl.BlockSpec((rows_per_block, d), lambda b, g, *_: (b, 0)),
            pl.BlockSpec((n_sub, ROUTE_FIELDS, sub), lambda b, g, *_: (b, 0, 0)),
        ],
        out_specs=[
            pl.BlockSpec((step_rows, d), lambda b, g, *_: (b * (rcap // step_rows) + g, 0)),
            pl.BlockSpec((step_rows, 1), lambda b, g, *_: (b * (rcap // step_rows) + g, 0)),
        ],
    )
    return pl.pallas_call(
        functools.partial(_dispatch_kernel, n_exp=n_exp, sub=sub, chunks_per_step=cps,
                          chunks_per_block=cpb),
        grid_spec=grid_spec,
        out_shape=[jax.ShapeDtypeStruct((n_blocks * rcap, d), BF16),
                   jax.ShapeDtypeStruct((n_blocks * rcap, 1), F32)],
        compiler_params=_cparams(("arbitrary", "arbitrary")),
        name="moe_dispatch",
    )(off, lo, hi, cexp, h, rrow)


def _expert_ffn_kernel(perm_ref, pexp_ref, pfill_ref, x_ref, gs_ref, wg_ref, wu_ref, wd_ref,
                       y_ref, *, tf):
    w = pl.program_id(0)
    fill = pfill_ref[w]
    ff = wg_ref.shape[2]

    def ffn(n_rows):
        x = x_ref[0:n_rows, :]
        gs = gs_ref[0:n_rows, :]
        acc = jnp.zeros((n_rows, y_ref.shape[1]), F32)
        for f0 in range(0, ff, tf):
            gate = _dot(x, wg_ref[0, :, f0:f0 + tf])
            up = _dot(x, wu_ref[0, :, f0:f0 + tf])
            a = ((_silu(gate) * up) * gs).astype(BF16)
            acc = acc + _dot(a, wd_ref[0, f0:f0 + tf, :])
        return acc.astype(BF16)

    for level in range(PIECE // PIECE_QUANTUM + 1):
        n_rows = level * PIECE_QUANTUM

        @pl.when(fill == level)
        def _(n_rows=n_rows):
            if n_rows:
                y_ref[0:n_rows, :] = ffn(n_rows)
            if n_rows < PIECE:
                y_ref[n_rows:PIECE, :] = jnp.zeros((PIECE - n_rows, y_ref.shape[1]), BF16)


def _expert_ffn(perm, pexp, pfill, xg, gs, wg, wu, wd, tf):
    rows, d = xg.shape
    n_exp, _, ff = wg.shape
    grid_spec = pltpu.PrefetchScalarGridSpec(
        num_scalar_prefetch=3,
        grid=(rows // PIECE,),
        in_specs=[
            pl.BlockSpec((PIECE, d), lambda w, perm, pexp, pfill: (perm[w], 0)),
            pl.BlockSpec((PIECE, 1), lambda w, perm, pexp, pfill: (perm[w], 0)),
            pl.BlockSpec((1, d, ff), lambda w, perm, pexp, pfill: (pexp[w], 0, 0)),
            pl.BlockSpec((1, d, ff), lambda w, perm, pexp, pfill: (pexp[w], 0, 0)),
            pl.BlockSpec((1, ff, d), lambda w, perm, pexp, pfill: (pexp[w], 0, 0)),
        ],
        out_specs=pl.BlockSpec((PIECE, d), lambda w, perm, pexp, pfill: (perm[w], 0)),
    )
    return pl.pallas_call(
        functools.partial(_expert_ffn_kernel, tf=tf),
        grid_spec=grid_spec,
        out_shape=jax.ShapeDtypeStruct((rows, d), BF16),
        compiler_params=_cparams(("arbitrary",)),
        name="expert_ffn",
    )(perm, pexp, pfill, xg, gs, wg, wu, wd)


def _combine_kernel(off_ref, start_ref, end_ref, yg_ref, rcol_ref, x_ref, mod_ref, gfin_ref, o_ref,
                    acc_ref, *, n_exp, sub, subs_per_block):
    blk = pl.program_id(0)
    t = pl.program_id(1)
    tm, d = x_ref.shape
    col = lax.broadcasted_iota(jnp.int32, (1, COMBINE_WINDOW), 1)
    align = COMBINE_WINDOW // 2

    def token_info(s):
        rc = rcol_ref[s * sub:(s + 1) * sub, :]
        return tuple(jnp.broadcast_to(rc[:, k:k + 1], (sub, COMBINE_WINDOW)) for k in range(4))

    def gathered(info, e, w0):
        i1, i2, rank1, rank2 = info
        rank = jnp.where(i1 == e, rank1, jnp.where(i2 == e, rank2, -float(COMBINE_WINDOW)))
        in_region = (w0 - off_ref[blk * n_exp + e] + col).astype(F32)
        onehot = jnp.where(rank == in_region, 1.0, 0.0).astype(BF16)
        return _dot(onehot, yg_ref[pl.ds(pl.multiple_of(w0, align), COMBINE_WINDOW), :])

    def first_window(s, e):
        idx = (blk * subs_per_block + t * (tm // sub) + s) * n_exp + e
        w0 = lax.shift_right_logical(start_ref[idx], align.bit_length() - 1) * align
        return w0, end_ref[idx]

    for s in range(tm // sub):
        info = token_info(s)
        acc = jnp.zeros((sub, d), F32)
        for e in range(n_exp):
            acc += gathered(info, e, first_window(s, e)[0])
        acc_ref[s * sub:(s + 1) * sub, :] = acc

    for s in range(tm // sub):
        info = token_info(s)
        for e in range(n_exp):
            w0, end = first_window(s, e)
            n_more = lax.shift_right_logical(
                jnp.maximum(end - w0 - 1, 0), COMBINE_WINDOW.bit_length() - 1)

            def add(k, _, info=info, e=e, w0=w0, s=s):
                rows = slice(s * sub, (s + 1) * sub)
                acc_ref[rows, :] = acc_ref[rows, :] + gathered(info, e, w0 + k * COMBINE_WINDOW)
                return 0

            lax.fori_loop(1, n_more + 1, add, 0)

    x = x_ref[...] + mod_ref[0][5:6] * acc_ref[...]
    o_ref[...] = _rms_scale(x) * gfin_ref[...]


def _combine(off, lo, hi, yg, rcol, x2, mod, gfin, n_blocks, rows_per_block, rcap, n_exp, sub, tm):
    n, d = x2.shape
    tpb = rows_per_block // tm
    grid_spec = pltpu.PrefetchScalarGridSpec(
        num_scalar_prefetch=3,
        grid=(n_blocks, tpb),
        in_specs=[
            pl.BlockSpec((rcap, d), lambda b, t, *_: (b, 0), pipeline_mode=pl.Buffered(1)),
            pl.BlockSpec((tm, LANES), lambda b, t, *_: (b * tpb + t, 0)),
            pl.BlockSpec((tm, d), lambda b, t, *_: (b * tpb + t, 0)),
            pl.BlockSpec((1, 6, d), lambda b, t, *_: (b, 0, 0)),
            pl.BlockSpec((1, d), lambda b, t, *_: (0, 0)),
        ],
        out_specs=pl.BlockSpec((tm, d), lambda b, t, *_: (b * tpb + t, 0)),
        scratch_shapes=[pltpu.VMEM((tm, d), F32)],
    )
    return pl.pallas_call(
        functools.partial(_combine_kernel, n_exp=n_exp, sub=sub,
                          subs_per_block=rows_per_block // sub),
        grid_spec=grid_spec,
        out_shape=jax.ShapeDtypeStruct((n, d), F32),
        compiler_params=_cparams(("arbitrary", "arbitrary")),
        name="moe_combine",
    )(off, lo, hi, yg, rcol, x2, mod, gfin)


def _routing_tables(cnt, n_blocks, n_sub, n_exp, rcap):
    cum_incl = cnt.astype(jnp.int32)
    cum_excl = jnp.concatenate([jnp.zeros_like(cum_incl[:, :1]), cum_incl[:, :-1]], axis=1)
    total = cum_incl[:, -1]
    psz = (total + PIECE - 1) // PIECE * PIECE
    off = jnp.cumsum(psz, axis=1) - psz
    start = off[:, None, :] + cum_excl
    end = off[:, None, :] + cum_incl

    c0 = jnp.arange(rcap // DISPATCH_CHUNK, dtype=jnp.int32)[None, :, None, None] * DISPATCH_CHUNK
    hit = ((start[:, None] < c0 + DISPATCH_CHUNK) & (end[:, None] > c0)
           & (end[:, None] > start[:, None])).any(-1)
    sub_ids = jnp.arange(n_sub, dtype=jnp.int32)
    d_lo = jnp.min(jnp.where(hit, sub_ids, n_sub), axis=-1)
    d_hi = jnp.max(jnp.where(hit, sub_ids, -1), axis=-1)
    d_lo = jnp.where(d_hi < 0, 0, d_lo)


    ppb = rcap // PIECE
    prow = jnp.arange(ppb, dtype=jnp.int32)[None, :, None] * PIECE
    inside = (prow >= off[:, None, :]) & (prow < (off + psz)[:, None, :])
    p_exp = jnp.argmax(inside, axis=-1).astype(jnp.int32)
    used = inside.any(-1)
    left = jnp.take_along_axis(total + off, p_exp, axis=1) - prow[..., 0]
    p_fill = jnp.where(used, jnp.clip((left + PIECE_QUANTUM - 1) // PIECE_QUANTUM, 0,
                                      PIECE // PIECE_QUANTUM), 0).astype(jnp.int32)
    key = jnp.where(p_fill > 0, p_exp, n_exp).reshape(-1)
    perm = jnp.argsort(key, stable=True).astype(jnp.int32)
    p_exp_sorted = jnp.minimum(key[perm], n_exp - 1).astype(jnp.int32)
    d_exp = jnp.repeat(p_exp, PIECE // DISPATCH_CHUNK, axis=1).reshape(-1)
    return (off.reshape(-1).astype(jnp.int32), d_lo.reshape(-1), d_hi.reshape(-1), d_exp,
            start.reshape(-1).astype(jnp.int32), end.reshape(-1).astype(jnp.int32),
            perm, p_exp_sorted, p_fill.reshape(-1)[perm])


def _pick(n, pref):
    t = min(n, pref)
    while n % t:
        t //= 2
    return t


def kernel(x, c, positions, ada_w, ada_b, mix_norm_g, ffn_norm_g, pool_w_in, pool_w_grp,
           pool_scale, pool_w_out, kv_ada_w, kv_ada_b, kv_norm_g, w_dkv, ckv_norm_g, w_kr,
           w_uk, w_uv, w_dq, q_norm_g, w_uq, w_o, ffn_w_gate, ffn_w_up, ffn_w_down,
           router_w, moe_w_gate, moe_w_up, moe_w_down, final_norm_g):
    b, s, d = x.shape
    n = b * s
    depth = ada_w.shape[0]
    assert depth == 2 and pool_w_in.shape[0] == 1 and w_dq.shape[0] == 1
    assert b <= 8 and d % LANES == 0

    ts = _pick(s, 512)
    tm = _pick(s, 512)
    tf = _pick(ffn_w_gate.shape[-1], 512)

    c_pad = jnp.zeros((8, d), F32).at[:b].set(c)
    mods = _adaln(c_pad, ada_w, ada_b[:, None, :], _pick(6 * d, 1536))
    mods = mods[:, :b].reshape(depth, b, 6, d)
    kvmod = _adaln(c_pad, kv_ada_w[None], kv_ada_b[None, None, :], _pick(2 * d, 1024))
    kvmod = kvmod[0, :b].reshape(b, 2, d)

    r = jnp.arange(min(ts, POOL_ROW_BLOCK))[:, None]
    cidx = jnp.arange(min(ts, POOL_ROW_BLOCK) + POOL_HALO)[None, :] - POOL_HALO
    band = jnp.stack([((cidx <= r) & (cidx > r - w)) for w in POOL_WINDOWS]).astype(BF16)
    inv_freq = ROPE_THETA ** (-jnp.arange(0, D_ROPE, 2, dtype=F32) / D_ROPE)
    freq_row = jnp.concatenate([inv_freq, inv_freq, jnp.zeros((LANES - D_ROPE,), F32)])[None, :]
    x1, fwg, fwu, fwd, rope_tab, rope_tab_t, h1 = _pool_mixer(
        x, mods[0], mix_norm_g[0:1], pool_w_in[0].astype(BF16), band, pool_w_grp[0].astype(BF16),
        pool_scale[0:1], pool_w_out[0].astype(BF16), ffn_w_gate, ffn_w_up, ffn_w_down,
        positions.astype(F32)[..., None], freq_row, ffn_norm_g[0:1], ts)

    x2 = _dense_ffn(x1.reshape(n, d), h1.reshape(n, d), mods[0], fwg, fwu, fwd, tm, tf, s)


    q_lora = w_dq.shape[-1]
    wuq = w_uq[0].reshape(q_lora, N_HEADS, D_NOPE + D_ROPE)
    wqn_t = wuq[:, :, :D_NOPE].reshape(q_lora, N_HEADS * D_NOPE).T.astype(BF16)
    wqr_t = wuq[:, :, D_NOPE:].reshape(q_lora, N_HEADS * D_ROPE).T.astype(BF16)
    wkr = jnp.pad(w_kr, ((0, 0), (0, LANES - D_ROPE))).astype(BF16)
    qt, k, vt = _qkv_proj(
        x2.reshape(b, s, d), mods[1], kvmod, mix_norm_g[1:2], kv_norm_g[None, :],
        w_dq[0].astype(BF16), q_norm_g[0:1], wqn_t, wqr_t, w_dkv.astype(BF16),
        ckv_norm_g[None, :], wkr, w_uk.astype(BF16), w_uv.T.astype(BF16), rope_tab, rope_tab_t, ts)
    o, ewg, ewu, ewd = _attention(qt, k, vt, moe_w_gate[0], moe_w_up[0], moe_w_down[0], ts)

    n_exp = router_w.shape[-1]
    sub = _pick(s, ROUTE_SUB)
    n_sub = s // sub
    rcap = 2 * s + n_exp * PIECE
    ltri = (jnp.arange(ts)[None, :] < jnp.arange(ts)[:, None]).astype(BF16)
    assert n_exp <= LANES
    wr = jnp.pad(router_w[0], ((0, 0), (0, LANES - n_exp)))
    wr_hi = wr.astype(BF16)
    wr_split = jnp.concatenate([wr_hi, (wr - wr_hi.astype(F32)).astype(BF16)], axis=1)
    x3, h3, rcol, rrow, cnt = _attn_out(o.reshape(n, N_HEADS * D_V), x2, mods[1],
                                        w_o[0].astype(BF16), ffn_norm_g[1:2], wr_split, n_exp,
                                        ltri, ts, sub, s)
    cnt = cnt[:, :ts // sub, :n_exp].reshape(b, n_sub, n_exp)
    off, d_lo, d_hi, d_exp, c_lo, c_hi, perm, p_exp, p_fill = _routing_tables(cnt, b, n_sub, n_exp, rcap)
    xg, gs = _dispatch(off, d_lo, d_hi, d_exp, h3, rrow, b, s, rcap, n_exp, sub)
    yg = _expert_ffn(perm, p_exp, p_fill, xg, gs, ewg, ewu, ewd, tf)
    out = _combine(off, c_lo, c_hi, yg, rcol, x3, mods[1], final_norm_g[None, :], b, s, rcap,
                   n_exp, sub, ts)
    return out.reshape(b, s, d)
```

```python
import functools
import math

import jax
import jax.numpy as jnp
from jax import lax
from jax.experimental import pallas as pl
from jax.experimental.pallas import tpu as pltpu

F32 = jnp.float32
BF16 = jnp.bfloat16

N_HEADS = 8
D_NOPE = 128
D_ROPE = 64
D_V = 128
HEAD_PAD = 256
ROPE_HALF = D_ROPE // 2
ROPE_THETA = 10000.0
ATTN_SCALE = 1.0 / math.sqrt(D_NOPE + D_ROPE)
Q_SCALE = ATTN_SCALE * math.log2(math.e)
ATTN_QUERY_TILE = 512
ATTN_KEY_BLOCK = 512
ATTN_LOOKAHEAD = 2
POOL_WINDOWS = (2, 4, 8, 16)
POOL_HALO = 16
POOL_ROW_BLOCK = 128
EPS = 1e-6

ROUTE_SUB = 256
DISPATCH_CHUNK = 128
DISPATCH_WINDOW = 3
COMBINE_WINDOW = 256
PIECE = 512
PIECE_QUANTUM = 128
ROUTE_FIELDS = 8

LANES = 128
VMEM_LIMIT_BYTES = 56 * 1024 * 1024


def _cparams(sem):
    return pltpu.CompilerParams(dimension_semantics=sem, vmem_limit_bytes=VMEM_LIMIT_BYTES)


def _dot(a, b):
    return jnp.dot(a, b, preferred_element_type=F32)


def _rms_scale(x):
    return x * lax.rsqrt(jnp.mean(x * x, axis=-1, keepdims=True) + EPS)


def _silu(x):
    return x * jax.nn.sigmoid(x)


def _adaln_kernel(c_ref, w_ref, b_ref, o_ref):
    sc = _silu(c_ref[...])
    o_ref[0] = jnp.dot(sc, w_ref[0], preferred_element_type=F32,
                       precision=lax.Precision.HIGHEST) + b_ref[0]


def _adaln(c_pad, w, b, tn):
    nl, d, m = w.shape
    return pl.pallas_call(
        _adaln_kernel,
        grid=(nl, m // tn),
        in_specs=[
            pl.BlockSpec((8, d), lambda l, j: (0, 0)),
            pl.BlockSpec((1, d, tn), lambda l, j: (l, 0, j)),
            pl.BlockSpec((1, 1, tn), lambda l, j: (l, 0, j)),
        ],
        out_specs=pl.BlockSpec((1, 8, tn), lambda l, j: (l, 0, j)),
        out_shape=jax.ShapeDtypeStruct((nl, 8, m), F32),
        compiler_params=_cparams(("arbitrary", "arbitrary")),
        name="adaln",
    )(c_pad, w, b)


def _rope_table_side_job(pos_ref, freq_ref, cos_ref, sin_ref):
    ang = freq_ref[...] * pos_ref[0]
    cos_ref[0, 0] = jnp.cos(ang)
    sin_ref[0, 0] = jnp.sin(ang)


def _rope_t(x, cos_t, sin_t):
    x1, x2 = x[:ROPE_HALF], x[ROPE_HALF:]
    return x1 * cos_t - x2 * sin_t, x2 * cos_t + x1 * sin_t


def _cast_specs(n_exp, d, ff, steps, step_of):
    assert steps >= n_exp, "too few grid steps to carry the weight cast"
    k = max(c for c in range(1, steps // n_exp + 1) if ff % (c * LANES) == 0)
    cw = ff // k

    def chunk(*idx):
        c = jnp.minimum(step_of(*idx), n_exp * k - 1)
        return c // k, c % k

    gate_up = pl.BlockSpec((1, d, cw), lambda *idx: (chunk(*idx)[0], 0, chunk(*idx)[1]))
    down = pl.BlockSpec((1, cw, d), lambda *idx: (chunk(*idx)[0], chunk(*idx)[1], 0))
    return gate_up, down


def _cast_side_job(src_refs, dst_refs):
    for src, dst in zip(src_refs, dst_refs):
        dst[...] = src[...].astype(BF16)


def _pool_kernel(x_ref, mod_ref, g_ref, win_ref, band_ref, wgrp_ref, scale_ref, wout_ref,
                 fg_ref, fu_ref, fd_ref, pos_ref, freq_ref, gffn_ref,
                 o_ref, fg_out_ref, fu_out_ref, fd_out_ref, cos_ref, sin_ref, h_out_ref, uext_ref,
                 *, ts, gw):
    s = pl.program_id(1)

    @pl.when(s == 0)
    def _():
        uext_ref[0:POOL_HALO, :] = jnp.zeros((POOL_HALO, uext_ref.shape[1]), BF16)

    x = x_ref[0]
    mod = mod_ref[0]
    h = (_rms_scale(x) * g_ref[...]) * (1.0 + mod[1:2]) + mod[0:1]
    u = _dot(h.astype(BF16), win_ref[...])
    uext_ref[POOL_HALO:POOL_HALO + ts, :] = u.astype(BF16)

    t = s * ts + lax.broadcasted_iota(jnp.int32, (ts, 1), 0)
    zs = []
    for g, w in enumerate(POOL_WINDOWS):
        cols = slice(g * gw, (g + 1) * gw)
        rb = band_ref.shape[1]
        wsum = jnp.concatenate(
            [_dot(band_ref[g], uext_ref[r0:r0 + rb + POOL_HALO, cols]) for r0 in range(0, ts, rb)],
            axis=0)
        cnt = jnp.minimum(t + 1, w).astype(F32)
        p = wsum / cnt - u[:, cols]
        zs.append(_dot(p.astype(BF16), wgrp_ref[g]))
    z = jnp.concatenate(zs, axis=-1) * scale_ref[...]
    y = _dot(z.astype(BF16), wout_ref[...])
    x_out = x + mod[2:3] * y
    o_ref[0] = x_out
    h_out_ref[0] = ((_rms_scale(x_out) * gffn_ref[...]) * (1.0 + mod[4:5]) + mod[3:4]).astype(BF16)

    uext_ref[0:POOL_HALO, :] = uext_ref[ts:ts + POOL_HALO, :]

    _cast_side_job((fg_ref, fu_ref, fd_ref), (fg_out_ref, fu_out_ref, fd_out_ref))
    _rope_table_side_job(pos_ref, freq_ref, cos_ref, sin_ref)


def _pool_mixer(x, mod, g, w_in, band, w_grp, scale, w_out, fg, fu, fd, pos_f, freq_row, g_ffn,
                ts):
    b, s, d = x.shape
    ng, gw, _ = w_grp.shape
    n_s = s // ts
    gate_up, down = _cast_specs(1, d, fg.shape[2], b * n_s, lambda i, j: i * n_s + j)
    const2 = lambda i, j: (0, 0)
    const3 = lambda i, j: (0, 0, 0)
    return pl.pallas_call(
        functools.partial(_pool_kernel, ts=ts, gw=gw),
        grid=(b, s // ts),
        in_specs=[
            pl.BlockSpec((1, ts, d), lambda i, j: (i, j, 0)),
            pl.BlockSpec((1, 6, d), lambda i, j: (i, 0, 0)),
            pl.BlockSpec((1, d), const2),
            pl.BlockSpec((d, d), const2),
            pl.BlockSpec(band.shape, const3),
            pl.BlockSpec((ng, gw, gw), const3),
            pl.BlockSpec((1, d), const2),
            pl.BlockSpec((d, d), const2),
            gate_up, gate_up, down,
            pl.BlockSpec((1, 1, ts), lambda i, j: (i, 0, j)),
            pl.BlockSpec((ROPE_HALF, 1), const2),
            pl.BlockSpec((1, d), const2),
        ],
        out_specs=[pl.BlockSpec((1, ts, d), lambda i, j: (i, j, 0)), gate_up, gate_up, down,
                   pl.BlockSpec((1, 1, ROPE_HALF, ts), lambda i, j: (i, j, 0, 0)),
                   pl.BlockSpec((1, 1, ROPE_HALF, ts), lambda i, j: (i, j, 0, 0)),
                   pl.BlockSpec((1, ts, d), lambda i, j: (i, j, 0))],
        out_shape=[jax.ShapeDtypeStruct((b, s, d), F32),
                   jax.ShapeDtypeStruct(fg.shape, BF16),
                   jax.ShapeDtypeStruct(fu.shape, BF16),
                   jax.ShapeDtypeStruct(fd.shape, BF16),
                   jax.ShapeDtypeStruct((b, s // ts, ROPE_HALF, ts), F32),
                   jax.ShapeDtypeStruct((b, s // ts, ROPE_HALF, ts), F32),
                   jax.ShapeDtypeStruct((b, s, d), BF16)],
        scratch_shapes=[pltpu.VMEM((POOL_HALO + ts, d), BF16)],
        compiler_params=_cparams(("arbitrary", "arbitrary")),
        name="pool_mixer",
    )(x, mod, g, w_in, band, w_grp, scale, w_out, fg, fu, fd, pos_f, freq_row, g_ffn)


def _ffn_kernel(x_ref, h_ref, mod_ref, wg_ref, wu_ref, wd_ref, o_ref, *, tf):
    h = h_ref[...]
    ff = wg_ref.shape[2]
    acc = jnp.zeros(x_ref.shape, F32)
    for f0 in range(0, ff, tf):
        gate = _dot(h, wg_ref[0, :, f0:f0 + tf])
        up = _dot(h, wu_ref[0, :, f0:f0 + tf])
        a = (_silu(gate) * up).astype(BF16)
        acc = acc + _dot(a, wd_ref[0, f0:f0 + tf, :])
    o_ref[...] = x_ref[...] + mod_ref[0][5:6] * acc


def _dense_ffn(x2, h2, mod, wg, wu, wd, tm, tf, rows_per_batch):
    n, d = x2.shape
    tpb = rows_per_batch // tm
    resident = lambda a: pl.BlockSpec(a.shape, lambda i: (0, 0, 0), pipeline_mode=pl.Buffered(1))
    return pl.pallas_call(
        functools.partial(_ffn_kernel, tf=tf),
        grid=(n // tm,),
        in_specs=[
            pl.BlockSpec((tm, d), lambda i: (i, 0)),
            pl.BlockSpec((tm, d), lambda i: (i, 0)),
            pl.BlockSpec((1, 6, d), lambda i: (i // tpb, 0, 0)),
            resident(wg), resident(wu), resident(wd),
        ],
        out_specs=pl.BlockSpec((tm, d), lambda i: (i, 0)),
        out_shape=jax.ShapeDtypeStruct((n, d), F32),
        compiler_params=_cparams(("arbitrary",)),
        name="dense_ffn",
    )(x2, h2, mod, wg, wu, wd)


def _dot_nt(a, b):
    return lax.dot_general(a, b, (((1,), (1,)), ((), ())), preferred_element_type=F32)


def _qkv_kernel(x_ref, mod_ref, kvmod_ref, gmix_ref, gkv_ref, wdq_ref, qg_ref, wqn_t_ref,
                wqr_t_ref, wdkv_ref, ckvg_ref, wkr_ref, wuk_ref, wuv_t_ref, cos_ref, sin_ref,
                qt_ref, k_ref, vt_ref):
    x = x_ref[0]
    xn = _rms_scale(x)
    mod = mod_ref[0]
    kvmod = kvmod_ref[0]
    hq = ((xn * gmix_ref[...]) * (1.0 + mod[1:2]) + mod[0:1]).astype(BF16)
    hk = ((xn * gkv_ref[...]) * (1.0 + kvmod[1:2]) + kvmod[0:1]).astype(BF16)

    cq = (_rms_scale(_dot(hq, wdq_ref[...])) * qg_ref[...]).astype(BF16)
    qn_t = _dot_nt(wqn_t_ref[...], cq) * Q_SCALE
    qr_t = _dot_nt(wqr_t_ref[...], cq) * Q_SCALE
    cos_t = cos_ref[0, 0]
    sin_t = sin_ref[0, 0]
    zero_rows = jnp.zeros((HEAD_PAD - LANES - D_ROPE, x.shape[0]), BF16)

    ckv = (_rms_scale(_dot(hk, wdkv_ref[...])) * ckvg_ref[...]).astype(BF16)
    kn = _dot(ckv, wuk_ref[...])
    vt_ref[0, 0] = _dot_nt(wuv_t_ref[...], ckv).astype(BF16)
    kr_raw_t = _dot(hk, wkr_ref[...]).T
    k1, k2 = _rope_t(kr_raw_t[:D_ROPE], cos_t, sin_t)
    kr_t = jnp.concatenate([k1, k2, kr_raw_t[D_ROPE:]], axis=0)
    kr = kr_t.T.astype(BF16)

    for h in range(N_HEADS):
        lo = h * HEAD_PAD
        hs = slice(h * LANES, (h + 1) * LANES)
        qt_ref[0, 0, lo:lo + LANES, :] = qn_t[hs].astype(BF16)
        q1, q2 = _rope_t(qr_t[h * D_ROPE:(h + 1) * D_ROPE], cos_t, sin_t)
        ro = lo + LANES
        qt_ref[0, 0, ro:ro + ROPE_HALF, :] = q1.astype(BF16)
        qt_ref[0, 0, ro + ROPE_HALF:ro + D_ROPE, :] = q2.astype(BF16)
        qt_ref[0, 0, ro + D_ROPE:lo + HEAD_PAD, :] = zero_rows
        k_ref[0, :, lo:lo + LANES] = kn[:, hs].astype(BF16)
        k_ref[0, :, lo + LANES:lo + HEAD_PAD] = kr


def _qkv_proj(x, mod, kvmod, gmix, gkv, wdq, qg, wqn_t, wqr_t, wdkv, ckvg, wkr, wuk, wuv_t,
              cos_t, sin_t, ts):
    b, s, d = x.shape
    row = lambda i, j: (i, j, 0)
    tile = lambda i, j: (i, j, 0, 0)
    per_b = lambda i, j: (i, 0, 0)

    def full(a):
        return pl.BlockSpec(a.shape, lambda i, j: (0,) * a.ndim)

    weights = (gmix, gkv, wdq, qg, wqn_t, wqr_t, wdkv, ckvg, wkr, wuk, wuv_t)
    return pl.pallas_call(
        _qkv_kernel,
        grid=(b, s // ts),
        in_specs=[pl.BlockSpec((1, ts, d), row),
                  pl.BlockSpec((1, 6, d), per_b),
                  pl.BlockSpec((1, 2, d), per_b)]
                 + [full(a) for a in weights]
                 + [pl.BlockSpec((1, 1, ROPE_HALF, ts), tile)] * 2,
        out_specs=[pl.BlockSpec((1, 1, N_HEADS * HEAD_PAD, ts), tile),
                   pl.BlockSpec((1, ts, N_HEADS * HEAD_PAD), row),
                   pl.BlockSpec((1, 1, N_HEADS * D_V, ts), tile)],
        out_shape=[jax.ShapeDtypeStruct((b, s // ts, N_HEADS * HEAD_PAD, ts), BF16),
                   jax.ShapeDtypeStruct((b, s, N_HEADS * HEAD_PAD), BF16),
                   jax.ShapeDtypeStruct((b, s // ts, N_HEADS * D_V, ts), BF16)],
        compiler_params=_cparams(("arbitrary", "arbitrary")),
        name="qkv_proj",
    )(x, mod, kvmod, *weights, cos_t, sin_t)


def _attn_kernel(qt_ref, k_ref, vt_ref, eg_ref, eu_ref, ed_ref,
                 o_ref, eg_out_ref, eu_out_ref, ed_out_ref, *, tq, tk):
    ts = qt_ref.shape[3]
    s = k_ref.shape[1]
    key = lax.broadcasted_iota(jnp.int32, (tk, tq), 0)
    qry = lax.broadcasted_iota(jnp.int32, (tk, tq), 1)
    last_block = lambda qi: ((qi + 1) * tq - 1) // tk
    blocks = [(qi, j) for qi in range(s // tq) for j in range(last_block(qi) + 1)]

    def tile(ref, start, size):
        return ref[0, start // ts, :, start % ts:start % ts + size]

    def scores_t(qi, j):
        st = _dot(k_ref[0, j * tk:(j + 1) * tk, :], tile(qt_ref, qi * tq, tq))
        if (j + 1) * tk - 1 > qi * tq:
            st = jnp.where(key + (j * tk - qi * tq) <= qry, st, -jnp.inf)
        return st

    ahead = [scores_t(*blk) for blk in blocks[:ATTN_LOOKAHEAD]]
    for n, (qi, j) in enumerate(blocks):
        st = ahead.pop(0)
        if n + ATTN_LOOKAHEAD < len(blocks):
            ahead.append(scores_t(*blocks[n + ATTN_LOOKAHEAD]))
        if j == 0:
            m = jnp.full((1, tq), -jnp.inf, F32)
            l = jnp.zeros((1, tq), F32)
            acc = jnp.zeros((D_V, tq), F32)
        m_new = jnp.maximum(m, jnp.max(st, axis=0, keepdims=True))
        alpha = jnp.exp2(m - m_new)
        pt = jnp.exp2(st - m_new)
        l = alpha * l + jnp.sum(pt, axis=0, keepdims=True)
        acc = alpha * acc + _dot(tile(vt_ref, j * tk, tk), pt.astype(BF16))
        m = m_new
        if j == last_block(qi):
            o_ref[0, qi * tq:(qi + 1) * tq, :] = (acc / l).T.astype(BF16)

    _cast_side_job((eg_ref, eu_ref, ed_ref), (eg_out_ref, eu_out_ref, ed_out_ref))


def _attention(qt, k, vt, eg, eu, ed, ts):
    b, s, _ = k.shape
    n_exp, d, eff = eg.shape
    gate_up, down = _cast_specs(n_exp, d, eff, b * N_HEADS, lambda i, h: i * N_HEADS + h)
    return pl.pallas_call(
        functools.partial(_attn_kernel, tq=min(ts, ATTN_QUERY_TILE), tk=min(ts, ATTN_KEY_BLOCK)),
        grid=(b, N_HEADS),
        in_specs=[
            pl.BlockSpec((1, s // ts, HEAD_PAD, ts), lambda i, h: (i, 0, h, 0)),
            pl.BlockSpec((1, s, HEAD_PAD), lambda i, h: (i, 0, h)),
            pl.BlockSpec((1, s // ts, D_V, ts), lambda i, h: (i, 0, h, 0)),
            gate_up, gate_up, down,
        ],
        out_specs=[pl.BlockSpec((1, s, D_V), lambda i, h: (i, 0, h)), gate_up, gate_up, down],
        out_shape=[jax.ShapeDtypeStruct((b, s, N_HEADS * D_V), BF16),
                   jax.ShapeDtypeStruct(eg.shape, BF16),
                   jax.ShapeDtypeStruct(eu.shape, BF16),
                   jax.ShapeDtypeStruct(ed.shape, BF16)],
        compiler_params=_cparams(("arbitrary", "arbitrary")),
        name="flash_attn",
    )(qt, k, vt, eg, eu, ed)


def _attn_out_kernel(o_ref, x_ref, mod_ref, wo_ref, g_ref, wr_ref, ltri_ref,
                     x_out_ref, h_out_ref, rcol_ref, rrow_ref, cnt_ref, carry_ref,
                     *, n_exp, tiles_per_block):
    i = pl.program_id(0)

    @pl.when(i % tiles_per_block == 0)
    def _():
        carry_ref[...] = jnp.zeros_like(carry_ref)

    mod = mod_ref[0]
    x = x_ref[...] + mod[2:3] * _dot(o_ref[...], wo_ref[...])
    x_out_ref[...] = x
    h = (_rms_scale(x) * g_ref[...]) * (1.0 + mod[4:5]) + mod[3:4]
    h_hi = h.astype(BF16)
    h_out_ref[...] = h_hi

    tm = x.shape[0]
    h_lo = (h - h_hi.astype(F32)).astype(BF16)
    parts = _dot(jnp.concatenate([h_hi, h_lo], axis=0), wr_ref[...])
    logits = (parts[:tm, :LANES] + parts[:tm, LANES:]) + (parts[tm:, :LANES] + parts[tm:, LANES:])
    lane = lax.broadcasted_iota(jnp.int32, (tm, LANES), 1)
    lane_f = lane.astype(F32)

    def first_argmax(vals):
        m = jnp.max(vals, axis=-1, keepdims=True)
        idx = jnp.min(jnp.where(vals == m, lane_f, float(LANES)), axis=-1, keepdims=True)
        return m, idx.astype(jnp.int32)

    logits = jnp.where(lane < n_exp, logits, -jnp.inf)
    m1, i1 = first_argmax(logits)
    m2, i2 = first_argmax(jnp.where(lane == i1, -jnp.inf, logits))
    ex = jnp.exp(m2 - m1)
    w1 = 1.0 / (1.0 + ex)
    w2 = ex / (1.0 + ex)

    mask = jnp.where((lane == i1) | (lane == i2), 1.0, 0.0)
    ranks = _dot(ltri_ref[...], mask.astype(BF16)) + carry_ref[0:1]
    incl = ranks + mask
    rank1 = jnp.sum(jnp.where(lane == i1, ranks, 0.0), axis=-1, keepdims=True)
    rank2 = jnp.sum(jnp.where(lane == i2, ranks, 0.0), axis=-1, keepdims=True)
    carry_ref[0:1] = incl[tm - 1:tm]
    n_sub = tm // rrow_ref.shape[2]
    sub = rrow_ref.shape[2]
    cnt_ref[0] = jnp.zeros(cnt_ref.shape[1:], F32)
    for j in range(n_sub):
        cnt_ref[0, j:j + 1] = incl[(j + 1) * sub - 1:(j + 1) * sub]

    fields = (i1.astype(F32), i2.astype(F32), rank1, rank2, w1, w2)
    rcol = jnp.zeros((tm, LANES), F32)
    for k, val in enumerate(fields):
        rcol = jnp.where(lane == k, val, rcol)
    rcol_ref[...] = rcol
    rrow = rcol.T
    for j in range(n_sub):
        rrow_ref[j] = rrow[0:ROUTE_FIELDS, j * sub:(j + 1) * sub]


def _attn_out(o2, x2, mod, wo, g, wr, n_exp, ltri, tm, sub, rows_per_block):
    n, d = x2.shape
    tpb = rows_per_block // tm
    n_sub = tm // sub
    return pl.pallas_call(
        functools.partial(_attn_out_kernel, n_exp=n_exp, tiles_per_block=tpb),
        grid=(n // tm,),
        in_specs=[
            pl.BlockSpec((tm, o2.shape[1]), lambda i: (i, 0)),
            pl.BlockSpec((tm, d), lambda i: (i, 0)),
            pl.BlockSpec((1, 6, d), lambda i: (i // tpb, 0, 0)),
            pl.BlockSpec(wo.shape, lambda i: (0, 0)),
            pl.BlockSpec((1, d), lambda i: (0, 0)),
            pl.BlockSpec(wr.shape, lambda i: (0, 0)),
            pl.BlockSpec((tm, tm), lambda i: (0, 0)),
        ],
        out_specs=[pl.BlockSpec((tm, d), lambda i: (i, 0)),
                   pl.BlockSpec((tm, d), lambda i: (i, 0)),
                   pl.BlockSpec((tm, LANES), lambda i: (i, 0)),
                   pl.BlockSpec((n_sub, ROUTE_FIELDS, sub), lambda i: (i, 0, 0)),
                   pl.BlockSpec((1, 8, LANES), lambda i: (i, 0, 0))],
        out_shape=[jax.ShapeDtypeStruct((n, d), F32),
                   jax.ShapeDtypeStruct((n, d), BF16),
                   jax.ShapeDtypeStruct((n, LANES), F32),
                   jax.ShapeDtypeStruct((n // sub, ROUTE_FIELDS, sub), F32),
                   jax.ShapeDtypeStruct((n // tm, 8, LANES), F32)],
        scratch_shapes=[pltpu.VMEM((8, LANES), F32)],
        compiler_params=_cparams(("arbitrary",)),
        name="attn_out_router",
    )(o2, x2, mod, wo, g, wr, ltri)


def _dispatch_kernel(off_ref, lo_ref, hi_ref, cexp_ref, h_ref, rrow_ref, xg_ref, gs_ref,
                     *, n_exp, sub, chunks_per_step, chunks_per_block):
    blk = pl.program_id(0)
    g = pl.program_id(1)
    d = h_ref.shape[1]
    n_sub = rrow_ref.shape[0]
    win = min(DISPATCH_WINDOW, n_sub)
    iota = lax.broadcasted_iota(jnp.int32, (DISPATCH_CHUNK, 1), 0)

    def contribution(info, sb):
        row, exp_id, exp_off = info
        r = rrow_ref[sb]
        first = r[0:1] == exp_id
        second = r[1:2] == exp_id
        pos = jnp.where(first, r[2:3] + exp_off, jnp.where(second, r[3:4] + exp_off, -1.0))
        hit = pos == row
        onehot = jnp.where(hit, 1.0, 0.0).astype(BF16)
        tok = h_ref[pl.ds(pl.multiple_of(sb * sub, sub), sub), :]
        gate = jnp.sum(jnp.where(hit, jnp.where(first, r[4:5], r[5:6]), 0.0),
                       axis=-1, keepdims=True)
        return _dot(onehot, tok), gate

    def chunk_info(j):
        c = g * chunks_per_step + j
        t = blk * chunks_per_block + c
        row = (c * DISPATCH_CHUNK + iota).astype(F32)
        exp_id = cexp_ref[t]
        exp_off = off_ref[blk * n_exp + exp_id].astype(F32)
        start = jnp.minimum(lo_ref[t], n_sub - win)
        return (row, exp_id.astype(F32), exp_off), start, hi_ref[t]

    first_chunk = blk * chunks_per_block + g * chunks_per_step
    used = functools.reduce(jnp.logical_or,
                            [hi_ref[first_chunk + j] >= 0 for j in range(chunks_per_step)])

    @pl.when(jnp.logical_not(used))
    def _():
        xg_ref[...] = jnp.zeros(xg_ref.shape, BF16)
        gs_ref[...] = jnp.zeros(gs_ref.shape, F32)

    @pl.when(used)
    def _():
        for j in range(chunks_per_step):
            info, start, _ = chunk_info(j)
            acc = jnp.zeros((DISPATCH_CHUNK, d), F32)
            gate = jnp.zeros((DISPATCH_CHUNK, 1), F32)
            for k in range(win):
                a, b = contribution(info, start + k)
                acc += a
                gate += b
            rows = slice(j * DISPATCH_CHUNK, (j + 1) * DISPATCH_CHUNK)
            xg_ref[rows, :] = acc.astype(BF16)
            gs_ref[rows, :] = gate

        def extra(j, _):
            info, start, last = chunk_info(j)
            rows = pl.ds(pl.multiple_of(j * DISPATCH_CHUNK, DISPATCH_CHUNK), DISPATCH_CHUNK)

            def add(sb, _):
                a, b = contribution(info, sb)
                xg_ref[rows, :] = (xg_ref[rows, :].astype(F32) + a).astype(BF16)
                gs_ref[rows, :] = gs_ref[rows, :] + b
                return 0

            return lax.fori_loop(start + win, last + 1, add, 0)

        lax.fori_loop(0, chunks_per_step, extra, 0)


def _dispatch(off, lo, hi, cexp, h, rrow, n_blocks, rows_per_block, rcap, n_exp, sub):
    n, d = h.shape
    n_sub = rows_per_block // sub
    step_rows = _pick(rcap, 1024)
    cps = step_rows // DISPATCH_CHUNK
    cpb = rcap // DISPATCH_CHUNK
    grid_spec = pltpu.PrefetchScalarGridSpec(
        num_scalar_prefetch=4,
        grid=(n_blocks, rcap // step_rows),
        in_specs=[
            pl.BlockSpec((rows_per_block, d), lambda b, g, *_: (b, 0)),
            pl.BlockSpec((n_sub, ROUTE_FIELDS, sub), lambda b, g, *_: (b, 0, 0)),
        ],
        out_specs=[
            pl.BlockSpec((step_rows, d), lambda b, g, *_: (b * (rcap // step_rows) + g, 0)),
            pl.BlockSpec((step_rows, 1), lambda b, g, *_: (b * (rcap // step_rows) + g, 0)),
        ],
    )
    return pl.pallas_call(
        functools.partial(_dispatch_kernel, n_exp=n_exp, sub=sub, chunks_per_step=cps,
                          chunks_per_block=cpb),
        grid_spec=grid_spec,
        out_shape=[jax.ShapeDtypeStruct((n_blocks * rcap, d), BF16),
                   jax.ShapeDtypeStruct((n_blocks * rcap, 1), F32)],
        compiler_params=_cparams(("arbitrary", "arbitrary")),
        name="moe_dispatch",
    )(off, lo, hi, cexp, h, rrow)


def _expert_ffn_kernel(perm_ref, pexp_ref, pfill_ref, x_ref, gs_ref, wg_ref, wu_ref, wd_ref,
                       y_ref, *, tf):
    w = pl.program_id(0)
    fill = pfill_ref[w]
    ff = wg_ref.shape[2]

    def ffn(n_rows):
        x = x_ref[0:n_rows, :]
        gs = gs_ref[0:n_rows, :]
        acc = jnp.zeros((n_rows, y_ref.shape[1]), F32)
        for f0 in range(0, ff, tf):
            gate = _dot(x, wg_ref[0, :, f0:f0 + tf])
            up = _dot(x, wu_ref[0, :, f0:f0 + tf])
            a = ((_silu(gate) * up) * gs).astype(BF16)
            acc = acc + _dot(a, wd_ref[0, f0:f0 + tf, :])
        return acc.astype(BF16)

    for level in range(PIECE // PIECE_QUANTUM + 1):
        n_rows = level * PIECE_QUANTUM

        @pl.when(fill == level)
        def _(n_rows=n_rows):
            if n_rows:
                y_ref[0:n_rows, :] = ffn(n_rows)
            if n_rows < PIECE:
                y_ref[n_rows:PIECE, :] = jnp.zeros((PIECE - n_rows, y_ref.shape[1]), BF16)


def _expert_ffn(perm, pexp, pfill, xg, gs, wg, wu, wd, tf):
    rows, d = xg.shape
    n_exp, _, ff = wg.shape
    grid_spec = pltpu.PrefetchScalarGridSpec(
        num_scalar_prefetch=3,
        grid=(rows // PIECE,),
        in_specs=[
            pl.BlockSpec((PIECE, d), lambda w, perm, pexp, pfill: (perm[w], 0)),
            pl.BlockSpec((PIECE, 1), lambda w, perm, pexp, pfill: (perm[w], 0)),
            pl.BlockSpec((1, d, ff), lambda w, perm, pexp, pfill: (pexp[w], 0, 0)),
            pl.BlockSpec((1, d, ff), lambda w, perm, pexp, pfill: (pexp[w], 0, 0)),
            pl.BlockSpec((1, ff, d), lambda w, perm, pexp, pfill: (pexp[w], 0, 0)),
        ],
        out_specs=pl.BlockSpec((PIECE, d), lambda w, perm, pexp, pfill: (perm[w], 0)),
    )
    return pl.pallas_call(
        functools.partial(_expert_ffn_kernel, tf=tf),
        grid_spec=grid_spec,
        out_shape=jax.ShapeDtypeStruct((rows, d), BF16),
        compiler_params=_cparams(("arbitrary",)),
        name="expert_ffn",
    )(perm, pexp, pfill, xg, gs, wg, wu, wd)


def _combine_kernel(off_ref, start_ref, end_ref, yg_ref, rcol_ref, x_ref, mod_ref, gfin_ref, o_ref,
                    acc_ref, *, n_exp, sub, subs_per_block):
    blk = pl.program_id(0)
    t = pl.program_id(1)
    tm, d = x_ref.shape
    col = lax.broadcasted_iota(jnp.int32, (1, COMBINE_WINDOW), 1)
    align = COMBINE_WINDOW // 2

    def token_info(s):
        rc = rcol_ref[s * sub:(s + 1) * sub, :]
        return tuple(jnp.broadcast_to(rc[:, k:k + 1], (sub, COMBINE_WINDOW)) for k in range(4))

    def gathered(info, e, w0):
        i1, i2, rank1, rank2 = info
        rank = jnp.where(i1 == e, rank1, jnp.where(i2 == e, rank2, -float(COMBINE_WINDOW)))
        in_region = (w0 - off_ref[blk * n_exp + e] + col).astype(F32)
        onehot = jnp.where(rank == in_region, 1.0, 0.0).astype(BF16)
        return _dot(onehot, yg_ref[pl.ds(pl.multiple_of(w0, align), COMBINE_WINDOW), :])

    def first_window(s, e):
        idx = (blk * subs_per_block + t * (tm // sub) + s) * n_exp + e
        w0 = lax.shift_right_logical(start_ref[idx], align.bit_length() - 1) * align
        return w0, end_ref[idx]

    for s in range(tm // sub):
        info = token_info(s)
        acc = jnp.zeros((sub, d), F32)
        for e in range(n_exp):
            acc += gathered(info, e, first_window(s, e)[0])
        acc_ref[s * sub:(s + 1) * sub, :] = acc

    for s in range(tm // sub):
        info = token_info(s)
        for e in range(n_exp):
            w0, end = first_window(s, e)
            n_more = lax.shift_right_logical(
                jnp.maximum(end - w0 - 1, 0), COMBINE_WINDOW.bit_length() - 1)

            def add(k, _, info=info, e=e, w0=w0, s=s):
                rows = slice(s * sub, (s + 1) * sub)
                acc_ref[rows, :] = acc_ref[rows, :] + gathered(info, e, w0 + k * COMBINE_WINDOW)
                return 0

            lax.fori_loop(1, n_more + 1, add, 0)

    x = x_ref[...] + mod_ref[0][5:6] * acc_ref[...]
    o_ref[...] = _rms_scale(x) * gfin_ref[...]


def _combine(off, lo, hi, yg, rcol, x2, mod, gfin, n_blocks, rows_per_block, rcap, n_exp, sub, tm):
    n, d = x2.shape
    tpb = rows_per_block // tm
    grid_spec = pltpu.PrefetchScalarGridSpec(
        num_scalar_prefetch=3,
        grid=(n_blocks, tpb),
        in_specs=[
            pl.BlockSpec((rcap, d), lambda b, t, *_: (b, 0), pipeline_mode=pl.Buffered(1)),
            pl.BlockSpec((tm, LANES), lambda b, t, *_: (b * tpb + t, 0)),
            pl.BlockSpec((tm, d), lambda b, t, *_: (b * tpb + t, 0)),
            pl.BlockSpec((1, 6, d), lambda b, t, *_: (b, 0, 0)),
            pl.BlockSpec((1, d), lambda b, t, *_: (0, 0)),
        ],
        out_specs=pl.BlockSpec((tm, d), lambda b, t, *_: (b * tpb + t, 0)),
        scratch_shapes=[pltpu.VMEM((tm, d), F32)],
    )
    return pl.pallas_call(
        functools.partial(_combine_kernel, n_exp=n_exp, sub=sub,
                          subs_per_block=rows_per_block // sub),
        grid_spec=grid_spec,
        out_shape=jax.ShapeDtypeStruct((n, d), F32),
        compiler_params=_cparams(("arbitrary", "arbitrary")),
        name="moe_combine",
    )(off, lo, hi, yg, rcol, x2, mod, gfin)


def _routing_tables(cnt, n_blocks, n_sub, n_exp, rcap):
    cum_incl = cnt.astype(jnp.int32)
    cum_excl = jnp.concatenate([jnp.zeros_like(cum_incl[:, :1]), cum_incl[:, :-1]], axis=1)
    total = cum_incl[:, -1]
    psz = (total + PIECE - 1) // PIECE * PIECE
    off = jnp.cumsum(psz, axis=1) - psz
    start = off[:, None, :] + cum_excl
    end = off[:, None, :] + cum_incl

    c0 = jnp.arange(rcap // DISPATCH_CHUNK, dtype=jnp.int32)[None, :, None, None] * DISPATCH_CHUNK
    hit = ((start[:, None] < c0 + DISPATCH_CHUNK) & (end[:, None] > c0)
           & (end[:, None] > start[:, None])).any(-1)
    sub_ids = jnp.arange(n_sub, dtype=jnp.int32)
    d_lo = jnp.min(jnp.where(hit, sub_ids, n_sub), axis=-1)
    d_hi = jnp.max(jnp.where(hit, sub_ids, -1), axis=-1)
    d_lo = jnp.where(d_hi < 0, 0, d_lo)


    ppb = rcap // PIECE
    prow = jnp.arange(ppb, dtype=jnp.int32)[None, :, None] * PIECE
    inside = (prow >= off[:, None, :]) & (prow < (off + psz)[:, None, :])
    p_exp = jnp.argmax(inside, axis=-1).astype(jnp.int32)
    used = inside.any(-1)
    left = jnp.take_along_axis(total + off, p_exp, axis=1) - prow[..., 0]
    p_fill = jnp.where(used, jnp.clip((left + PIECE_QUANTUM - 1) // PIECE_QUANTUM, 0,
                                      PIECE // PIECE_QUANTUM), 0).astype(jnp.int32)
    key = jnp.where(p_fill > 0, p_exp, n_exp).reshape(-1)
    perm = jnp.argsort(key, stable=True).astype(jnp.int32)
    p_exp_sorted = jnp.minimum(key[perm], n_exp - 1).astype(jnp.int32)
    d_exp = jnp.repeat(p_exp, PIECE // DISPATCH_CHUNK, axis=1).reshape(-1)
    return (off.reshape(-1).astype(jnp.int32), d_lo.reshape(-1), d_hi.reshape(-1), d_exp,
            start.reshape(-1).astype(jnp.int32), end.reshape(-1).astype(jnp.int32),
            perm, p_exp_sorted, p_fill.reshape(-1)[perm])


def _pick(n, pref):
    t = min(n, pref)
    while n % t:
        t //= 2
    return t


def kernel(x, c, positions, ada_w, ada_b, mix_norm_g, ffn_norm_g, pool_w_in, pool_w_grp,
           pool_scale, pool_w_out, kv_ada_w, kv_ada_b, kv_norm_g, w_dkv, ckv_norm_g, w_kr,
           w_uk, w_uv, w_dq, q_norm_g, w_uq, w_o, ffn_w_gate, ffn_w_up, ffn_w_down,
           router_w, moe_w_gate, moe_w_up, moe_w_down, final_norm_g):
    b, s, d = x.shape
    n = b * s
    depth = ada_w.shape[0]
    assert depth == 2 and pool_w_in.shape[0] == 1 and w_dq.shape[0] == 1
    assert b <= 8 and d % LANES == 0

    ts = _pick(s, 512)
    tm = _pick(s, 512)
    tf = _pick(ffn_w_gate.shape[-1], 512)

    c_pad = jnp.zeros((8, d), F32).at[:b].set(c)
    mods = _adaln(c_pad, ada_w, ada_b[:, None, :], _pick(6 * d, 1536))
    mods = mods[:, :b].reshape(depth, b, 6, d)
    kvmod = _adaln(c_pad, kv_ada_w[None], kv_ada_b[None, None, :], _pick(2 * d, 1024))
    kvmod = kvmod[0, :b].reshape(b, 2, d)

    r = jnp.arange(min(ts, POOL_ROW_BLOCK))[:, None]
    cidx = jnp.arange(min(ts, POOL_ROW_BLOCK) + POOL_HALO)[None, :] - POOL_HALO
    band = jnp.stack([((cidx <= r) & (cidx > r - w)) for w in POOL_WINDOWS]).astype(BF16)
    inv_freq = ROPE_THETA ** (-jnp.arange(0, D_ROPE, 2, dtype=F32) / D_ROPE)
    x1, fwg, fwu, fwd, rope_cos, rope_sin, h1 = _pool_mixer(
        x, mods[0], mix_norm_g[0:1], pool_w_in[0].astype(BF16), band, pool_w_grp[0].astype(BF16),
        pool_scale[0:1], pool_w_out[0].astype(BF16), ffn_w_gate, ffn_w_up, ffn_w_down,
        positions.astype(F32)[:, None, :], inv_freq[:, None], ffn_norm_g[0:1], ts)

    x2 = _dense_ffn(x1.reshape(n, d), h1.reshape(n, d), mods[0], fwg, fwu, fwd, tm, tf, s)


    q_lora = w_dq.shape[-1]
    wuq = w_uq[0].reshape(q_lora, N_HEADS, D_NOPE + D_ROPE)
    wqn_t = wuq[:, :, :D_NOPE].reshape(q_lora, N_HEADS * D_NOPE).T.astype(BF16)
    wqr_t = wuq[:, :, D_NOPE:].reshape(q_lora, N_HEADS * D_ROPE).T.astype(BF16)
    qt, k, vt = _qkv_proj(
        x2.reshape(b, s, d), mods[1], kvmod, mix_norm_g[1:2], kv_norm_g[None, :],
        w_dq[0].astype(BF16), q_norm_g[0:1], wqn_t, wqr_t, w_dkv.astype(BF16),
        ckv_norm_g[None, :], jnp.pad(w_kr, ((0, 0), (0, LANES - D_ROPE))).astype(BF16),
        w_uk.astype(BF16), w_uv.T.astype(BF16), rope_cos, rope_sin, ts)
    o, ewg, ewu, ewd = _attention(qt, k, vt, moe_w_gate[0], moe_w_up[0], moe_w_down[0], ts)

    n_exp = router_w.shape[-1]
    sub = _pick(s, ROUTE_SUB)
    n_sub = s // sub
    rcap = 2 * s + n_exp * PIECE
    ltri = (jnp.arange(ts)[None, :] < jnp.arange(ts)[:, None]).astype(BF16)
    assert n_exp <= LANES
    wr = jnp.pad(router_w[0], ((0, 0), (0, LANES - n_exp)))
    wr_hi = wr.astype(BF16)
    wr_split = jnp.concatenate([wr_hi, (wr - wr_hi.astype(F32)).astype(BF16)], axis=1)
    x3, h3, rcol, rrow, cnt = _attn_out(o.reshape(n, N_HEADS * D_V), x2, mods[1],
                                        w_o[0].astype(BF16), ffn_norm_g[1:2], wr_split, n_exp,
                                        ltri, ts, sub, s)
    cnt = cnt[:, :ts // sub, :n_exp].reshape(b, n_sub, n_exp)
    off, d_lo, d_hi, d_exp, c_lo, c_hi, perm, p_exp, p_fill = _routing_tables(cnt, b, n_sub, n_exp, rcap)
    xg, gs = _dispatch(off, d_lo, d_hi, d_exp, h3, rrow, b, s, rcap, n_exp, sub)
    yg = _expert_ffn(perm, p_exp, p_fill, xg, gs, ewg, ewu, ewd, tf)
    out = _combine(off, c_lo, c_hi, yg, rcol, x3, mods[1], final_norm_g[None, :], b, s, rcap,
                   n_exp, sub, ts)
    return out.reshape(b, s, d)
```

```python
import functools
import math

import jax
import jax.numpy as jnp
from jax import lax
from jax.experimental import pallas as pl
from jax.experimental.pallas import tpu as pltpu

F32 = jnp.float32
BF16 = jnp.bfloat16

N_HEADS = 8
D_NOPE = 128
D_ROPE = 64
D_V = 128
HEAD_PAD = 256
ROPE_HALF = D_ROPE // 2
ROPE_THETA = 10000.0
ATTN_SCALE = 1.0 / math.sqrt(D_NOPE + D_ROPE)
Q_SCALE = ATTN_SCALE * math.log2(math.e)
ATTN_QUERY_TILE = 512
ATTN_KEY_BLOCK = 512
ATTN_LOOKAHEAD = 2
POOL_WINDOWS = (2, 4, 8, 16)
POOL_HALO = 16
POOL_ROW_BLOCK = 128
EPS = 1e-6

ROUTE_SUB = 256
DISPATCH_CHUNK = 128
DISPATCH_WINDOW = 3
COMBINE_WINDOW = 256
PIECE = 512
PIECE_QUANTUM = 128
ROUTE_FIELDS = 8

LANES = 128
VMEM_LIMIT_BYTES = 56 * 1024 * 1024


def _cparams(sem):
    return pltpu.CompilerParams(dimension_semantics=sem, vmem_limit_bytes=VMEM_LIMIT_BYTES)


def _dot(a, b):
    return jnp.dot(a, b, preferred_element_type=F32)


def _rms_scale(x):
    return x * lax.rsqrt(jnp.mean(x * x, axis=-1, keepdims=True) + EPS)


def _silu(x):
    return x * jax.nn.sigmoid(x)


def _adaln_kernel(c_ref, w_ref, b_ref, o_ref):
    sc = _silu(c_ref[...])
    o_ref[0] = jnp.dot(sc, w_ref[0], preferred_element_type=F32,
                       precision=lax.Precision.HIGHEST) + b_ref[0]


def _adaln(c_pad, w, b, tn):
    nl, d, m = w.shape
    return pl.pallas_call(
        _adaln_kernel,
        grid=(nl, m // tn),
        in_specs=[
            pl.BlockSpec((8, d), lambda l, j: (0, 0)),
            pl.BlockSpec((1, d, tn), lambda l, j: (l, 0, j)),
            pl.BlockSpec((1, 1, tn), lambda l, j: (l, 0, j)),
        ],
        out_specs=pl.BlockSpec((1, 8, tn), lambda l, j: (l, 0, j)),
        out_shape=jax.ShapeDtypeStruct((nl, 8, m), F32),
        compiler_params=_cparams(("arbitrary", "arbitrary")),
        name="adaln",
    )(c_pad, w, b)


def _rope_table_side_job(pos_ref, freq_ref, cos_ref, sin_ref):
    ang = freq_ref[...] * pos_ref[0]
    cos_ref[0, 0] = jnp.cos(ang)
    sin_ref[0, 0] = jnp.sin(ang)


def _rope_t(x, cos_t, sin_t):
    x1, x2 = x[:ROPE_HALF], x[ROPE_HALF:]
    return x1 * cos_t - x2 * sin_t, x2 * cos_t + x1 * sin_t


def _cast_specs(n_exp, d, ff, steps, step_of):
    assert steps >= n_exp, "too few grid steps to carry the weight cast"
    k = max(c for c in range(1, steps // n_exp + 1) if ff % (c * LANES) == 0)
    cw = ff // k

    def chunk(*idx):
        c = jnp.minimum(step_of(*idx), n_exp * k - 1)
        return c // k, c % k

    gate_up = pl.BlockSpec((1, d, cw), lambda *idx: (chunk(*idx)[0], 0, chunk(*idx)[1]))
    down = pl.BlockSpec((1, cw, d), lambda *idx: (chunk(*idx)[0], chunk(*idx)[1], 0))
    return gate_up, down


def _cast_side_job(src_refs, dst_refs):
    for src, dst in zip(src_refs, dst_refs):
        dst[...] = src[...].astype(BF16)


def _pool_kernel(x_ref, mod_ref, g_ref, win_ref, band_ref, wgrp_ref, scale_ref, wout_ref,
                 fg_ref, fu_ref, fd_ref, pos_ref, freq_ref, gffn_ref,
                 o_ref, fg_out_ref, fu_out_ref, fd_out_ref, cos_ref, sin_ref, h_out_ref, uext_ref,
                 *, ts, gw):
    s = pl.program_id(1)

    @pl.when(s == 0)
    def _():
        uext_ref[0:POOL_HALO, :] = jnp.zeros((POOL_HALO, uext_ref.shape[1]), BF16)

    x = x_ref[0]
    mod = mod_ref[0]
    h = (_rms_scale(x) * g_ref[...]) * (1.0 + mod[1:2]) + mod[0:1]
    u = _dot(h.astype(BF16), win_ref[...])
    uext_ref[POOL_HALO:POOL_HALO + ts, :] = u.astype(BF16)

    t = s * ts + lax.broadcasted_iota(jnp.int32, (ts, 1), 0)
    zs = []
    for g, w in enumerate(POOL_WINDOWS):
        cols = slice(g * gw, (g + 1) * gw)
        rb = band_ref.shape[1]
        wsum = jnp.concatenate(
            [_dot(band_ref[g], uext_ref[r0:r0 + rb + POOL_HALO, cols]) for r0 in range(0, ts, rb)],
            axis=0)
        cnt = jnp.minimum(t + 1, w).astype(F32)
        p = wsum / cnt - u[:, cols]
        zs.append(_dot(p.astype(BF16), wgrp_ref[g]))
    z = jnp.concatenate(zs, axis=-1) * scale_ref[...]
    y = _dot(z.astype(BF16), wout_ref[...])
    x_out = x + mod[2:3] * y
    o_ref[0] = x_out
    h_out_ref[0] = ((_rms_scale(x_out) * gffn_ref[...]) * (1.0 + mod[4:5]) + mod[3:4]).astype(BF16)

    uext_ref[0:POOL_HALO, :] = uext_ref[ts:ts + POOL_HALO, :]

    _cast_side_job((fg_ref, fu_ref, fd_ref), (fg_out_ref, fu_out_ref, fd_out_ref))
    _rope_table_side_job(pos_ref, freq_ref, cos_ref, sin_ref)


def _pool_mixer(x, mod, g, w_in, band, w_grp, scale, w_out, fg, fu, fd, pos_f, freq_row, g_ffn,
                ts):
    b, s, d = x.shape
    ng, gw, _ = w_grp.shape
    n_s = s // ts
    gate_up, down = _cast_specs(1, d, fg.shape[2], b * n_s, lambda i, j: i * n_s + j)
    const2 = lambda i, j: (0, 0)
    const3 = lambda i, j: (0, 0, 0)
    return pl.pallas_call(
        functools.partial(_pool_kernel, ts=ts, gw=gw),
        grid=(b, s // ts),
        in_specs=[
            pl.BlockSpec((1, ts, d), lambda i, j: (i, j, 0)),
            pl.BlockSpec((1, 6, d), lambda i, j: (i, 0, 0)),
            pl.BlockSpec((1, d), const2),
            pl.BlockSpec((d, d), const2),
            pl.BlockSpec(band.shape, const3),
            pl.BlockSpec((ng, gw, gw), const3),
            pl.BlockSpec((1, d), const2),
            pl.BlockSpec((d, d), const2),
            gate_up, gate_up, down,
            pl.BlockSpec((1, 1, ts), lambda i, j: (i, 0, j)),
            pl.BlockSpec((ROPE_HALF, 1), const2),
            pl.BlockSpec((1, d), const2),
        ],
        out_specs=[pl.BlockSpec((1, ts, d), lambda i, j: (i, j, 0)), gate_up, gate_up, down,
                   pl.BlockSpec((1, 1, ROPE_HALF, ts), lambda i, j: (i, j, 0, 0)),
                   pl.BlockSpec((1, 1, ROPE_HALF, ts), lambda i, j: (i, j, 0, 0)),
                   pl.BlockSpec((1, ts, d), lambda i, j: (i, j, 0))],
        out_shape=[jax.ShapeDtypeStruct((b, s, d), F32),
                   jax.ShapeDtypeStruct(fg.shape, BF16),
                   jax.ShapeDtypeStruct(fu.shape, BF16),
                   jax.ShapeDtypeStruct(fd.shape, BF16),
                   jax.ShapeDtypeStruct((b, s // ts, ROPE_HALF, ts), F32),
                   jax.ShapeDtypeStruct((b, s // ts, ROPE_HALF, ts), F32),
                   jax.ShapeDtypeStruct((b, s, d), BF16)],
        scratch_shapes=[pltpu.VMEM((POOL_HALO + ts, d), BF16)],
        compiler_params=_cparams(("arbitrary", "arbitrary")),
        name="pool_mixer",
    )(x, mod, g, w_in, band, w_grp, scale, w_out, fg, fu, fd, pos_f, freq_row, g_ffn)


def _ffn_kernel(x_ref, h_ref, mod_ref, wg_ref, wu_ref, wd_ref, o_ref, *, tf):
    h = h_ref[...]
    ff = wg_ref.shape[2]
    acc = jnp.zeros(x_ref.shape, F32)
    for f0 in range(0, ff, tf):
        gate = _dot(h, wg_ref[0, :, f0:f0 + tf])
        up = _dot(h, wu_ref[0, :, f0:f0 + tf])
        a = (_silu(gate) * up).astype(BF16)
        acc = acc + _dot(a, wd_ref[0, f0:f0 + tf, :])
    o_ref[...] = x_ref[...] + mod_ref[0][5:6] * acc


def _dense_ffn(x2, h2, mod, wg, wu, wd, tm, tf, rows_per_batch):
    n, d = x2.shape
    tpb = rows_per_batch // tm
    resident = lambda a: pl.BlockSpec(a.shape, lambda i: (0, 0, 0), pipeline_mode=pl.Buffered(1))
    return pl.pallas_call(
        functools.partial(_ffn_kernel, tf=tf),
        grid=(n // tm,),
        in_specs=[
            pl.BlockSpec((tm, d), lambda i: (i, 0)),
            pl.BlockSpec((tm, d), lambda i: (i, 0)),
            pl.BlockSpec((1, 6, d), lambda i: (i // tpb, 0, 0)),
            resident(wg), resident(wu), resident(wd),
        ],
        out_specs=pl.BlockSpec((tm, d), lambda i: (i, 0)),
        out_shape=jax.ShapeDtypeStruct((n, d), F32),
        compiler_params=_cparams(("arbitrary",)),
        name="dense_ffn",
    )(x2, h2, mod, wg, wu, wd)


def _dot_nt(a, b):
    return lax.dot_general(a, b, (((1,), (1,)), ((), ())), preferred_element_type=F32)


def _qkv_kernel(x_ref, mod_ref, kvmod_ref, gmix_ref, gkv_ref, wdq_ref, qg_ref, wqn_t_ref,
                wqr_t_ref, wdkv_ref, ckvg_ref, wkr_ref, wuk_ref, wuv_t_ref, cos_ref, sin_ref,
                qt_ref, k_ref, vt_ref):
    x = x_ref[0]
    xn = _rms_scale(x)
    mod = mod_ref[0]
    kvmod = kvmod_ref[0]
    hq = ((xn * gmix_ref[...]) * (1.0 + mod[1:2]) + mod[0:1]).astype(BF16)
    hk = ((xn * gkv_ref[...]) * (1.0 + kvmod[1:2]) + kvmod[0:1]).astype(BF16)

    cq = (_rms_scale(_dot(hq, wdq_ref[...])) * qg_ref[...]).astype(BF16)
    qn_t = _dot_nt(wqn_t_ref[...], cq) * Q_SCALE
    qr_t = _dot_nt(wqr_t_ref[...], cq) * Q_SCALE
    cos_t = cos_ref[0, 0]
    sin_t = sin_ref[0, 0]
    zero_rows = jnp.zeros((HEAD_PAD - LANES - D_ROPE, x.shape[0]), BF16)

    ckv = (_rms_scale(_dot(hk, wdkv_ref[...])) * ckvg_ref[...]).astype(BF16)
    kn = _dot(ckv, wuk_ref[...])
    vt_ref[0, 0] = _dot_nt(wuv_t_ref[...], ckv).astype(BF16)
    kr_raw_t = _dot(hk, wkr_ref[...]).T
    k1, k2 = _rope_t(kr_raw_t[:D_ROPE], cos_t, sin_t)
    kr_t = jnp.concatenate([k1, k2, kr_raw_t[D_ROPE:]], axis=0)
    kr = kr_t.T.astype(BF16)

    for h in range(N_HEADS):
        lo = h * HEAD_PAD
        hs = slice(h * LANES, (h + 1) * LANES)
        qt_ref[0, 0, lo:lo + LANES, :] = qn_t[hs].astype(BF16)
        q1, q2 = _rope_t(qr_t[h * D_ROPE:(h + 1) * D_ROPE], cos_t, sin_t)
        ro = lo + LANES
        qt_ref[0, 0, ro:ro + ROPE_HALF, :] = q1.astype(BF16)
        qt_ref[0, 0, ro + ROPE_HALF:ro + D_ROPE, :] = q2.astype(BF16)
        qt_ref[0, 0, ro + D_ROPE:lo + HEAD_PAD, :] = zero_rows
        k_ref[0, :, lo:lo + LANES] = kn[:, hs].astype(BF16)
        k_ref[0, :, lo + LANES:lo + HEAD_PAD] = kr


def _qkv_proj(x, mod, kvmod, gmix, gkv, wdq, qg, wqn_t, wqr_t, wdkv, ckvg, wkr, wuk, wuv_t,
              cos_t, sin_t, ts):
    b, s, d = x.shape
    row = lambda i, j: (i, j, 0)
    tile = lambda i, j: (i, j, 0, 0)
    per_b = lambda i, j: (i, 0, 0)

    def full(a):
        return pl.BlockSpec(a.shape, lambda i, j: (0,) * a.ndim)

    weights = (gmix, gkv, wdq, qg, wqn_t, wqr_t, wdkv, ckvg, wkr, wuk, wuv_t)
    return pl.pallas_call(
        _qkv_kernel,
        grid=(b, s // ts),
        in_specs=[pl.BlockSpec((1, ts, d), row),
                  pl.BlockSpec((1, 6, d), per_b),
                  pl.BlockSpec((1, 2, d), per_b)]
                 + [full(a) for a in weights]
                 + [pl.BlockSpec((1, 1, ROPE_HALF, ts), tile)] * 2,
        out_specs=[pl.BlockSpec((1, 1, N_HEADS * HEAD_PAD, ts), tile),
                   pl.BlockSpec((1, ts, N_HEADS * HEAD_PAD), row),
                   pl.BlockSpec((1, 1, N_HEADS * D_V, ts), tile)],
        out_shape=[jax.ShapeDtypeStruct((b, s // ts, N_HEADS * HEAD_PAD, ts), BF16),
                   jax.ShapeDtypeStruct((b, s, N_HEADS * HEAD_PAD), BF16),
                   jax.ShapeDtypeStruct((b, s // ts, N_HEADS * D_V, ts), BF16)],
        compiler_params=_cparams(("arbitrary", "arbitrary")),
        name="qkv_proj",
    )(x, mod, kvmod, *weights, cos_t, sin_t)


def _attn_kernel(qt_ref, k_ref, vt_ref, eg_ref, eu_ref, ed_ref,
                 o_ref, eg_out_ref, eu_out_ref, ed_out_ref, *, tq, tk):
    ts = qt_ref.shape[3]
    s = k_ref.shape[1]
    key = lax.broadcasted_iota(jnp.int32, (tk, tq), 0)
    qry = lax.broadcasted_iota(jnp.int32, (tk, tq), 1)
    last_block = lambda qi: ((qi + 1) * tq - 1) // tk
    blocks = [(qi, j) for qi in range(s // tq) for j in range(last_block(qi) + 1)]

    def tile(ref, start, size):
        return ref[0, start // ts, :, start % ts:start % ts + size]

    def scores_t(qi, j):
        st = _dot(k_ref[0, j * tk:(j + 1) * tk, :], tile(qt_ref, qi * tq, tq))
        if (j + 1) * tk - 1 > qi * tq:
            st = jnp.where(key + (j * tk - qi * tq) <= qry, st, -jnp.inf)
        return st

    ahead = [scores_t(*blk) for blk in blocks[:ATTN_LOOKAHEAD]]
    for n, (qi, j) in enumerate(blocks):
        st = ahead.pop(0)
        if n + ATTN_LOOKAHEAD < len(blocks):
            ahead.append(scores_t(*blocks[n + ATTN_LOOKAHEAD]))
        if j == 0:
            m = jnp.full((1, tq), -jnp.inf, F32)
            l = jnp.zeros((1, tq), F32)
            acc = jnp.zeros((D_V, tq), F32)
        m_new = jnp.maximum(m, jnp.max(st, axis=0, keepdims=True))
        alpha = jnp.exp2(m - m_new)
        pt = jnp.exp2(st - m_new)
        l = alpha * l + jnp.sum(pt, axis=0, keepdims=True)
        acc = alpha * acc + _dot(tile(vt_ref, j * tk, tk), pt.astype(BF16))
        m = m_new
        if j == last_block(qi):
            o_ref[0, qi * tq:(qi + 1) * tq, :] = (acc / l).T.astype(BF16)

    _cast_side_job((eg_ref, eu_ref, ed_ref), (eg_out_ref, eu_out_ref, ed_out_ref))


def _attention(qt, k, vt, eg, eu, ed, ts):
    b, s, _ = k.shape
    n_exp, d, eff = eg.shape
    gate_up, down = _cast_specs(n_exp, d, eff, b * N_HEADS, lambda i, h: i * N_HEADS + h)
    return pl.pallas_call(
        functools.partial(_attn_kernel, tq=min(ts, ATTN_QUERY_TILE), tk=min(ts, ATTN_KEY_BLOCK)),
        grid=(b, N_HEADS),
        in_specs=[
            pl.BlockSpec((1, s // ts, HEAD_PAD, ts), lambda i, h: (i, 0, h, 0)),
            pl.BlockSpec((1, s, HEAD_PAD), lambda i, h: (i, 0, h)),
            pl.BlockSpec((1, s // ts, D_V, ts), lambda i, h: (i, 0, h, 0)),
            gate_up, gate_up, down,
        ],
        out_specs=[pl.BlockSpec((1, s, D_V), lambda i, h: (i, 0, h)), gate_up, gate_up, down],
        out_shape=[jax.ShapeDtypeStruct((b, s, N_HEADS * D_V), BF16),
                   jax.ShapeDtypeStruct(eg.shape, BF16),
                   jax.ShapeDtypeStruct(eu.shape, BF16),
                   jax.ShapeDtypeStruct(ed.shape, BF16)],
        compiler_params=_cparams(("arbitrary", "arbitrary")),
        name="flash_attn",
    )(qt, k, vt, eg, eu, ed)


def _attn_out_kernel(o_ref, x_ref, mod_ref, wo_ref, g_ref, wr_ref, ltri_ref,
                     x_out_ref, h_out_ref, rcol_ref, rrow_ref, cnt_ref, carry_ref,
                     *, n_exp, tiles_per_block):
    i = pl.program_id(0)

    @pl.when(i % tiles_per_block == 0)
    def _():
        carry_ref[...] = jnp.zeros_like(carry_ref)

    mod = mod_ref[0]
    x = x_ref[...] + mod[2:3] * _dot(o_ref[...], wo_ref[...])
    x_out_ref[...] = x
    h = (_rms_scale(x) * g_ref[...]) * (1.0 + mod[4:5]) + mod[3:4]
    h_hi = h.astype(BF16)
    h_out_ref[...] = h_hi

    tm = x.shape[0]
    h_lo = (h - h_hi.astype(F32)).astype(BF16)
    parts = _dot(jnp.concatenate([h_hi, h_lo], axis=0), wr_ref[...])
    logits = (parts[:tm, :LANES] + parts[:tm, LANES:]) + (parts[tm:, :LANES] + parts[tm:, LANES:])
    lane = lax.broadcasted_iota(jnp.int32, (tm, LANES), 1)
    lane_f = lane.astype(F32)

    def first_argmax(vals):
        m = jnp.max(vals, axis=-1, keepdims=True)
        idx = jnp.min(jnp.where(vals == m, lane_f, float(LANES)), axis=-1, keepdims=True)
        return m, idx.astype(jnp.int32)

    logits = jnp.where(lane < n_exp, logits, -jnp.inf)
    m1, i1 = first_argmax(logits)
    m2, i2 = first_argmax(jnp.where(lane == i1, -jnp.inf, logits))
    ex = jnp.exp(m2 - m1)
    w1 = 1.0 / (1.0 + ex)
    w2 = ex / (1.0 + ex)

    mask = jnp.where((lane == i1) | (lane == i2), 1.0, 0.0)
    ranks = _dot(ltri_ref[...], mask.astype(BF16)) + carry_ref[0:1]
    incl = ranks + mask
    rank1 = jnp.sum(jnp.where(lane == i1, ranks, 0.0), axis=-1, keepdims=True)
    rank2 = jnp.sum(jnp.where(lane == i2, ranks, 0.0), axis=-1, keepdims=True)
    carry_ref[0:1] = incl[tm - 1:tm]
    n_sub = tm // rrow_ref.shape[2]
    sub = rrow_ref.shape[2]
    cnt_ref[0] = jnp.zeros(cnt_ref.shape[1:], F32)
    for j in range(n_sub):
        cnt_ref[0, j:j + 1] = incl[(j + 1) * sub - 1:(j + 1) * sub]

    fields = (i1.astype(F32), i2.astype(F32), rank1, rank2, w1, w2)
    rcol = jnp.zeros((tm, LANES), F32)
    for k, val in enumerate(fields):
        rcol = jnp.where(lane == k, val, rcol)
    rcol_ref[...] = rcol
    rrow = rcol.T
    for j in range(n_sub):
        rrow_ref[j] = rrow[0:ROUTE_FIELDS, j * sub:(j + 1) * sub]


def _attn_out(o2, x2, mod, wo, g, wr, n_exp, ltri, tm, sub, rows_per_block):
    n, d = x2.shape
    tpb = rows_per_block // tm
    n_sub = tm // sub
    return pl.pallas_call(
        functools.partial(_attn_out_kernel, n_exp=n_exp, tiles_per_block=tpb),
        grid=(n // tm,),
        in_specs=[
            pl.BlockSpec((tm, o2.shape[1]), lambda i: (i, 0)),
            pl.BlockSpec((tm, d), lambda i: (i, 0)),
            pl.BlockSpec((1, 6, d), lambda i: (i // tpb, 0, 0)),
            pl.BlockSpec(wo.shape, lambda i: (0, 0)),
            pl.BlockSpec((1, d), lambda i: (0, 0)),
            pl.BlockSpec(wr.shape, lambda i: (0, 0)),
            pl.BlockSpec((tm, tm), lambda i: (0, 0)),
        ],
        out_specs=[pl.BlockSpec((tm, d), lambda i: (i, 0)),
                   pl.BlockSpec((tm, d), lambda i: (i, 0)),
                   pl.BlockSpec((tm, LANES), lambda i: (i, 0)),
                   pl.BlockSpec((n_sub, ROUTE_FIELDS, sub), lambda i: (i, 0, 0)),
                   pl.BlockSpec((1, 8, LANES), lambda i: (i, 0, 0))],
        out_shape=[jax.ShapeDtypeStruct((n, d), F32),
                   jax.ShapeDtypeStruct((n, d), BF16),
                   jax.ShapeDtypeStruct((n, LANES), F32),
                   jax.ShapeDtypeStruct((n // sub, ROUTE_FIELDS, sub), F32),
                   jax.ShapeDtypeStruct((n // tm, 8, LANES), F32)],
        scratch_shapes=[pltpu.VMEM((8, LANES), F32)],
        compiler_params=_cparams(("arbitrary",)),
        name="attn_out_router",
    )(o2, x2, mod, wo, g, wr, ltri)


def _dispatch_kernel(off_ref, lo_ref, hi_ref, cexp_ref, h_ref, rrow_ref, xg_ref, gs_ref,
                     *, n_exp, sub, chunks_per_step, chunks_per_block):
    blk = pl.program_id(0)
    g = pl.program_id(1)
    d = h_ref.shape[1]
    n_sub = rrow_ref.shape[0]
    win = min(DISPATCH_WINDOW, n_sub)
    iota = lax.broadcasted_iota(jnp.int32, (DISPATCH_CHUNK, 1), 0)

    def contribution(info, sb):
        row, exp_id, exp_off = info
        r = rrow_ref[sb]
        first = r[0:1] == exp_id
        second = r[1:2] == exp_id
        pos = jnp.where(first, r[2:3] + exp_off, jnp.where(second, r[3:4] + exp_off, -1.0))
        hit = pos == row
        onehot = jnp.where(hit, 1.0, 0.0).astype(BF16)
        tok = h_ref[pl.ds(pl.multiple_of(sb * sub, sub), sub), :]
        gate = jnp.sum(jnp.where(hit, jnp.where(first, r[4:5], r[5:6]), 0.0),
                       axis=-1, keepdims=True)
        return _dot(onehot, tok), gate

    def chunk_info(j):
        c = g * chunks_per_step + j
        t = blk * chunks_per_block + c
        row = (c * DISPATCH_CHUNK + iota).astype(F32)
        exp_id = cexp_ref[t]
        exp_off = off_ref[blk * n_exp + exp_id].astype(F32)
        start = jnp.minimum(lo_ref[t], n_sub - win)
        return (row, exp_id.astype(F32), exp_off), start, hi_ref[t]

    first_chunk = blk * chunks_per_block + g * chunks_per_step
    used = functools.reduce(jnp.logical_or,
                            [hi_ref[first_chunk + j] >= 0 for j in range(chunks_per_step)])

    @pl.when(jnp.logical_not(used))
    def _():
        xg_ref[...] = jnp.zeros(xg_ref.shape, BF16)
        gs_ref[...] = jnp.zeros(gs_ref.shape, F32)

    @pl.when(used)
    def _():
        for j in range(chunks_per_step):
            info, start, _ = chunk_info(j)
            acc = jnp.zeros((DISPATCH_CHUNK, d), F32)
            gate = jnp.zeros((DISPATCH_CHUNK, 1), F32)
            for k in range(win):
                a, b = contribution(info, start + k)
                acc += a
                gate += b
            rows = slice(j * DISPATCH_CHUNK, (j + 1) * DISPATCH_CHUNK)
            xg_ref[rows, :] = acc.astype(BF16)
            gs_ref[rows, :] = gate

        def extra(j, _):
            info, start, last = chunk_info(j)
            rows = pl.ds(pl.multiple_of(j * DISPATCH_CHUNK, DISPATCH_CHUNK), DISPATCH_CHUNK)

            def add(sb, _):
                a, b = contribution(info, sb)
                xg_ref[rows, :] = (xg_ref[rows, :].astype(F32) + a).astype(BF16)
                gs_ref[rows, :] = gs_ref[rows, :] + b
                return 0

            return lax.fori_loop(start + win, last + 1, add, 0)

        lax.fori_loop(0, chunks_per_step, extra, 0)


def _dispatch(off, lo, hi, cexp, h, rrow, n_blocks, rows_per_block, rcap, n_exp, sub):
    n, d = h.shape
    n_sub = rows_per_block // sub
    step_rows = _pick(rcap, 1024)
    cps = step_rows // DISPATCH_CHUNK
    cpb = rcap // DISPATCH_CHUNK
    grid_spec = pltpu.PrefetchScalarGridSpec(
        num_scalar_prefetch=4,
        grid=(n_blocks, rcap // step_rows),
        in_specs=[
            pl.BlockSpec((rows_per_block, d), lambda b, g, *_: (b, 0)),
            pl.BlockSpec((n_sub, ROUTE_FIELDS, sub), lambda b, g, *_: (b, 0, 0)),
        ],
        out_specs=[
            pl.BlockSpec((step_rows, d), lambda b, g, *_: (b * (rcap // step_rows) + g, 0)),
            pl.BlockSpec((step_rows, 1), lambda b, g, *_: (b * (rcap // step_rows) + g, 0)),
        ],
    )
    return pl.pallas_call(
        functools.partial(_dispatch_kernel, n_exp=n_exp, sub=sub, chunks_per_step=cps,
                          chunks_per_block=cpb),
        grid_spec=grid_spec,
        out_shape=[jax.ShapeDtypeStruct((n_blocks * rcap, d), BF16),
                   jax.ShapeDtypeStruct((n_blocks * rcap, 1), F32)],
        compiler_params=_cparams(("arbitrary", "arbitrary")),
        name="moe_dispatch",
    )(off, lo, hi, cexp, h, rrow)


def _expert_ffn_kernel(perm_ref, pexp_ref, pfill_ref, x_ref, gs_ref, wg_ref, wu_ref, wd_ref,
                       y_ref, *, tf):
    w = pl.program_id(0)
    fill = pfill_ref[w]
    ff = wg_ref.shape[2]

    def ffn(n_rows):
        x = x_ref[0:n_rows, :]
        gs = gs_ref[0:n_rows, :]
        acc = jnp.zeros((n_rows, y_ref.shape[1]), F32)
        for f0 in range(0, ff, tf):
            gate = _dot(x, wg_ref[0, :, f0:f0 + tf])
            up = _dot(x, wu_ref[0, :, f0:f0 + tf])
            a = ((_silu(gate) * up) * gs).astype(BF16)
            acc = acc + _dot(a, wd_ref[0, f0:f0 + tf, :])
        return acc.astype(BF16)

    for level in range(PIECE // PIECE_QUANTUM + 1):
        n_rows = level * PIECE_QUANTUM

        @pl.when(fill == level)
        def _(n_rows=n_rows):
            if n_rows:
                y_ref[0:n_rows, :] = ffn(n_rows)
            if n_rows < PIECE:
                y_ref[n_rows:PIECE, :] = jnp.zeros((PIECE - n_rows, y_ref.shape[1]), BF16)


def _expert_ffn(perm, pexp, pfill, xg, gs, wg, wu, wd, tf):
    rows, d = xg.shape
    n_exp, _, ff = wg.shape
    grid_spec = pltpu.PrefetchScalarGridSpec(
        num_scalar_prefetch=3,
        grid=(rows // PIECE,),
        in_specs=[
            pl.BlockSpec((PIECE, d), lambda w, perm, pexp, pfill: (perm[w], 0)),
            pl.BlockSpec((PIECE, 1), lambda w, perm, pexp, pfill: (perm[w], 0)),
            pl.BlockSpec((1, d, ff), lambda w, perm, pexp, pfill: (pexp[w], 0, 0)),
            pl.BlockSpec((1, d, ff), lambda w, perm, pexp, pfill: (pexp[w], 0, 0)),
            pl.BlockSpec((1, ff, d), lambda w, perm, pexp, pfill: (pexp[w], 0, 0)),
        ],
        out_specs=pl.BlockSpec((PIECE, d), lambda w, perm, pexp, pfill: (perm[w], 0)),
    )
    return pl.pallas_call(
        functools.partial(_expert_ffn_kernel, tf=tf),
        grid_spec=grid_spec,
        out_shape=jax.ShapeDtypeStruct((rows, d), BF16),
        compiler_params=_cparams(("arbitrary",)),
        name="expert_ffn",
    )(perm, pexp, pfill, xg, gs, wg, wu, wd)


def _combine_kernel(off_ref, start_ref, end_ref, yg_ref, rcol_ref, x_ref, mod_ref, gfin_ref, o_ref,
                    acc_ref, *, n_exp, sub, subs_per_block):
    blk = pl.program_id(0)
    t = pl.program_id(1)
    tm, d = x_ref.shape
    col = lax.broadcasted_iota(jnp.int32, (1, COMBINE_WINDOW), 1)
    align = COMBINE_WINDOW // 2

    def token_info(s):
        rc = rcol_ref[s * sub:(s + 1) * sub, :]
        return tuple(jnp.broadcast_to(rc[:, k:k + 1], (sub, COMBINE_WINDOW)) for k in range(4))

    def gathered(info, e, w0):
        i1, i2, rank1, rank2 = info
        rank = jnp.where(i1 == e, rank1, jnp.where(i2 == e, rank2, -float(COMBINE_WINDOW)))
        in_region = (w0 - off_ref[blk * n_exp + e] + col).astype(F32)
        onehot = jnp.where(rank == in_region, 1.0, 0.0).astype(BF16)
        return _dot(onehot, yg_ref[pl.ds(pl.multiple_of(w0, align), COMBINE_WINDOW), :])

    def first_window(s, e):
        idx = (blk * subs_per_block + t * (tm // sub) + s) * n_exp + e
        w0 = lax.shift_right_logical(start_ref[idx], align.bit_length() - 1) * align
        return w0, end_ref[idx]

    for s in range(tm // sub):
        info = token_info(s)
        acc = jnp.zeros((sub, d), F32)
        for e in range(n_exp):
            acc += gathered(info, e, first_window(s, e)[0])
        acc_ref[s * sub:(s + 1) * sub, :] = acc

    for s in range(tm // sub):
        info = token_info(s)
        for e in range(n_exp):
            w0, end = first_window(s, e)
            n_more = lax.shift_right_logical(
                jnp.maximum(end - w0 - 1, 0), COMBINE_WINDOW.bit_length() - 1)

            def add(k, _, info=info, e=e, w0=w0, s=s):
                rows = slice(s * sub, (s + 1) * sub)
                acc_ref[rows, :] = acc_ref[rows, :] + gathered(info, e, w0 + k * COMBINE_WINDOW)
                return 0

            lax.fori_loop(1, n_more + 1, add, 0)

    x = x_ref[...] + mod_ref[0][5:6] * acc_ref[...]
    o_ref[...] = _rms_scale(x) * gfin_ref[...]


def _combine(off, lo, hi, yg, rcol, x2, mod, gfin, n_blocks, rows_per_block, rcap, n_exp, sub, tm):
    n, d = x2.shape
    tpb = rows_per_block // tm
    grid_spec = pltpu.PrefetchScalarGridSpec(
        num_scalar_prefetch=3,
        grid=(n_blocks, tpb),
        in_specs=[
            pl.BlockSpec((rcap, d), lambda b, t, *_: (b, 0)),
            pl.BlockSpec((tm, LANES), lambda b, t, *_: (b * tpb + t, 0)),
            pl.BlockSpec((tm, d), lambda b, t, *_: (b * tpb + t, 0)),
            pl.BlockSpec((1, 6, d), lambda b, t, *_: (b, 0, 0)),
            pl.BlockSpec((1, d), lambda b, t, *_: (0, 0)),
        ],
        out_specs=pl.BlockSpec((tm, d), lambda b, t, *_: (b * tpb + t, 0)),
        scratch_shapes=[pltpu.VMEM((tm, d), F32)],
    )
    return pl.pallas_call(
        functools.partial(_combine_kernel, n_exp=n_exp, sub=sub,
                          subs_per_block=rows_per_block // sub),
        grid_spec=grid_spec,
        out_shape=jax.ShapeDtypeStruct((n, d), F32),
        compiler_params=_cparams(("arbitrary", "arbitrary")),
        name="moe_combine",
    )(off, lo, hi, yg, rcol, x2, mod, gfin)


def _routing_tables(cnt, n_blocks, n_sub, n_exp, rcap):
    cum_incl = cnt.astype(jnp.int32)
    cum_excl = jnp.concatenate([jnp.zeros_like(cum_incl[:, :1]), cum_incl[:, :-1]], axis=1)
    total = cum_incl[:, -1]
    psz = (total + PIECE - 1) // PIECE * PIECE
    off = jnp.cumsum(psz, axis=1) - psz
    start = off[:, None, :] + cum_excl
    end = off[:, None, :] + cum_incl

    c0 = jnp.arange(rcap // DISPATCH_CHUNK, dtype=jnp.int32)[None, :, None, None] * DISPATCH_CHUNK
    hit = ((start[:, None] < c0 + DISPATCH_CHUNK) & (end[:, None] > c0)
           & (end[:, None] > start[:, None])).any(-1)
    sub_ids = jnp.arange(n_sub, dtype=jnp.int32)
    d_lo = jnp.min(jnp.where(hit, sub_ids, n_sub), axis=-1)
    d_hi = jnp.max(jnp.where(hit, sub_ids, -1), axis=-1)
    d_lo = jnp.where(d_hi < 0, 0, d_lo)


    ppb = rcap // PIECE
    prow = jnp.arange(ppb, dtype=jnp.int32)[None, :, None] * PIECE
    inside = (prow >= off[:, None, :]) & (prow < (off + psz)[:, None, :])
    p_exp = jnp.argmax(inside, axis=-1).astype(jnp.int32)
    used = inside.any(-1)
    left = jnp.take_along_axis(total + off, p_exp, axis=1) - prow[..., 0]
    p_fill = jnp.where(used, jnp.clip((left + PIECE_QUANTUM - 1) // PIECE_QUANTUM, 0,
                                      PIECE // PIECE_QUANTUM), 0).astype(jnp.int32)
    key = jnp.where(p_fill > 0, p_exp, n_exp).reshape(-1)
    perm = jnp.argsort(key, stable=True).astype(jnp.int32)
    p_exp_sorted = jnp.minimum(key[perm], n_exp - 1).astype(jnp.int32)
    d_exp = jnp.repeat(p_exp, PIECE // DISPATCH_CHUNK, axis=1).reshape(-1)
    return (off.reshape(-1).astype(jnp.int32), d_lo.reshape(-1), d_hi.reshape(-1), d_exp,
            start.reshape(-1).astype(jnp.int32), end.reshape(-1).astype(jnp.int32),
            perm, p_exp_sorted, p_fill.reshape(-1)[perm])


def _pick(n, pref):
    t = min(n, pref)
    while n % t:
        t //= 2
    return t


def kernel(x, c, positions, ada_w, ada_b, mix_norm_g, ffn_norm_g, pool_w_in, pool_w_grp,
           pool_scale, pool_w_out, kv_ada_w, kv_ada_b, kv_norm_g, w_dkv, ckv_norm_g, w_kr,
           w_uk, w_uv, w_dq, q_norm_g, w_uq, w_o, ffn_w_gate, ffn_w_up, ffn_w_down,
           router_w, moe_w_gate, moe_w_up, moe_w_down, final_norm_g):
    b, s, d = x.shape
    n = b * s
    depth = ada_w.shape[0]
    assert depth == 2 and pool_w_in.shape[0] == 1 and w_dq.shape[0] == 1
    assert b <= 8 and d % LANES == 0

    ts = _pick(s, 512)
    tm = _pick(s, 512)
    tf = _pick(ffn_w_gate.shape[-1], 512)

    c_pad = jnp.zeros((8, d), F32).at[:b].set(c)
    mods = _adaln(c_pad, ada_w, ada_b[:, None, :], _pick(6 * d, 1536))
    mods = mods[:, :b].reshape(depth, b, 6, d)
    kvmod = _adaln(c_pad, kv_ada_w[None], kv_ada_b[None, None, :], _pick(2 * d, 1024))
    kvmod = kvmod[0, :b].reshape(b, 2, d)

    r = jnp.arange(min(ts, POOL_ROW_BLOCK))[:, None]
    cidx = jnp.arange(min(ts, POOL_ROW_BLOCK) + POOL_HALO)[None, :] - POOL_HALO
    band = jnp.stack([((cidx <= r) & (cidx > r - w)) for w in POOL_WINDOWS]).astype(BF16)
    inv_freq = ROPE_THETA ** (-jnp.arange(0, D_ROPE, 2, dtype=F32) / D_ROPE)
    x1, fwg, fwu, fwd, rope_cos, rope_sin, h1 = _pool_mixer(
        x, mods[0], mix_norm_g[0:1], pool_w_in[0].astype(BF16), band, pool_w_grp[0].astype(BF16),
        pool_scale[0:1], pool_w_out[0].astype(BF16), ffn_w_gate, ffn_w_up, ffn_w_down,
        positions.astype(F32)[:, None, :], inv_freq[:, None], ffn_norm_g[0:1], ts)

    x2 = _dense_ffn(x1.reshape(n, d), h1.reshape(n, d), mods[0], fwg, fwu, fwd, tm, tf, s)


    q_lora = w_dq.shape[-1]
    wuq = w_uq[0].reshape(q_lora, N_HEADS, D_NOPE + D_ROPE)
    wqn_t = wuq[:, :, :D_NOPE].reshape(q_lora, N_HEADS * D_NOPE).T.astype(BF16)
    wqr_t = wuq[:, :, D_NOPE:].reshape(q_lora, N_HEADS * D_ROPE).T.astype(BF16)
    qt, k, vt = _qkv_proj(
        x2.reshape(b, s, d), mods[1], kvmod, mix_norm_g[1:2], kv_norm_g[None, :],
        w_dq[0].astype(BF16), q_norm_g[0:1], wqn_t, wqr_t, w_dkv.astype(BF16),
        ckv_norm_g[None, :], jnp.pad(w_kr, ((0, 0), (0, LANES - D_ROPE))).astype(BF16),
        w_uk.astype(BF16), w_uv.T.astype(BF16), rope_cos, rope_sin, ts)
    o, ewg, ewu, ewd = _attention(qt, k, vt, moe_w_gate[0], moe_w_up[0], moe_w_down[0], ts)

    n_exp = router_w.shape[-1]
    sub = _pick(s, ROUTE_SUB)
    n_sub = s // sub
    rcap = 2 * s + n_exp * PIECE
    ltri = (jnp.arange(ts)[None, :] < jnp.arange(ts)[:, None]).astype(BF16)
    assert n_exp <= LANES
    wr = jnp.pad(router_w[0], ((0, 0), (0, LANES - n_exp)))
    wr_hi = wr.astype(BF16)
    wr_split = jnp.concatenate([wr_hi, (wr - wr_hi.astype(F32)).astype(BF16)], axis=1)
    x3, h3, rcol, rrow, cnt = _attn_out(o.reshape(n, N_HEADS * D_V), x2, mods[1],
                                        w_o[0].astype(BF16), ffn_norm_g[1:2], wr_split, n_exp,
                                        ltri, ts, sub, s)
    cnt = cnt[:, :ts // sub, :n_exp].reshape(b, n_sub, n_exp)
    off, d_lo, d_hi, d_exp, c_lo, c_hi, perm, p_exp, p_fill = _routing_tables(cnt, b, n_sub, n_exp, rcap)
    xg, gs = _dispatch(off, d_lo, d_hi, d_exp, h3, rrow, b, s, rcap, n_exp, sub)
    yg = _expert_ffn(perm, p_exp, p_fill, xg, gs, ewg, ewu, ewd, tf)
    out = _combine(off, c_lo, c_hi, yg, rcol, x3, mods[1], final_norm_g[None, :], b, s, rcap,
                   n_exp, sub, sub)
    return out.reshape(b, s, d)
```

```python
import functools
import math

import jax
import jax.numpy as jnp
from jax import lax
from jax.experimental import pallas as pl
from jax.experimental.pallas import tpu as pltpu

F32 = jnp.float32
BF16 = jnp.bfloat16

N_HEADS = 8
D_NOPE = 128
D_ROPE = 64
D_V = 128
HEAD_PAD = 256
ROPE_HALF = D_ROPE // 2
ROPE_THETA = 10000.0
ATTN_SCALE = 1.0 / math.sqrt(D_NOPE + D_ROPE)
Q_SCALE = ATTN_SCALE * math.log2(math.e)
ATTN_QUERY_TILE = 512
ATTN_KEY_BLOCK = 512
ATTN_LOOKAHEAD = 2
POOL_WINDOWS = (2, 4, 8, 16)
POOL_HALO = 16
POOL_ROW_BLOCK = 128
EPS = 1e-6

ROUTE_SUB = 256
DISPATCH_CHUNK = 128
DISPATCH_WINDOW = 3
COMBINE_WINDOW = 256
PIECE = 512
PIECE_QUANTUM = 128
ROUTE_FIELDS = 8

LANES = 128
VMEM_LIMIT_BYTES = 56 * 1024 * 1024


def _cparams(sem):
    return pltpu.CompilerParams(dimension_semantics=sem, vmem_limit_bytes=VMEM_LIMIT_BYTES)


def _dot(a, b):
    return jnp.dot(a, b, preferred_element_type=F32)


def _rms_scale(x):
    return x * lax.rsqrt(jnp.mean(x * x, axis=-1, keepdims=True) + EPS)


def _silu(x):
    return x * jax.nn.sigmoid(x)


def _adaln_kernel(c_ref, w_ref, b_ref, o_ref):
    sc = _silu(c_ref[...])
    o_ref[0] = jnp.dot(sc, w_ref[0], preferred_element_type=F32,
                       precision=lax.Precision.HIGHEST) + b_ref[0]


def _adaln(c_pad, w, b, tn):
    nl, d, m = w.shape
    return pl.pallas_call(
        _adaln_kernel,
        grid=(nl, m // tn),
        in_specs=[
            pl.BlockSpec((8, d), lambda l, j: (0, 0)),
            pl.BlockSpec((1, d, tn), lambda l, j: (l, 0, j)),
            pl.BlockSpec((1, 1, tn), lambda l, j: (l, 0, j)),
        ],
        out_specs=pl.BlockSpec((1, 8, tn), lambda l, j: (l, 0, j)),
        out_shape=jax.ShapeDtypeStruct((nl, 8, m), F32),
        compiler_params=_cparams(("arbitrary", "arbitrary")),
        name="adaln",
    )(c_pad, w, b)


def _rope_table_side_job(pos_ref, freq_ref, cos_ref, sin_ref):
    ang = freq_ref[...] * pos_ref[0]
    cos_ref[0, 0] = jnp.cos(ang)
    sin_ref[0, 0] = jnp.sin(ang)


def _rope_t(x, cos_t, sin_t):
    x1, x2 = x[:ROPE_HALF], x[ROPE_HALF:]
    return x1 * cos_t - x2 * sin_t, x2 * cos_t + x1 * sin_t


def _cast_specs(n_exp, d, ff, steps, step_of):
    assert steps >= n_exp, "too few grid steps to carry the weight cast"
    k = max(c for c in range(1, steps // n_exp + 1) if ff % (c * LANES) == 0)
    cw = ff // k

    def chunk(*idx):
        c = jnp.minimum(step_of(*idx), n_exp * k - 1)
        return c // k, c % k

    gate_up = pl.BlockSpec((1, d, cw), lambda *idx: (chunk(*idx)[0], 0, chunk(*idx)[1]))
    down = pl.BlockSpec((1, cw, d), lambda *idx: (chunk(*idx)[0], chunk(*idx)[1], 0))
    return gate_up, down


def _cast_side_job(src_refs, dst_refs):
    for src, dst in zip(src_refs, dst_refs):
        dst[...] = src[...].astype(BF16)


def _pool_kernel(x_ref, mod_ref, g_ref, win_ref, band_ref, wgrp_ref, scale_ref, wout_ref,
                 fg_ref, fu_ref, fd_ref, pos_ref, freq_ref, gffn_ref,
                 o_ref, fg_out_ref, fu_out_ref, fd_out_ref, cos_ref, sin_ref, h_out_ref, uext_ref,
                 *, ts, gw):
    s = pl.program_id(1)

    @pl.when(s == 0)
    def _():
        uext_ref[0:POOL_HALO, :] = jnp.zeros((POOL_HALO, uext_ref.shape[1]), BF16)

    x = x_ref[0]
    mod = mod_ref[0]
    h = (_rms_scale(x) * g_ref[...]) * (1.0 + mod[1:2]) + mod[0:1]
    u = _dot(h.astype(BF16), win_ref[...])
    uext_ref[POOL_HALO:POOL_HALO + ts, :] = u.astype(BF16)

    t = s * ts + lax.broadcasted_iota(jnp.int32, (ts, 1), 0)
    zs = []
    for g, w in enumerate(POOL_WINDOWS):
        cols = slice(g * gw, (g + 1) * gw)
        rb = band_ref.shape[1]
        wsum = jnp.concatenate(
            [_dot(band_ref[g], uext_ref[r0:r0 + rb + POOL_HALO, cols]) for r0 in range(0, ts, rb)],
            axis=0)
        cnt = jnp.minimum(t + 1, w).astype(F32)
        p = wsum / cnt - u[:, cols]
        zs.append(_dot(p.astype(BF16), wgrp_ref[g]))
    z = jnp.concatenate(zs, axis=-1) * scale_ref[...]
    y = _dot(z.astype(BF16), wout_ref[...])
    x_out = x + mod[2:3] * y
    o_ref[0] = x_out
    h_out_ref[0] = ((_rms_scale(x_out) * gffn_ref[...]) * (1.0 + mod[4:5]) + mod[3:4]).astype(BF16)

    uext_ref[0:POOL_HALO, :] = uext_ref[ts:ts + POOL_HALO, :]

    _cast_side_job((fg_ref, fu_ref, fd_ref), (fg_out_ref, fu_out_ref, fd_out_ref))
    _rope_table_side_job(pos_ref, freq_ref, cos_ref, sin_ref)


def _pool_mixer(x, mod, g, w_in, band, w_grp, scale, w_out, fg, fu, fd, pos_f, freq_row, g_ffn,
                ts):
    b, s, d = x.shape
    ng, gw, _ = w_grp.shape
    n_s = s // ts
    gate_up, down = _cast_specs(1, d, fg.shape[2], b * n_s, lambda i, j: i * n_s + j)
    const2 = lambda i, j: (0, 0)
    const3 = lambda i, j: (0, 0, 0)
    return pl.pallas_call(
        functools.partial(_pool_kernel, ts=ts, gw=gw),
        grid=(b, s // ts),
        in_specs=[
            pl.BlockSpec((1, ts, d), lambda i, j: (i, j, 0)),
            pl.BlockSpec((1, 6, d), lambda i, j: (i, 0, 0)),
            pl.BlockSpec((1, d), const2),
            pl.BlockSpec((d, d), const2),
            pl.BlockSpec(band.shape, const3),
            pl.BlockSpec((ng, gw, gw), const3),
            pl.BlockSpec((1, d), const2),
            pl.BlockSpec((d, d), const2),
            gate_up, gate_up, down,
            pl.BlockSpec((1, 1, ts), lambda i, j: (i, 0, j)),
            pl.BlockSpec((ROPE_HALF, 1), const2),
            pl.BlockSpec((1, d), const2),
        ],
        out_specs=[pl.BlockSpec((1, ts, d), lambda i, j: (i, j, 0)), gate_up, gate_up, down,
                   pl.BlockSpec((1, 1, ROPE_HALF, ts), lambda i, j: (i, j, 0, 0)),
                   pl.BlockSpec((1, 1, ROPE_HALF, ts), lambda i, j: (i, j, 0, 0)),
                   pl.BlockSpec((1, ts, d), lambda i, j: (i, j, 0))],
        out_shape=[jax.ShapeDtypeStruct((b, s, d), F32),
                   jax.ShapeDtypeStruct(fg.shape, BF16),
                   jax.ShapeDtypeStruct(fu.shape, BF16),
                   jax.ShapeDtypeStruct(fd.shape, BF16),
                   jax.ShapeDtypeStruct((b, s // ts, ROPE_HALF, ts), F32),
                   jax.ShapeDtypeStruct((b, s // ts, ROPE_HALF, ts), F32),
                   jax.ShapeDtypeStruct((b, s, d), BF16)],
        scratch_shapes=[pltpu.VMEM((POOL_HALO + ts, d), BF16)],
        compiler_params=_cparams(("arbitrary", "arbitrary")),
        name="pool_mixer",
    )(x, mod, g, w_in, band, w_grp, scale, w_out, fg, fu, fd, pos_f, freq_row, g_ffn)


def _ffn_kernel(x_ref, h_ref, mod_ref, wg_ref, wu_ref, wd_ref, o_ref, *, tf):
    h = h_ref[...]
    ff = wg_ref.shape[2]
    acc = jnp.zeros(x_ref.shape, F32)
    for f0 in range(0, ff, tf):
        gate = _dot(h, wg_ref[0, :, f0:f0 + tf])
        up = _dot(h, wu_ref[0, :, f0:f0 + tf])
        a = (_silu(gate) * up).astype(BF16)
        acc = acc + _dot(a, wd_ref[0, f0:f0 + tf, :])
    o_ref[...] = x_ref[...] + mod_ref[0][5:6] * acc


def _dense_ffn(x2, h2, mod, wg, wu, wd, tm, tf, rows_per_batch):
    n, d = x2.shape
    tpb = rows_per_batch // tm
    resident = lambda a: pl.BlockSpec(a.shape, lambda i: (0, 0, 0), pipeline_mode=pl.Buffered(1))
    return pl.pallas_call(
        functools.partial(_ffn_kernel, tf=tf),
        grid=(n // tm,),
        in_specs=[
            pl.BlockSpec((tm, d), lambda i: (i, 0)),
            pl.BlockSpec((tm, d), lambda i: (i, 0)),
            pl.BlockSpec((1, 6, d), lambda i: (i // tpb, 0, 0)),
            resident(wg), resident(wu), resident(wd),
        ],
        out_specs=pl.BlockSpec((tm, d), lambda i: (i, 0)),
        out_shape=jax.ShapeDtypeStruct((n, d), F32),
        compiler_params=_cparams(("arbitrary",)),
        name="dense_ffn",
    )(x2, h2, mod, wg, wu, wd)


def _dot_nt(a, b):
    return lax.dot_general(a, b, (((1,), (1,)), ((), ())), preferred_element_type=F32)


def _qkv_kernel(x_ref, mod_ref, kvmod_ref, gmix_ref, gkv_ref, wdq_ref, qg_ref, wqn_t_ref,
                wqr_t_ref, wdkv_ref, ckvg_ref, wkr_ref, wuk_ref, wuv_t_ref, cos_ref, sin_ref,
                qt_ref, k_ref, vt_ref):
    x = x_ref[0]
    xn = _rms_scale(x)
    mod = mod_ref[0]
    kvmod = kvmod_ref[0]
    hq = ((xn * gmix_ref[...]) * (1.0 + mod[1:2]) + mod[0:1]).astype(BF16)
    hk = ((xn * gkv_ref[...]) * (1.0 + kvmod[1:2]) + kvmod[0:1]).astype(BF16)

    cq = (_rms_scale(_dot(hq, wdq_ref[...])) * qg_ref[...]).astype(BF16)
    qn_t = _dot_nt(wqn_t_ref[...], cq) * Q_SCALE
    qr_t = _dot_nt(wqr_t_ref[...], cq) * Q_SCALE
    cos_t = cos_ref[0, 0]
    sin_t = sin_ref[0, 0]
    zero_rows = jnp.zeros((HEAD_PAD - LANES - D_ROPE, x.shape[0]), BF16)

    ckv = (_rms_scale(_dot(hk, wdkv_ref[...])) * ckvg_ref[...]).astype(BF16)
    kn = _dot(ckv, wuk_ref[...])
    vt_ref[0, 0] = _dot_nt(wuv_t_ref[...], ckv).astype(BF16)
    kr_raw_t = _dot(hk, wkr_ref[...]).T
    k1, k2 = _rope_t(kr_raw_t[:D_ROPE], cos_t, sin_t)
    kr_t = jnp.concatenate([k1, k2, kr_raw_t[D_ROPE:]], axis=0)
    kr = kr_t.T.astype(BF16)

    for h in range(N_HEADS):
        lo = h * HEAD_PAD
        hs = slice(h * LANES, (h + 1) * LANES)
        qt_ref[0, 0, lo:lo + LANES, :] = qn_t[hs].astype(BF16)
        q1, q2 = _rope_t(qr_t[h * D_ROPE:(h + 1) * D_ROPE], cos_t, sin_t)
        ro = lo + LANES
        qt_ref[0, 0, ro:ro + ROPE_HALF, :] = q1.astype(BF16)
        qt_ref[0, 0, ro + ROPE_HALF:ro + D_ROPE, :] = q2.astype(BF16)
        qt_ref[0, 0, ro + D_ROPE:lo + HEAD_PAD, :] = zero_rows
        k_ref[0, :, lo:lo + LANES] = kn[:, hs].astype(BF16)
        k_ref[0, :, lo + LANES:lo + HEAD_PAD] = kr


def _qkv_proj(x, mod, kvmod, gmix, gkv, wdq, qg, wqn_t, wqr_t, wdkv, ckvg, wkr, wuk, wuv_t,
              cos_t, sin_t, ts):
    b, s, d = x.shape
    row = lambda i, j: (i, j, 0)
    tile = lambda i, j: (i, j, 0, 0)
    per_b = lambda i, j: (i, 0, 0)

    def full(a):
        return pl.BlockSpec(a.shape, lambda i, j: (0,) * a.ndim)

    weights = (gmix, gkv, wdq, qg, wqn_t, wqr_t, wdkv, ckvg, wkr, wuk, wuv_t)
    return pl.pallas_call(
        _qkv_kernel,
        grid=(b, s // ts),
        in_specs=[pl.BlockSpec((1, ts, d), row),
                  pl.BlockSpec((1, 6, d), per_b),
                  pl.BlockSpec((1, 2, d), per_b)]
                 + [full(a) for a in weights]
                 + [pl.BlockSpec((1, 1, ROPE_HALF, ts),
                                 lambda i, j: (i, j * ts // cos_t.shape[3], 0,
                                               j % (cos_t.shape[3] // ts)))] * 2,
        out_specs=[pl.BlockSpec((1, 1, N_HEADS * HEAD_PAD, ts), tile),
                   pl.BlockSpec((1, ts, N_HEADS * HEAD_PAD), row),
                   pl.BlockSpec((1, 1, N_HEADS * D_V, ts), tile)],
        out_shape=[jax.ShapeDtypeStruct((b, s // ts, N_HEADS * HEAD_PAD, ts), BF16),
                   jax.ShapeDtypeStruct((b, s, N_HEADS * HEAD_PAD), BF16),
                   jax.ShapeDtypeStruct((b, s // ts, N_HEADS * D_V, ts), BF16)],
        compiler_params=_cparams(("arbitrary", "arbitrary")),
        name="qkv_proj",
    )(x, mod, kvmod, *weights, cos_t, sin_t)


def _attn_kernel(qt_ref, k_ref, vt_ref, eg_ref, eu_ref, ed_ref,
                 o_ref, eg_out_ref, eu_out_ref, ed_out_ref, *, tq, tk):
    ts = qt_ref.shape[3]
    s = k_ref.shape[1]
    key = lax.broadcasted_iota(jnp.int32, (tk, tq), 0)
    qry = lax.broadcasted_iota(jnp.int32, (tk, tq), 1)
    last_block = lambda qi: ((qi + 1) * tq - 1) // tk
    blocks = [(qi, j) for qi in range(s // tq) for j in range(last_block(qi) + 1)]

    def tile(ref, start, size):
        return ref[0, start // ts, :, start % ts:start % ts + size]

    def scores_t(qi, j):
        st = _dot(k_ref[0, j * tk:(j + 1) * tk, :], tile(qt_ref, qi * tq, tq))
        if (j + 1) * tk - 1 > qi * tq:
            st = jnp.where(key + (j * tk - qi * tq) <= qry, st, -jnp.inf)
        return st

    ahead = [scores_t(*blk) for blk in blocks[:ATTN_LOOKAHEAD]]
    for n, (qi, j) in enumerate(blocks):
        st = ahead.pop(0)
        if n + ATTN_LOOKAHEAD < len(blocks):
            ahead.append(scores_t(*blocks[n + ATTN_LOOKAHEAD]))
        if j == 0:
            m = jnp.full((1, tq), -jnp.inf, F32)
            l = jnp.zeros((1, tq), F32)
            acc = jnp.zeros((D_V, tq), F32)
        m_new = jnp.maximum(m, jnp.max(st, axis=0, keepdims=True))
        alpha = jnp.exp2(m - m_new)
        pt = jnp.exp2(st - m_new)
        l = alpha * l + jnp.sum(pt, axis=0, keepdims=True)
        acc = alpha * acc + _dot(tile(vt_ref, j * tk, tk), pt.astype(BF16))
        m = m_new
        if j == last_block(qi):
            o_ref[0, qi * tq:(qi + 1) * tq, :] = (acc / l).T.astype(BF16)

    _cast_side_job((eg_ref, eu_ref, ed_ref), (eg_out_ref, eu_out_ref, ed_out_ref))


def _attention(qt, k, vt, eg, eu, ed, ts):
    b, s, _ = k.shape
    n_exp, d, eff = eg.shape
    gate_up, down = _cast_specs(n_exp, d, eff, b * N_HEADS, lambda i, h: i * N_HEADS + h)
    return pl.pallas_call(
        functools.partial(_attn_kernel, tq=min(ts, ATTN_QUERY_TILE), tk=min(ts, ATTN_KEY_BLOCK)),
        grid=(b, N_HEADS),
        in_specs=[
            pl.BlockSpec((1, s // ts, HEAD_PAD, ts), lambda i, h: (i, 0, h, 0)),
            pl.BlockSpec((1, s, HEAD_PAD), lambda i, h: (i, 0, h)),
            pl.BlockSpec((1, s // ts, D_V, ts), lambda i, h: (i, 0, h, 0)),
            gate_up, gate_up, down,
        ],
        out_specs=[pl.BlockSpec((1, s, D_V), lambda i, h: (i, 0, h)), gate_up, gate_up, down],
        out_shape=[jax.ShapeDtypeStruct((b, s, N_HEADS * D_V), BF16),
                   jax.ShapeDtypeStruct(eg.shape, BF16),
                   jax.ShapeDtypeStruct(eu.shape, BF16),
                   jax.ShapeDtypeStruct(ed.shape, BF16)],
        compiler_params=_cparams(("arbitrary", "arbitrary")),
        name="flash_attn",
    )(qt, k, vt, eg, eu, ed)


def _attn_out_kernel(o_ref, x_ref, mod_ref, wo_ref, g_ref, wr_ref, ltri_ref,
                     x_out_ref, h_out_ref, rcol_ref, rrow_ref, cnt_ref, carry_ref,
                     *, n_exp, tiles_per_block):
    i = pl.program_id(0)

    @pl.when(i % tiles_per_block == 0)
    def _():
        carry_ref[...] = jnp.zeros_like(carry_ref)

    mod = mod_ref[0]
    x = x_ref[...] + mod[2:3] * _dot(o_ref[...], wo_ref[...])
    x_out_ref[...] = x
    h = (_rms_scale(x) * g_ref[...]) * (1.0 + mod[4:5]) + mod[3:4]
    h_hi = h.astype(BF16)
    h_out_ref[...] = h_hi

    tm = x.shape[0]
    h_lo = (h - h_hi.astype(F32)).astype(BF16)
    parts = _dot(jnp.concatenate([h_hi, h_lo], axis=0), wr_ref[...])
    logits = (parts[:tm, :LANES] + parts[:tm, LANES:]) + (parts[tm:, :LANES] + parts[tm:, LANES:])
    lane = lax.broadcasted_iota(jnp.int32, (tm, LANES), 1)
    lane_f = lane.astype(F32)

    def first_argmax(vals):
        m = jnp.max(vals, axis=-1, keepdims=True)
        idx = jnp.min(jnp.where(vals == m, lane_f, float(LANES)), axis=-1, keepdims=True)
        return m, idx.astype(jnp.int32)

    logits = jnp.where(lane < n_exp, logits, -jnp.inf)
    m1, i1 = first_argmax(logits)
    m2, i2 = first_argmax(jnp.where(lane == i1, -jnp.inf, logits))
    ex = jnp.exp(m2 - m1)
    w1 = 1.0 / (1.0 + ex)
    w2 = ex / (1.0 + ex)

    mask = jnp.where((lane == i1) | (lane == i2), 1.0, 0.0)
    ranks = _dot(ltri_ref[...], mask.astype(BF16)) + carry_ref[0:1]
    incl = ranks + mask
    rank1 = jnp.sum(jnp.where(lane == i1, ranks, 0.0), axis=-1, keepdims=True)
    rank2 = jnp.sum(jnp.where(lane == i2, ranks, 0.0), axis=-1, keepdims=True)
    carry_ref[0:1] = incl[tm - 1:tm]
    n_sub = tm // rrow_ref.shape[2]
    sub = rrow_ref.shape[2]
    cnt_ref[0] = jnp.zeros(cnt_ref.shape[1:], F32)
    for j in range(n_sub):
        cnt_ref[0, j:j + 1] = incl[(j + 1) * sub - 1:(j + 1) * sub]

    fields = (i1.astype(F32), i2.astype(F32), rank1, rank2, w1, w2)
    rcol = jnp.zeros((tm, LANES), F32)
    for k, val in enumerate(fields):
        rcol = jnp.where(lane == k, val, rcol)
    rcol_ref[...] = rcol
    rrow = rcol.T
    for j in range(n_sub):
        rrow_ref[j] = rrow[0:ROUTE_FIELDS, j * sub:(j + 1) * sub]


def _attn_out(o2, x2, mod, wo, g, wr, n_exp, ltri, tm, sub, rows_per_block):
    n, d = x2.shape
    tpb = rows_per_block // tm
    n_sub = tm // sub
    return pl.pallas_call(
        functools.partial(_attn_out_kernel, n_exp=n_exp, tiles_per_block=tpb),
        grid=(n // tm,),
        in_specs=[
            pl.BlockSpec((tm, o2.shape[1]), lambda i: (i, 0)),
            pl.BlockSpec((tm, d), lambda i: (i, 0)),
            pl.BlockSpec((1, 6, d), lambda i: (i // tpb, 0, 0)),
            pl.BlockSpec(wo.shape, lambda i: (0, 0)),
            pl.BlockSpec((1, d), lambda i: (0, 0)),
            pl.BlockSpec(wr.shape, lambda i: (0, 0)),
            pl.BlockSpec((tm, tm), lambda i: (0, 0)),
        ],
        out_specs=[pl.BlockSpec((tm, d), lambda i: (i, 0)),
                   pl.BlockSpec((tm, d), lambda i: (i, 0)),
                   pl.BlockSpec((tm, LANES), lambda i: (i, 0)),
                   pl.BlockSpec((n_sub, ROUTE_FIELDS, sub), lambda i: (i, 0, 0)),
                   pl.BlockSpec((1, 8, LANES), lambda i: (i, 0, 0))],
        out_shape=[jax.ShapeDtypeStruct((n, d), F32),
                   jax.ShapeDtypeStruct((n, d), BF16),
                   jax.ShapeDtypeStruct((n, LANES), F32),
                   jax.ShapeDtypeStruct((n // sub, ROUTE_FIELDS, sub), F32),
                   jax.ShapeDtypeStruct((n // tm, 8, LANES), F32)],
        scratch_shapes=[pltpu.VMEM((8, LANES), F32)],
        compiler_params=_cparams(("arbitrary",)),
        name="attn_out_router",
    )(o2, x2, mod, wo, g, wr, ltri)


def _dispatch_kernel(off_ref, lo_ref, hi_ref, cexp_ref, h_ref, rrow_ref, xg_ref, gs_ref,
                     *, n_exp, sub, chunks_per_step, chunks_per_block):
    blk = pl.program_id(0)
    g = pl.program_id(1)
    d = h_ref.shape[1]
    n_sub = rrow_ref.shape[0]
    win = min(DISPATCH_WINDOW, n_sub)
    iota = lax.broadcasted_iota(jnp.int32, (DISPATCH_CHUNK, 1), 0)

    def contribution(info, sb):
        row, exp_id, exp_off = info
        r = rrow_ref[sb]
        first = r[0:1] == exp_id
        second = r[1:2] == exp_id
        pos = jnp.where(first, r[2:3] + exp_off, jnp.where(second, r[3:4] + exp_off, -1.0))
        hit = pos == row
        onehot = jnp.where(hit, 1.0, 0.0).astype(BF16)
        tok = h_ref[pl.ds(pl.multiple_of(sb * sub, sub), sub), :]
        gate = jnp.sum(jnp.where(hit, jnp.where(first, r[4:5], r[5:6]), 0.0),
                       axis=-1, keepdims=True)
        return _dot(onehot, tok), gate

    def chunk_info(j):
        c = g * chunks_per_step + j
        t = blk * chunks_per_block + c
        row = (c * DISPATCH_CHUNK + iota).astype(F32)
        exp_id = cexp_ref[t]
        exp_off = off_ref[blk * n_exp + exp_id].astype(F32)
        start = jnp.minimum(lo_ref[t], n_sub - win)
        return (row, exp_id.astype(F32), exp_off), start, hi_ref[t]

    first_chunk = blk * chunks_per_block + g * chunks_per_step
    used = functools.reduce(jnp.logical_or,
                            [hi_ref[first_chunk + j] >= 0 for j in range(chunks_per_step)])

    @pl.when(jnp.logical_not(used))
    def _():
        xg_ref[...] = jnp.zeros(xg_ref.shape, BF16)
        gs_ref[...] = jnp.zeros(gs_ref.shape, F32)

    @pl.when(used)
    def _():
        for j in range(chunks_per_step):
            info, start, _ = chunk_info(j)
            acc = jnp.zeros((DISPATCH_CHUNK, d), F32)
            gate = jnp.zeros((DISPATCH_CHUNK, 1), F32)
            for k in range(win):
                a, b = contribution(info, start + k)
                acc += a
                gate += b
            rows = slice(j * DISPATCH_CHUNK, (j + 1) * DISPATCH_CHUNK)
            xg_ref[rows, :] = acc.astype(BF16)
            gs_ref[rows, :] = gate

        def extra(j, _):
            info, start, last = chunk_info(j)
            rows = pl.ds(pl.multiple_of(j * DISPATCH_CHUNK, DISPATCH_CHUNK), DISPATCH_CHUNK)

            def add(sb, _):
                a, b = contribution(info, sb)
                xg_ref[rows, :] = (xg_ref[rows, :].astype(F32) + a).astype(BF16)
                gs_ref[rows, :] = gs_ref[rows, :] + b
                return 0

            return lax.fori_loop(start + win, last + 1, add, 0)

        lax.fori_loop(0, chunks_per_step, extra, 0)


def _dispatch(off, lo, hi, cexp, h, rrow, n_blocks, rows_per_block, rcap, n_exp, sub):
    n, d = h.shape
    n_sub = rows_per_block // sub
    step_rows = _pick(rcap, 1024)
    cps = step_rows // DISPATCH_CHUNK
    cpb = rcap // DISPATCH_CHUNK
    grid_spec = pltpu.PrefetchScalarGridSpec(
        num_scalar_prefetch=4,
        grid=(n_blocks, rcap // step_rows),
        in_specs=[
            pl.BlockSpec((rows_per_block, d), lambda b, g, *_: (b, 0)),
            pl.BlockSpec((n_sub, ROUTE_FIELDS, sub), lambda b, g, *_: (b, 0, 0)),
        ],
        out_specs=[
            pl.BlockSpec((step_rows, d), lambda b, g, *_: (b * (rcap // step_rows) + g, 0)),
            pl.BlockSpec((step_rows, 1), lambda b, g, *_: (b * (rcap // step_rows) + g, 0)),
        ],
    )
    return pl.pallas_call(
        functools.partial(_dispatch_kernel, n_exp=n_exp, sub=sub, chunks_per_step=cps,
                          chunks_per_block=cpb),
        grid_spec=grid_spec,
        out_shape=[jax.ShapeDtypeStruct((n_blocks * rcap, d), BF16),
                   jax.ShapeDtypeStruct((n_blocks * rcap, 1), F32)],
        compiler_params=_cparams(("arbitrary", "arbitrary")),
        name="moe_dispatch",
    )(off, lo, hi, cexp, h, rrow)


def _expert_ffn_kernel(perm_ref, pexp_ref, pfill_ref, x_ref, gs_ref, wg_ref, wu_ref, wd_ref,
                       y_ref, *, tf):
    w = pl.program_id(0)
    fill = pfill_ref[w]
    ff = wg_ref.shape[2]

    def ffn(n_rows):
        x = x_ref[0:n_rows, :]
        gs = gs_ref[0:n_rows, :]
        acc = jnp.zeros((n_rows, y_ref.shape[1]), F32)
        for f0 in range(0, ff, tf):
            gate = _dot(x, wg_ref[0, :, f0:f0 + tf])
            up = _dot(x, wu_ref[0, :, f0:f0 + tf])
            a = ((_silu(gate) * up) * gs).astype(BF16)
            acc = acc + _dot(a, wd_ref[0, f0:f0 + tf, :])
        return acc.astype(BF16)

    for level in range(PIECE // PIECE_QUANTUM + 1):
        n_rows = level * PIECE_QUANTUM

        @pl.when(fill == level)
        def _(n_rows=n_rows):
            if n_rows:
                y_ref[0:n_rows, :] = ffn(n_rows)
            if n_rows < PIECE:
                y_ref[n_rows:PIECE, :] = jnp.zeros((PIECE - n_rows, y_ref.shape[1]), BF16)


def _expert_ffn(perm, pexp, pfill, xg, gs, wg, wu, wd, tf):
    rows, d = xg.shape
    n_exp, _, ff = wg.shape
    grid_spec = pltpu.PrefetchScalarGridSpec(
        num_scalar_prefetch=3,
        grid=(rows // PIECE,),
        in_specs=[
            pl.BlockSpec((PIECE, d), lambda w, perm, pexp, pfill: (perm[w], 0)),
            pl.BlockSpec((PIECE, 1), lambda w, perm, pexp, pfill: (perm[w], 0)),
            pl.BlockSpec((1, d, ff), lambda w, perm, pexp, pfill: (pexp[w], 0, 0)),
            pl.BlockSpec((1, d, ff), lambda w, perm, pexp, pfill: (pexp[w], 0, 0)),
            pl.BlockSpec((1, ff, d), lambda w, perm, pexp, pfill: (pexp[w], 0, 0)),
        ],
        out_specs=pl.BlockSpec((PIECE, d), lambda w, perm, pexp, pfill: (perm[w], 0)),
    )
    return pl.pallas_call(
        functools.partial(_expert_ffn_kernel, tf=tf),
        grid_spec=grid_spec,
        out_shape=jax.ShapeDtypeStruct((rows, d), BF16),
        compiler_params=_cparams(("arbitrary",)),
        name="expert_ffn",
    )(perm, pexp, pfill, xg, gs, wg, wu, wd)


def _combine_kernel(off_ref, start_ref, end_ref, yg_ref, rcol_ref, x_ref, mod_ref, gfin_ref, o_ref,
                    acc_ref, *, n_exp, sub, subs_per_block):
    blk = pl.program_id(0)
    t = pl.program_id(1)
    tm, d = x_ref.shape
    col = lax.broadcasted_iota(jnp.int32, (1, COMBINE_WINDOW), 1)
    align = COMBINE_WINDOW // 2

    def token_info(s):
        rc = rcol_ref[s * sub:(s + 1) * sub, :]
        return tuple(jnp.broadcast_to(rc[:, k:k + 1], (sub, COMBINE_WINDOW)) for k in range(4))

    def gathered(info, e, w0):
        i1, i2, rank1, rank2 = info
        rank = jnp.where(i1 == e, rank1, jnp.where(i2 == e, rank2, -float(COMBINE_WINDOW)))
        in_region = (w0 - off_ref[blk * n_exp + e] + col).astype(F32)
        onehot = jnp.where(rank == in_region, 1.0, 0.0).astype(BF16)
        return _dot(onehot, yg_ref[pl.ds(pl.multiple_of(w0, align), COMBINE_WINDOW), :])

    def first_window(s, e):
        idx = (blk * subs_per_block + t * (tm // sub) + s) * n_exp + e
        w0 = lax.shift_right_logical(start_ref[idx], align.bit_length() - 1) * align
        return w0, end_ref[idx]

    for s in range(tm // sub):
        info = token_info(s)
        acc = jnp.zeros((sub, d), F32)
        for e in range(n_exp):
            acc += gathered(info, e, first_window(s, e)[0])
        acc_ref[s * sub:(s + 1) * sub, :] = acc

    for s in range(tm // sub):
        info = token_info(s)
        for e in range(n_exp):
            w0, end = first_window(s, e)
            n_more = lax.shift_right_logical(
                jnp.maximum(end - w0 - 1, 0), COMBINE_WINDOW.bit_length() - 1)

            def add(k, _, info=info, e=e, w0=w0, s=s):
                rows = slice(s * sub, (s + 1) * sub)
                acc_ref[rows, :] = acc_ref[rows, :] + gathered(info, e, w0 + k * COMBINE_WINDOW)
                return 0

            lax.fori_loop(1, n_more + 1, add, 0)

    x = x_ref[...] + mod_ref[0][5:6] * acc_ref[...]
    o_ref[...] = _rms_scale(x) * gfin_ref[...]


def _combine(off, lo, hi, yg, rcol, x2, mod, gfin, n_blocks, rows_per_block, rcap, n_exp, sub, tm):
    n, d = x2.shape
    tpb = rows_per_block // tm
    grid_spec = pltpu.PrefetchScalarGridSpec(
        num_scalar_prefetch=3,
        grid=(n_blocks, tpb),
        in_specs=[
            pl.BlockSpec((rcap, d), lambda b, t, *_: (b, 0), pipeline_mode=pl.Buffered(1)),
            pl.BlockSpec((tm, LANES), lambda b, t, *_: (b * tpb + t, 0)),
            pl.BlockSpec((tm, d), lambda b, t, *_: (b * tpb + t, 0)),
            pl.BlockSpec((1, 6, d), lambda b, t, *_: (b, 0, 0)),
            pl.BlockSpec((1, d), lambda b, t, *_: (0, 0)),
        ],
        out_specs=pl.BlockSpec((tm, d), lambda b, t, *_: (b * tpb + t, 0)),
        scratch_shapes=[pltpu.VMEM((tm, d), F32)],
    )
    return pl.pallas_call(
        functools.partial(_combine_kernel, n_exp=n_exp, sub=sub,
                          subs_per_block=rows_per_block // sub),
        grid_spec=grid_spec,
        out_shape=jax.ShapeDtypeStruct((n, d), F32),
        compiler_params=_cparams(("arbitrary", "arbitrary")),
        name="moe_combine",
    )(off, lo, hi, yg, rcol, x2, mod, gfin)


def _routing_tables(cnt, n_blocks, n_sub, n_exp, rcap):
    cum_incl = cnt.astype(jnp.int32)
    cum_excl = jnp.concatenate([jnp.zeros_like(cum_incl[:, :1]), cum_incl[:, :-1]], axis=1)
    total = cum_incl[:, -1]
    psz = (total + PIECE - 1) // PIECE * PIECE
    off = jnp.cumsum(psz, axis=1) - psz
    start = off[:, None, :] + cum_excl
    end = off[:, None, :] + cum_incl

    c0 = jnp.arange(rcap // DISPATCH_CHUNK, dtype=jnp.int32)[None, :, None, None] * DISPATCH_CHUNK
    hit = ((start[:, None] < c0 + DISPATCH_CHUNK) & (end[:, None] > c0)
           & (end[:, None] > start[:, None])).any(-1)
    sub_ids = jnp.arange(n_sub, dtype=jnp.int32)
    d_lo = jnp.min(jnp.where(hit, sub_ids, n_sub), axis=-1)
    d_hi = jnp.max(jnp.where(hit, sub_ids, -1), axis=-1)
    d_lo = jnp.where(d_hi < 0, 0, d_lo)


    ppb = rcap // PIECE
    prow = jnp.arange(ppb, dtype=jnp.int32)[None, :, None] * PIECE
    inside = (prow >= off[:, None, :]) & (prow < (off + psz)[:, None, :])
    p_exp = jnp.argmax(inside, axis=-1).astype(jnp.int32)
    used = inside.any(-1)
    left = jnp.take_along_axis(total + off, p_exp, axis=1) - prow[..., 0]
    p_fill = jnp.where(used, jnp.clip((left + PIECE_QUANTUM - 1) // PIECE_QUANTUM, 0,
                                      PIECE // PIECE_QUANTUM), 0).astype(jnp.int32)
    key = jnp.where(p_fill > 0, p_exp, n_exp).reshape(-1)
    perm = jnp.argsort(key, stable=True).astype(jnp.int32)
    p_exp_sorted = jnp.minimum(key[perm], n_exp - 1).astype(jnp.int32)
    d_exp = jnp.repeat(p_exp, PIECE // DISPATCH_CHUNK, axis=1).reshape(-1)
    return (off.reshape(-1).astype(jnp.int32), d_lo.reshape(-1), d_hi.reshape(-1), d_exp,
            start.reshape(-1).astype(jnp.int32), end.reshape(-1).astype(jnp.int32),
            perm, p_exp_sorted, p_fill.reshape(-1)[perm])


def _pick(n, pref):
    t = min(n, pref)
    while n % t:
        t //= 2
    return t


def kernel(x, c, positions, ada_w, ada_b, mix_norm_g, ffn_norm_g, pool_w_in, pool_w_grp,
           pool_scale, pool_w_out, kv_ada_w, kv_ada_b, kv_norm_g, w_dkv, ckv_norm_g, w_kr,
           w_uk, w_uv, w_dq, q_norm_g, w_uq, w_o, ffn_w_gate, ffn_w_up, ffn_w_down,
           router_w, moe_w_gate, moe_w_up, moe_w_down, final_norm_g):
    b, s, d = x.shape
    n = b * s
    depth = ada_w.shape[0]
    assert depth == 2 and pool_w_in.shape[0] == 1 and w_dq.shape[0] == 1
    assert b <= 8 and d % LANES == 0

    ts = _pick(s, 512)
    tp = _pick(s, 1024)
    ta = _pick(s, 1024)
    tc = _pick(s, 512)
    tm = _pick(s, 512)
    tf = _pick(ffn_w_gate.shape[-1], 512)

    c_pad = jnp.zeros((8, d), F32).at[:b].set(c)
    mods = _adaln(c_pad, ada_w, ada_b[:, None, :], _pick(6 * d, 1536))
    mods = mods[:, :b].reshape(depth, b, 6, d)
    kvmod = _adaln(c_pad, kv_ada_w[None], kv_ada_b[None, None, :], _pick(2 * d, 1024))
    kvmod = kvmod[0, :b].reshape(b, 2, d)

    r = jnp.arange(min(tp, POOL_ROW_BLOCK))[:, None]
    cidx = jnp.arange(min(tp, POOL_ROW_BLOCK) + POOL_HALO)[None, :] - POOL_HALO
    band = jnp.stack([((cidx <= r) & (cidx > r - w)) for w in POOL_WINDOWS]).astype(BF16)
    inv_freq = ROPE_THETA ** (-jnp.arange(0, D_ROPE, 2, dtype=F32) / D_ROPE)
    x1, fwg, fwu, fwd, rope_cos, rope_sin, h1 = _pool_mixer(
        x, mods[0], mix_norm_g[0:1], pool_w_in[0].astype(BF16), band, pool_w_grp[0].astype(BF16),
        pool_scale[0:1], pool_w_out[0].astype(BF16), ffn_w_gate, ffn_w_up, ffn_w_down,
        positions.astype(F32)[:, None, :], inv_freq[:, None], ffn_norm_g[0:1], tp)

    x2 = _dense_ffn(x1.reshape(n, d), h1.reshape(n, d), mods[0], fwg, fwu, fwd, tm, tf, s)


    q_lora = w_dq.shape[-1]
    wuq = w_uq[0].reshape(q_lora, N_HEADS, D_NOPE + D_ROPE)
    wqn_t = wuq[:, :, :D_NOPE].reshape(q_lora, N_HEADS * D_NOPE).T.astype(BF16)
    wqr_t = wuq[:, :, D_NOPE:].reshape(q_lora, N_HEADS * D_ROPE).T.astype(BF16)
    qt, k, vt = _qkv_proj(
        x2.reshape(b, s, d), mods[1], kvmod, mix_norm_g[1:2], kv_norm_g[None, :],
        w_dq[0].astype(BF16), q_norm_g[0:1], wqn_t, wqr_t, w_dkv.astype(BF16),
        ckv_norm_g[None, :], jnp.pad(w_kr, ((0, 0), (0, LANES - D_ROPE))).astype(BF16),
        w_uk.astype(BF16), w_uv.T.astype(BF16), rope_cos, rope_sin, ts)
    o, ewg, ewu, ewd = _attention(qt, k, vt, moe_w_gate[0], moe_w_up[0], moe_w_down[0], ts)

    n_exp = router_w.shape[-1]
    sub = _pick(s, ROUTE_SUB)
    n_sub = s // sub
    rcap = 2 * s + n_exp * PIECE
    ltri = (jnp.arange(ta)[None, :] < jnp.arange(ta)[:, None]).astype(BF16)
    assert n_exp <= LANES
    wr = jnp.pad(router_w[0], ((0, 0), (0, LANES - n_exp)))
    wr_hi = wr.astype(BF16)
    wr_split = jnp.concatenate([wr_hi, (wr - wr_hi.astype(F32)).astype(BF16)], axis=1)
    x3, h3, rcol, rrow, cnt = _attn_out(o.reshape(n, N_HEADS * D_V), x2, mods[1],
                                        w_o[0].astype(BF16), ffn_norm_g[1:2], wr_split, n_exp,
                                        ltri, ta, sub, s)
    cnt = cnt[:, :ta // sub, :n_exp].reshape(b, n_sub, n_exp)
    off, d_lo, d_hi, d_exp, c_lo, c_hi, perm, p_exp, p_fill = _routing_tables(cnt, b, n_sub, n_exp, rcap)
    xg, gs = _dispatch(off, d_lo, d_hi, d_exp, h3, rrow, b, s, rcap, n_exp, sub)
    yg = _expert_ffn(perm, p_exp, p_fill, xg, gs, ewg, ewu, ewd, tf)
    out = _combine(off, c_lo, c_hi, yg, rcol, x3, mods[1], final_norm_g[None, :], b, s, rcap,
                   n_exp, sub, tc)
    return out.reshape(b, s, d)
```

```python
import functools
import math

import jax
import jax.numpy as jnp
from jax import lax
from jax.experimental import pallas as pl
from jax.experimental.pallas import tpu as pltpu

F32 = jnp.float32
BF16 = jnp.bfloat16

N_HEADS = 8
D_NOPE = 128
D_ROPE = 64
D_V = 128
HEAD_PAD = 256
ROPE_HALF = D_ROPE // 2
ROPE_THETA = 10000.0
ATTN_SCALE = 1.0 / math.sqrt(D_NOPE + D_ROPE)
Q_SCALE = ATTN_SCALE * math.log2(math.e)
ATTN_QUERY_TILE = 512
ATTN_KEY_BLOCK = 512
ATTN_LOOKAHEAD = 3
POOL_WINDOWS = (2, 4, 8, 16)
POOL_HALO = 16
POOL_ROW_BLOCK = 128
EPS = 1e-6

ROUTE_SUB = 256
DISPATCH_CHUNK = 128
DISPATCH_WINDOW = 3
DISPATCH_STEP_ROWS = 2048
COMBINE_WINDOW = 256
PIECE = 512
PIECE_QUANTUM = 128
ROUTE_FIELDS = 8

LANES = 128
VMEM_LIMIT_BYTES = 56 * 1024 * 1024


def _cparams(sem):
    return pltpu.CompilerParams(dimension_semantics=sem, vmem_limit_bytes=VMEM_LIMIT_BYTES)


def _dot(a, b):
    return jnp.dot(a, b, preferred_element_type=F32)


def _rms_scale(x):
    return x * lax.rsqrt(jnp.mean(x * x, axis=-1, keepdims=True) + EPS)


def _silu(x):
    return x * jax.nn.sigmoid(x)


def _adaln_kernel(c_ref, w_ref, b_ref, o_ref):
    sc = _silu(c_ref[...])
    o_ref[0] = jnp.dot(sc, w_ref[0], preferred_element_type=F32,
                       precision=lax.Precision.HIGHEST) + b_ref[0]


def _adaln(c_pad, w, b, tn):
    nl, d, m = w.shape
    return pl.pallas_call(
        _adaln_kernel,
        grid=(nl, m // tn),
        in_specs=[
            pl.BlockSpec((8, d), lambda l, j: (0, 0)),
            pl.BlockSpec((1, d, tn), lambda l, j: (l, 0, j)),
            pl.BlockSpec((1, 1, tn), lambda l, j: (l, 0, j)),
        ],
        out_specs=pl.BlockSpec((1, 8, tn), lambda l, j: (l, 0, j)),
        out_shape=jax.ShapeDtypeStruct((nl, 8, m), F32),
        compiler_params=_cparams(("arbitrary", "arbitrary")),
        name="adaln",
    )(c_pad, w, b)


def _rope_table_side_job(pos_ref, freq_ref, cos_ref, sin_ref):
    ang = freq_ref[...] * pos_ref[0]
    cos_ref[0, 0] = jnp.cos(ang)
    sin_ref[0, 0] = jnp.sin(ang)


def _rope_t(x, cos_t, sin_t):
    x1, x2 = x[:ROPE_HALF], x[ROPE_HALF:]
    return x1 * cos_t - x2 * sin_t, x2 * cos_t + x1 * sin_t


def _cast_specs(n_exp, d, ff, steps, step_of):
    assert steps >= n_exp, "too few grid steps to carry the weight cast"
    k = max(c for c in range(1, steps // n_exp + 1) if ff % (c * LANES) == 0)
    cw = ff // k

    def chunk(*idx):
        c = jnp.minimum(step_of(*idx), n_exp * k - 1)
        return c // k, c % k

    gate_up = pl.BlockSpec((1, d, cw), lambda *idx: (chunk(*idx)[0], 0, chunk(*idx)[1]))
    down = pl.BlockSpec((1, cw, d), lambda *idx: (chunk(*idx)[0], chunk(*idx)[1], 0))
    return gate_up, down


def _cast_side_job(src_refs, dst_refs):
    for src, dst in zip(src_refs, dst_refs):
        dst[...] = src[...].astype(BF16)


def _pool_kernel(x_ref, mod_ref, g_ref, win_ref, band_ref, wgrp_ref, scale_ref, wout_ref,
                 fg_ref, fu_ref, fd_ref, pos_ref, freq_ref, gffn_ref,
                 o_ref, fg_out_ref, fu_out_ref, fd_out_ref, cos_ref, sin_ref, h_out_ref, uext_ref,
                 *, ts, gw):
    s = pl.program_id(1)

    @pl.when(s == 0)
    def _():
        uext_ref[0:POOL_HALO, :] = jnp.zeros((POOL_HALO, uext_ref.shape[1]), BF16)

    x = x_ref[0]
    mod = mod_ref[0]
    h = (_rms_scale(x) * g_ref[...]) * (1.0 + mod[1:2]) + mod[0:1]
    u = _dot(h.astype(BF16), win_ref[...])
    uext_ref[POOL_HALO:POOL_HALO + ts, :] = u.astype(BF16)

    t = s * ts + lax.broadcasted_iota(jnp.int32, (ts, 1), 0)
    zs = []
    for g, w in enumerate(POOL_WINDOWS):
        cols = slice(g * gw, (g + 1) * gw)
        rb = band_ref.shape[1]
        wsum = jnp.concatenate(
            [_dot(band_ref[g], uext_ref[r0:r0 + rb + POOL_HALO, cols]) for r0 in range(0, ts, rb)],
            axis=0)
        cnt = jnp.minimum(t + 1, w).astype(F32)
        p = wsum / cnt - u[:, cols]
        zs.append(_dot(p.astype(BF16), wgrp_ref[g]))
    z = jnp.concatenate(zs, axis=-1) * scale_ref[...]
    y = _dot(z.astype(BF16), wout_ref[...])
    x_out = x + mod[2:3] * y
    o_ref[0] = x_out
    h_out_ref[0] = ((_rms_scale(x_out) * gffn_ref[...]) * (1.0 + mod[4:5]) + mod[3:4]).astype(BF16)

    uext_ref[0:POOL_HALO, :] = uext_ref[ts:ts + POOL_HALO, :]

    _cast_side_job((fg_ref, fu_ref, fd_ref), (fg_out_ref, fu_out_ref, fd_out_ref))
    _rope_table_side_job(pos_ref, freq_ref, cos_ref, sin_ref)


def _pool_mixer(x, mod, g, w_in, band, w_grp, scale, w_out, fg, fu, fd, pos_f, freq_row, g_ffn,
                ts):
    b, s, d = x.shape
    ng, gw, _ = w_grp.shape
    n_s = s // ts
    gate_up, down = _cast_specs(1, d, fg.shape[2], b * n_s, lambda i, j: i * n_s + j)
    const2 = lambda i, j: (0, 0)
    const3 = lambda i, j: (0, 0, 0)
    return pl.pallas_call(
        functools.partial(_pool_kernel, ts=ts, gw=gw),
        grid=(b, s // ts),
        in_specs=[
            pl.BlockSpec((1, ts, d), lambda i, j: (i, j, 0)),
            pl.BlockSpec((1, 6, d), lambda i, j: (i, 0, 0)),
            pl.BlockSpec((1, d), const2),
            pl.BlockSpec((d, d), const2),
            pl.BlockSpec(band.shape, const3),
            pl.BlockSpec((ng, gw, gw), const3),
            pl.BlockSpec((1, d), const2),
            pl.BlockSpec((d, d), const2),
            gate_up, gate_up, down,
            pl.BlockSpec((1, 1, ts), lambda i, j: (i, 0, j)),
            pl.BlockSpec((ROPE_HALF, 1), const2),
            pl.BlockSpec((1, d), const2),
        ],
        out_specs=[pl.BlockSpec((1, ts, d), lambda i, j: (i, j, 0)), gate_up, gate_up, down,
                   pl.BlockSpec((1, 1, ROPE_HALF, ts), lambda i, j: (i, j, 0, 0)),
                   pl.BlockSpec((1, 1, ROPE_HALF, ts), lambda i, j: (i, j, 0, 0)),
                   pl.BlockSpec((1, ts, d), lambda i, j: (i, j, 0))],
        out_shape=[jax.ShapeDtypeStruct((b, s, d), F32),
                   jax.ShapeDtypeStruct(fg.shape, BF16),
                   jax.ShapeDtypeStruct(fu.shape, BF16),
                   jax.ShapeDtypeStruct(fd.shape, BF16),
                   jax.ShapeDtypeStruct((b, s // ts, ROPE_HALF, ts), F32),
                   jax.ShapeDtypeStruct((b, s // ts, ROPE_HALF, ts), F32),
                   jax.ShapeDtypeStruct((b, s, d), BF16)],
        scratch_shapes=[pltpu.VMEM((POOL_HALO + ts, d), BF16)],
        compiler_params=_cparams(("arbitrary", "arbitrary")),
        name="pool_mixer",
    )(x, mod, g, w_in, band, w_grp, scale, w_out, fg, fu, fd, pos_f, freq_row, g_ffn)


def _ffn_kernel(x_ref, h_ref, mod_ref, wg_ref, wu_ref, wd_ref, o_ref, *, tf):
    h = h_ref[...]
    ff = wg_ref.shape[2]
    acc = jnp.zeros(x_ref.shape, F32)
    for f0 in range(0, ff, tf):
        gate = _dot(h, wg_ref[0, :, f0:f0 + tf])
        up = _dot(h, wu_ref[0, :, f0:f0 + tf])
        a = (_silu(gate) * up).astype(BF16)
        acc = acc + _dot(a, wd_ref[0, f0:f0 + tf, :])
    o_ref[...] = x_ref[...] + mod_ref[0][5:6] * acc


def _dense_ffn(x2, h2, mod, wg, wu, wd, tm, tf, rows_per_batch):
    n, d = x2.shape
    tpb = rows_per_batch // tm
    resident = lambda a: pl.BlockSpec(a.shape, lambda i: (0, 0, 0), pipeline_mode=pl.Buffered(1))
    return pl.pallas_call(
        functools.partial(_ffn_kernel, tf=tf),
        grid=(n // tm,),
        in_specs=[
            pl.BlockSpec((tm, d), lambda i: (i, 0)),
            pl.BlockSpec((tm, d), lambda i: (i, 0)),
            pl.BlockSpec((1, 6, d), lambda i: (i // tpb, 0, 0)),
            resident(wg), resident(wu), resident(wd),
        ],
        out_specs=pl.BlockSpec((tm, d), lambda i: (i, 0)),
        out_shape=jax.ShapeDtypeStruct((n, d), F32),
        compiler_params=_cparams(("arbitrary",)),
        name="dense_ffn",
    )(x2, h2, mod, wg, wu, wd)


def _dot_nt(a, b):
    return lax.dot_general(a, b, (((1,), (1,)), ((), ())), preferred_element_type=F32)


def _qkv_kernel(x_ref, mod_ref, kvmod_ref, gmix_ref, gkv_ref, wdq_ref, qg_ref, wqn_t_ref,
                wqr_t_ref, wdkv_ref, ckvg_ref, wkr_ref, wuk_ref, wuv_t_ref, cos_ref, sin_ref,
                qt_ref, k_ref, vt_ref):
    x = x_ref[0]
    xn = _rms_scale(x)
    mod = mod_ref[0]
    kvmod = kvmod_ref[0]
    hq = ((xn * gmix_ref[...]) * (1.0 + mod[1:2]) + mod[0:1]).astype(BF16)
    hk = ((xn * gkv_ref[...]) * (1.0 + kvmod[1:2]) + kvmod[0:1]).astype(BF16)

    cq = (_rms_scale(_dot(hq, wdq_ref[...])) * qg_ref[...]).astype(BF16)
    qn_t = _dot_nt(wqn_t_ref[...], cq) * Q_SCALE
    qr_t = _dot_nt(wqr_t_ref[...], cq) * Q_SCALE
    cos_t = cos_ref[0, 0]
    sin_t = sin_ref[0, 0]
    zero_rows = jnp.zeros((HEAD_PAD - LANES - D_ROPE, x.shape[0]), BF16)

    ckv = (_rms_scale(_dot(hk, wdkv_ref[...])) * ckvg_ref[...]).astype(BF16)
    kn = _dot(ckv, wuk_ref[...])
    vt_ref[0, 0] = _dot_nt(wuv_t_ref[...], ckv).astype(BF16)
    kr_raw_t = _dot(hk, wkr_ref[...]).T
    k1, k2 = _rope_t(kr_raw_t[:D_ROPE], cos_t, sin_t)
    kr_t = jnp.concatenate([k1, k2, kr_raw_t[D_ROPE:]], axis=0)
    kr = kr_t.T.astype(BF16)

    for h in range(N_HEADS):
        lo = h * HEAD_PAD
        hs = slice(h * LANES, (h + 1) * LANES)
        qt_ref[0, 0, lo:lo + LANES, :] = qn_t[hs].astype(BF16)
        q1, q2 = _rope_t(qr_t[h * D_ROPE:(h + 1) * D_ROPE], cos_t, sin_t)
        ro = lo + LANES
        qt_ref[0, 0, ro:ro + ROPE_HALF, :] = q1.astype(BF16)
        qt_ref[0, 0, ro + ROPE_HALF:ro + D_ROPE, :] = q2.astype(BF16)
        qt_ref[0, 0, ro + D_ROPE:lo + HEAD_PAD, :] = zero_rows
        k_ref[0, :, lo:lo + LANES] = kn[:, hs].astype(BF16)
        k_ref[0, :, lo + LANES:lo + HEAD_PAD] = kr


def _qkv_proj(x, mod, kvmod, gmix, gkv, wdq, qg, wqn_t, wqr_t, wdkv, ckvg, wkr, wuk, wuv_t,
              cos_t, sin_t, ts):
    b, s, d = x.shape
    row = lambda i, j: (i, j, 0)
    tile = lambda i, j: (i, j, 0, 0)
    per_b = lambda i, j: (i, 0, 0)

    def full(a):
        return pl.BlockSpec(a.shape, lambda i, j: (0,) * a.ndim)

    weights = (gmix, gkv, wdq, qg, wqn_t, wqr_t, wdkv, ckvg, wkr, wuk, wuv_t)
    return pl.pallas_call(
        _qkv_kernel,
        grid=(b, s // ts),
        in_specs=[pl.BlockSpec((1, ts, d), row),
                  pl.BlockSpec((1, 6, d), per_b),
                  pl.BlockSpec((1, 2, d), per_b)]
                 + [full(a) for a in weights]
                 + [pl.BlockSpec((1, 1, ROPE_HALF, ts),
                                 lambda i, j: (i, j * ts // cos_t.shape[3], 0,
                                               j % (cos_t.shape[3] // ts)))] * 2,
        out_specs=[pl.BlockSpec((1, 1, N_HEADS * HEAD_PAD, ts), tile),
                   pl.BlockSpec((1, ts, N_HEADS * HEAD_PAD), row),
                   pl.BlockSpec((1, 1, N_HEADS * D_V, ts), tile)],
        out_shape=[jax.ShapeDtypeStruct((b, s // ts, N_HEADS * HEAD_PAD, ts), BF16),
                   jax.ShapeDtypeStruct((b, s, N_HEADS * HEAD_PAD), BF16),
                   jax.ShapeDtypeStruct((b, s // ts, N_HEADS * D_V, ts), BF16)],
        compiler_params=_cparams(("arbitrary", "arbitrary")),
        name="qkv_proj",
    )(x, mod, kvmod, *weights, cos_t, sin_t)


def _attn_kernel(qt_ref, k_ref, vt_ref, eg_ref, eu_ref, ed_ref,
                 o_ref, eg_out_ref, eu_out_ref, ed_out_ref, *, tq, tk):
    ts = qt_ref.shape[3]
    s = k_ref.shape[1]
    key = lax.broadcasted_iota(jnp.int32, (tk, tq), 0)
    qry = lax.broadcasted_iota(jnp.int32, (tk, tq), 1)
    last_block = lambda qi: ((qi + 1) * tq - 1) // tk
    blocks = [(qi, j) for qi in range(s // tq) for j in range(last_block(qi) + 1)]

    def tile(ref, start, size):
        return ref[0, start // ts, :, start % ts:start % ts + size]

    def scores_t(qi, j):
        st = _dot(k_ref[0, j * tk:(j + 1) * tk, :], tile(qt_ref, qi * tq, tq))
        if (j + 1) * tk - 1 > qi * tq:
            st = jnp.where(key + (j * tk - qi * tq) <= qry, st, -jnp.inf)
        return st

    ahead = [scores_t(*blk) for blk in blocks[:ATTN_LOOKAHEAD]]
    for n, (qi, j) in enumerate(blocks):
        st = ahead.pop(0)
        if n + ATTN_LOOKAHEAD < len(blocks):
            ahead.append(scores_t(*blocks[n + ATTN_LOOKAHEAD]))
        if j == 0:
            m = jnp.full((1, tq), -jnp.inf, F32)
            l = jnp.zeros((1, tq), F32)
            acc = jnp.zeros((D_V, tq), F32)
        m_new = jnp.maximum(m, jnp.max(st, axis=0, keepdims=True))
        alpha = jnp.exp2(m - m_new)
        pt = jnp.exp2(st - m_new)
        l = alpha * l + jnp.sum(pt, axis=0, keepdims=True)
        acc = alpha * acc + _dot(tile(vt_ref, j * tk, tk), pt.astype(BF16))
        m = m_new
        if j == last_block(qi):
            o_ref[0, qi * tq:(qi + 1) * tq, :] = (acc / l).T.astype(BF16)

    _cast_side_job((eg_ref, eu_ref, ed_ref), (eg_out_ref, eu_out_ref, ed_out_ref))


def _attention(qt, k, vt, eg, eu, ed, ts):
    b, s, _ = k.shape
    n_exp, d, eff = eg.shape
    gate_up, down = _cast_specs(n_exp, d, eff, b * N_HEADS, lambda i, h: i * N_HEADS + h)
    return pl.pallas_call(
        functools.partial(_attn_kernel, tq=min(ts, ATTN_QUERY_TILE), tk=min(ts, ATTN_KEY_BLOCK)),
        grid=(b, N_HEADS),
        in_specs=[
            pl.BlockSpec((1, s // ts, HEAD_PAD, ts), lambda i, h: (i, 0, h, 0)),
            pl.BlockSpec((1, s, HEAD_PAD), lambda i, h: (i, 0, h)),
            pl.BlockSpec((1, s // ts, D_V, ts), lambda i, h: (i, 0, h, 0)),
            gate_up, gate_up, down,
        ],
        out_specs=[pl.BlockSpec((1, s, D_V), lambda i, h: (i, 0, h)), gate_up, gate_up, down],
        out_shape=[jax.ShapeDtypeStruct((b, s, N_HEADS * D_V), BF16),
                   jax.ShapeDtypeStruct(eg.shape, BF16),
                   jax.ShapeDtypeStruct(eu.shape, BF16),
                   jax.ShapeDtypeStruct(ed.shape, BF16)],
        compiler_params=_cparams(("arbitrary", "arbitrary")),
        name="flash_attn",
    )(qt, k, vt, eg, eu, ed)


def _attn_out_kernel(o_ref, x_ref, mod_ref, wo_ref, g_ref, wr_ref, ltri_ref,
                     x_out_ref, h_out_ref, rcol_ref, rrow_ref, cnt_ref, carry_ref,
                     *, n_exp, tiles_per_block):
    i = pl.program_id(0)

    @pl.when(i % tiles_per_block == 0)
    def _():
        carry_ref[...] = jnp.zeros_like(carry_ref)

    mod = mod_ref[0]
    x = x_ref[...] + mod[2:3] * _dot(o_ref[...], wo_ref[...])
    x_out_ref[...] = x
    h = (_rms_scale(x) * g_ref[...]) * (1.0 + mod[4:5]) + mod[3:4]
    h_hi = h.astype(BF16)
    h_out_ref[...] = h_hi

    tm = x.shape[0]
    h_lo = (h - h_hi.astype(F32)).astype(BF16)
    parts = _dot(jnp.concatenate([h_hi, h_lo], axis=0), wr_ref[...])
    logits = (parts[:tm, :LANES] + parts[:tm, LANES:]) + (parts[tm:, :LANES] + parts[tm:, LANES:])
    lane = lax.broadcasted_iota(jnp.int32, (tm, LANES), 1)
    lane_f = lane.astype(F32)

    def first_argmax(vals):
        m = jnp.max(vals, axis=-1, keepdims=True)
        idx = jnp.min(jnp.where(vals == m, lane_f, float(LANES)), axis=-1, keepdims=True)
        return m, idx.astype(jnp.int32)

    logits = jnp.where(lane < n_exp, logits, -jnp.inf)
    m1, i1 = first_argmax(logits)
    m2, i2 = first_argmax(jnp.where(lane == i1, -jnp.inf, logits))
    ex = jnp.exp(m2 - m1)
    w1 = 1.0 / (1.0 + ex)
    w2 = ex / (1.0 + ex)

    mask = jnp.where((lane == i1) | (lane == i2), 1.0, 0.0)
    ranks = _dot(ltri_ref[...], mask.astype(BF16)) + carry_ref[0:1]
    incl = ranks + mask
    rank1 = jnp.sum(jnp.where(lane == i1, ranks, 0.0), axis=-1, keepdims=True)
    rank2 = jnp.sum(jnp.where(lane == i2, ranks, 0.0), axis=-1, keepdims=True)
    carry_ref[0:1] = incl[tm - 1:tm]
    n_sub = tm // rrow_ref.shape[2]
    sub = rrow_ref.shape[2]
    cnt_ref[0] = jnp.zeros(cnt_ref.shape[1:], F32)
    for j in range(n_sub):
        cnt_ref[0, j:j + 1] = incl[(j + 1) * sub - 1:(j + 1) * sub]

    fields = (i1.astype(F32), i2.astype(F32), rank1, rank2, w1, w2)
    rcol = jnp.zeros((tm, LANES), F32)
    for k, val in enumerate(fields):
        rcol = jnp.where(lane == k, val, rcol)
    rcol_ref[...] = rcol
    rrow = rcol.T
    for j in range(n_sub):
        rrow_ref[j] = rrow[0:ROUTE_FIELDS, j * sub:(j + 1) * sub]


def _attn_out(o2, x2, mod, wo, g, wr, n_exp, ltri, tm, sub, rows_per_block):
    n, d = x2.shape
    tpb = rows_per_block // tm
    n_sub = tm // sub
    return pl.pallas_call(
        functools.partial(_attn_out_kernel, n_exp=n_exp, tiles_per_block=tpb),
        grid=(n // tm,),
        in_specs=[
            pl.BlockSpec((tm, o2.shape[1]), lambda i: (i, 0)),
            pl.BlockSpec((tm, d), lambda i: (i, 0)),
            pl.BlockSpec((1, 6, d), lambda i: (i // tpb, 0, 0)),
            pl.BlockSpec(wo.shape, lambda i: (0, 0)),
            pl.BlockSpec((1, d), lambda i: (0, 0)),
            pl.BlockSpec(wr.shape, lambda i: (0, 0)),
            pl.BlockSpec((tm, tm), lambda i: (0, 0)),
        ],
        out_specs=[pl.BlockSpec((tm, d), lambda i: (i, 0)),
                   pl.BlockSpec((tm, d), lambda i: (i, 0)),
                   pl.BlockSpec((tm, LANES), lambda i: (i, 0)),
                   pl.BlockSpec((n_sub, ROUTE_FIELDS, sub), lambda i: (i, 0, 0)),
                   pl.BlockSpec((1, 8, LANES), lambda i: (i, 0, 0))],
        out_shape=[jax.ShapeDtypeStruct((n, d), F32),
                   jax.ShapeDtypeStruct((n, d), BF16),
                   jax.ShapeDtypeStruct((n, LANES), F32),
                   jax.ShapeDtypeStruct((n // sub, ROUTE_FIELDS, sub), F32),
                   jax.ShapeDtypeStruct((n // tm, 8, LANES), F32)],
        scratch_shapes=[pltpu.VMEM((8, LANES), F32)],
        compiler_params=_cparams(("arbitrary",)),
        name="attn_out_router",
    )(o2, x2, mod, wo, g, wr, ltri)


def _dispatch_kernel(off_ref, lo_ref, hi_ref, cexp_ref, h_ref, rrow_ref, xg_ref, gs_ref,
                     *, n_exp, sub, chunks_per_step, chunks_per_block):
    blk = pl.program_id(0)
    g = pl.program_id(1)
    d = h_ref.shape[1]
    n_sub = rrow_ref.shape[0]
    win = min(DISPATCH_WINDOW, n_sub)
    iota = lax.broadcasted_iota(jnp.int32, (DISPATCH_CHUNK, 1), 0)

    def contribution(info, sb):
        row, exp_id, exp_off = info
        r = rrow_ref[sb]
        first = r[0:1] == exp_id
        second = r[1:2] == exp_id
        pos = jnp.where(first, r[2:3] + exp_off, jnp.where(second, r[3:4] + exp_off, -1.0))
        hit = pos == row
        onehot = jnp.where(hit, 1.0, 0.0).astype(BF16)
        tok = h_ref[pl.ds(pl.multiple_of(sb * sub, sub), sub), :]
        gate = jnp.sum(jnp.where(hit, jnp.where(first, r[4:5], r[5:6]), 0.0),
                       axis=-1, keepdims=True)
        return _dot(onehot, tok), gate

    def chunk_info(j):
        c = g * chunks_per_step + j
        t = blk * chunks_per_block + c
        row = (c * DISPATCH_CHUNK + iota).astype(F32)
        exp_id = cexp_ref[t]
        exp_off = off_ref[blk * n_exp + exp_id].astype(F32)
        start = jnp.minimum(lo_ref[t], n_sub - win)
        return (row, exp_id.astype(F32), exp_off), start, hi_ref[t]

    first_chunk = blk * chunks_per_block + g * chunks_per_step
    used = functools.reduce(jnp.logical_or,
                            [hi_ref[first_chunk + j] >= 0 for j in range(chunks_per_step)])

    @pl.when(jnp.logical_not(used))
    def _():
        xg_ref[...] = jnp.zeros(xg_ref.shape, BF16)
        gs_ref[...] = jnp.zeros(gs_ref.shape, F32)

    @pl.when(used)
    def _():
        for j in range(chunks_per_step):
            info, start, _ = chunk_info(j)
            acc = jnp.zeros((DISPATCH_CHUNK, d), F32)
            gate = jnp.zeros((DISPATCH_CHUNK, 1), F32)
            for k in range(win):
                a, b = contribution(info, start + k)
                acc += a
                gate += b
            rows = slice(j * DISPATCH_CHUNK, (j + 1) * DISPATCH_CHUNK)
            xg_ref[rows, :] = acc.astype(BF16)
            gs_ref[rows, :] = gate

        def extra(j, _):
            info, start, last = chunk_info(j)
            rows = pl.ds(pl.multiple_of(j * DISPATCH_CHUNK, DISPATCH_CHUNK), DISPATCH_CHUNK)

            def add(sb, _):
                a, b = contribution(info, sb)
                xg_ref[rows, :] = (xg_ref[rows, :].astype(F32) + a).astype(BF16)
                gs_ref[rows, :] = gs_ref[rows, :] + b
                return 0

            return lax.fori_loop(start + win, last + 1, add, 0)

        lax.fori_loop(0, chunks_per_step, extra, 0)


def _dispatch(off, lo, hi, cexp, h, rrow, n_blocks, rows_per_block, rcap, n_exp, sub):
    n, d = h.shape
    n_sub = rows_per_block // sub
    step_rows = _pick(rcap, DISPATCH_STEP_ROWS)
    cps = step_rows // DISPATCH_CHUNK
    cpb = rcap // DISPATCH_CHUNK
    grid_spec = pltpu.PrefetchScalarGridSpec(
        num_scalar_prefetch=4,
        grid=(n_blocks, rcap // step_rows),
        in_specs=[
            pl.BlockSpec((rows_per_block, d), lambda b, g, *_: (b, 0)),
            pl.BlockSpec((n_sub, ROUTE_FIELDS, sub), lambda b, g, *_: (b, 0, 0)),
        ],
        out_specs=[
            pl.BlockSpec((step_rows, d), lambda b, g, *_: (b * (rcap // step_rows) + g, 0)),
            pl.BlockSpec((step_rows, 1), lambda b, g, *_: (b * (rcap // step_rows) + g, 0)),
        ],
    )
    return pl.pallas_call(
        functools.partial(_dispatch_kernel, n_exp=n_exp, sub=sub, chunks_per_step=cps,
                          chunks_per_block=cpb),
        grid_spec=grid_spec,
        out_shape=[jax.ShapeDtypeStruct((n_blocks * rcap, d), BF16),
                   jax.ShapeDtypeStruct((n_blocks * rcap, 1), F32)],
        compiler_params=_cparams(("arbitrary", "arbitrary")),
        name="moe_dispatch",
    )(off, lo, hi, cexp, h, rrow)


def _expert_ffn_kernel(perm_ref, pexp_ref, pfill_ref, x_ref, gs_ref, wg_ref, wu_ref, wd_ref,
                       y_ref, *, tf):
    w = pl.program_id(0)
    fill = pfill_ref[w]
    ff = wg_ref.shape[2]

    def ffn(n_rows):
        x = x_ref[0:n_rows, :]
        gs = gs_ref[0:n_rows, :]
        acc = jnp.zeros((n_rows, y_ref.shape[1]), F32)
        for f0 in range(0, ff, tf):
            gate = _dot(x, wg_ref[0, :, f0:f0 + tf])
            up = _dot(x, wu_ref[0, :, f0:f0 + tf])
            a = ((_silu(gate) * up) * gs).astype(BF16)
            acc = acc + _dot(a, wd_ref[0, f0:f0 + tf, :])
        return acc.astype(BF16)

    for level in range(PIECE // PIECE_QUANTUM + 1):
        n_rows = level * PIECE_QUANTUM

        @pl.when(fill == level)
        def _(n_rows=n_rows):
            if n_rows:
                y_ref[0:n_rows, :] = ffn(n_rows)
            if n_rows < PIECE:
                y_ref[n_rows:PIECE, :] = jnp.zeros((PIECE - n_rows, y_ref.shape[1]), BF16)


def _expert_ffn(perm, pexp, pfill, xg, gs, wg, wu, wd, tf):
    rows, d = xg.shape
    n_exp, _, ff = wg.shape
    grid_spec = pltpu.PrefetchScalarGridSpec(
        num_scalar_prefetch=3,
        grid=(rows // PIECE,),
        in_specs=[
            pl.BlockSpec((PIECE, d), lambda w, perm, pexp, pfill: (perm[w], 0)),
            pl.BlockSpec((PIECE, 1), lambda w, perm, pexp, pfill: (perm[w], 0)),
            pl.BlockSpec((1, d, ff), lambda w, perm, pexp, pfill: (pexp[w], 0, 0)),
            pl.BlockSpec((1, d, ff), lambda w, perm, pexp, pfill: (pexp[w], 0, 0)),
            pl.BlockSpec((1, ff, d), lambda w, perm, pexp, pfill: (pexp[w], 0, 0)),
        ],
        out_specs=pl.BlockSpec((PIECE, d), lambda w, perm, pexp, pfill: (perm[w], 0)),
    )
    return pl.pallas_call(
        functools.partial(_expert_ffn_kernel, tf=tf),
        grid_spec=grid_spec,
        out_shape=jax.ShapeDtypeStruct((rows, d), BF16),
        compiler_params=_cparams(("arbitrary",)),
        name="expert_ffn",
    )(perm, pexp, pfill, xg, gs, wg, wu, wd)


def _combine_kernel(off_ref, start_ref, end_ref, yg_ref, rcol_ref, x_ref, mod_ref, gfin_ref, o_ref,
                    acc_ref, *, n_exp, sub, subs_per_block):
    blk = pl.program_id(0)
    t = pl.program_id(1)
    tm, d = x_ref.shape
    col = lax.broadcasted_iota(jnp.int32, (1, COMBINE_WINDOW), 1)
    align = COMBINE_WINDOW // 2

    def token_info(s):
        rc = rcol_ref[s * sub:(s + 1) * sub, :]
        return tuple(jnp.broadcast_to(rc[:, k:k + 1], (sub, COMBINE_WINDOW)) for k in range(4))

    def gathered(info, e, w0):
        i1, i2, rank1, rank2 = info
        rank = jnp.where(i1 == e, rank1, jnp.where(i2 == e, rank2, -float(COMBINE_WINDOW)))
        in_region = (w0 - off_ref[blk * n_exp + e] + col).astype(F32)
        onehot = jnp.where(rank == in_region, 1.0, 0.0).astype(BF16)
        return _dot(onehot, yg_ref[pl.ds(pl.multiple_of(w0, align), COMBINE_WINDOW), :])

    def first_window(s, e):
        idx = (blk * subs_per_block + t * (tm // sub) + s) * n_exp + e
        w0 = lax.shift_right_logical(start_ref[idx], align.bit_length() - 1) * align
        return w0, end_ref[idx]

    for s in range(tm // sub):
        info = token_info(s)
        acc = jnp.zeros((sub, d), F32)
        for e in range(n_exp):
            acc += gathered(info, e, first_window(s, e)[0])
        acc_ref[s * sub:(s + 1) * sub, :] = acc

    for s in range(tm // sub):
        info = token_info(s)
        for e in range(n_exp):
            w0, end = first_window(s, e)
            n_more = lax.shift_right_logical(
                jnp.maximum(end - w0 - 1, 0), COMBINE_WINDOW.bit_length() - 1)

            def add(k, _, info=info, e=e, w0=w0, s=s):
                rows = slice(s * sub, (s + 1) * sub)
                acc_ref[rows, :] = acc_ref[rows, :] + gathered(info, e, w0 + k * COMBINE_WINDOW)
                return 0

            lax.fori_loop(1, n_more + 1, add, 0)

    x = x_ref[...] + mod_ref[0][5:6] * acc_ref[...]
    o_ref[...] = _rms_scale(x) * gfin_ref[...]


def _combine(off, lo, hi, yg, rcol, x2, mod, gfin, n_blocks, rows_per_block, rcap, n_exp, sub, tm):
    n, d = x2.shape
    tpb = rows_per_block // tm
    grid_spec = pltpu.PrefetchScalarGridSpec(
        num_scalar_prefetch=3,
        grid=(n_blocks, tpb),
        in_specs=[
            pl.BlockSpec((rcap, d), lambda b, t, *_: (b, 0), pipeline_mode=pl.Buffered(1)),
            pl.BlockSpec((tm, LANES), lambda b, t, *_: (b * tpb + t, 0)),
            pl.BlockSpec((tm, d), lambda b, t, *_: (b * tpb + t, 0)),
            pl.BlockSpec((1, 6, d), lambda b, t, *_: (b, 0, 0)),
            pl.BlockSpec((1, d), lambda b, t, *_: (0, 0)),
        ],
        out_specs=pl.BlockSpec((tm, d), lambda b, t, *_: (b * tpb + t, 0)),
        scratch_shapes=[pltpu.VMEM((tm, d), F32)],
    )
    return pl.pallas_call(
        functools.partial(_combine_kernel, n_exp=n_exp, sub=sub,
                          subs_per_block=rows_per_block // sub),
        grid_spec=grid_spec,
        out_shape=jax.ShapeDtypeStruct((n, d), F32),
        compiler_params=_cparams(("arbitrary", "arbitrary")),
        name="moe_combine",
    )(off, lo, hi, yg, rcol, x2, mod, gfin)


def _routing_tables(cnt, n_blocks, n_sub, n_exp, rcap):
    cum_incl = cnt.astype(jnp.int32)
    cum_excl = jnp.concatenate([jnp.zeros_like(cum_incl[:, :1]), cum_incl[:, :-1]], axis=1)
    total = cum_incl[:, -1]
    psz = (total + PIECE - 1) // PIECE * PIECE
    off = jnp.cumsum(psz, axis=1) - psz
    start = off[:, None, :] + cum_excl
    end = off[:, None, :] + cum_incl

    c0 = jnp.arange(rcap // DISPATCH_CHUNK, dtype=jnp.int32)[None, :, None, None] * DISPATCH_CHUNK
    hit = ((start[:, None] < c0 + DISPATCH_CHUNK) & (end[:, None] > c0)
           & (end[:, None] > start[:, None])).any(-1)
    sub_ids = jnp.arange(n_sub, dtype=jnp.int32)
    d_lo = jnp.min(jnp.where(hit, sub_ids, n_sub), axis=-1)
    d_hi = jnp.max(jnp.where(hit, sub_ids, -1), axis=-1)
    d_lo = jnp.where(d_hi < 0, 0, d_lo)


    ppb = rcap // PIECE
    prow = jnp.arange(ppb, dtype=jnp.int32)[None, :, None] * PIECE
    inside = (prow >= off[:, None, :]) & (prow < (off + psz)[:, None, :])
    p_exp = jnp.argmax(inside, axis=-1).astype(jnp.int32)
    used = inside.any(-1)
    left = jnp.take_along_axis(total + off, p_exp, axis=1) - prow[..., 0]
    p_fill = jnp.where(used, jnp.clip((left + PIECE_QUANTUM - 1) // PIECE_QUANTUM, 0,
                                      PIECE // PIECE_QUANTUM), 0).astype(jnp.int32)
    key = jnp.where(p_fill > 0, p_exp, n_exp).reshape(-1)
    perm = jnp.argsort(key, stable=True).astype(jnp.int32)
    p_exp_sorted = jnp.minimum(key[perm], n_exp - 1).astype(jnp.int32)
    d_exp = jnp.repeat(p_exp, PIECE // DISPATCH_CHUNK, axis=1).reshape(-1)
    return (off.reshape(-1).astype(jnp.int32), d_lo.reshape(-1), d_hi.reshape(-1), d_exp,
            start.reshape(-1).astype(jnp.int32), end.reshape(-1).astype(jnp.int32),
            perm, p_exp_sorted, p_fill.reshape(-1)[perm])


def _pick(n, pref):
    t = min(n, pref)
    while n % t:
        t //= 2
    return t


def kernel(x, c, positions, ada_w, ada_b, mix_norm_g, ffn_norm_g, pool_w_in, pool_w_grp,
           pool_scale, pool_w_out, kv_ada_w, kv_ada_b, kv_norm_g, w_dkv, ckv_norm_g, w_kr,
           w_uk, w_uv, w_dq, q_norm_g, w_uq, w_o, ffn_w_gate, ffn_w_up, ffn_w_down,
           router_w, moe_w_gate, moe_w_up, moe_w_down, final_norm_g):
    b, s, d = x.shape
    n = b * s
    depth = ada_w.shape[0]
    assert depth == 2 and pool_w_in.shape[0] == 1 and w_dq.shape[0] == 1
    assert b <= 8 and d % LANES == 0

    ts = _pick(s, 512)
    tp = _pick(s, 1024)
    ta = _pick(s, 1024)
    tc = _pick(s, 1024)
    tm = _pick(s, 512)
    tf = _pick(ffn_w_gate.shape[-1], 512)

    c_pad = jnp.zeros((8, d), F32).at[:b].set(c)
    mods = _adaln(c_pad, ada_w, ada_b[:, None, :], _pick(6 * d, 1536))
    mods = mods[:, :b].reshape(depth, b, 6, d)
    kvmod = _adaln(c_pad, kv_ada_w[None], kv_ada_b[None, None, :], _pick(2 * d, 1024))
    kvmod = kvmod[0, :b].reshape(b, 2, d)

    r = jnp.arange(min(tp, POOL_ROW_BLOCK))[:, None]
    cidx = jnp.arange(min(tp, POOL_ROW_BLOCK) + POOL_HALO)[None, :] - POOL_HALO
    band = jnp.stack([((cidx <= r) & (cidx > r - w)) for w in POOL_WINDOWS]).astype(BF16)
    inv_freq = ROPE_THETA ** (-jnp.arange(0, D_ROPE, 2, dtype=F32) / D_ROPE)
    x1, fwg, fwu, fwd, rope_cos, rope_sin, h1 = _pool_mixer(
        x, mods[0], mix_norm_g[0:1], pool_w_in[0].astype(BF16), band, pool_w_grp[0].astype(BF16),
        pool_scale[0:1], pool_w_out[0].astype(BF16), ffn_w_gate, ffn_w_up, ffn_w_down,
        positions.astype(F32)[:, None, :], inv_freq[:, None], ffn_norm_g[0:1], tp)

    x2 = _dense_ffn(x1.reshape(n, d), h1.reshape(n, d), mods[0], fwg, fwu, fwd, tm, tf, s)


    q_lora = w_dq.shape[-1]
    wuq = w_uq[0].reshape(q_lora, N_HEADS, D_NOPE + D_ROPE)
    wqn_t = wuq[:, :, :D_NOPE].reshape(q_lora, N_HEADS * D_NOPE).T.astype(BF16)
    wqr_t = wuq[:, :, D_NOPE:].reshape(q_lora, N_HEADS * D_ROPE).T.astype(BF16)
    qt, k, vt = _qkv_proj(
        x2.reshape(b, s, d), mods[1], kvmod, mix_norm_g[1:2], kv_norm_g[None, :],
        w_dq[0].astype(BF16), q_norm_g[0:1], wqn_t, wqr_t, w_dkv.astype(BF16),
        ckv_norm_g[None, :], jnp.pad(w_kr, ((0, 0), (0, LANES - D_ROPE))).astype(BF16),
        w_uk.astype(BF16), w_uv.T.astype(BF16), rope_cos, rope_sin, ts)
    o, ewg, ewu, ewd = _attention(qt, k, vt, moe_w_gate[0], moe_w_up[0], moe_w_down[0], ts)

    n_exp = router_w.shape[-1]
    sub = _pick(s, ROUTE_SUB)
    n_sub = s // sub
    rcap = 2 * s + n_exp * PIECE
    ltri = (jnp.arange(ta)[None, :] < jnp.arange(ta)[:, None]).astype(BF16)
    assert n_exp <= LANES
    wr = jnp.pad(router_w[0], ((0, 0), (0, LANES - n_exp)))
    wr_hi = wr.astype(BF16)
    wr_split = jnp.concatenate([wr_hi, (wr - wr_hi.astype(F32)).astype(BF16)], axis=1)
    x3, h3, rcol, rrow, cnt = _attn_out(o.reshape(n, N_HEADS * D_V), x2, mods[1],
                                        w_o[0].astype(BF16), ffn_norm_g[1:2], wr_split, n_exp,
                                        ltri, ta, sub, s)
    cnt = cnt[:, :ta // sub, :n_exp].reshape(b, n_sub, n_exp)
    off, d_lo, d_hi, d_exp, c_lo, c_hi, perm, p_exp, p_fill = _routing_tables(cnt, b, n_sub, n_exp, rcap)
    xg, gs = _dispatch(off, d_lo, d_hi, d_exp, h3, rrow, b, s, rcap, n_exp, sub)
    yg = _expert_ffn(perm, p_exp, p_fill, xg, gs, ewg, ewu, ewd, tf)
    out = _combine(off, c_lo, c_hi, yg, rcol, x3, mods[1], final_norm_g[None, :], b, s, rcap,
                   n_exp, sub, tc)
    return out.reshape(b, s, d)
```

```python
import functools
import math

import jax
import jax.numpy as jnp
from jax import lax
from jax.experimental import pallas as pl
from jax.experimental.pallas import tpu as pltpu

F32 = jnp.float32
BF16 = jnp.bfloat16

N_HEADS = 8
D_NOPE = 128
D_ROPE = 64
D_V = 128
HEAD_PAD = 256
ROPE_HALF = D_ROPE // 2
ROPE_THETA = 10000.0
ATTN_SCALE = 1.0 / math.sqrt(D_NOPE + D_ROPE)
Q_SCALE = ATTN_SCALE * math.log2(math.e)
ATTN_QUERY_TILE = 512
ATTN_KEY_BLOCK = 512
ATTN_LOOKAHEAD = 3
POOL_WINDOWS = (2, 4, 8, 16)
POOL_HALO = 16
POOL_ROW_BLOCK = 128
EPS = 1e-6

ROUTE_SUB = 256
DISPATCH_CHUNK = 128
DISPATCH_WINDOW = 3
DISPATCH_STEP_ROWS = 1024
COMBINE_WINDOW = 256
PIECE = 512
PIECE_QUANTUM = 128
ROUTE_FIELDS = 8

LANES = 128
VMEM_LIMIT_BYTES = 56 * 1024 * 1024


def _cparams(sem):
    return pltpu.CompilerParams(dimension_semantics=sem, vmem_limit_bytes=VMEM_LIMIT_BYTES)


def _dot(a, b):
    return jnp.dot(a, b, preferred_element_type=F32)


def _rms_scale(x):
    return x * lax.rsqrt(jnp.mean(x * x, axis=-1, keepdims=True) + EPS)


def _silu(x):
    return x * jax.nn.sigmoid(x)


def _adaln_kernel(c_ref, w_ref, b_ref, o_ref):
    sc = _silu(c_ref[...])
    o_ref[0] = jnp.dot(sc, w_ref[0], preferred_element_type=F32,
                       precision=lax.Precision.HIGHEST) + b_ref[0]


def _adaln(c_pad, w, b, tn):
    nl, d, m = w.shape
    return pl.pallas_call(
        _adaln_kernel,
        grid=(nl, m // tn),
        in_specs=[
            pl.BlockSpec((8, d), lambda l, j: (0, 0)),
            pl.BlockSpec((1, d, tn), lambda l, j: (l, 0, j)),
            pl.BlockSpec((1, 1, tn), lambda l, j: (l, 0, j)),
        ],
        out_specs=pl.BlockSpec((1, 8, tn), lambda l, j: (l, 0, j)),
        out_shape=jax.ShapeDtypeStruct((nl, 8, m), F32),
        compiler_params=_cparams(("arbitrary", "arbitrary")),
        name="adaln",
    )(c_pad, w, b)


def _rope_table_side_job(pos_ref, freq_ref, cos_ref, sin_ref):
    ang = freq_ref[...] * pos_ref[0]
    cos_ref[0, 0] = jnp.cos(ang)
    sin_ref[0, 0] = jnp.sin(ang)


def _rope_t(x, cos_t, sin_t):
    x1, x2 = x[:ROPE_HALF], x[ROPE_HALF:]
    return x1 * cos_t - x2 * sin_t, x2 * cos_t + x1 * sin_t


def _cast_specs(n_exp, d, ff, steps, step_of):
    assert steps >= n_exp, "too few grid steps to carry the weight cast"
    k = max(c for c in range(1, steps // n_exp + 1) if ff % (c * LANES) == 0)
    cw = ff // k

    def chunk(*idx):
        c = jnp.minimum(step_of(*idx), n_exp * k - 1)
        return c // k, c % k

    gate_up = pl.BlockSpec((1, d, cw), lambda *idx: (chunk(*idx)[0], 0, chunk(*idx)[1]))
    down = pl.BlockSpec((1, cw, d), lambda *idx: (chunk(*idx)[0], chunk(*idx)[1], 0))
    return gate_up, down


def _cast_side_job(src_refs, dst_refs):
    for src, dst in zip(src_refs, dst_refs):
        dst[...] = src[...].astype(BF16)


def _pool_kernel(x_ref, mod_ref, g_ref, win_ref, band_ref, wgrp_ref, scale_ref, wout_ref,
                 fg_ref, fu_ref, fd_ref, pos_ref, freq_ref, gffn_ref,
                 o_ref, fg_out_ref, fu_out_ref, fd_out_ref, cos_ref, sin_ref, h_out_ref, uext_ref,
                 *, ts, gw):
    s = pl.program_id(1)

    @pl.when(s == 0)
    def _():
        uext_ref[0:POOL_HALO, :] = jnp.zeros((POOL_HALO, uext_ref.shape[1]), BF16)

    x = x_ref[0]
    mod = mod_ref[0]
    h = (_rms_scale(x) * g_ref[...]) * (1.0 + mod[1:2]) + mod[0:1]
    u = _dot(h.astype(BF16), win_ref[...])
    uext_ref[POOL_HALO:POOL_HALO + ts, :] = u.astype(BF16)

    t = s * ts + lax.broadcasted_iota(jnp.int32, (ts, 1), 0)
    zs = []
    for g, w in enumerate(POOL_WINDOWS):
        cols = slice(g * gw, (g + 1) * gw)
        rb = band_ref.shape[1]
        wsum = jnp.concatenate(
            [_dot(band_ref[g], uext_ref[r0:r0 + rb + POOL_HALO, cols]) for r0 in range(0, ts, rb)],
            axis=0)
        cnt = jnp.minimum(t + 1, w).astype(F32)
        p = wsum / cnt - u[:, cols]
        zs.append(_dot(p.astype(BF16), wgrp_ref[g]))
    z = jnp.concatenate(zs, axis=-1) * scale_ref[...]
    y = _dot(z.astype(BF16), wout_ref[...])
    x_out = x + mod[2:3] * y
    o_ref[0] = x_out
    h_out_ref[0] = ((_rms_scale(x_out) * gffn_ref[...]) * (1.0 + mod[4:5]) + mod[3:4]).astype(BF16)

    uext_ref[0:POOL_HALO, :] = uext_ref[ts:ts + POOL_HALO, :]

    _cast_side_job((fg_ref, fu_ref, fd_ref), (fg_out_ref, fu_out_ref, fd_out_ref))
    _rope_table_side_job(pos_ref, freq_ref, cos_ref, sin_ref)


def _pool_mixer(x, mod, g, w_in, band, w_grp, scale, w_out, fg, fu, fd, pos_f, freq_row, g_ffn,
                ts):
    b, s, d = x.shape
    ng, gw, _ = w_grp.shape
    n_s = s // ts
    gate_up, down = _cast_specs(1, d, fg.shape[2], b * n_s, lambda i, j: i * n_s + j)
    const2 = lambda i, j: (0, 0)
    const3 = lambda i, j: (0, 0, 0)
    return pl.pallas_call(
        functools.partial(_pool_kernel, ts=ts, gw=gw),
        grid=(b, s // ts),
        in_specs=[
            pl.BlockSpec((1, ts, d), lambda i, j: (i, j, 0)),
            pl.BlockSpec((1, 6, d), lambda i, j: (i, 0, 0)),
            pl.BlockSpec((1, d), const2),
            pl.BlockSpec((d, d), const2),
            pl.BlockSpec(band.shape, const3),
            pl.BlockSpec((ng, gw, gw), const3),
            pl.BlockSpec((1, d), const2),
            pl.BlockSpec((d, d), const2),
            gate_up, gate_up, down,
            pl.BlockSpec((1, 1, ts), lambda i, j: (i, 0, j)),
            pl.BlockSpec((ROPE_HALF, 1), const2),
            pl.BlockSpec((1, d), const2),
        ],
        out_specs=[pl.BlockSpec((1, ts, d), lambda i, j: (i, j, 0)), gate_up, gate_up, down,
                   pl.BlockSpec((1, 1, ROPE_HALF, ts), lambda i, j: (i, j, 0, 0)),
                   pl.BlockSpec((1, 1, ROPE_HALF, ts), lambda i, j: (i, j, 0, 0)),
                   pl.BlockSpec((1, ts, d), lambda i, j: (i, j, 0))],
        out_shape=[jax.ShapeDtypeStruct((b, s, d), F32),
                   jax.ShapeDtypeStruct(fg.shape, BF16),
                   jax.ShapeDtypeStruct(fu.shape, BF16),
                   jax.ShapeDtypeStruct(fd.shape, BF16),
                   jax.ShapeDtypeStruct((b, s // ts, ROPE_HALF, ts), F32),
                   jax.ShapeDtypeStruct((b, s // ts, ROPE_HALF, ts), F32),
                   jax.ShapeDtypeStruct((b, s, d), BF16)],
        scratch_shapes=[pltpu.VMEM((POOL_HALO + ts, d), BF16)],
        compiler_params=_cparams(("arbitrary", "arbitrary")),
        name="pool_mixer",
    )(x, mod, g, w_in, band, w_grp, scale, w_out, fg, fu, fd, pos_f, freq_row, g_ffn)


def _ffn_kernel(x_ref, h_ref, mod_ref, wg_ref, wu_ref, wd_ref, o_ref, *, tf):
    h = h_ref[...]
    ff = wg_ref.shape[2]
    acc = jnp.zeros(x_ref.shape, F32)
    for f0 in range(0, ff, tf):
        gate = _dot(h, wg_ref[0, :, f0:f0 + tf])
        up = _dot(h, wu_ref[0, :, f0:f0 + tf])
        a = (_silu(gate) * up).astype(BF16)
        acc = acc + _dot(a, wd_ref[0, f0:f0 + tf, :])
    o_ref[...] = x_ref[...] + mod_ref[0][5:6] * acc


def _dense_ffn(x2, h2, mod, wg, wu, wd, tm, tf, rows_per_batch):
    n, d = x2.shape
    tpb = rows_per_batch // tm
    resident = lambda a: pl.BlockSpec(a.shape, lambda i: (0, 0, 0), pipeline_mode=pl.Buffered(1))
    return pl.pallas_call(
        functools.partial(_ffn_kernel, tf=tf),
        grid=(n // tm,),
        in_specs=[
            pl.BlockSpec((tm, d), lambda i: (i, 0)),
            pl.BlockSpec((tm, d), lambda i: (i, 0)),
            pl.BlockSpec((1, 6, d), lambda i: (i // tpb, 0, 0)),
            resident(wg), resident(wu), resident(wd),
        ],
        out_specs=pl.BlockSpec((tm, d), lambda i: (i, 0)),
        out_shape=jax.ShapeDtypeStruct((n, d), F32),
        compiler_params=_cparams(("arbitrary",)),
        name="dense_ffn",
    )(x2, h2, mod, wg, wu, wd)


def _dot_nt(a, b):
    return lax.dot_general(a, b, (((1,), (1,)), ((), ())), preferred_element_type=F32)


def _qkv_kernel(x_ref, mod_ref, kvmod_ref, gmix_ref, gkv_ref, wdq_ref, qg_ref, wqn_t_ref,
                wqr_t_ref, wdkv_ref, ckvg_ref, wkr_ref, wuk_ref, wuv_t_ref, cos_ref, sin_ref,
                qt_ref, k_ref, vt_ref):
    x = x_ref[0]
    xn = _rms_scale(x)
    mod = mod_ref[0]
    kvmod = kvmod_ref[0]
    hq = ((xn * gmix_ref[...]) * (1.0 + mod[1:2]) + mod[0:1]).astype(BF16)
    hk = ((xn * gkv_ref[...]) * (1.0 + kvmod[1:2]) + kvmod[0:1]).astype(BF16)

    cq = (_rms_scale(_dot(hq, wdq_ref[...])) * qg_ref[...]).astype(BF16)
    qn_t = _dot_nt(wqn_t_ref[...], cq) * Q_SCALE
    qr_t = _dot_nt(wqr_t_ref[...], cq) * Q_SCALE
    cos_t = cos_ref[0, 0]
    sin_t = sin_ref[0, 0]
    zero_rows = jnp.zeros((HEAD_PAD - LANES - D_ROPE, x.shape[0]), BF16)

    ckv = (_rms_scale(_dot(hk, wdkv_ref[...])) * ckvg_ref[...]).astype(BF16)
    kn = _dot(ckv, wuk_ref[...])
    vt_ref[0, 0] = _dot_nt(wuv_t_ref[...], ckv).astype(BF16)
    kr_raw_t = _dot(hk, wkr_ref[...]).T
    k1, k2 = _rope_t(kr_raw_t[:D_ROPE], cos_t, sin_t)
    kr_t = jnp.concatenate([k1, k2, kr_raw_t[D_ROPE:]], axis=0)
    kr = kr_t.T.astype(BF16)

    for h in range(N_HEADS):
        lo = h * HEAD_PAD
        hs = slice(h * LANES, (h + 1) * LANES)
        qt_ref[0, 0, lo:lo + LANES, :] = qn_t[hs].astype(BF16)
        q1, q2 = _rope_t(qr_t[h * D_ROPE:(h + 1) * D_ROPE], cos_t, sin_t)
        ro = lo + LANES
        qt_ref[0, 0, ro:ro + ROPE_HALF, :] = q1.astype(BF16)
        qt_ref[0, 0, ro + ROPE_HALF:ro + D_ROPE, :] = q2.astype(BF16)
        qt_ref[0, 0, ro + D_ROPE:lo + HEAD_PAD, :] = zero_rows
        k_ref[0, :, lo:lo + LANES] = kn[:, hs].astype(BF16)
        k_ref[0, :, lo + LANES:lo + HEAD_PAD] = kr


def _qkv_proj(x, mod, kvmod, gmix, gkv, wdq, qg, wqn_t, wqr_t, wdkv, ckvg, wkr, wuk, wuv_t,
              cos_t, sin_t, ts):
    b, s, d = x.shape
    row = lambda i, j: (i, j, 0)
    tile = lambda i, j: (i, j, 0, 0)
    per_b = lambda i, j: (i, 0, 0)

    def full(a):
        return pl.BlockSpec(a.shape, lambda i, j: (0,) * a.ndim)

    weights = (gmix, gkv, wdq, qg, wqn_t, wqr_t, wdkv, ckvg, wkr, wuk, wuv_t)
    return pl.pallas_call(
        _qkv_kernel,
        grid=(b, s // ts),
        in_specs=[pl.BlockSpec((1, ts, d), row),
                  pl.BlockSpec((1, 6, d), per_b),
                  pl.BlockSpec((1, 2, d), per_b)]
                 + [full(a) for a in weights]
                 + [pl.BlockSpec((1, 1, ROPE_HALF, ts),
                                 lambda i, j: (i, j * ts // cos_t.shape[3], 0,
                                               j % (cos_t.shape[3] // ts)))] * 2,
        out_specs=[pl.BlockSpec((1, 1, N_HEADS * HEAD_PAD, ts), tile),
                   pl.BlockSpec((1, ts, N_HEADS * HEAD_PAD), row),
                   pl.BlockSpec((1, 1, N_HEADS * D_V, ts), tile)],
        out_shape=[jax.ShapeDtypeStruct((b, s // ts, N_HEADS * HEAD_PAD, ts), BF16),
                   jax.ShapeDtypeStruct((b, s, N_HEADS * HEAD_PAD), BF16),
                   jax.ShapeDtypeStruct((b, s // ts, N_HEADS * D_V, ts), BF16)],
        compiler_params=_cparams(("arbitrary", "arbitrary")),
        name="qkv_proj",
    )(x, mod, kvmod, *weights, cos_t, sin_t)


def _attn_kernel(qt_ref, k_ref, vt_ref, eg_ref, eu_ref, ed_ref,
                 o_ref, eg_out_ref, eu_out_ref, ed_out_ref, *, tq, tk):
    ts = qt_ref.shape[3]
    s = k_ref.shape[1]
    key = lax.broadcasted_iota(jnp.int32, (tk, tq), 0)
    qry = lax.broadcasted_iota(jnp.int32, (tk, tq), 1)
    last_block = lambda qi: ((qi + 1) * tq - 1) // tk
    blocks = [(qi, j) for qi in range(s // tq) for j in range(last_block(qi) + 1)]

    def tile(ref, start, size):
        return ref[0, start // ts, :, start % ts:start % ts + size]

    def scores_t(qi, j):
        st = _dot(k_ref[0, j * tk:(j + 1) * tk, :], tile(qt_ref, qi * tq, tq))
        if (j + 1) * tk - 1 > qi * tq:
            st = jnp.where(key + (j * tk - qi * tq) <= qry, st, -jnp.inf)
        return st

    ahead = [scores_t(*blk) for blk in blocks[:ATTN_LOOKAHEAD]]
    for n, (qi, j) in enumerate(blocks):
        st = ahead.pop(0)
        if n + ATTN_LOOKAHEAD < len(blocks):
            ahead.append(scores_t(*blocks[n + ATTN_LOOKAHEAD]))
        if j == 0:
            m = jnp.full((1, tq), -jnp.inf, F32)
            l = jnp.zeros((1, tq), F32)
            acc = jnp.zeros((D_V, tq), F32)
        m_new = jnp.maximum(m, jnp.max(st, axis=0, keepdims=True))
        alpha = jnp.exp2(m - m_new)
        pt = jnp.exp2(st - m_new)
        l = alpha * l + jnp.sum(pt, axis=0, keepdims=True)
        acc = alpha * acc + _dot(tile(vt_ref, j * tk, tk), pt.astype(BF16))
        m = m_new
        if j == last_block(qi):
            o_ref[0, qi * tq:(qi + 1) * tq, :] = (acc / l).T.astype(BF16)

    _cast_side_job((eg_ref, eu_ref, ed_ref), (eg_out_ref, eu_out_ref, ed_out_ref))


def _attention(qt, k, vt, eg, eu, ed, ts):
    b, s, _ = k.shape
    n_exp, d, eff = eg.shape
    gate_up, down = _cast_specs(n_exp, d, eff, b * N_HEADS, lambda i, h: i * N_HEADS + h)
    return pl.pallas_call(
        functools.partial(_attn_kernel, tq=min(ts, ATTN_QUERY_TILE), tk=min(ts, ATTN_KEY_BLOCK)),
        grid=(b, N_HEADS),
        in_specs=[
            pl.BlockSpec((1, s // ts, HEAD_PAD, ts), lambda i, h: (i, 0, h, 0)),
            pl.BlockSpec((1, s, HEAD_PAD), lambda i, h: (i, 0, h)),
            pl.BlockSpec((1, s // ts, D_V, ts), lambda i, h: (i, 0, h, 0)),
            gate_up, gate_up, down,
        ],
        out_specs=[pl.BlockSpec((1, s, D_V), lambda i, h: (i, 0, h)), gate_up, gate_up, down],
        out_shape=[jax.ShapeDtypeStruct((b, s, N_HEADS * D_V), BF16),
                   jax.ShapeDtypeStruct(eg.shape, BF16),
                   jax.ShapeDtypeStruct(eu.shape, BF16),
                   jax.ShapeDtypeStruct(ed.shape, BF16)],
        compiler_params=_cparams(("arbitrary", "arbitrary")),
        name="flash_attn",
    )(qt, k, vt, eg, eu, ed)


def _attn_out_kernel(o_ref, x_ref, mod_ref, wo_ref, g_ref, wr_ref, ltri_ref,
                     x_out_ref, h_out_ref, rcol_ref, rrow_ref, cnt_ref, carry_ref,
                     *, n_exp, tiles_per_block):
    i = pl.program_id(0)

    @pl.when(i % tiles_per_block == 0)
    def _():
        carry_ref[...] = jnp.zeros_like(carry_ref)

    mod = mod_ref[0]
    x = x_ref[...] + mod[2:3] * _dot(o_ref[...], wo_ref[...])
    x_out_ref[...] = x
    h = (_rms_scale(x) * g_ref[...]) * (1.0 + mod[4:5]) + mod[3:4]
    h_hi = h.astype(BF16)
    h_out_ref[...] = h_hi

    tm = x.shape[0]
    h_lo = (h - h_hi.astype(F32)).astype(BF16)
    parts = _dot(jnp.concatenate([h_hi, h_lo], axis=0), wr_ref[...])
    logits = (parts[:tm, :LANES] + parts[:tm, LANES:]) + (parts[tm:, :LANES] + parts[tm:, LANES:])
    lane = lax.broadcasted_iota(jnp.int32, (tm, LANES), 1)
    lane_f = lane.astype(F32)

    def first_argmax(vals):
        m = jnp.max(vals, axis=-1, keepdims=True)
        idx = jnp.min(jnp.where(vals == m, lane_f, float(LANES)), axis=-1, keepdims=True)
        return m, idx.astype(jnp.int32)

    logits = jnp.where(lane < n_exp, logits, -jnp.inf)
    m1, i1 = first_argmax(logits)
    m2, i2 = first_argmax(jnp.where(lane == i1, -jnp.inf, logits))
    ex = jnp.exp(m2 - m1)
    w1 = 1.0 / (1.0 + ex)
    w2 = ex / (1.0 + ex)

    mask = jnp.where((lane == i1) | (lane == i2), 1.0, 0.0)
    ranks = _dot(ltri_ref[...], mask.astype(BF16)) + carry_ref[0:1]
    incl = ranks + mask
    rank1 = jnp.sum(jnp.where(lane == i1, ranks, 0.0), axis=-1, keepdims=True)
    rank2 = jnp.sum(jnp.where(lane == i2, ranks, 0.0), axis=-1, keepdims=True)
    carry_ref[0:1] = incl[tm - 1:tm]
    n_sub = tm // rrow_ref.shape[2]
    sub = rrow_ref.shape[2]
    cnt_ref[0] = jnp.zeros(cnt_ref.shape[1:], F32)
    for j in range(n_sub):
        cnt_ref[0, j:j + 1] = incl[(j + 1) * sub - 1:(j + 1) * sub]

    fields = (i1.astype(F32), i2.astype(F32), rank1, rank2, w1, w2)
    rcol = jnp.zeros((tm, LANES), F32)
    for k, val in enumerate(fields):
        rcol = jnp.where(lane == k, val, rcol)
    rcol_ref[...] = rcol
    rrow = rcol.T
    for j in range(n_sub):
        rrow_ref[j] = rrow[0:ROUTE_FIELDS, j * sub:(j + 1) * sub]


def _attn_out(o2, x2, mod, wo, g, wr, n_exp, ltri, tm, sub, rows_per_block):
    n, d = x2.shape
    tpb = rows_per_block // tm
    n_sub = tm // sub
    return pl.pallas_call(
        functools.partial(_attn_out_kernel, n_exp=n_exp, tiles_per_block=tpb),
        grid=(n // tm,),
        in_specs=[
            pl.BlockSpec((tm, o2.shape[1]), lambda i: (i, 0)),
            pl.BlockSpec((tm, d), lambda i: (i, 0)),
            pl.BlockSpec((1, 6, d), lambda i: (i // tpb, 0, 0)),
            pl.BlockSpec(wo.shape, lambda i: (0, 0)),
            pl.BlockSpec((1, d), lambda i: (0, 0)),
            pl.BlockSpec(wr.shape, lambda i: (0, 0)),
            pl.BlockSpec((tm, tm), lambda i: (0, 0)),
        ],
        out_specs=[pl.BlockSpec((tm, d), lambda i: (i, 0)),
                   pl.BlockSpec((tm, d), lambda i: (i, 0)),
                   pl.BlockSpec((tm, LANES), lambda i: (i, 0)),
                   pl.BlockSpec((n_sub, ROUTE_FIELDS, sub), lambda i: (i, 0, 0)),
                   pl.BlockSpec((1, 8, LANES), lambda i: (i, 0, 0))],
        out_shape=[jax.ShapeDtypeStruct((n, d), F32),
                   jax.ShapeDtypeStruct((n, d), BF16),
                   jax.ShapeDtypeStruct((n, LANES), F32),
                   jax.ShapeDtypeStruct((n // sub, ROUTE_FIELDS, sub), F32),
                   jax.ShapeDtypeStruct((n // tm, 8, LANES), F32)],
        scratch_shapes=[pltpu.VMEM((8, LANES), F32)],
        compiler_params=_cparams(("arbitrary",)),
        name="attn_out_router",
    )(o2, x2, mod, wo, g, wr, ltri)


def _dispatch_kernel(off_ref, lo_ref, hi_ref, cexp_ref, h_ref, rrow_ref, xg_ref, gs_ref,
                     *, n_exp, sub, chunks_per_step, chunks_per_block):
    blk = pl.program_id(0)
    g = pl.program_id(1)
    d = h_ref.shape[1]
    n_sub = rrow_ref.shape[0]
    win = min(DISPATCH_WINDOW, n_sub)
    iota = lax.broadcasted_iota(jnp.int32, (DISPATCH_CHUNK, 1), 0)

    def contribution(info, sb):
        row, exp_id, exp_off = info
        r = rrow_ref[sb]
        first = r[0:1] == exp_id
        second = r[1:2] == exp_id
        pos = jnp.where(first, r[2:3] + exp_off, jnp.where(second, r[3:4] + exp_off, -1.0))
        hit = pos == row
        onehot = jnp.where(hit, 1.0, 0.0).astype(BF16)
        tok = h_ref[pl.ds(pl.multiple_of(sb * sub, sub), sub), :]
        gate = jnp.sum(jnp.where(hit, jnp.where(first, r[4:5], r[5:6]), 0.0),
                       axis=-1, keepdims=True)
        return _dot(onehot, tok), gate

    def chunk_info(j):
        c = g * chunks_per_step + j
        t = blk * chunks_per_block + c
        row = (c * DISPATCH_CHUNK + iota).astype(F32)
        exp_id = cexp_ref[t]
        exp_off = off_ref[blk * n_exp + exp_id].astype(F32)
        start = jnp.minimum(lo_ref[t], n_sub - win)
        return (row, exp_id.astype(F32), exp_off), start, hi_ref[t]

    first_chunk = blk * chunks_per_block + g * chunks_per_step
    used = functools.reduce(jnp.logical_or,
                            [hi_ref[first_chunk + j] >= 0 for j in range(chunks_per_step)])

    @pl.when(jnp.logical_not(used))
    def _():
        xg_ref[...] = jnp.zeros(xg_ref.shape, BF16)
        gs_ref[...] = jnp.zeros(gs_ref.shape, F32)

    @pl.when(used)
    def _():
        for j in range(chunks_per_step):
            info, start, _ = chunk_info(j)
            acc = jnp.zeros((DISPATCH_CHUNK, d), F32)
            gate = jnp.zeros((DISPATCH_CHUNK, 1), F32)
            for k in range(win):
                a, b = contribution(info, start + k)
                acc += a
                gate += b
            rows = slice(j * DISPATCH_CHUNK, (j + 1) * DISPATCH_CHUNK)
            xg_ref[rows, :] = acc.astype(BF16)
            gs_ref[rows, :] = gate

        def extra(j, _):
            info, start, last = chunk_info(j)
            rows = pl.ds(pl.multiple_of(j * DISPATCH_CHUNK, DISPATCH_CHUNK), DISPATCH_CHUNK)

            def add(sb, _):
                a, b = contribution(info, sb)
                xg_ref[rows, :] = (xg_ref[rows, :].astype(F32) + a).astype(BF16)
                gs_ref[rows, :] = gs_ref[rows, :] + b
                return 0

            return lax.fori_loop(start + win, last + 1, add, 0)

        lax.fori_loop(0, chunks_per_step, extra, 0)


def _dispatch(off, lo, hi, cexp, h, rrow, n_blocks, rows_per_block, rcap, n_exp, sub):
    n, d = h.shape
    n_sub = rows_per_block // sub
    step_rows = _pick(rcap, DISPATCH_STEP_ROWS)
    cps = step_rows // DISPATCH_CHUNK
    cpb = rcap // DISPATCH_CHUNK
    grid_spec = pltpu.PrefetchScalarGridSpec(
        num_scalar_prefetch=4,
        grid=(n_blocks, rcap // step_rows),
        in_specs=[
            pl.BlockSpec((rows_per_block, d), lambda b, g, *_: (b, 0)),
            pl.BlockSpec((n_sub, ROUTE_FIELDS, sub), lambda b, g, *_: (b, 0, 0)),
        ],
        out_specs=[
            pl.BlockSpec((step_rows, d), lambda b, g, *_: (b * (rcap // step_rows) + g, 0)),
            pl.BlockSpec((step_rows, 1), lambda b, g, *_: (b * (rcap // step_rows) + g, 0)),
        ],
    )
    return pl.pallas_call(
        functools.partial(_dispatch_kernel, n_exp=n_exp, sub=sub, chunks_per_step=cps,
                          chunks_per_block=cpb),
        grid_spec=grid_spec,
        out_shape=[jax.ShapeDtypeStruct((n_blocks * rcap, d), BF16),
                   jax.ShapeDtypeStruct((n_blocks * rcap, 1), F32)],
        compiler_params=_cparams(("arbitrary", "arbitrary")),
        name="moe_dispatch",
    )(off, lo, hi, cexp, h, rrow)


def _expert_ffn_kernel(perm_ref, pexp_ref, pfill_ref, x_ref, gs_ref, wg_ref, wu_ref, wd_ref,
                       y_ref, *, tf):
    w = pl.program_id(0)
    fill = pfill_ref[w]
    ff = wg_ref.shape[2]

    def ffn(n_rows):
        x = x_ref[0:n_rows, :]
        gs = gs_ref[0:n_rows, :]
        acc = jnp.zeros((n_rows, y_ref.shape[1]), F32)
        for f0 in range(0, ff, tf):
            gate = _dot(x, wg_ref[0, :, f0:f0 + tf])
            up = _dot(x, wu_ref[0, :, f0:f0 + tf])
            a = ((_silu(gate) * up) * gs).astype(BF16)
            acc = acc + _dot(a, wd_ref[0, f0:f0 + tf, :])
        return acc.astype(BF16)

    for level in range(PIECE // PIECE_QUANTUM + 1):
        n_rows = level * PIECE_QUANTUM

        @pl.when(fill == level)
        def _(n_rows=n_rows):
            if n_rows:
                y_ref[0:n_rows, :] = ffn(n_rows)
            if n_rows < PIECE:
                y_ref[n_rows:PIECE, :] = jnp.zeros((PIECE - n_rows, y_ref.shape[1]), BF16)


def _expert_ffn(perm, pexp, pfill, xg, gs, wg, wu, wd, tf):
    rows, d = xg.shape
    n_exp, _, ff = wg.shape
    grid_spec = pltpu.PrefetchScalarGridSpec(
        num_scalar_prefetch=3,
        grid=(rows // PIECE,),
        in_specs=[
            pl.BlockSpec((PIECE, d), lambda w, perm, pexp, pfill: (perm[w], 0)),
            pl.BlockSpec((PIECE, 1), lambda w, perm, pexp, pfill: (perm[w], 0)),
            pl.BlockSpec((1, d, ff), lambda w, perm, pexp, pfill: (pexp[w], 0, 0)),
            pl.BlockSpec((1, d, ff), lambda w, perm, pexp, pfill: (pexp[w], 0, 0)),
            pl.BlockSpec((1, ff, d), lambda w, perm, pexp, pfill: (pexp[w], 0, 0)),
        ],
        out_specs=pl.BlockSpec((PIECE, d), lambda w, perm, pexp, pfill: (perm[w], 0)),
    )
    return pl.pallas_call(
        functools.partial(_expert_ffn_kernel, tf=tf),
        grid_spec=grid_spec,
        out_shape=jax.ShapeDtypeStruct((rows, d), BF16),
        compiler_params=_cparams(("arbitrary",)),
        name="expert_ffn",
    )(perm, pexp, pfill, xg, gs, wg, wu, wd)


def _combine_kernel(off_ref, start_ref, end_ref, yg_ref, rcol_ref, x_ref, mod_ref, gfin_ref, o_ref,
                    acc_ref, *, n_exp, sub, subs_per_block):
    blk = pl.program_id(0)
    t = pl.program_id(1)
    tm, d = x_ref.shape
    col = lax.broadcasted_iota(jnp.int32, (1, COMBINE_WINDOW), 1)
    align = COMBINE_WINDOW // 2

    def token_info(s):
        rc = rcol_ref[s * sub:(s + 1) * sub, :]
        return tuple(jnp.broadcast_to(rc[:, k:k + 1], (sub, COMBINE_WINDOW)) for k in range(4))

    def gathered(info, e, w0):
        i1, i2, rank1, rank2 = info
        rank = jnp.where(i1 == e, rank1, jnp.where(i2 == e, rank2, -float(COMBINE_WINDOW)))
        in_region = (w0 - off_ref[blk * n_exp + e] + col).astype(F32)
        onehot = jnp.where(rank == in_region, 1.0, 0.0).astype(BF16)
        return _dot(onehot, yg_ref[pl.ds(pl.multiple_of(w0, align), COMBINE_WINDOW), :])

    def first_window(s, e):
        idx = (blk * subs_per_block + t * (tm // sub) + s) * n_exp + e
        w0 = lax.shift_right_logical(start_ref[idx], align.bit_length() - 1) * align
        return w0, end_ref[idx]

    for s in range(tm // sub):
        info = token_info(s)
        acc = jnp.zeros((sub, d), F32)
        for e in range(n_exp):
            acc += gathered(info, e, first_window(s, e)[0])
        acc_ref[s * sub:(s + 1) * sub, :] = acc

    for s in range(tm // sub):
        info = token_info(s)
        for e in range(n_exp):
            w0, end = first_window(s, e)
            n_more = lax.shift_right_logical(
                jnp.maximum(end - w0 - 1, 0), COMBINE_WINDOW.bit_length() - 1)

            def add(k, _, info=info, e=e, w0=w0, s=s):
                rows = slice(s * sub, (s + 1) * sub)
                acc_ref[rows, :] = acc_ref[rows, :] + gathered(info, e, w0 + k * COMBINE_WINDOW)
                return 0

            lax.fori_loop(1, n_more + 1, add, 0)

    x = x_ref[...] + mod_ref[0][5:6] * acc_ref[...]
    o_ref[...] = _rms_scale(x) * gfin_ref[...]


def _combine(off, lo, hi, yg, rcol, x2, mod, gfin, n_blocks, rows_per_block, rcap, n_exp, sub, tm):
    n, d = x2.shape
    tpb = rows_per_block // tm
    grid_spec = pltpu.PrefetchScalarGridSpec(
        num_scalar_prefetch=3,
        grid=(n_blocks, tpb),
        in_specs=[
            pl.BlockSpec((rcap, d), lambda b, t, *_: (b, 0), pipeline_mode=pl.Buffered(1)),
            pl.BlockSpec((tm, LANES), lambda b, t, *_: (b * tpb + t, 0)),
            pl.BlockSpec((tm, d), lambda b, t, *_: (b * tpb + t, 0)),
            pl.BlockSpec((1, 6, d), lambda b, t, *_: (b, 0, 0)),
            pl.BlockSpec((1, d), lambda b, t, *_: (0, 0)),
        ],
        out_specs=pl.BlockSpec((tm, d), lambda b, t, *_: (b * tpb + t, 0)),
        scratch_shapes=[pltpu.VMEM((tm, d), F32)],
    )
    return pl.pallas_call(
        functools.partial(_combine_kernel, n_exp=n_exp, sub=sub,
                          subs_per_block=rows_per_block // sub),
        grid_spec=grid_spec,
        out_shape=jax.ShapeDtypeStruct((n, d), F32),
        compiler_params=_cparams(("arbitrary", "arbitrary")),
        name="moe_combine",
    )(off, lo, hi, yg, rcol, x2, mod, gfin)


def _routing_tables(cnt, n_blocks, n_sub, n_exp, rcap):
    cum_incl = cnt.astype(jnp.int32)
    cum_excl = jnp.concatenate([jnp.zeros_like(cum_incl[:, :1]), cum_incl[:, :-1]], axis=1)
    total = cum_incl[:, -1]
    psz = (total + PIECE - 1) // PIECE * PIECE
    off = jnp.cumsum(psz, axis=1) - psz
    start = off[:, None, :] + cum_excl
    end = off[:, None, :] + cum_incl

    c0 = jnp.arange(rcap // DISPATCH_CHUNK, dtype=jnp.int32)[None, :, None, None] * DISPATCH_CHUNK
    hit = ((start[:, None] < c0 + DISPATCH_CHUNK) & (end[:, None] > c0)
           & (end[:, None] > start[:, None])).any(-1)
    sub_ids = jnp.arange(n_sub, dtype=jnp.int32)
    d_lo = jnp.min(jnp.where(hit, sub_ids, n_sub), axis=-1)
    d_hi = jnp.max(jnp.where(hit, sub_ids, -1), axis=-1)
    d_lo = jnp.where(d_hi < 0, 0, d_lo)


    ppb = rcap // PIECE
    prow = jnp.arange(ppb, dtype=jnp.int32)[None, :, None] * PIECE
    inside = (prow >= off[:, None, :]) & (prow < (off + psz)[:, None, :])
    p_exp = jnp.argmax(inside, axis=-1).astype(jnp.int32)
    used = inside.any(-1)
    left = jnp.take_along_axis(total + off, p_exp, axis=1) - prow[..., 0]
    p_fill = jnp.where(used, jnp.clip((left + PIECE_QUANTUM - 1) // PIECE_QUANTUM, 0,
                                      PIECE // PIECE_QUANTUM), 0).astype(jnp.int32)
    key = jnp.where(p_fill > 0, p_exp, n_exp).reshape(-1)
    perm = jnp.argsort(key, stable=True).astype(jnp.int32)
    p_exp_sorted = jnp.minimum(key[perm], n_exp - 1).astype(jnp.int32)
    d_exp = jnp.repeat(p_exp, PIECE // DISPATCH_CHUNK, axis=1).reshape(-1)
    return (off.reshape(-1).astype(jnp.int32), d_lo.reshape(-1), d_hi.reshape(-1), d_exp,
            start.reshape(-1).astype(jnp.int32), end.reshape(-1).astype(jnp.int32),
            perm, p_exp_sorted, p_fill.reshape(-1)[perm])


def _pick(n, pref):
    t = min(n, pref)
    while n % t:
        t //= 2
    return t


def kernel(x, c, positions, ada_w, ada_b, mix_norm_g, ffn_norm_g, pool_w_in, pool_w_grp,
           pool_scale, pool_w_out, kv_ada_w, kv_ada_b, kv_norm_g, w_dkv, ckv_norm_g, w_kr,
           w_uk, w_uv, w_dq, q_norm_g, w_uq, w_o, ffn_w_gate, ffn_w_up, ffn_w_down,
           router_w, moe_w_gate, moe_w_up, moe_w_down, final_norm_g):
    b, s, d = x.shape
    n = b * s
    depth = ada_w.shape[0]
    assert depth == 2 and pool_w_in.shape[0] == 1 and w_dq.shape[0] == 1
    assert b <= 8 and d % LANES == 0

    ts = _pick(s, 512)
    tp = _pick(s, 1024)
    ta = _pick(s, 1024)
    tc = _pick(s, 512)
    tm = _pick(s, 1024)
    tf = _pick(ffn_w_gate.shape[-1], 512)

    c_pad = jnp.zeros((8, d), F32).at[:b].set(c)
    mods = _adaln(c_pad, ada_w, ada_b[:, None, :], _pick(6 * d, 1536))
    mods = mods[:, :b].reshape(depth, b, 6, d)
    kvmod = _adaln(c_pad, kv_ada_w[None], kv_ada_b[None, None, :], _pick(2 * d, 1024))
    kvmod = kvmod[0, :b].reshape(b, 2, d)

    r = jnp.arange(min(tp, POOL_ROW_BLOCK))[:, None]
    cidx = jnp.arange(min(tp, POOL_ROW_BLOCK) + POOL_HALO)[None, :] - POOL_HALO
    band = jnp.stack([((cidx <= r) & (cidx > r - w)) for w in POOL_WINDOWS]).astype(BF16)
    inv_freq = ROPE_THETA ** (-jnp.arange(0, D_ROPE, 2, dtype=F32) / D_ROPE)
    x1, fwg, fwu, fwd, rope_cos, rope_sin, h1 = _pool_mixer(
        x, mods[0], mix_norm_g[0:1], pool_w_in[0].astype(BF16), band, pool_w_grp[0].astype(BF16),
        pool_scale[0:1], pool_w_out[0].astype(BF16), ffn_w_gate, ffn_w_up, ffn_w_down,
        positions.astype(F32)[:, None, :], inv_freq[:, None], ffn_norm_g[0:1], tp)

    x2 = _dense_ffn(x1.reshape(n, d), h1.reshape(n, d), mods[0], fwg, fwu, fwd, tm, tf, s)


    q_lora = w_dq.shape[-1]
    wuq = w_uq[0].reshape(q_lora, N_HEADS, D_NOPE + D_ROPE)
    wqn_t = wuq[:, :, :D_NOPE].reshape(q_lora, N_HEADS * D_NOPE).T.astype(BF16)
    wqr_t = wuq[:, :, D_NOPE:].reshape(q_lora, N_HEADS * D_ROPE).T.astype(BF16)
    qt, k, vt = _qkv_proj(
        x2.reshape(b, s, d), mods[1], kvmod, mix_norm_g[1:2], kv_norm_g[None, :],
        w_dq[0].astype(BF16), q_norm_g[0:1], wqn_t, wqr_t, w_dkv.astype(BF16),
        ckv_norm_g[None, :], jnp.pad(w_kr, ((0, 0), (0, LANES - D_ROPE))).astype(BF16),
        w_uk.astype(BF16), w_uv.T.astype(BF16), rope_cos, rope_sin, ts)
    o, ewg, ewu, ewd = _attention(qt, k, vt, moe_w_gate[0], moe_w_up[0], moe_w_down[0], ts)

    n_exp = router_w.shape[-1]
    sub = _pick(s, ROUTE_SUB)
    n_sub = s // sub
    rcap = 2 * s + n_exp * PIECE
    ltri = (jnp.arange(ta)[None, :] < jnp.arange(ta)[:, None]).astype(BF16)
    assert n_exp <= LANES
    wr = jnp.pad(router_w[0], ((0, 0), (0, LANES - n_exp)))
    wr_hi = wr.astype(BF16)
    wr_split = jnp.concatenate([wr_hi, (wr - wr_hi.astype(F32)).astype(BF16)], axis=1)
    x3, h3, rcol, rrow, cnt = _attn_out(o.reshape(n, N_HEADS * D_V), x2, mods[1],
                                        w_o[0].astype(BF16), ffn_norm_g[1:2], wr_split, n_exp,
                                        ltri, ta, sub, s)
    cnt = cnt[:, :ta // sub, :n_exp].reshape(b, n_sub, n_exp)
    off, d_lo, d_hi, d_exp, c_lo, c_hi, perm, p_exp, p_fill = _routing_tables(cnt, b, n_sub, n_exp, rcap)
    xg, gs = _dispatch(off, d_lo, d_hi, d_exp, h3, rrow, b, s, rcap, n_exp, sub)
    yg = _expert_ffn(perm, p_exp, p_fill, xg, gs, ewg, ewu, ewd, tf)
    out = _combine(off, c_lo, c_hi, yg, rcol, x3, mods[1], final_norm_g[None, :], b, s, rcap,
                   n_exp, sub, tc)
    return out.reshape(b, s, d)
```

```python
import functools
import math

import jax
import jax.numpy as jnp
from jax import lax
from jax.experimental import pallas as pl
from jax.experimental.pallas import tpu as pltpu

F32 = jnp.float32
BF16 = jnp.bfloat16

N_HEADS = 8
D_NOPE = 128
D_ROPE = 64
D_V = 128
HEAD_PAD = 256
ROPE_HALF = D_ROPE // 2
ROPE_THETA = 10000.0
ATTN_SCALE = 1.0 / math.sqrt(D_NOPE + D_ROPE)
Q_SCALE = ATTN_SCALE * math.log2(math.e)
ATTN_QUERY_TILE = 512
ATTN_KEY_BLOCK = 512
ATTN_LOOKAHEAD = 3
POOL_WINDOWS = (2, 4, 8, 16)
POOL_HALO = 16
POOL_ROW_BLOCK = 128
EPS = 1e-6

ROUTE_SUB = 256
DISPATCH_CHUNK = 128
DISPATCH_WINDOW = 3
DISPATCH_STEP_ROWS = 1024
COMBINE_WINDOW = 256
PIECE = 512
PIECE_QUANTUM = 128
ROUTE_FIELDS = 8

LANES = 128
VMEM_LIMIT_BYTES = 56 * 1024 * 1024


def _cparams(sem):
    return pltpu.CompilerParams(dimension_semantics=sem, vmem_limit_bytes=VMEM_LIMIT_BYTES)


def _dot(a, b):
    return jnp.dot(a, b, preferred_element_type=F32)


def _rms_scale(x):
    return x * lax.rsqrt(jnp.mean(x * x, axis=-1, keepdims=True) + EPS)


def _silu(x):
    return x * jax.nn.sigmoid(x)


def _adaln_kernel(c_ref, w_ref, b_ref, o_ref):
    sc = _silu(c_ref[...])
    o_ref[0] = jnp.dot(sc, w_ref[0], preferred_element_type=F32,
                       precision=lax.Precision.HIGHEST) + b_ref[0]


def _adaln(c_pad, w, b, tn):
    nl, d, m = w.shape
    return pl.pallas_call(
        _adaln_kernel,
        grid=(nl, m // tn),
        in_specs=[
            pl.BlockSpec((8, d), lambda l, j: (0, 0)),
            pl.BlockSpec((1, d, tn), lambda l, j: (l, 0, j)),
            pl.BlockSpec((1, 1, tn), lambda l, j: (l, 0, j)),
        ],
        out_specs=pl.BlockSpec((1, 8, tn), lambda l, j: (l, 0, j)),
        out_shape=jax.ShapeDtypeStruct((nl, 8, m), F32),
        compiler_params=_cparams(("arbitrary", "arbitrary")),
        name="adaln",
    )(c_pad, w, b)


def _rope_table_side_job(pos_ref, freq_ref, cos_ref, sin_ref):
    ang = freq_ref[...] * pos_ref[0]
    cos_ref[0, 0] = jnp.cos(ang)
    sin_ref[0, 0] = jnp.sin(ang)


def _rope_t(x, cos_t, sin_t):
    x1, x2 = x[:ROPE_HALF], x[ROPE_HALF:]
    return x1 * cos_t - x2 * sin_t, x2 * cos_t + x1 * sin_t


def _cast_specs(n_exp, d, ff, steps, step_of):
    assert steps >= n_exp, "too few grid steps to carry the weight cast"
    k = max(c for c in range(1, steps // n_exp + 1) if ff % (c * LANES) == 0)
    cw = ff // k

    def chunk(*idx):
        c = jnp.minimum(step_of(*idx), n_exp * k - 1)
        return c // k, c % k

    gate_up = pl.BlockSpec((1, d, cw), lambda *idx: (chunk(*idx)[0], 0, chunk(*idx)[1]))
    down = pl.BlockSpec((1, cw, d), lambda *idx: (chunk(*idx)[0], chunk(*idx)[1], 0))
    return gate_up, down


def _cast_side_job(src_refs, dst_refs):
    for src, dst in zip(src_refs, dst_refs):
        dst[...] = src[...].astype(BF16)


def _pool_kernel(x_ref, mod_ref, g_ref, win_ref, band_ref, wgrp_ref, scale_ref, wout_ref,
                 fg_ref, fu_ref, fd_ref, pos_ref, freq_ref, gffn_ref,
                 o_ref, fg_out_ref, fu_out_ref, fd_out_ref, cos_ref, sin_ref, h_out_ref, uext_ref,
                 *, ts, gw):
    s = pl.program_id(1)

    @pl.when(s == 0)
    def _():
        uext_ref[0:POOL_HALO, :] = jnp.zeros((POOL_HALO, uext_ref.shape[1]), BF16)

    x = x_ref[0]
    mod = mod_ref[0]
    h = (_rms_scale(x) * g_ref[...]) * (1.0 + mod[1:2]) + mod[0:1]
    u = _dot(h.astype(BF16), win_ref[...])
    uext_ref[POOL_HALO:POOL_HALO + ts, :] = u.astype(BF16)

    t = s * ts + lax.broadcasted_iota(jnp.int32, (ts, 1), 0)
    zs = []
    for g, w in enumerate(POOL_WINDOWS):
        cols = slice(g * gw, (g + 1) * gw)
        rb = band_ref.shape[1]
        wsum = jnp.concatenate(
            [_dot(band_ref[g], uext_ref[r0:r0 + rb + POOL_HALO, cols]) for r0 in range(0, ts, rb)],
            axis=0)
        cnt = jnp.minimum(t + 1, w).astype(F32)
        p = wsum / cnt - u[:, cols]
        zs.append(_dot(p.astype(BF16), wgrp_ref[g]))
    z = jnp.concatenate(zs, axis=-1) * scale_ref[...]
    y = _dot(z.astype(BF16), wout_ref[...])
    x_out = x + mod[2:3] * y
    o_ref[0] = x_out
    h_out_ref[0] = ((_rms_scale(x_out) * gffn_ref[...]) * (1.0 + mod[4:5]) + mod[3:4]).astype(BF16)

    uext_ref[0:POOL_HALO, :] = uext_ref[ts:ts + POOL_HALO, :]

    _cast_side_job((fg_ref, fu_ref, fd_ref), (fg_out_ref, fu_out_ref, fd_out_ref))
    _rope_table_side_job(pos_ref, freq_ref, cos_ref, sin_ref)


def _pool_mixer(x, mod, g, w_in, band, w_grp, scale, w_out, fg, fu, fd, pos_f, freq_row, g_ffn,
                ts):
    b, s, d = x.shape
    ng, gw, _ = w_grp.shape
    n_s = s // ts
    gate_up, down = _cast_specs(1, d, fg.shape[2], b * n_s, lambda i, j: i * n_s + j)
    const2 = lambda i, j: (0, 0)
    const3 = lambda i, j: (0, 0, 0)
    return pl.pallas_call(
        functools.partial(_pool_kernel, ts=ts, gw=gw),
        grid=(b, s // ts),
        in_specs=[
            pl.BlockSpec((1, ts, d), lambda i, j: (i, j, 0)),
            pl.BlockSpec((1, 6, d), lambda i, j: (i, 0, 0)),
            pl.BlockSpec((1, d), const2),
            pl.BlockSpec((d, d), const2),
            pl.BlockSpec(band.shape, const3),
            pl.BlockSpec((ng, gw, gw), const3),
            pl.BlockSpec((1, d), const2),
            pl.BlockSpec((d, d), const2),
            gate_up, gate_up, down,
            pl.BlockSpec((1, 1, ts), lambda i, j: (i, 0, j)),
            pl.BlockSpec((ROPE_HALF, 1), const2),
            pl.BlockSpec((1, d), const2),
        ],
        out_specs=[pl.BlockSpec((1, ts, d), lambda i, j: (i, j, 0)), gate_up, gate_up, down,
                   pl.BlockSpec((1, 1, ROPE_HALF, ts), lambda i, j: (i, j, 0, 0)),
                   pl.BlockSpec((1, 1, ROPE_HALF, ts), lambda i, j: (i, j, 0, 0)),
                   pl.BlockSpec((1, ts, d), lambda i, j: (i, j, 0))],
        out_shape=[jax.ShapeDtypeStruct((b, s, d), F32),
                   jax.ShapeDtypeStruct(fg.shape, BF16),
                   jax.ShapeDtypeStruct(fu.shape, BF16),
                   jax.ShapeDtypeStruct(fd.shape, BF16),
                   jax.ShapeDtypeStruct((b, s // ts, ROPE_HALF, ts), F32),
                   jax.ShapeDtypeStruct((b, s // ts, ROPE_HALF, ts), F32),
                   jax.ShapeDtypeStruct((b, s, d), BF16)],
        scratch_shapes=[pltpu.VMEM((POOL_HALO + ts, d), BF16)],
        compiler_params=_cparams(("arbitrary", "arbitrary")),
        name="pool_mixer",
    )(x, mod, g, w_in, band, w_grp, scale, w_out, fg, fu, fd, pos_f, freq_row, g_ffn)


def _ffn_kernel(x_ref, h_ref, mod_ref, wg_ref, wu_ref, wd_ref, o_ref, *, tf):
    h = h_ref[...]
    ff = wg_ref.shape[2]
    acc = jnp.zeros(x_ref.shape, F32)
    for f0 in range(0, ff, tf):
        gate = _dot(h, wg_ref[0, :, f0:f0 + tf])
        up = _dot(h, wu_ref[0, :, f0:f0 + tf])
        a = (_silu(gate) * up).astype(BF16)
        acc = acc + _dot(a, wd_ref[0, f0:f0 + tf, :])
    o_ref[...] = x_ref[...] + mod_ref[0][5:6] * acc


def _dense_ffn(x2, h2, mod, wg, wu, wd, tm, tf, rows_per_batch):
    n, d = x2.shape
    tpb = rows_per_batch // tm
    resident = lambda a: pl.BlockSpec(a.shape, lambda i: (0, 0, 0), pipeline_mode=pl.Buffered(1))
    return pl.pallas_call(
        functools.partial(_ffn_kernel, tf=tf),
        grid=(n // tm,),
        in_specs=[
            pl.BlockSpec((tm, d), lambda i: (i, 0)),
            pl.BlockSpec((tm, d), lambda i: (i, 0)),
            pl.BlockSpec((1, 6, d), lambda i: (i // tpb, 0, 0)),
            resident(wg), resident(wu), resident(wd),
        ],
        out_specs=pl.BlockSpec((tm, d), lambda i: (i, 0)),
        out_shape=jax.ShapeDtypeStruct((n, d), F32),
        compiler_params=_cparams(("arbitrary",)),
        name="dense_ffn",
    )(x2, h2, mod, wg, wu, wd)


def _dot_nt(a, b):
    return lax.dot_general(a, b, (((1,), (1,)), ((), ())), preferred_element_type=F32)


def _qkv_kernel(x_ref, mod_ref, kvmod_ref, gmix_ref, gkv_ref, wdq_ref, qg_ref, wqn_t_ref,
                wqr_t_ref, wdkv_ref, ckvg_ref, wkr_ref, wuk_ref, wuv_t_ref, cos_ref, sin_ref,
                qt_ref, k_ref, vt_ref):
    x = x_ref[0]
    xn = _rms_scale(x)
    mod = mod_ref[0]
    kvmod = kvmod_ref[0]
    hq = ((xn * gmix_ref[...]) * (1.0 + mod[1:2]) + mod[0:1]).astype(BF16)
    hk = ((xn * gkv_ref[...]) * (1.0 + kvmod[1:2]) + kvmod[0:1]).astype(BF16)

    cq = (_rms_scale(_dot(hq, wdq_ref[...])) * qg_ref[...]).astype(BF16)
    qn_t = _dot_nt(wqn_t_ref[...], cq) * Q_SCALE
    qr_t = _dot_nt(wqr_t_ref[...], cq) * Q_SCALE
    cos_t = cos_ref[0, 0]
    sin_t = sin_ref[0, 0]
    zero_rows = jnp.zeros((HEAD_PAD - LANES - D_ROPE, x.shape[0]), BF16)

    ckv = (_rms_scale(_dot(hk, wdkv_ref[...])) * ckvg_ref[...]).astype(BF16)
    kn = _dot(ckv, wuk_ref[...])
    vt_ref[0, 0] = _dot_nt(wuv_t_ref[...], ckv).astype(BF16)
    kr_raw_t = _dot(hk, wkr_ref[...]).T
    k1, k2 = _rope_t(kr_raw_t[:D_ROPE], cos_t, sin_t)
    kr_t = jnp.concatenate([k1, k2, kr_raw_t[D_ROPE:]], axis=0)
    kr = kr_t.T.astype(BF16)

    for h in range(N_HEADS):
        lo = h * HEAD_PAD
        hs = slice(h * LANES, (h + 1) * LANES)
        qt_ref[0, 0, lo:lo + LANES, :] = qn_t[hs].astype(BF16)
        q1, q2 = _rope_t(qr_t[h * D_ROPE:(h + 1) * D_ROPE], cos_t, sin_t)
        ro = lo + LANES
        qt_ref[0, 0, ro:ro + ROPE_HALF, :] = q1.astype(BF16)
        qt_ref[0, 0, ro + ROPE_HALF:ro + D_ROPE, :] = q2.astype(BF16)
        qt_ref[0, 0, ro + D_ROPE:lo + HEAD_PAD, :] = zero_rows
        k_ref[0, :, lo:lo + LANES] = kn[:, hs].astype(BF16)
        k_ref[0, :, lo + LANES:lo + HEAD_PAD] = kr


def _qkv_proj(x, mod, kvmod, gmix, gkv, wdq, qg, wqn_t, wqr_t, wdkv, ckvg, wkr, wuk, wuv_t,
              cos_t, sin_t, ts):
    b, s, d = x.shape
    row = lambda i, j: (i, j, 0)
    tile = lambda i, j: (i, j, 0, 0)
    per_b = lambda i, j: (i, 0, 0)

    def full(a):
        return pl.BlockSpec(a.shape, lambda i, j: (0,) * a.ndim)

    weights = (gmix, gkv, wdq, qg, wqn_t, wqr_t, wdkv, ckvg, wkr, wuk, wuv_t)
    return pl.pallas_call(
        _qkv_kernel,
        grid=(b, s // ts),
        in_specs=[pl.BlockSpec((1, ts, d), row),
                  pl.BlockSpec((1, 6, d), per_b),
                  pl.BlockSpec((1, 2, d), per_b)]
                 + [full(a) for a in weights]
                 + [pl.BlockSpec((1, 1, ROPE_HALF, ts),
                                 lambda i, j: (i, j * ts // cos_t.shape[3], 0,
                                               j % (cos_t.shape[3] // ts)))] * 2,
        out_specs=[pl.BlockSpec((1, 1, N_HEADS * HEAD_PAD, ts), tile),
                   pl.BlockSpec((1, ts, N_HEADS * HEAD_PAD), row),
                   pl.BlockSpec((1, 1, N_HEADS * D_V, ts), tile)],
        out_shape=[jax.ShapeDtypeStruct((b, s // ts, N_HEADS * HEAD_PAD, ts), BF16),
                   jax.ShapeDtypeStruct((b, s, N_HEADS * HEAD_PAD), BF16),
                   jax.ShapeDtypeStruct((b, s // ts, N_HEADS * D_V, ts), BF16)],
        compiler_params=_cparams(("arbitrary", "arbitrary")),
        name="qkv_proj",
    )(x, mod, kvmod, *weights, cos_t, sin_t)


def _attn_kernel(qt_ref, k_ref, vt_ref, eg_ref, eu_ref, ed_ref,
                 o_ref, eg_out_ref, eu_out_ref, ed_out_ref, *, tq, tk):
    ts = qt_ref.shape[3]
    s = k_ref.shape[1]
    key = lax.broadcasted_iota(jnp.int32, (tk, tq), 0)
    qry = lax.broadcasted_iota(jnp.int32, (tk, tq), 1)
    last_block = lambda qi: ((qi + 1) * tq - 1) // tk
    blocks = [(qi, j) for qi in range(s // tq) for j in range(last_block(qi) + 1)]

    def tile(ref, start, size):
        return ref[0, start // ts, :, start % ts:start % ts + size]

    def scores_t(qi, j):
        st = _dot(k_ref[0, j * tk:(j + 1) * tk, :], tile(qt_ref, qi * tq, tq))
        if (j + 1) * tk - 1 > qi * tq:
            st = jnp.where(key + (j * tk - qi * tq) <= qry, st, -jnp.inf)
        return st

    ahead = [scores_t(*blk) for blk in blocks[:ATTN_LOOKAHEAD]]
    for n, (qi, j) in enumerate(blocks):
        st = ahead.pop(0)
        if n + ATTN_LOOKAHEAD < len(blocks):
            ahead.append(scores_t(*blocks[n + ATTN_LOOKAHEAD]))
        if j == 0:
            m = jnp.full((1, tq), -jnp.inf, F32)
            l = jnp.zeros((1, tq), F32)
            acc = jnp.zeros((D_V, tq), F32)
        m_new = jnp.maximum(m, jnp.max(st, axis=0, keepdims=True))
        alpha = jnp.exp2(m - m_new)
        pt = jnp.exp2(st - m_new)
        l = alpha * l + jnp.sum(pt, axis=0, keepdims=True)
        acc = alpha * acc + _dot(tile(vt_ref, j * tk, tk), pt.astype(BF16))
        m = m_new
        if j == last_block(qi):
            o_ref[0, qi * tq:(qi + 1) * tq, :] = (acc / l).T.astype(BF16)

    _cast_side_job((eg_ref, eu_ref, ed_ref), (eg_out_ref, eu_out_ref, ed_out_ref))


def _attention(qt, k, vt, eg, eu, ed, ts):
    b, s, _ = k.shape
    n_exp, d, eff = eg.shape
    gate_up, down = _cast_specs(n_exp, d, eff, b * N_HEADS, lambda i, h: i * N_HEADS + h)
    return pl.pallas_call(
        functools.partial(_attn_kernel, tq=min(ts, ATTN_QUERY_TILE), tk=min(ts, ATTN_KEY_BLOCK)),
        grid=(b, N_HEADS),
        in_specs=[
            pl.BlockSpec((1, s // ts, HEAD_PAD, ts), lambda i, h: (i, 0, h, 0)),
            pl.BlockSpec((1, s, HEAD_PAD), lambda i, h: (i, 0, h)),
            pl.BlockSpec((1, s // ts, D_V, ts), lambda i, h: (i, 0, h, 0)),
            gate_up, gate_up, down,
        ],
        out_specs=[pl.BlockSpec((1, s, D_V), lambda i, h: (i, 0, h)), gate_up, gate_up, down],
        out_shape=[jax.ShapeDtypeStruct((b, s, N_HEADS * D_V), BF16),
                   jax.ShapeDtypeStruct(eg.shape, BF16),
                   jax.ShapeDtypeStruct(eu.shape, BF16),
                   jax.ShapeDtypeStruct(ed.shape, BF16)],
        compiler_params=_cparams(("arbitrary", "arbitrary")),
        name="flash_attn",
    )(qt, k, vt, eg, eu, ed)


def _attn_out_kernel(o_ref, x_ref, mod_ref, wo_ref, g_ref, wr_ref, ltri_ref,
                     x_out_ref, h_out_ref, rcol_ref, rrow_ref, cnt_ref, carry_ref,
                     *, n_exp, tiles_per_block):
    i = pl.program_id(0)

    @pl.when(i % tiles_per_block == 0)
    def _():
        carry_ref[...] = jnp.zeros_like(carry_ref)

    mod = mod_ref[0]
    x = x_ref[...] + mod[2:3] * _dot(o_ref[...], wo_ref[...])
    x_out_ref[...] = x
    h = (_rms_scale(x) * g_ref[...]) * (1.0 + mod[4:5]) + mod[3:4]
    h_hi = h.astype(BF16)
    h_out_ref[...] = h_hi

    tm = x.shape[0]
    h_lo = (h - h_hi.astype(F32)).astype(BF16)
    parts = _dot(jnp.concatenate([h_hi, h_lo], axis=0), wr_ref[...])
    logits = (parts[:tm, :LANES] + parts[:tm, LANES:]) + (parts[tm:, :LANES] + parts[tm:, LANES:])
    lane = lax.broadcasted_iota(jnp.int32, (tm, LANES), 1)
    lane_f = lane.astype(F32)

    def first_argmax(vals):
        m = jnp.max(vals, axis=-1, keepdims=True)
        idx = jnp.min(jnp.where(vals == m, lane_f, float(LANES)), axis=-1, keepdims=True)
        return m, idx.astype(jnp.int32)

    logits = jnp.where(lane < n_exp, logits, -jnp.inf)
    m1, i1 = first_argmax(logits)
    m2, i2 = first_argmax(jnp.where(lane == i1, -jnp.inf, logits))
    ex = jnp.exp(m2 - m1)
    w1 = 1.0 / (1.0 + ex)
    w2 = ex / (1.0 + ex)

    mask = jnp.where((lane == i1) | (lane == i2), 1.0, 0.0)
    ranks = _dot(ltri_ref[...], mask.astype(BF16)) + carry_ref[0:1]
    incl = ranks + mask
    rank1 = jnp.sum(jnp.where(lane == i1, ranks, 0.0), axis=-1, keepdims=True)
    rank2 = jnp.sum(jnp.where(lane == i2, ranks, 0.0), axis=-1, keepdims=True)
    carry_ref[0:1] = incl[tm - 1:tm]
    n_sub = tm // rrow_ref.shape[2]
    sub = rrow_ref.shape[2]
    cnt_ref[0] = jnp.zeros(cnt_ref.shape[1:], F32)
    for j in range(n_sub):
        cnt_ref[0, j:j + 1] = incl[(j + 1) * sub - 1:(j + 1) * sub]

    fields = (i1.astype(F32), i2.astype(F32), rank1, rank2, w1, w2)
    rcol = jnp.zeros((tm, LANES), F32)
    for k, val in enumerate(fields):
        rcol = jnp.where(lane == k, val, rcol)
    rcol_ref[...] = rcol
    rrow = rcol.T
    for j in range(n_sub):
        rrow_ref[j] = rrow[0:ROUTE_FIELDS, j * sub:(j + 1) * sub]


def _attn_out(o2, x2, mod, wo, g, wr, n_exp, ltri, tm, sub, rows_per_block):
    n, d = x2.shape
    tpb = rows_per_block // tm
    n_sub = tm // sub
    return pl.pallas_call(
        functools.partial(_attn_out_kernel, n_exp=n_exp, tiles_per_block=tpb),
        grid=(n // tm,),
        in_specs=[
            pl.BlockSpec((tm, o2.shape[1]), lambda i: (i, 0)),
            pl.BlockSpec((tm, d), lambda i: (i, 0)),
            pl.BlockSpec((1, 6, d), lambda i: (i // tpb, 0, 0)),
            pl.BlockSpec(wo.shape, lambda i: (0, 0)),
            pl.BlockSpec((1, d), lambda i: (0, 0)),
            pl.BlockSpec(wr.shape, lambda i: (0, 0)),
            pl.BlockSpec((tm, tm), lambda i: (0, 0)),
        ],
        out_specs=[pl.BlockSpec((tm, d), lambda i: (i, 0)),
                   pl.BlockSpec((tm, d), lambda i: (i, 0)),
                   pl.BlockSpec((tm, LANES), lambda i: (i, 0)),
                   pl.BlockSpec((n_sub, ROUTE_FIELDS, sub), lambda i: (i, 0, 0)),
                   pl.BlockSpec((1, 8, LANES), lambda i: (i, 0, 0))],
        out_shape=[jax.ShapeDtypeStruct((n, d), F32),
                   jax.ShapeDtypeStruct((n, d), BF16),
                   jax.ShapeDtypeStruct((n, LANES), F32),
                   jax.ShapeDtypeStruct((n // sub, ROUTE_FIELDS, sub), F32),
                   jax.ShapeDtypeStruct((n // tm, 8, LANES), F32)],
        scratch_shapes=[pltpu.VMEM((8, LANES), F32)],
        compiler_params=_cparams(("arbitrary",)),
        name="attn_out_router",
    )(o2, x2, mod, wo, g, wr, ltri)


def _dispatch_kernel(off_ref, lo_ref, hi_ref, cexp_ref, h_ref, rrow_ref, xg_ref, gs_ref,
                     *, n_exp, sub, chunks_per_step, chunks_per_block):
    blk = pl.program_id(0)
    g = pl.program_id(1)
    d = h_ref.shape[1]
    n_sub = rrow_ref.shape[0]
    win = min(DISPATCH_WINDOW, n_sub)
    iota = lax.broadcasted_iota(jnp.int32, (DISPATCH_CHUNK, 1), 0)

    def contribution(info, sb):
        row, exp_id, exp_off = info
        r = rrow_ref[sb]
        first = r[0:1] == exp_id
        second = r[1:2] == exp_id
        pos = jnp.where(first, r[2:3] + exp_off, jnp.where(second, r[3:4] + exp_off, -1.0))
        hit = pos == row
        onehot = jnp.where(hit, 1.0, 0.0).astype(BF16)
        tok = h_ref[pl.ds(pl.multiple_of(sb * sub, sub), sub), :]
        gate = jnp.sum(jnp.where(hit, jnp.where(first, r[4:5], r[5:6]), 0.0),
                       axis=-1, keepdims=True)
        return _dot(onehot, tok), gate

    def chunk_info(j):
        c = g * chunks_per_step + j
        t = blk * chunks_per_block + c
        row = (c * DISPATCH_CHUNK + iota).astype(F32)
        exp_id = cexp_ref[t]
        exp_off = off_ref[blk * n_exp + exp_id].astype(F32)
        start = jnp.minimum(lo_ref[t], n_sub - win)
        return (row, exp_id.astype(F32), exp_off), start, hi_ref[t]

    first_chunk = blk * chunks_per_block + g * chunks_per_step
    used = functools.reduce(jnp.logical_or,
                            [hi_ref[first_chunk + j] >= 0 for j in range(chunks_per_step)])

    @pl.when(jnp.logical_not(used))
    def _():
        xg_ref[...] = jnp.zeros(xg_ref.shape, BF16)
        gs_ref[...] = jnp.zeros(gs_ref.shape, F32)

    @pl.when(used)
    def _():
        for j in range(chunks_per_step):
            info, start, _ = chunk_info(j)
            acc = jnp.zeros((DISPATCH_CHUNK, d), F32)
            gate = jnp.zeros((DISPATCH_CHUNK, 1), F32)
            for k in range(win):
                a, b = contribution(info, start + k)
                acc += a
                gate += b
            rows = slice(j * DISPATCH_CHUNK, (j + 1) * DISPATCH_CHUNK)
            xg_ref[rows, :] = acc.astype(BF16)
            gs_ref[rows, :] = gate

        def extra(j, _):
            info, start, last = chunk_info(j)
            rows = pl.ds(pl.multiple_of(j * DISPATCH_CHUNK, DISPATCH_CHUNK), DISPATCH_CHUNK)

            def add(sb, _):
                a, b = contribution(info, sb)
                xg_ref[rows, :] = (xg_ref[rows, :].astype(F32) + a).astype(BF16)
                gs_ref[rows, :] = gs_ref[rows, :] + b
                return 0

            return lax.fori_loop(start + win, last + 1, add, 0)

        lax.fori_loop(0, chunks_per_step, extra, 0)


def _dispatch(off, lo, hi, cexp, h, rrow, n_blocks, rows_per_block, rcap, n_exp, sub):
    n, d = h.shape
    n_sub = rows_per_block // sub
    step_rows = _pick(rcap, DISPATCH_STEP_ROWS)
    cps = step_rows // DISPATCH_CHUNK
    cpb = rcap // DISPATCH_CHUNK
    grid_spec = pltpu.PrefetchScalarGridSpec(
        num_scalar_prefetch=4,
        grid=(n_blocks, rcap // step_rows),
        in_specs=[
            pl.BlockSpec((rows_per_block, d), lambda b, g, *_: (b, 0)),
            pl.BlockSpec((n_sub, ROUTE_FIELDS, sub), lambda b, g, *_: (b, 0, 0)),
        ],
        out_specs=[
            pl.BlockSpec((step_rows, d), lambda b, g, *_: (b * (rcap // step_rows) + g, 0)),
            pl.BlockSpec((step_rows, 1), lambda b, g, *_: (b * (rcap // step_rows) + g, 0)),
        ],
    )
    return pl.pallas_call(
        functools.partial(_dispatch_kernel, n_exp=n_exp, sub=sub, chunks_per_step=cps,
                          chunks_per_block=cpb),
        grid_spec=grid_spec,
        out_shape=[jax.ShapeDtypeStruct((n_blocks * rcap, d), BF16),
                   jax.ShapeDtypeStruct((n_blocks * rcap, 1), F32)],
        compiler_params=_cparams(("arbitrary", "arbitrary")),
        name="moe_dispatch",
    )(off, lo, hi, cexp, h, rrow)


def _expert_ffn_kernel(perm_ref, pexp_ref, pfill_ref, x_ref, gs_ref, wg_ref, wu_ref, wd_ref,
                       y_ref, *, tf):
    w = pl.program_id(0)
    fill = pfill_ref[w]
    ff = wg_ref.shape[2]

    def ffn(n_rows):
        x = x_ref[0:n_rows, :]
        gs = gs_ref[0:n_rows, :]
        acc = jnp.zeros((n_rows, y_ref.shape[1]), F32)
        for f0 in range(0, ff, tf):
            gate = _dot(x, wg_ref[0, :, f0:f0 + tf])
            up = _dot(x, wu_ref[0, :, f0:f0 + tf])
            a = ((_silu(gate) * up) * gs).astype(BF16)
            acc = acc + _dot(a, wd_ref[0, f0:f0 + tf, :])
        return acc.astype(BF16)

    for level in range(PIECE // PIECE_QUANTUM + 1):
        n_rows = level * PIECE_QUANTUM

        @pl.when(fill == level)
        def _(n_rows=n_rows):
            if n_rows:
                y_ref[0:n_rows, :] = ffn(n_rows)
            if n_rows < PIECE:
                y_ref[n_rows:PIECE, :] = jnp.zeros((PIECE - n_rows, y_ref.shape[1]), BF16)


def _expert_ffn(perm, pexp, pfill, xg, gs, wg, wu, wd, tf):
    rows, d = xg.shape
    n_exp, _, ff = wg.shape
    grid_spec = pltpu.PrefetchScalarGridSpec(
        num_scalar_prefetch=3,
        grid=(rows // PIECE,),
        in_specs=[
            pl.BlockSpec((PIECE, d), lambda w, perm, pexp, pfill: (perm[w], 0)),
            pl.BlockSpec((PIECE, 1), lambda w, perm, pexp, pfill: (perm[w], 0)),
            pl.BlockSpec((1, d, ff), lambda w, perm, pexp, pfill: (pexp[w], 0, 0)),
            pl.BlockSpec((1, d, ff), lambda w, perm, pexp, pfill: (pexp[w], 0, 0)),
            pl.BlockSpec((1, ff, d), lambda w, perm, pexp, pfill: (pexp[w], 0, 0)),
        ],
        out_specs=pl.BlockSpec((PIECE, d), lambda w, perm, pexp, pfill: (perm[w], 0)),
    )
    return pl.pallas_call(
        functools.partial(_expert_ffn_kernel, tf=tf),
        grid_spec=grid_spec,
        out_shape=jax.ShapeDtypeStruct((rows, d), BF16),
        compiler_params=_cparams(("arbitrary",)),
        name="expert_ffn",
    )(perm, pexp, pfill, xg, gs, wg, wu, wd)


def _combine_kernel(off_ref, start_ref, end_ref, yg_ref, rcol_ref, x_ref, mod_ref, gfin_ref, o_ref,
                    acc_ref, *, n_exp, sub, subs_per_block):
    blk = pl.program_id(0)
    t = pl.program_id(1)
    tm, d = x_ref.shape
    col = lax.broadcasted_iota(jnp.int32, (1, COMBINE_WINDOW), 1)
    align = COMBINE_WINDOW // 2

    def token_info(s):
        rc = rcol_ref[s * sub:(s + 1) * sub, :]
        return tuple(jnp.broadcast_to(rc[:, k:k + 1], (sub, COMBINE_WINDOW)) for k in range(4))

    def gathered(info, e, w0):
        i1, i2, rank1, rank2 = info
        rank = jnp.where(i1 == e, rank1, jnp.where(i2 == e, rank2, -float(COMBINE_WINDOW)))
        in_region = (w0 - off_ref[blk * n_exp + e] + col).astype(F32)
        onehot = jnp.where(rank == in_region, 1.0, 0.0).astype(BF16)
        return _dot(onehot, yg_ref[pl.ds(pl.multiple_of(w0, align), COMBINE_WINDOW), :])

    def first_window(s, e):
        idx = (blk * subs_per_block + t * (tm // sub) + s) * n_exp + e
        w0 = lax.shift_right_logical(start_ref[idx], align.bit_length() - 1) * align
        return w0, end_ref[idx]

    for s in range(tm // sub):
        info = token_info(s)
        acc = jnp.zeros((sub, d), F32)
        for e in range(n_exp):
            acc += gathered(info, e, first_window(s, e)[0])
        acc_ref[s * sub:(s + 1) * sub, :] = acc

    for s in range(tm // sub):
        info = token_info(s)
        for e in range(n_exp):
            w0, end = first_window(s, e)
            n_more = lax.shift_right_logical(
                jnp.maximum(end - w0 - 1, 0), COMBINE_WINDOW.bit_length() - 1)

            def add(k, _, info=info, e=e, w0=w0, s=s):
                rows = slice(s * sub, (s + 1) * sub)
                acc_ref[rows, :] = acc_ref[rows, :] + gathered(info, e, w0 + k * COMBINE_WINDOW)
                return 0

            lax.fori_loop(1, n_more + 1, add, 0)

    x = x_ref[...] + mod_ref[0][5:6] * acc_ref[...]
    o_ref[...] = _rms_scale(x) * gfin_ref[...]


def _combine(off, row_start, row_end, yg, rcol, x2, mod, gfin, n_blocks, rows_per_block, rcap,
             n_exp, sub, tm):
    n, d = x2.shape
    tpb = rows_per_block // tm
    grid_spec = pltpu.PrefetchScalarGridSpec(
        num_scalar_prefetch=3,
        grid=(n_blocks, tpb),
        in_specs=[
            pl.BlockSpec((rcap, d), lambda b, t, *_: (b, 0), pipeline_mode=pl.Buffered(1)),
            pl.BlockSpec((tm, LANES), lambda b, t, *_: (b * tpb + t, 0)),
            pl.BlockSpec((tm, d), lambda b, t, *_: (b * tpb + t, 0)),
            pl.BlockSpec((1, 6, d), lambda b, t, *_: (b, 0, 0)),
            pl.BlockSpec((1, d), lambda b, t, *_: (0, 0)),
        ],
        out_specs=pl.BlockSpec((tm, d), lambda b, t, *_: (b * tpb + t, 0)),
        scratch_shapes=[pltpu.VMEM((tm, d), F32)],
    )
    return pl.pallas_call(
        functools.partial(_combine_kernel, n_exp=n_exp, sub=sub,
                          subs_per_block=rows_per_block // sub),
        grid_spec=grid_spec,
        out_shape=jax.ShapeDtypeStruct((n, d), F32),
        compiler_params=_cparams(("arbitrary", "arbitrary")),
        name="moe_combine",
    )(off, row_start, row_end, yg, rcol, x2, mod, gfin)


def _routing_tables(cnt, n_blocks, n_sub, n_exp, rcap):
    cum_incl = cnt.astype(jnp.int32)
    cum_excl = jnp.concatenate([jnp.zeros_like(cum_incl[:, :1]), cum_incl[:, :-1]], axis=1)
    total = cum_incl[:, -1]
    psz = (total + PIECE - 1) // PIECE * PIECE
    off = jnp.cumsum(psz, axis=1) - psz
    start = off[:, None, :] + cum_excl
    end = off[:, None, :] + cum_incl

    c0 = jnp.arange(rcap // DISPATCH_CHUNK, dtype=jnp.int32)[None, :, None, None] * DISPATCH_CHUNK
    hit = ((start[:, None] < c0 + DISPATCH_CHUNK) & (end[:, None] > c0)
           & (end[:, None] > start[:, None])).any(-1)
    sub_ids = jnp.arange(n_sub, dtype=jnp.int32)
    d_lo = jnp.min(jnp.where(hit, sub_ids, n_sub), axis=-1)
    d_hi = jnp.max(jnp.where(hit, sub_ids, -1), axis=-1)
    d_lo = jnp.where(d_hi < 0, 0, d_lo)

    ppb = rcap // PIECE
    prow = jnp.arange(ppb, dtype=jnp.int32)[None, :, None] * PIECE
    inside = (prow >= off[:, None, :]) & (prow < (off + psz)[:, None, :])
    p_exp = jnp.argmax(inside, axis=-1).astype(jnp.int32)
    used = inside.any(-1)
    left = jnp.take_along_axis(total + off, p_exp, axis=1) - prow[..., 0]
    p_fill = jnp.where(used, jnp.clip((left + PIECE_QUANTUM - 1) // PIECE_QUANTUM, 0,
                                      PIECE // PIECE_QUANTUM), 0).astype(jnp.int32)
    key = jnp.where(p_fill > 0, p_exp, n_exp).reshape(-1)
    perm = jnp.argsort(key, stable=True).astype(jnp.int32)
    p_exp_sorted = jnp.minimum(key[perm], n_exp - 1).astype(jnp.int32)
    d_exp = jnp.repeat(p_exp, PIECE // DISPATCH_CHUNK, axis=1).reshape(-1)
    return (off.reshape(-1).astype(jnp.int32), d_lo.reshape(-1), d_hi.reshape(-1), d_exp,
            start.reshape(-1).astype(jnp.int32), end.reshape(-1).astype(jnp.int32),
            perm, p_exp_sorted, p_fill.reshape(-1)[perm])


def _pick(n, pref):
    t = min(n, pref)
    while n % t:
        t //= 2
    return t


def kernel(x, c, positions, ada_w, ada_b, mix_norm_g, ffn_norm_g, pool_w_in, pool_w_grp,
           pool_scale, pool_w_out, kv_ada_w, kv_ada_b, kv_norm_g, w_dkv, ckv_norm_g, w_kr,
           w_uk, w_uv, w_dq, q_norm_g, w_uq, w_o, ffn_w_gate, ffn_w_up, ffn_w_down,
           router_w, moe_w_gate, moe_w_up, moe_w_down, final_norm_g):
    b, s, d = x.shape
    n = b * s
    depth = ada_w.shape[0]
    assert depth == 2 and pool_w_in.shape[0] == 1 and w_dq.shape[0] == 1
    assert b <= 8 and d % LANES == 0

    ts = _pick(s, 512)
    tp = _pick(s, 1024)
    ta = _pick(s, 1024)
    tc = _pick(s, 512)
    tm = _pick(s, 1024)
    tf = _pick(ffn_w_gate.shape[-1], 512)

    c_pad = jnp.zeros((8, d), F32).at[:b].set(c)
    mods = _adaln(c_pad, ada_w, ada_b[:, None, :], _pick(6 * d, 1536))
    mods = mods[:, :b].reshape(depth, b, 6, d)
    kvmod = _adaln(c_pad, kv_ada_w[None], kv_ada_b[None, None, :], _pick(2 * d, 1024))
    kvmod = kvmod[0, :b].reshape(b, 2, d)

    r = jnp.arange(min(tp, POOL_ROW_BLOCK))[:, None]
    cidx = jnp.arange(min(tp, POOL_ROW_BLOCK) + POOL_HALO)[None, :] - POOL_HALO
    band = jnp.stack([((cidx <= r) & (cidx > r - w)) for w in POOL_WINDOWS]).astype(BF16)
    inv_freq = ROPE_THETA ** (-jnp.arange(0, D_ROPE, 2, dtype=F32) / D_ROPE)
    x1, fwg, fwu, fwd, rope_cos, rope_sin, h1 = _pool_mixer(
        x, mods[0], mix_norm_g[0:1], pool_w_in[0].astype(BF16), band, pool_w_grp[0].astype(BF16),
        pool_scale[0:1], pool_w_out[0].astype(BF16), ffn_w_gate, ffn_w_up, ffn_w_down,
        positions.astype(F32)[:, None, :], inv_freq[:, None], ffn_norm_g[0:1], tp)

    x2 = _dense_ffn(x1.reshape(n, d), h1.reshape(n, d), mods[0], fwg, fwu, fwd, tm, tf, s)


    q_lora = w_dq.shape[-1]
    wuq = w_uq[0].reshape(q_lora, N_HEADS, D_NOPE + D_ROPE)
    wqn_t = wuq[:, :, :D_NOPE].reshape(q_lora, N_HEADS * D_NOPE).T.astype(BF16)
    wqr_t = wuq[:, :, D_NOPE:].reshape(q_lora, N_HEADS * D_ROPE).T.astype(BF16)
    qt, k, vt = _qkv_proj(
        x2.reshape(b, s, d), mods[1], kvmod, mix_norm_g[1:2], kv_norm_g[None, :],
        w_dq[0].astype(BF16), q_norm_g[0:1], wqn_t, wqr_t, w_dkv.astype(BF16),
        ckv_norm_g[None, :], jnp.pad(w_kr, ((0, 0), (0, LANES - D_ROPE))).astype(BF16),
        w_uk.astype(BF16), w_uv.T.astype(BF16), rope_cos, rope_sin, ts)
    o, ewg, ewu, ewd = _attention(qt, k, vt, moe_w_gate[0], moe_w_up[0], moe_w_down[0], ts)

    n_exp = router_w.shape[-1]
    sub = _pick(s, ROUTE_SUB)
    n_sub = s // sub
    rcap = 2 * s + n_exp * PIECE
    ltri = (jnp.arange(ta)[None, :] < jnp.arange(ta)[:, None]).astype(BF16)
    assert n_exp <= LANES
    wr = jnp.pad(router_w[0], ((0, 0), (0, LANES - n_exp)))
    wr_hi = wr.astype(BF16)
    wr_split = jnp.concatenate([wr_hi, (wr - wr_hi.astype(F32)).astype(BF16)], axis=1)
    x3, h3, rcol, rrow, cnt = _attn_out(o.reshape(n, N_HEADS * D_V), x2, mods[1],
                                        w_o[0].astype(BF16), ffn_norm_g[1:2], wr_split, n_exp,
                                        ltri, ta, sub, s)
    cnt = cnt[:, :ta // sub, :n_exp].reshape(b, n_sub, n_exp)
    (off, d_lo, d_hi, d_exp, row_start, row_end, perm, p_exp,
     p_fill) = _routing_tables(cnt, b, n_sub, n_exp, rcap)
    xg, gs = _dispatch(off, d_lo, d_hi, d_exp, h3, rrow, b, s, rcap, n_exp, sub)
    yg = _expert_ffn(perm, p_exp, p_fill, xg, gs, ewg, ewu, ewd, tf)
    out = _combine(off, row_start, row_end, yg, rcol, x3, mods[1], final_norm_g[None, :], b, s,
                   rcap, n_exp, sub, tc)
    return out.reshape(b, s, d)
```

```python
import functools
import math

import jax
import jax.numpy as jnp
from jax import lax
from jax.experimental import pallas as pl
from jax.experimental.pallas import tpu as pltpu

F32 = jnp.float32
BF16 = jnp.bfloat16

N_HEADS = 8
D_NOPE = 128
D_ROPE = 64
D_V = 128
HEAD_PAD = 256
ROPE_HALF = D_ROPE // 2
ROPE_THETA = 10000.0
ATTN_SCALE = 1.0 / math.sqrt(D_NOPE + D_ROPE)
Q_SCALE = ATTN_SCALE * math.log2(math.e)
ATTN_QUERY_TILE = 512
ATTN_KEY_BLOCK = 512
ATTN_LOOKAHEAD = 3
POOL_WINDOWS = (2, 4, 8, 16)
POOL_HALO = 16
POOL_ROW_BLOCK = 128
EPS = 1e-6

ROUTE_SUB = 256
DISPATCH_CHUNK = 128
DISPATCH_WINDOW = 3
DISPATCH_STEP_ROWS = 1024
COMBINE_WINDOW = 256
PIECE = 512
PIECE_QUANTUM = 128
ROUTE_FIELDS = 8

LANES = 128
VMEM_LIMIT_BYTES = 56 * 1024 * 1024


def _cparams(sem):
    return pltpu.CompilerParams(dimension_semantics=sem, vmem_limit_bytes=VMEM_LIMIT_BYTES)


def _dot(a, b):
    return jnp.dot(a, b, preferred_element_type=F32)


def _rms_scale(x):
    return x * lax.rsqrt(jnp.mean(x * x, axis=-1, keepdims=True) + EPS)


def _silu(x):
    return x * jax.nn.sigmoid(x)


def _adaln_kernel(c_ref, w_ref, b_ref, o_ref):
    sc = _silu(c_ref[...])
    o_ref[0] = jnp.dot(sc, w_ref[0], preferred_element_type=F32,
                       precision=lax.Precision.HIGHEST) + b_ref[0]


def _adaln(c_pad, w, b, tn):
    nl, d, m = w.shape
    return pl.pallas_call(
        _adaln_kernel,
        grid=(nl, m // tn),
        in_specs=[
            pl.BlockSpec((8, d), lambda l, j: (0, 0)),
            pl.BlockSpec((1, d, tn), lambda l, j: (l, 0, j)),
            pl.BlockSpec((1, 1, tn), lambda l, j: (l, 0, j)),
        ],
        out_specs=pl.BlockSpec((1, 8, tn), lambda l, j: (l, 0, j)),
        out_shape=jax.ShapeDtypeStruct((nl, 8, m), F32),
        compiler_params=_cparams(("arbitrary", "arbitrary")),
        name="adaln",
    )(c_pad, w, b)


def _rope_table_side_job(pos_ref, freq_ref, cos_ref, sin_ref):
    ang = freq_ref[...] * pos_ref[0]
    cos_ref[0, 0] = jnp.cos(ang)
    sin_ref[0, 0] = jnp.sin(ang)


def _rope_t(x, cos_t, sin_t):
    x1, x2 = x[:ROPE_HALF], x[ROPE_HALF:]
    return x1 * cos_t - x2 * sin_t, x2 * cos_t + x1 * sin_t


def _cast_specs(n_exp, d, ff, steps, step_of):
    assert steps >= n_exp, "too few grid steps to carry the weight cast"
    k = max(c for c in range(1, steps // n_exp + 1) if ff % (c * LANES) == 0)
    cw = ff // k

    def chunk(*idx):
        c = jnp.minimum(step_of(*idx), n_exp * k - 1)
        return c // k, c % k

    gate_up = pl.BlockSpec((1, d, cw), lambda *idx: (chunk(*idx)[0], 0, chunk(*idx)[1]))
    down = pl.BlockSpec((1, cw, d), lambda *idx: (chunk(*idx)[0], chunk(*idx)[1], 0))
    return gate_up, down


def _cast_side_job(src_refs, dst_refs):
    for src, dst in zip(src_refs, dst_refs):
        dst[...] = src[...].astype(BF16)


def _pool_kernel(x_ref, mod_ref, g_ref, win_ref, band_ref, wgrp_ref, scale_ref, wout_ref,
                 fg_ref, fu_ref, fd_ref, pos_ref, freq_ref, gffn_ref,
                 o_ref, fg_out_ref, fu_out_ref, fd_out_ref, cos_ref, sin_ref, h_out_ref, uext_ref,
                 *, ts, gw):
    s = pl.program_id(1)

    @pl.when(s == 0)
    def _():
        uext_ref[0:POOL_HALO, :] = jnp.zeros((POOL_HALO, uext_ref.shape[1]), BF16)

    x = x_ref[0]
    mod = mod_ref[0]
    h = (_rms_scale(x) * g_ref[...]) * (1.0 + mod[1:2]) + mod[0:1]
    u = _dot(h.astype(BF16), win_ref[...])
    uext_ref[POOL_HALO:POOL_HALO + ts, :] = u.astype(BF16)

    t = s * ts + lax.broadcasted_iota(jnp.int32, (ts, 1), 0)
    zs = []
    for g, w in enumerate(POOL_WINDOWS):
        cols = slice(g * gw, (g + 1) * gw)
        rb = band_ref.shape[1]
        wsum = jnp.concatenate(
            [_dot(band_ref[g], uext_ref[r0:r0 + rb + POOL_HALO, cols]) for r0 in range(0, ts, rb)],
            axis=0)
        cnt = jnp.minimum(t + 1, w).astype(F32)
        p = wsum / cnt - u[:, cols]
        zs.append(_dot(p.astype(BF16), wgrp_ref[g]))
    z = jnp.concatenate(zs, axis=-1) * scale_ref[...]
    y = _dot(z.astype(BF16), wout_ref[...])
    x_out = x + mod[2:3] * y
    o_ref[0] = x_out
    h_out_ref[0] = ((_rms_scale(x_out) * gffn_ref[...]) * (1.0 + mod[4:5]) + mod[3:4]).astype(BF16)

    uext_ref[0:POOL_HALO, :] = uext_ref[ts:ts + POOL_HALO, :]

    _cast_side_job((fg_ref, fu_ref, fd_ref), (fg_out_ref, fu_out_ref, fd_out_ref))
    _rope_table_side_job(pos_ref, freq_ref, cos_ref, sin_ref)


def _pool_mixer(x, mod, g, w_in, band, w_grp, scale, w_out, fg, fu, fd, pos_f, freq_row, g_ffn,
                ts):
    b, s, d = x.shape
    ng, gw, _ = w_grp.shape
    n_s = s // ts
    gate_up, down = _cast_specs(1, d, fg.shape[2], b * n_s, lambda i, j: i * n_s + j)
    const2 = lambda i, j: (0, 0)
    const3 = lambda i, j: (0, 0, 0)
    return pl.pallas_call(
        functools.partial(_pool_kernel, ts=ts, gw=gw),
        grid=(b, s // ts),
        in_specs=[
            pl.BlockSpec((1, ts, d), lambda i, j: (i, j, 0)),
            pl.BlockSpec((1, 6, d), lambda i, j: (i, 0, 0)),
            pl.BlockSpec((1, d), const2),
            pl.BlockSpec((d, d), const2),
            pl.BlockSpec(band.shape, const3),
            pl.BlockSpec((ng, gw, gw), const3),
            pl.BlockSpec((1, d), const2),
            pl.BlockSpec((d, d), const2),
            gate_up, gate_up, down,
            pl.BlockSpec((1, 1, ts), lambda i, j: (i, 0, j)),
            pl.BlockSpec((ROPE_HALF, 1), const2),
            pl.BlockSpec((1, d), const2),
        ],
        out_specs=[pl.BlockSpec((1, ts, d), lambda i, j: (i, j, 0)), gate_up, gate_up, down,
                   pl.BlockSpec((1, 1, ROPE_HALF, ts), lambda i, j: (i, j, 0, 0)),
                   pl.BlockSpec((1, 1, ROPE_HALF, ts), lambda i, j: (i, j, 0, 0)),
                   pl.BlockSpec((1, ts, d), lambda i, j: (i, j, 0))],
        out_shape=[jax.ShapeDtypeStruct((b, s, d), F32),
                   jax.ShapeDtypeStruct(fg.shape, BF16),
                   jax.ShapeDtypeStruct(fu.shape, BF16),
                   jax.ShapeDtypeStruct(fd.shape, BF16),
                   jax.ShapeDtypeStruct((b, s // ts, ROPE_HALF, ts), F32),
                   jax.ShapeDtypeStruct((b, s // ts, ROPE_HALF, ts), F32),
                   jax.ShapeDtypeStruct((b, s, d), BF16)],
        scratch_shapes=[pltpu.VMEM((POOL_HALO + ts, d), BF16)],
        compiler_params=_cparams(("arbitrary", "arbitrary")),
        name="pool_mixer",
    )(x, mod, g, w_in, band, w_grp, scale, w_out, fg, fu, fd, pos_f, freq_row, g_ffn)


def _ffn_kernel(x_ref, h_ref, mod_ref, wg_ref, wu_ref, wd_ref, o_ref, *, tf):
    h = h_ref[...]
    ff = wg_ref.shape[2]
    acc = jnp.zeros(x_ref.shape, F32)
    for f0 in range(0, ff, tf):
        gate = _dot(h, wg_ref[0, :, f0:f0 + tf])
        up = _dot(h, wu_ref[0, :, f0:f0 + tf])
        a = (_silu(gate) * up).astype(BF16)
        acc = acc + _dot(a, wd_ref[0, f0:f0 + tf, :])
    o_ref[...] = x_ref[...] + mod_ref[0][5:6] * acc


def _dense_ffn(x2, h2, mod, wg, wu, wd, tm, tf, rows_per_batch):
    n, d = x2.shape
    tpb = rows_per_batch // tm
    resident = lambda a: pl.BlockSpec(a.shape, lambda i: (0, 0, 0), pipeline_mode=pl.Buffered(1))
    return pl.pallas_call(
        functools.partial(_ffn_kernel, tf=tf),
        grid=(n // tm,),
        in_specs=[
            pl.BlockSpec((tm, d), lambda i: (i, 0)),
            pl.BlockSpec((tm, d), lambda i: (i, 0)),
            pl.BlockSpec((1, 6, d), lambda i: (i // tpb, 0, 0)),
            resident(wg), resident(wu), resident(wd),
        ],
        out_specs=pl.BlockSpec((tm, d), lambda i: (i, 0)),
        out_shape=jax.ShapeDtypeStruct((n, d), F32),
        compiler_params=_cparams(("arbitrary",)),
        name="dense_ffn",
    )(x2, h2, mod, wg, wu, wd)


def _dot_nt(a, b):
    return lax.dot_general(a, b, (((1,), (1,)), ((), ())), preferred_element_type=F32)


def _qkv_kernel(x_ref, mod_ref, kvmod_ref, gmix_ref, gkv_ref, wdq_ref, qg_ref, wqn_t_ref,
                wqr_t_ref, wdkv_ref, ckvg_ref, wkr_ref, wuk_ref, wuv_t_ref, cos_ref, sin_ref,
                qt_ref, k_ref, vt_ref):
    x = x_ref[0]
    xn = _rms_scale(x)
    mod = mod_ref[0]
    kvmod = kvmod_ref[0]
    hq = ((xn * gmix_ref[...]) * (1.0 + mod[1:2]) + mod[0:1]).astype(BF16)
    hk = ((xn * gkv_ref[...]) * (1.0 + kvmod[1:2]) + kvmod[0:1]).astype(BF16)

    cq_raw = _dot(hq, wdq_ref[...])
    ckv_raw = _dot(hk, wdkv_ref[...])
    kr_raw = _dot(hk, wkr_ref[...])

    cq = (_rms_scale(cq_raw) * qg_ref[...]).astype(BF16)
    ckv = (_rms_scale(ckv_raw) * ckvg_ref[...]).astype(BF16)
    qn_t = _dot_nt(wqn_t_ref[...], cq) * Q_SCALE
    qr_t = _dot_nt(wqr_t_ref[...], cq) * Q_SCALE
    cos_t = cos_ref[0, 0]
    sin_t = sin_ref[0, 0]
    zero_rows = jnp.zeros((HEAD_PAD - LANES - D_ROPE, x.shape[0]), BF16)

    kn = _dot(ckv, wuk_ref[...])
    vt_ref[0, 0] = _dot_nt(wuv_t_ref[...], ckv).astype(BF16)
    kr_raw_t = kr_raw.T
    k1, k2 = _rope_t(kr_raw_t[:D_ROPE], cos_t, sin_t)
    kr_t = jnp.concatenate([k1, k2, kr_raw_t[D_ROPE:]], axis=0)
    kr = kr_t.T.astype(BF16)

    for h in range(N_HEADS):
        lo = h * HEAD_PAD
        hs = slice(h * LANES, (h + 1) * LANES)
        qt_ref[0, 0, lo:lo + LANES, :] = qn_t[hs].astype(BF16)
        q1, q2 = _rope_t(qr_t[h * D_ROPE:(h + 1) * D_ROPE], cos_t, sin_t)
        ro = lo + LANES
        qt_ref[0, 0, ro:ro + ROPE_HALF, :] = q1.astype(BF16)
        qt_ref[0, 0, ro + ROPE_HALF:ro + D_ROPE, :] = q2.astype(BF16)
        qt_ref[0, 0, ro + D_ROPE:lo + HEAD_PAD, :] = zero_rows
        k_ref[0, :, lo:lo + LANES] = kn[:, hs].astype(BF16)
        k_ref[0, :, lo + LANES:lo + HEAD_PAD] = kr


def _qkv_proj(x, mod, kvmod, gmix, gkv, wdq, qg, wqn_t, wqr_t, wdkv, ckvg, wkr, wuk, wuv_t,
              cos_t, sin_t, ts):
    b, s, d = x.shape
    row = lambda i, j: (i, j, 0)
    tile = lambda i, j: (i, j, 0, 0)
    per_b = lambda i, j: (i, 0, 0)

    def full(a):
        return pl.BlockSpec(a.shape, lambda i, j: (0,) * a.ndim)

    weights = (gmix, gkv, wdq, qg, wqn_t, wqr_t, wdkv, ckvg, wkr, wuk, wuv_t)
    return pl.pallas_call(
        _qkv_kernel,
        grid=(b, s // ts),
        in_specs=[pl.BlockSpec((1, ts, d), row),
                  pl.BlockSpec((1, 6, d), per_b),
                  pl.BlockSpec((1, 2, d), per_b)]
                 + [full(a) for a in weights]
                 + [pl.BlockSpec((1, 1, ROPE_HALF, ts),
                                 lambda i, j: (i, j * ts // cos_t.shape[3], 0,
                                               j % (cos_t.shape[3] // ts)))] * 2,
        out_specs=[pl.BlockSpec((1, 1, N_HEADS * HEAD_PAD, ts), tile),
                   pl.BlockSpec((1, ts, N_HEADS * HEAD_PAD), row),
                   pl.BlockSpec((1, 1, N_HEADS * D_V, ts), tile)],
        out_shape=[jax.ShapeDtypeStruct((b, s // ts, N_HEADS * HEAD_PAD, ts), BF16),
                   jax.ShapeDtypeStruct((b, s, N_HEADS * HEAD_PAD), BF16),
                   jax.ShapeDtypeStruct((b, s // ts, N_HEADS * D_V, ts), BF16)],
        compiler_params=_cparams(("arbitrary", "arbitrary")),
        name="qkv_proj",
    )(x, mod, kvmod, *weights, cos_t, sin_t)


def _attn_kernel(qt_ref, k_ref, vt_ref, eg_ref, eu_ref, ed_ref,
                 o_ref, eg_out_ref, eu_out_ref, ed_out_ref, *, tq, tk):
    ts = qt_ref.shape[3]
    s = k_ref.shape[1]
    key = lax.broadcasted_iota(jnp.int32, (tk, tq), 0)
    qry = lax.broadcasted_iota(jnp.int32, (tk, tq), 1)
    last_block = lambda qi: ((qi + 1) * tq - 1) // tk
    blocks = [(qi, j) for qi in range(s // tq) for j in range(last_block(qi) + 1)]

    def tile(ref, start, size):
        return ref[0, start // ts, :, start % ts:start % ts + size]

    def scores_t(qi, j):
        st = _dot(k_ref[0, j * tk:(j + 1) * tk, :], tile(qt_ref, qi * tq, tq))
        if (j + 1) * tk - 1 > qi * tq:
            st = jnp.where(key + (j * tk - qi * tq) <= qry, st, -jnp.inf)
        return st

    ahead = [scores_t(*blk) for blk in blocks[:ATTN_LOOKAHEAD]]
    for n, (qi, j) in enumerate(blocks):
        st = ahead.pop(0)
        if n + ATTN_LOOKAHEAD < len(blocks):
            ahead.append(scores_t(*blocks[n + ATTN_LOOKAHEAD]))
        if j == 0:
            m = jnp.full((1, tq), -jnp.inf, F32)
            l = jnp.zeros((1, tq), F32)
            acc = jnp.zeros((D_V, tq), F32)
        m_new = jnp.maximum(m, jnp.max(st, axis=0, keepdims=True))
        alpha = jnp.exp2(m - m_new)
        pt = jnp.exp2(st - m_new)
        l = alpha * l + jnp.sum(pt, axis=0, keepdims=True)
        acc = alpha * acc + _dot(tile(vt_ref, j * tk, tk), pt.astype(BF16))
        m = m_new
        if j == last_block(qi):
            o_ref[0, qi * tq:(qi + 1) * tq, :] = (acc / l).T.astype(BF16)

    _cast_side_job((eg_ref, eu_ref, ed_ref), (eg_out_ref, eu_out_ref, ed_out_ref))


def _attention(qt, k, vt, eg, eu, ed, ts):
    b, s, _ = k.shape
    n_exp, d, eff = eg.shape
    gate_up, down = _cast_specs(n_exp, d, eff, b * N_HEADS, lambda i, h: i * N_HEADS + h)
    return pl.pallas_call(
        functools.partial(_attn_kernel, tq=min(ts, ATTN_QUERY_TILE), tk=min(ts, ATTN_KEY_BLOCK)),
        grid=(b, N_HEADS),
        in_specs=[
            pl.BlockSpec((1, s // ts, HEAD_PAD, ts), lambda i, h: (i, 0, h, 0)),
            pl.BlockSpec((1, s, HEAD_PAD), lambda i, h: (i, 0, h)),
            pl.BlockSpec((1, s // ts, D_V, ts), lambda i, h: (i, 0, h, 0)),
            gate_up, gate_up, down,
        ],
        out_specs=[pl.BlockSpec((1, s, D_V), lambda i, h: (i, 0, h)), gate_up, gate_up, down],
        out_shape=[jax.ShapeDtypeStruct((b, s, N_HEADS * D_V), BF16),
                   jax.ShapeDtypeStruct(eg.shape, BF16),
                   jax.ShapeDtypeStruct(eu.shape, BF16),
                   jax.ShapeDtypeStruct(ed.shape, BF16)],
        compiler_params=_cparams(("arbitrary", "arbitrary")),
        name="flash_attn",
    )(qt, k, vt, eg, eu, ed)


def _attn_out_kernel(o_ref, x_ref, mod_ref, wo_ref, g_ref, wr_ref, ltri_ref,
                     x_out_ref, h_out_ref, rcol_ref, rrow_ref, cnt_ref, carry_ref,
                     *, n_exp, tiles_per_block):
    i = pl.program_id(0)

    @pl.when(i % tiles_per_block == 0)
    def _():
        carry_ref[...] = jnp.zeros_like(carry_ref)

    mod = mod_ref[0]
    x = x_ref[...] + mod[2:3] * _dot(o_ref[...], wo_ref[...])
    x_out_ref[...] = x
    h = (_rms_scale(x) * g_ref[...]) * (1.0 + mod[4:5]) + mod[3:4]
    h_hi = h.astype(BF16)
    h_out_ref[...] = h_hi

    tm = x.shape[0]
    h_lo = (h - h_hi.astype(F32)).astype(BF16)
    parts = _dot(jnp.concatenate([h_hi, h_lo], axis=0), wr_ref[...])
    logits = (parts[:tm, :LANES] + parts[:tm, LANES:]) + (parts[tm:, :LANES] + parts[tm:, LANES:])
    lane = lax.broadcasted_iota(jnp.int32, (tm, LANES), 1)
    lane_f = lane.astype(F32)

    def first_argmax(vals):
        m = jnp.max(vals, axis=-1, keepdims=True)
        idx = jnp.min(jnp.where(vals == m, lane_f, float(LANES)), axis=-1, keepdims=True)
        return m, idx.astype(jnp.int32)

    logits = jnp.where(lane < n_exp, logits, -jnp.inf)
    m1, i1 = first_argmax(logits)
    m2, i2 = first_argmax(jnp.where(lane == i1, -jnp.inf, logits))
    ex = jnp.exp(m2 - m1)
    w1 = 1.0 / (1.0 + ex)
    w2 = ex / (1.0 + ex)

    mask = jnp.where((lane == i1) | (lane == i2), 1.0, 0.0)
    ranks = _dot(ltri_ref[...], mask.astype(BF16)) + carry_ref[0:1]
    incl = ranks + mask
    rank1 = jnp.sum(jnp.where(lane == i1, ranks, 0.0), axis=-1, keepdims=True)
    rank2 = jnp.sum(jnp.where(lane == i2, ranks, 0.0), axis=-1, keepdims=True)
    carry_ref[0:1] = incl[tm - 1:tm]
    n_sub = tm // rrow_ref.shape[2]
    sub = rrow_ref.shape[2]
    cnt_ref[0] = jnp.zeros(cnt_ref.shape[1:], F32)
    for j in range(n_sub):
        cnt_ref[0, j:j + 1] = incl[(j + 1) * sub - 1:(j + 1) * sub]

    fields = (i1.astype(F32), i2.astype(F32), rank1, rank2, w1, w2)
    rcol = jnp.zeros((tm, LANES), F32)
    for k, val in enumerate(fields):
        rcol = jnp.where(lane == k, val, rcol)
    rcol_ref[...] = rcol
    rrow = rcol.T
    for j in range(n_sub):
        rrow_ref[j] = rrow[0:ROUTE_FIELDS, j * sub:(j + 1) * sub]


def _attn_out(o2, x2, mod, wo, g, wr, n_exp, ltri, tm, sub, rows_per_block):
    n, d = x2.shape
    tpb = rows_per_block // tm
    n_sub = tm // sub
    return pl.pallas_call(
        functools.partial(_attn_out_kernel, n_exp=n_exp, tiles_per_block=tpb),
        grid=(n // tm,),
        in_specs=[
            pl.BlockSpec((tm, o2.shape[1]), lambda i: (i, 0)),
            pl.BlockSpec((tm, d), lambda i: (i, 0)),
            pl.BlockSpec((1, 6, d), lambda i: (i // tpb, 0, 0)),
            pl.BlockSpec(wo.shape, lambda i: (0, 0)),
            pl.BlockSpec((1, d), lambda i: (0, 0)),
            pl.BlockSpec(wr.shape, lambda i: (0, 0)),
            pl.BlockSpec((tm, tm), lambda i: (0, 0)),
        ],
        out_specs=[pl.BlockSpec((tm, d), lambda i: (i, 0)),
                   pl.BlockSpec((tm, d), lambda i: (i, 0)),
                   pl.BlockSpec((tm, LANES), lambda i: (i, 0)),
                   pl.BlockSpec((n_sub, ROUTE_FIELDS, sub), lambda i: (i, 0, 0)),
                   pl.BlockSpec((1, 8, LANES), lambda i: (i, 0, 0))],
        out_shape=[jax.ShapeDtypeStruct((n, d), F32),
                   jax.ShapeDtypeStruct((n, d), BF16),
                   jax.ShapeDtypeStruct((n, LANES), F32),
                   jax.ShapeDtypeStruct((n // sub, ROUTE_FIELDS, sub), F32),
                   jax.ShapeDtypeStruct((n // tm, 8, LANES), F32)],
        scratch_shapes=[pltpu.VMEM((8, LANES), F32)],
        compiler_params=_cparams(("arbitrary",)),
        name="attn_out_router",
    )(o2, x2, mod, wo, g, wr, ltri)


def _dispatch_kernel(off_ref, lo_ref, hi_ref, cexp_ref, h_ref, rrow_ref, xg_ref, gs_ref,
                     *, n_exp, sub, chunks_per_step, chunks_per_block):
    blk = pl.program_id(0)
    g = pl.program_id(1)
    d = h_ref.shape[1]
    n_sub = rrow_ref.shape[0]
    win = min(DISPATCH_WINDOW, n_sub)
    iota = lax.broadcasted_iota(jnp.int32, (DISPATCH_CHUNK, 1), 0)

    def contribution(info, sb):
        row, exp_id, exp_off = info
        r = rrow_ref[sb]
        first = r[0:1] == exp_id
        second = r[1:2] == exp_id
        pos = jnp.where(first, r[2:3] + exp_off, jnp.where(second, r[3:4] + exp_off, -1.0))
        hit = pos == row
        onehot = jnp.where(hit, 1.0, 0.0).astype(BF16)
        tok = h_ref[pl.ds(pl.multiple_of(sb * sub, sub), sub), :]
        gate = jnp.sum(jnp.where(hit, jnp.where(first, r[4:5], r[5:6]), 0.0),
                       axis=-1, keepdims=True)
        return _dot(onehot, tok), gate

    def chunk_info(j):
        c = g * chunks_per_step + j
        t = blk * chunks_per_block + c
        row = (c * DISPATCH_CHUNK + iota).astype(F32)
        exp_id = cexp_ref[t]
        exp_off = off_ref[blk * n_exp + exp_id].astype(F32)
        start = jnp.minimum(lo_ref[t], n_sub - win)
        return (row, exp_id.astype(F32), exp_off), start, hi_ref[t]

    first_chunk = blk * chunks_per_block + g * chunks_per_step
    used = functools.reduce(jnp.logical_or,
                            [hi_ref[first_chunk + j] >= 0 for j in range(chunks_per_step)])

    @pl.when(jnp.logical_not(used))
    def _():
        xg_ref[...] = jnp.zeros(xg_ref.shape, BF16)
        gs_ref[...] = jnp.zeros(gs_ref.shape, F32)

    @pl.when(used)
    def _():
        for j in range(chunks_per_step):
            info, start, _ = chunk_info(j)
            acc = jnp.zeros((DISPATCH_CHUNK, d), F32)
            gate = jnp.zeros((DISPATCH_CHUNK, 1), F32)
            for k in range(win):
                a, b = contribution(info, start + k)
                acc += a
                gate += b
            rows = slice(j * DISPATCH_CHUNK, (j + 1) * DISPATCH_CHUNK)
            xg_ref[rows, :] = acc.astype(BF16)
            gs_ref[rows, :] = gate

        def extra(j, _):
            info, start, last = chunk_info(j)
            rows = pl.ds(pl.multiple_of(j * DISPATCH_CHUNK, DISPATCH_CHUNK), DISPATCH_CHUNK)

            def add(sb, _):
                a, b = contribution(info, sb)
                xg_ref[rows, :] = (xg_ref[rows, :].astype(F32) + a).astype(BF16)
                gs_ref[rows, :] = gs_ref[rows, :] + b
                return 0

            return lax.fori_loop(start + win, last + 1, add, 0)

        lax.fori_loop(0, chunks_per_step, extra, 0)


def _dispatch(off, lo, hi, cexp, h, rrow, n_blocks, rows_per_block, rcap, n_exp, sub):
    n, d = h.shape
    n_sub = rows_per_block // sub
    step_rows = _pick(rcap, DISPATCH_STEP_ROWS)
    cps = step_rows // DISPATCH_CHUNK
    cpb = rcap // DISPATCH_CHUNK
    grid_spec = pltpu.PrefetchScalarGridSpec(
        num_scalar_prefetch=4,
        grid=(n_blocks, rcap // step_rows),
        in_specs=[
            pl.BlockSpec((rows_per_block, d), lambda b, g, *_: (b, 0)),
            pl.BlockSpec((n_sub, ROUTE_FIELDS, sub), lambda b, g, *_: (b, 0, 0)),
        ],
        out_specs=[
            pl.BlockSpec((step_rows, d), lambda b, g, *_: (b * (rcap // step_rows) + g, 0)),
            pl.BlockSpec((step_rows, 1), lambda b, g, *_: (b * (rcap // step_rows) + g, 0)),
        ],
    )
    return pl.pallas_call(
        functools.partial(_dispatch_kernel, n_exp=n_exp, sub=sub, chunks_per_step=cps,
                          chunks_per_block=cpb),
        grid_spec=grid_spec,
        out_shape=[jax.ShapeDtypeStruct((n_blocks * rcap, d), BF16),
                   jax.ShapeDtypeStruct((n_blocks * rcap, 1), F32)],
        compiler_params=_cparams(("arbitrary", "arbitrary")),
        name="moe_dispatch",
    )(off, lo, hi, cexp, h, rrow)


def _expert_ffn_kernel(perm_ref, pexp_ref, pfill_ref, x_ref, gs_ref, wg_ref, wu_ref, wd_ref,
                       y_ref, *, tf):
    w = pl.program_id(0)
    fill = pfill_ref[w]
    ff = wg_ref.shape[2]

    def ffn(n_rows):
        x = x_ref[0:n_rows, :]
        gs = gs_ref[0:n_rows, :]
        acc = jnp.zeros((n_rows, y_ref.shape[1]), F32)
        for f0 in range(0, ff, tf):
            gate = _dot(x, wg_ref[0, :, f0:f0 + tf])
            up = _dot(x, wu_ref[0, :, f0:f0 + tf])
            a = ((_silu(gate) * up) * gs).astype(BF16)
            acc = acc + _dot(a, wd_ref[0, f0:f0 + tf, :])
        return acc.astype(BF16)

    for level in range(PIECE // PIECE_QUANTUM + 1):
        n_rows = level * PIECE_QUANTUM

        @pl.when(fill == level)
        def _(n_rows=n_rows):
            if n_rows:
                y_ref[0:n_rows, :] = ffn(n_rows)
            if n_rows < PIECE:
                y_ref[n_rows:PIECE, :] = jnp.zeros((PIECE - n_rows, y_ref.shape[1]), BF16)


def _expert_ffn(perm, pexp, pfill, xg, gs, wg, wu, wd, tf):
    rows, d = xg.shape
    n_exp, _, ff = wg.shape
    grid_spec = pltpu.PrefetchScalarGridSpec(
        num_scalar_prefetch=3,
        grid=(rows // PIECE,),
        in_specs=[
            pl.BlockSpec((PIECE, d), lambda w, perm, pexp, pfill: (perm[w], 0)),
            pl.BlockSpec((PIECE, 1), lambda w, perm, pexp, pfill: (perm[w], 0)),
            pl.BlockSpec((1, d, ff), lambda w, perm, pexp, pfill: (pexp[w], 0, 0)),
            pl.BlockSpec((1, d, ff), lambda w, perm, pexp, pfill: (pexp[w], 0, 0)),
            pl.BlockSpec((1, ff, d), lambda w, perm, pexp, pfill: (pexp[w], 0, 0)),
        ],
        out_specs=pl.BlockSpec((PIECE, d), lambda w, perm, pexp, pfill: (perm[w], 0)),
    )
    return pl.pallas_call(
        functools.partial(_expert_ffn_kernel, tf=tf),
        grid_spec=grid_spec,
        out_shape=jax.ShapeDtypeStruct((rows, d), BF16),
        compiler_params=_cparams(("arbitrary",)),
        name="expert_ffn",
    )(perm, pexp, pfill, xg, gs, wg, wu, wd)


def _combine_kernel(off_ref, start_ref, end_ref, yg_ref, rcol_ref, x_ref, mod_ref, gfin_ref, o_ref,
                    acc_ref, *, n_exp, sub, subs_per_block):
    blk = pl.program_id(0)
    t = pl.program_id(1)
    tm, d = x_ref.shape
    col = lax.broadcasted_iota(jnp.int32, (1, COMBINE_WINDOW), 1)
    align = COMBINE_WINDOW // 2

    def token_info(s):
        rc = rcol_ref[s * sub:(s + 1) * sub, :]
        return tuple(jnp.broadcast_to(rc[:, k:k + 1], (sub, COMBINE_WINDOW)) for k in range(4))

    def gathered(info, e, w0):
        i1, i2, rank1, rank2 = info
        rank = jnp.where(i1 == e, rank1, jnp.where(i2 == e, rank2, -float(COMBINE_WINDOW)))
        in_region = (w0 - off_ref[blk * n_exp + e] + col).astype(F32)
        onehot = jnp.where(rank == in_region, 1.0, 0.0).astype(BF16)
        return _dot(onehot, yg_ref[pl.ds(pl.multiple_of(w0, align), COMBINE_WINDOW), :])

    def first_window(s, e):
        idx = (blk * subs_per_block + t * (tm // sub) + s) * n_exp + e
        w0 = lax.shift_right_logical(start_ref[idx], align.bit_length() - 1) * align
        return w0, end_ref[idx]

    for s in range(tm // sub):
        info = token_info(s)
        acc = jnp.zeros((sub, d), F32)
        for e in range(n_exp):
            acc += gathered(info, e, first_window(s, e)[0])
        acc_ref[s * sub:(s + 1) * sub, :] = acc

    for s in range(tm // sub):
        info = token_info(s)
        for e in range(n_exp):
            w0, end = first_window(s, e)
            n_more = lax.shift_right_logical(
                jnp.maximum(end - w0 - 1, 0), COMBINE_WINDOW.bit_length() - 1)

            def add(k, _, info=info, e=e, w0=w0, s=s):
                rows = slice(s * sub, (s + 1) * sub)
                acc_ref[rows, :] = acc_ref[rows, :] + gathered(info, e, w0 + k * COMBINE_WINDOW)
                return 0

            lax.fori_loop(1, n_more + 1, add, 0)

    x = x_ref[...] + mod_ref[0][5:6] * acc_ref[...]
    o_ref[...] = _rms_scale(x) * gfin_ref[...]


def _combine(off, row_start, row_end, yg, rcol, x2, mod, gfin, n_blocks, rows_per_block, rcap,
             n_exp, sub, tm):
    n, d = x2.shape
    tpb = rows_per_block // tm
    grid_spec = pltpu.PrefetchScalarGridSpec(
        num_scalar_prefetch=3,
        grid=(n_blocks, tpb),
        in_specs=[
            pl.BlockSpec((rcap, d), lambda b, t, *_: (b, 0), pipeline_mode=pl.Buffered(1)),
            pl.BlockSpec((tm, LANES), lambda b, t, *_: (b * tpb + t, 0)),
            pl.BlockSpec((tm, d), lambda b, t, *_: (b * tpb + t, 0)),
            pl.BlockSpec((1, 6, d), lambda b, t, *_: (b, 0, 0)),
            pl.BlockSpec((1, d), lambda b, t, *_: (0, 0)),
        ],
        out_specs=pl.BlockSpec((tm, d), lambda b, t, *_: (b * tpb + t, 0)),
        scratch_shapes=[pltpu.VMEM((tm, d), F32)],
    )
    return pl.pallas_call(
        functools.partial(_combine_kernel, n_exp=n_exp, sub=sub,
                          subs_per_block=rows_per_block // sub),
        grid_spec=grid_spec,
        out_shape=jax.ShapeDtypeStruct((n, d), F32),
        compiler_params=_cparams(("arbitrary", "arbitrary")),
        name="moe_combine",
    )(off, row_start, row_end, yg, rcol, x2, mod, gfin)


def _routing_tables(cnt, n_blocks, n_sub, n_exp, rcap):
    cum_incl = cnt.astype(jnp.int32)
    cum_excl = jnp.concatenate([jnp.zeros_like(cum_incl[:, :1]), cum_incl[:, :-1]], axis=1)
    total = cum_incl[:, -1]
    psz = (total + PIECE - 1) // PIECE * PIECE
    off = jnp.cumsum(psz, axis=1) - psz
    start = off[:, None, :] + cum_excl
    end = off[:, None, :] + cum_incl

    c0 = jnp.arange(rcap // DISPATCH_CHUNK, dtype=jnp.int32)[None, :, None, None] * DISPATCH_CHUNK
    hit = ((start[:, None] < c0 + DISPATCH_CHUNK) & (end[:, None] > c0)
           & (end[:, None] > start[:, None])).any(-1)
    sub_ids = jnp.arange(n_sub, dtype=jnp.int32)
    d_lo = jnp.min(jnp.where(hit, sub_ids, n_sub), axis=-1)
    d_hi = jnp.max(jnp.where(hit, sub_ids, -1), axis=-1)
    d_lo = jnp.where(d_hi < 0, 0, d_lo)

    ppb = rcap // PIECE
    prow = jnp.arange(ppb, dtype=jnp.int32)[None, :, None] * PIECE
    inside = (prow >= off[:, None, :]) & (prow < (off + psz)[:, None, :])
    p_exp = jnp.argmax(inside, axis=-1).astype(jnp.int32)
    used = inside.any(-1)
    left = jnp.take_along_axis(total + off, p_exp, axis=1) - prow[..., 0]
    p_fill = jnp.where(used, jnp.clip((left + PIECE_QUANTUM - 1) // PIECE_QUANTUM, 0,
                                      PIECE // PIECE_QUANTUM), 0).astype(jnp.int32)
    key = jnp.where(p_fill > 0, p_exp, n_exp).reshape(-1)
    perm = jnp.argsort(key, stable=True).astype(jnp.int32)
    p_exp_sorted = jnp.minimum(key[perm], n_exp - 1).astype(jnp.int32)
    d_exp = jnp.repeat(p_exp, PIECE // DISPATCH_CHUNK, axis=1).reshape(-1)
    return (off.reshape(-1).astype(jnp.int32), d_lo.reshape(-1), d_hi.reshape(-1), d_exp,
            start.reshape(-1).astype(jnp.int32), end.reshape(-1).astype(jnp.int32),
            perm, p_exp_sorted, p_fill.reshape(-1)[perm])


def _pick(n, pref):
    t = min(n, pref)
    while n % t:
        t //= 2
    return t


def kernel(x, c, positions, ada_w, ada_b, mix_norm_g, ffn_norm_g, pool_w_in, pool_w_grp,
           pool_scale, pool_w_out, kv_ada_w, kv_ada_b, kv_norm_g, w_dkv, ckv_norm_g, w_kr,
           w_uk, w_uv, w_dq, q_norm_g, w_uq, w_o, ffn_w_gate, ffn_w_up, ffn_w_down,
           router_w, moe_w_gate, moe_w_up, moe_w_down, final_norm_g):
    b, s, d = x.shape
    n = b * s
    depth = ada_w.shape[0]
    assert depth == 2 and pool_w_in.shape[0] == 1 and w_dq.shape[0] == 1
    assert b <= 8 and d % LANES == 0

    ts = _pick(s, 512)
    tp = _pick(s, 1024)
    ta = _pick(s, 1024)
    tc = _pick(s, 512)
    tm = _pick(s, 1024)
    tf = _pick(ffn_w_gate.shape[-1], 512)

    c_pad = jnp.zeros((8, d), F32).at[:b].set(c)
    mods = _adaln(c_pad, ada_w, ada_b[:, None, :], _pick(6 * d, 1536))
    mods = mods[:, :b].reshape(depth, b, 6, d)
    kvmod = _adaln(c_pad, kv_ada_w[None], kv_ada_b[None, None, :], _pick(2 * d, 1024))
    kvmod = kvmod[0, :b].reshape(b, 2, d)

    r = jnp.arange(min(tp, POOL_ROW_BLOCK))[:, None]
    cidx = jnp.arange(min(tp, POOL_ROW_BLOCK) + POOL_HALO)[None, :] - POOL_HALO
    band = jnp.stack([((cidx <= r) & (cidx > r - w)) for w in POOL_WINDOWS]).astype(BF16)
    inv_freq = ROPE_THETA ** (-jnp.arange(0, D_ROPE, 2, dtype=F32) / D_ROPE)
    x1, fwg, fwu, fwd, rope_cos, rope_sin, h1 = _pool_mixer(
        x, mods[0], mix_norm_g[0:1], pool_w_in[0].astype(BF16), band, pool_w_grp[0].astype(BF16),
        pool_scale[0:1], pool_w_out[0].astype(BF16), ffn_w_gate, ffn_w_up, ffn_w_down,
        positions.astype(F32)[:, None, :], inv_freq[:, None], ffn_norm_g[0:1], tp)

    x2 = _dense_ffn(x1.reshape(n, d), h1.reshape(n, d), mods[0], fwg, fwu, fwd, tm, tf, s)


    q_lora = w_dq.shape[-1]
    wuq = w_uq[0].reshape(q_lora, N_HEADS, D_NOPE + D_ROPE)
    wqn_t = wuq[:, :, :D_NOPE].reshape(q_lora, N_HEADS * D_NOPE).T.astype(BF16)
    wqr_t = wuq[:, :, D_NOPE:].reshape(q_lora, N_HEADS * D_ROPE).T.astype(BF16)
    qt, k, vt = _qkv_proj(
        x2.reshape(b, s, d), mods[1], kvmod, mix_norm_g[1:2], kv_norm_g[None, :],
        w_dq[0].astype(BF16), q_norm_g[0:1], wqn_t, wqr_t, w_dkv.astype(BF16),
        ckv_norm_g[None, :], jnp.pad(w_kr, ((0, 0), (0, LANES - D_ROPE))).astype(BF16),
        w_uk.astype(BF16), w_uv.T.astype(BF16), rope_cos, rope_sin, ts)
    o, ewg, ewu, ewd = _attention(qt, k, vt, moe_w_gate[0], moe_w_up[0], moe_w_down[0], ts)

    n_exp = router_w.shape[-1]
    sub = _pick(s, ROUTE_SUB)
    n_sub = s // sub
    rcap = 2 * s + n_exp * PIECE
    ltri = (jnp.arange(ta)[None, :] < jnp.arange(ta)[:, None]).astype(BF16)
    assert n_exp <= LANES
    wr = jnp.pad(router_w[0], ((0, 0), (0, LANES - n_exp)))
    wr_hi = wr.astype(BF16)
    wr_split = jnp.concatenate([wr_hi, (wr - wr_hi.astype(F32)).astype(BF16)], axis=1)
    x3, h3, rcol, rrow, cnt = _attn_out(o.reshape(n, N_HEADS * D_V), x2, mods[1],
                                        w_o[0].astype(BF16), ffn_norm_g[1:2], wr_split, n_exp,
                                        ltri, ta, sub, s)
    cnt = cnt[:, :ta // sub, :n_exp].reshape(b, n_sub, n_exp)
    (off, d_lo, d_hi, d_exp, row_start, row_end, perm, p_exp,
     p_fill) = _routing_tables(cnt, b, n_sub, n_exp, rcap)
    xg, gs = _dispatch(off, d_lo, d_hi, d_exp, h3, rrow, b, s, rcap, n_exp, sub)
    yg = _expert_ffn(perm, p_exp, p_fill, xg, gs, ewg, ewu, ewd, tf)
    out = _combine(off, row_start, row_end, yg, rcol, x3, mods[1], final_norm_g[None, :], b, s,
                   rcap, n_exp, sub, tc)
    return out.reshape(b, s, d)
```

```python
import functools
import math

import jax
import jax.numpy as jnp
from jax import lax
from jax.experimental import pallas as pl
from jax.experimental.pallas import tpu as pltpu

F32 = jnp.float32
BF16 = jnp.bfloat16

N_HEADS = 8
D_NOPE = 128
D_ROPE = 64
D_V = 128
HEAD_PAD = 256
ROPE_HALF = D_ROPE // 2
ROPE_THETA = 10000.0
ATTN_SCALE = 1.0 / math.sqrt(D_NOPE + D_ROPE)
Q_SCALE = ATTN_SCALE * math.log2(math.e)
ATTN_QUERY_TILE = 512
ATTN_KEY_BLOCK = 512
ATTN_LOOKAHEAD = 3
POOL_WINDOWS = (2, 4, 8, 16)
POOL_HALO = 16
POOL_ROW_BLOCK = 128
EPS = 1e-6

ROUTE_SUB = 256
DISPATCH_CHUNK = 128
DISPATCH_WINDOW = 3
DISPATCH_STEP_ROWS = 1024
COMBINE_WINDOW = 256
PIECE = 512
PIECE_QUANTUM = 128
ROUTE_FIELDS = 8

LANES = 128
VMEM_LIMIT_BYTES = 56 * 1024 * 1024


def _cparams(sem):
    return pltpu.CompilerParams(dimension_semantics=sem, vmem_limit_bytes=VMEM_LIMIT_BYTES)


def _dot(a, b):
    return jnp.dot(a, b, preferred_element_type=F32)


def _rms_scale(x):
    return x * lax.rsqrt(jnp.mean(x * x, axis=-1, keepdims=True) + EPS)


def _silu(x):
    return x * jax.nn.sigmoid(x)


def _adaln_kernel(c_ref, w_ref, b_ref, o_ref):
    sc = _silu(c_ref[...])
    o_ref[0] = jnp.dot(sc, w_ref[0], preferred_element_type=F32,
                       precision=lax.Precision.HIGHEST) + b_ref[0]


def _adaln(c_pad, w, b, tn):
    nl, d, m = w.shape
    return pl.pallas_call(
        _adaln_kernel,
        grid=(nl, m // tn),
        in_specs=[
            pl.BlockSpec((8, d), lambda l, j: (0, 0)),
            pl.BlockSpec((1, d, tn), lambda l, j: (l, 0, j)),
            pl.BlockSpec((1, 1, tn), lambda l, j: (l, 0, j)),
        ],
        out_specs=pl.BlockSpec((1, 8, tn), lambda l, j: (l, 0, j)),
        out_shape=jax.ShapeDtypeStruct((nl, 8, m), F32),
        compiler_params=_cparams(("arbitrary", "arbitrary")),
        name="adaln",
    )(c_pad, w, b)


def _rope_table_side_job(pos_ref, freq_ref, cos_ref, sin_ref):
    ang = freq_ref[...] * pos_ref[0]
    cos_ref[0, 0] = jnp.cos(ang)
    sin_ref[0, 0] = jnp.sin(ang)


def _rope_t(x, cos_t, sin_t):
    x1, x2 = x[:ROPE_HALF], x[ROPE_HALF:]
    return x1 * cos_t - x2 * sin_t, x2 * cos_t + x1 * sin_t


def _cast_specs(n_exp, d, ff, steps, step_of):
    assert steps >= n_exp, "too few grid steps to carry the weight cast"
    k = max(c for c in range(1, steps // n_exp + 1) if ff % (c * LANES) == 0)
    cw = ff // k

    def chunk(*idx):
        c = jnp.minimum(step_of(*idx), n_exp * k - 1)
        return c // k, c % k

    gate_up = pl.BlockSpec((1, d, cw), lambda *idx: (chunk(*idx)[0], 0, chunk(*idx)[1]))
    down = pl.BlockSpec((1, cw, d), lambda *idx: (chunk(*idx)[0], chunk(*idx)[1], 0))
    return gate_up, down


def _cast_side_job(src_refs, dst_refs):
    for src, dst in zip(src_refs, dst_refs):
        dst[...] = src[...].astype(BF16)


def _pool_kernel(x_ref, mod_ref, g_ref, win_ref, band_ref, wgrp_ref, scale_ref, wout_ref,
                 fg_ref, fu_ref, fd_ref, pos_ref, freq_ref, gffn_ref,
                 o_ref, fg_out_ref, fu_out_ref, fd_out_ref, cos_ref, sin_ref, h_out_ref, uext_ref,
                 *, ts, gw):
    s = pl.program_id(1)

    @pl.when(s == 0)
    def _():
        uext_ref[0:POOL_HALO, :] = jnp.zeros((POOL_HALO, uext_ref.shape[1]), BF16)

    x = x_ref[0]
    mod = mod_ref[0]
    h = (_rms_scale(x) * g_ref[...]) * (1.0 + mod[1:2]) + mod[0:1]
    u = _dot(h.astype(BF16), win_ref[...])
    uext_ref[POOL_HALO:POOL_HALO + ts, :] = u.astype(BF16)

    t = s * ts + lax.broadcasted_iota(jnp.int32, (ts, 1), 0)
    zs = []
    for g, w in enumerate(POOL_WINDOWS):
        cols = slice(g * gw, (g + 1) * gw)
        rb = band_ref.shape[1]
        wsum = jnp.concatenate(
            [_dot(band_ref[g], uext_ref[r0:r0 + rb + POOL_HALO, cols]) for r0 in range(0, ts, rb)],
            axis=0)
        cnt = jnp.minimum(t + 1, w).astype(F32)
        p = wsum / cnt - u[:, cols]
        zs.append(_dot(p.astype(BF16), wgrp_ref[g]))
    z = jnp.concatenate(zs, axis=-1) * scale_ref[...]
    y = _dot(z.astype(BF16), wout_ref[...])
    x_out = x + mod[2:3] * y
    o_ref[0] = x_out
    h_out_ref[0] = ((_rms_scale(x_out) * gffn_ref[...]) * (1.0 + mod[4:5]) + mod[3:4]).astype(BF16)

    uext_ref[0:POOL_HALO, :] = uext_ref[ts:ts + POOL_HALO, :]

    _cast_side_job((fg_ref, fu_ref, fd_ref), (fg_out_ref, fu_out_ref, fd_out_ref))
    _rope_table_side_job(pos_ref, freq_ref, cos_ref, sin_ref)


def _pool_mixer(x, mod, g, w_in, band, w_grp, scale, w_out, fg, fu, fd, pos_f, freq_row, g_ffn,
                ts):
    b, s, d = x.shape
    ng, gw, _ = w_grp.shape
    n_s = s // ts
    gate_up, down = _cast_specs(1, d, fg.shape[2], b * n_s, lambda i, j: i * n_s + j)
    const2 = lambda i, j: (0, 0)
    const3 = lambda i, j: (0, 0, 0)
    return pl.pallas_call(
        functools.partial(_pool_kernel, ts=ts, gw=gw),
        grid=(b, s // ts),
        in_specs=[
            pl.BlockSpec((1, ts, d), lambda i, j: (i, j, 0)),
            pl.BlockSpec((1, 6, d), lambda i, j: (i, 0, 0)),
            pl.BlockSpec((1, d), const2),
            pl.BlockSpec((d, d), const2),
            pl.BlockSpec(band.shape, const3),
            pl.BlockSpec((ng, gw, gw), const3),
            pl.BlockSpec((1, d), const2),
            pl.BlockSpec((d, d), const2),
            gate_up, gate_up, down,
            pl.BlockSpec((1, 1, ts), lambda i, j: (i, 0, j)),
            pl.BlockSpec((ROPE_HALF, 1), const2),
            pl.BlockSpec((1, d), const2),
        ],
        out_specs=[pl.BlockSpec((1, ts, d), lambda i, j: (i, j, 0)), gate_up, gate_up, down,
                   pl.BlockSpec((1, 1, ROPE_HALF, ts), lambda i, j: (i, j, 0, 0)),
                   pl.BlockSpec((1, 1, ROPE_HALF, ts), lambda i, j: (i, j, 0, 0)),
                   pl.BlockSpec((1, ts, d), lambda i, j: (i, j, 0))],
        out_shape=[jax.ShapeDtypeStruct((b, s, d), F32),
                   jax.ShapeDtypeStruct(fg.shape, BF16),
                   jax.ShapeDtypeStruct(fu.shape, BF16),
                   jax.ShapeDtypeStruct(fd.shape, BF16),
                   jax.ShapeDtypeStruct((b, s // ts, ROPE_HALF, ts), F32),
                   jax.ShapeDtypeStruct((b, s // ts, ROPE_HALF, ts), F32),
                   jax.ShapeDtypeStruct((b, s, d), BF16)],
        scratch_shapes=[pltpu.VMEM((POOL_HALO + ts, d), BF16)],
        compiler_params=_cparams(("arbitrary", "arbitrary")),
        name="pool_mixer",
    )(x, mod, g, w_in, band, w_grp, scale, w_out, fg, fu, fd, pos_f, freq_row, g_ffn)


def _ffn_kernel(x_ref, h_ref, mod_ref, wg_ref, wu_ref, wd_ref, o_ref, *, tf):
    h = h_ref[...]
    ff = wg_ref.shape[2]
    acc = jnp.zeros(x_ref.shape, F32)
    for f0 in range(0, ff, tf):
        gate = _dot(h, wg_ref[0, :, f0:f0 + tf])
        up = _dot(h, wu_ref[0, :, f0:f0 + tf])
        a = (_silu(gate) * up).astype(BF16)
        acc = acc + _dot(a, wd_ref[0, f0:f0 + tf, :])
    o_ref[...] = x_ref[...] + mod_ref[0][5:6] * acc


def _dense_ffn(x2, h2, mod, wg, wu, wd, tm, tf, rows_per_batch):
    n, d = x2.shape
    tpb = rows_per_batch // tm
    resident = lambda a: pl.BlockSpec(a.shape, lambda i: (0, 0, 0), pipeline_mode=pl.Buffered(1))
    return pl.pallas_call(
        functools.partial(_ffn_kernel, tf=tf),
        grid=(n // tm,),
        in_specs=[
            pl.BlockSpec((tm, d), lambda i: (i, 0)),
            pl.BlockSpec((tm, d), lambda i: (i, 0)),
            pl.BlockSpec((1, 6, d), lambda i: (i // tpb, 0, 0)),
            resident(wg), resident(wu), resident(wd),
        ],
        out_specs=pl.BlockSpec((tm, d), lambda i: (i, 0)),
        out_shape=jax.ShapeDtypeStruct((n, d), F32),
        compiler_params=_cparams(("arbitrary",)),
        name="dense_ffn",
    )(x2, h2, mod, wg, wu, wd)


def _dot_nt(a, b):
    return lax.dot_general(a, b, (((1,), (1,)), ((), ())), preferred_element_type=F32)


def _qkv_kernel(x_ref, mod_ref, kvmod_ref, gmix_ref, gkv_ref, wdq_ref, qg_ref, wqn_t_ref,
                wqr_t_ref, wdkv_ref, ckvg_ref, wkr_ref, wuk_ref, wuv_t_ref, cos_ref, sin_ref,
                qt_ref, k_ref, vt_ref):
    x = x_ref[0]
    xn = _rms_scale(x)
    mod = mod_ref[0]
    kvmod = kvmod_ref[0]
    hq = ((xn * gmix_ref[...]) * (1.0 + mod[1:2]) + mod[0:1]).astype(BF16)
    hk = ((xn * gkv_ref[...]) * (1.0 + kvmod[1:2]) + kvmod[0:1]).astype(BF16)

    cq_raw = _dot(hq, wdq_ref[...])
    ckv_raw = _dot(hk, wdkv_ref[...])
    kr_raw = _dot(hk, wkr_ref[...])

    cq = (_rms_scale(cq_raw) * qg_ref[...]).astype(BF16)
    ckv = (_rms_scale(ckv_raw) * ckvg_ref[...]).astype(BF16)
    qn_t = _dot_nt(wqn_t_ref[...], cq) * Q_SCALE
    qr_t = _dot_nt(wqr_t_ref[...], cq) * Q_SCALE
    cos_t = cos_ref[0, 0]
    sin_t = sin_ref[0, 0]
    zero_rows = jnp.zeros((HEAD_PAD - LANES - D_ROPE, x.shape[0]), BF16)

    kn = _dot(ckv, wuk_ref[...])
    vt_ref[0, 0] = _dot_nt(wuv_t_ref[...], ckv).astype(BF16)
    kr_raw_t = kr_raw.T
    k1, k2 = _rope_t(kr_raw_t[:D_ROPE], cos_t, sin_t)
    kr_t = jnp.concatenate([k1, k2, kr_raw_t[D_ROPE:]], axis=0)
    kr = kr_t.T.astype(BF16)

    for h in range(N_HEADS):
        lo = h * HEAD_PAD
        hs = slice(h * LANES, (h + 1) * LANES)
        qt_ref[0, 0, lo:lo + LANES, :] = qn_t[hs].astype(BF16)
        q1, q2 = _rope_t(qr_t[h * D_ROPE:(h + 1) * D_ROPE], cos_t, sin_t)
        ro = lo + LANES
        qt_ref[0, 0, ro:ro + ROPE_HALF, :] = q1.astype(BF16)
        qt_ref[0, 0, ro + ROPE_HALF:ro + D_ROPE, :] = q2.astype(BF16)
        qt_ref[0, 0, ro + D_ROPE:lo + HEAD_PAD, :] = zero_rows
        k_ref[0, :, lo:lo + LANES] = kn[:, hs].astype(BF16)
        k_ref[0, :, lo + LANES:lo + HEAD_PAD] = kr


def _qkv_proj(x, mod, kvmod, gmix, gkv, wdq, qg, wqn_t, wqr_t, wdkv, ckvg, wkr, wuk, wuv_t,
              cos_t, sin_t, ts):
    b, s, d = x.shape
    row = lambda i, j: (i, j, 0)
    tile = lambda i, j: (i, j, 0, 0)
    per_b = lambda i, j: (i, 0, 0)

    def full(a):
        return pl.BlockSpec(a.shape, lambda i, j: (0,) * a.ndim)

    weights = (gmix, gkv, wdq, qg, wqn_t, wqr_t, wdkv, ckvg, wkr, wuk, wuv_t)
    return pl.pallas_call(
        _qkv_kernel,
        grid=(b, s // ts),
        in_specs=[pl.BlockSpec((1, ts, d), row),
                  pl.BlockSpec((1, 6, d), per_b),
                  pl.BlockSpec((1, 2, d), per_b)]
                 + [full(a) for a in weights]
                 + [pl.BlockSpec((1, 1, ROPE_HALF, ts),
                                 lambda i, j: (i, j * ts // cos_t.shape[3], 0,
                                               j % (cos_t.shape[3] // ts)))] * 2,
        out_specs=[pl.BlockSpec((1, 1, N_HEADS * HEAD_PAD, ts), tile),
                   pl.BlockSpec((1, ts, N_HEADS * HEAD_PAD), row),
                   pl.BlockSpec((1, 1, N_HEADS * D_V, ts), tile)],
        out_shape=[jax.ShapeDtypeStruct((b, s // ts, N_HEADS * HEAD_PAD, ts), BF16),
                   jax.ShapeDtypeStruct((b, s, N_HEADS * HEAD_PAD), BF16),
                   jax.ShapeDtypeStruct((b, s // ts, N_HEADS * D_V, ts), BF16)],
        compiler_params=_cparams(("arbitrary", "arbitrary")),
        name="qkv_proj",
    )(x, mod, kvmod, *weights, cos_t, sin_t)


def _attn_kernel(qt_ref, k_ref, vt_ref, eg_ref, eu_ref, ed_ref,
                 o_ref, eg_out_ref, eu_out_ref, ed_out_ref, *, tq, tk):
    ts = qt_ref.shape[3]
    s = k_ref.shape[1]
    key = lax.broadcasted_iota(jnp.int32, (tk, tq), 0)
    qry = lax.broadcasted_iota(jnp.int32, (tk, tq), 1)
    last_block = lambda qi: ((qi + 1) * tq - 1) // tk
    blocks = [(qi, j) for qi in range(s // tq) for j in range(last_block(qi) + 1)]

    def tile(ref, start, size):
        return ref[0, start // ts, :, start % ts:start % ts + size]

    def scores_t(qi, j):
        st = _dot(k_ref[0, j * tk:(j + 1) * tk, :], tile(qt_ref, qi * tq, tq))
        if (j + 1) * tk - 1 > qi * tq:
            st = jnp.where(key + (j * tk - qi * tq) <= qry, st, -jnp.inf)
        return st

    ahead = [scores_t(*blk) for blk in blocks[:ATTN_LOOKAHEAD]]
    for n, (qi, j) in enumerate(blocks):
        st = ahead.pop(0)
        if n + ATTN_LOOKAHEAD < len(blocks):
            ahead.append(scores_t(*blocks[n + ATTN_LOOKAHEAD]))
        if j == 0:
            m = jnp.full((1, tq), -jnp.inf, F32)
            l = jnp.zeros((1, tq), F32)
            acc = jnp.zeros((D_V, tq), F32)
        m_new = jnp.maximum(m, jnp.max(st, axis=0, keepdims=True))
        alpha = jnp.exp2(m - m_new)
        pt = jnp.exp2(st - m_new)
        l = alpha * l + jnp.sum(pt, axis=0, keepdims=True)
        acc = alpha * acc + _dot(tile(vt_ref, j * tk, tk), pt.astype(BF16))
        m = m_new
        if j == last_block(qi):
            o_ref[0, qi * tq:(qi + 1) * tq, :] = (acc / l).T.astype(BF16)

    _cast_side_job((eg_ref, eu_ref, ed_ref), (eg_out_ref, eu_out_ref, ed_out_ref))


def _attention(qt, k, vt, eg, eu, ed, ts):
    b, s, _ = k.shape
    n_exp, d, eff = eg.shape
    gate_up, down = _cast_specs(n_exp, d, eff, b * N_HEADS, lambda i, h: i * N_HEADS + h)
    return pl.pallas_call(
        functools.partial(_attn_kernel, tq=min(ts, ATTN_QUERY_TILE), tk=min(ts, ATTN_KEY_BLOCK)),
        grid=(b, N_HEADS),
        in_specs=[
            pl.BlockSpec((1, s // ts, HEAD_PAD, ts), lambda i, h: (i, 0, h, 0)),
            pl.BlockSpec((1, s, HEAD_PAD), lambda i, h: (i, 0, h)),
            pl.BlockSpec((1, s // ts, D_V, ts), lambda i, h: (i, 0, h, 0)),
            gate_up, gate_up, down,
        ],
        out_specs=[pl.BlockSpec((1, s, D_V), lambda i, h: (i, 0, h)), gate_up, gate_up, down],
        out_shape=[jax.ShapeDtypeStruct((b, s, N_HEADS * D_V), BF16),
                   jax.ShapeDtypeStruct(eg.shape, BF16),
                   jax.ShapeDtypeStruct(eu.shape, BF16),
                   jax.ShapeDtypeStruct(ed.shape, BF16)],
        compiler_params=_cparams(("arbitrary", "arbitrary")),
        name="flash_attn",
    )(qt, k, vt, eg, eu, ed)


def _attn_out_kernel(o_ref, x_ref, mod_ref, wo_ref, g_ref, wr_ref, ltri_ref,
                     x_out_ref, h_out_ref, rcol_ref, rrow_ref, cnt_ref, carry_ref,
                     *, n_exp, tiles_per_block):
    i = pl.program_id(0)

    @pl.when(i % tiles_per_block == 0)
    def _():
        carry_ref[...] = jnp.zeros_like(carry_ref)

    mod = mod_ref[0]
    x = x_ref[...] + mod[2:3] * _dot(o_ref[...], wo_ref[...])
    x_out_ref[...] = x
    h = (_rms_scale(x) * g_ref[...]) * (1.0 + mod[4:5]) + mod[3:4]
    h_hi = h.astype(BF16)
    h_out_ref[...] = h_hi

    tm = x.shape[0]
    h_lo = (h - h_hi.astype(F32)).astype(BF16)
    parts = _dot(jnp.concatenate([h_hi, h_lo], axis=0), wr_ref[...])
    logits = (parts[:tm, :LANES] + parts[:tm, LANES:]) + (parts[tm:, :LANES] + parts[tm:, LANES:])
    lane = lax.broadcasted_iota(jnp.int32, (tm, LANES), 1)
    lane_f = lane.astype(F32)

    def first_argmax(vals):
        m = jnp.max(vals, axis=-1, keepdims=True)
        idx = jnp.min(jnp.where(vals == m, lane_f, float(LANES)), axis=-1, keepdims=True)
        return m, idx.astype(jnp.int32)

    logits = jnp.where(lane < n_exp, logits, -jnp.inf)
    m1, i1 = first_argmax(logits)
    m2, i2 = first_argmax(jnp.where(lane == i1, -jnp.inf, logits))
    ex = jnp.exp(m2 - m1)
    w1 = 1.0 / (1.0 + ex)
    w2 = ex / (1.0 + ex)

    mask = jnp.where((lane == i1) | (lane == i2), 1.0, 0.0)
    ranks = _dot(ltri_ref[...], mask.astype(BF16)) + carry_ref[0:1]
    incl = ranks + mask
    rank1 = jnp.sum(jnp.where(lane == i1, ranks, 0.0), axis=-1, keepdims=True)
    rank2 = jnp.sum(jnp.where(lane == i2, ranks, 0.0), axis=-1, keepdims=True)
    carry_ref[0:1] = incl[tm - 1:tm]
    n_sub = tm // rrow_ref.shape[2]
    sub = rrow_ref.shape[2]
    cnt_ref[0] = jnp.zeros(cnt_ref.shape[1:], F32)
    for j in range(n_sub):
        cnt_ref[0, j:j + 1] = incl[(j + 1) * sub - 1:(j + 1) * sub]

    fields = (i1.astype(F32), i2.astype(F32), rank1, rank2, w1, w2)
    rcol = jnp.zeros((tm, LANES), F32)
    for k, val in enumerate(fields):
        rcol = jnp.where(lane == k, val, rcol)
    rcol_ref[...] = rcol
    rrow = rcol.T
    for j in range(n_sub):
        rrow_ref[j] = rrow[0:ROUTE_FIELDS, j * sub:(j + 1) * sub]


def _attn_out(o2, x2, mod, wo, g, wr, n_exp, ltri, tm, sub, rows_per_block):
    n, d = x2.shape
    tpb = rows_per_block // tm
    n_sub = tm // sub
    return pl.pallas_call(
        functools.partial(_attn_out_kernel, n_exp=n_exp, tiles_per_block=tpb),
        grid=(n // tm,),
        in_specs=[
            pl.BlockSpec((tm, o2.shape[1]), lambda i: (i, 0)),
            pl.BlockSpec((tm, d), lambda i: (i, 0)),
            pl.BlockSpec((1, 6, d), lambda i: (i // tpb, 0, 0)),
            pl.BlockSpec(wo.shape, lambda i: (0, 0)),
            pl.BlockSpec((1, d), lambda i: (0, 0)),
            pl.BlockSpec(wr.shape, lambda i: (0, 0)),
            pl.BlockSpec((tm, tm), lambda i: (0, 0)),
        ],
        out_specs=[pl.BlockSpec((tm, d), lambda i: (i, 0)),
                   pl.BlockSpec((tm, d), lambda i: (i, 0)),
                   pl.BlockSpec((tm, LANES), lambda i: (i, 0)),
                   pl.BlockSpec((n_sub, ROUTE_FIELDS, sub), lambda i: (i, 0, 0)),
                   pl.BlockSpec((1, 8, LANES), lambda i: (i, 0, 0))],
        out_shape=[jax.ShapeDtypeStruct((n, d), F32),
                   jax.ShapeDtypeStruct((n, d), BF16),
                   jax.ShapeDtypeStruct((n, LANES), F32),
                   jax.ShapeDtypeStruct((n // sub, ROUTE_FIELDS, sub), F32),
                   jax.ShapeDtypeStruct((n // tm, 8, LANES), F32)],
        scratch_shapes=[pltpu.VMEM((8, LANES), F32)],
        compiler_params=_cparams(("arbitrary",)),
        name="attn_out_router",
    )(o2, x2, mod, wo, g, wr, ltri)


def _dispatch_kernel(off_ref, lo_ref, hi_ref, cexp_ref, h_ref, rrow_ref, xg_ref, gs_ref,
                     *, n_exp, sub, chunks_per_step, chunks_per_block):
    blk = pl.program_id(0)
    g = pl.program_id(1)
    d = h_ref.shape[1]
    n_sub = rrow_ref.shape[0]
    win = min(DISPATCH_WINDOW, n_sub)
    iota = lax.broadcasted_iota(jnp.int32, (DISPATCH_CHUNK, 1), 0)

    def contribution(info, sb):
        row, exp_id, exp_off = info
        r = rrow_ref[sb]
        first = r[0:1] == exp_id
        second = r[1:2] == exp_id
        pos = jnp.where(first, r[2:3] + exp_off, jnp.where(second, r[3:4] + exp_off, -1.0))
        hit = pos == row
        onehot = jnp.where(hit, 1.0, 0.0).astype(BF16)
        tok = h_ref[pl.ds(pl.multiple_of(sb * sub, sub), sub), :]
        gate = jnp.sum(jnp.where(hit, jnp.where(first, r[4:5], r[5:6]), 0.0),
                       axis=-1, keepdims=True)
        return _dot(onehot, tok), gate

    def chunk_info(j):
        c = g * chunks_per_step + j
        t = blk * chunks_per_block + c
        row = (c * DISPATCH_CHUNK + iota).astype(F32)
        exp_id = cexp_ref[t]
        exp_off = off_ref[blk * n_exp + exp_id].astype(F32)
        start = jnp.minimum(lo_ref[t], n_sub - win)
        return (row, exp_id.astype(F32), exp_off), start, hi_ref[t]

    first_chunk = blk * chunks_per_block + g * chunks_per_step
    used = functools.reduce(jnp.logical_or,
                            [hi_ref[first_chunk + j] >= 0 for j in range(chunks_per_step)])

    @pl.when(jnp.logical_not(used))
    def _():
        xg_ref[...] = jnp.zeros(xg_ref.shape, BF16)
        gs_ref[...] = jnp.zeros(gs_ref.shape, F32)

    @pl.when(used)
    def _():
        for j in range(chunks_per_step):
            info, start, _ = chunk_info(j)
            acc = jnp.zeros((DISPATCH_CHUNK, d), F32)
            gate = jnp.zeros((DISPATCH_CHUNK, 1), F32)
            for k in range(win):
                a, b = contribution(info, start + k)
                acc += a
                gate += b
            rows = slice(j * DISPATCH_CHUNK, (j + 1) * DISPATCH_CHUNK)
            xg_ref[rows, :] = acc.astype(BF16)
            gs_ref[rows, :] = gate

        def extra(j, _):
            info, start, last = chunk_info(j)
            rows = pl.ds(pl.multiple_of(j * DISPATCH_CHUNK, DISPATCH_CHUNK), DISPATCH_CHUNK)

            def add(sb, _):
                a, b = contribution(info, sb)
                xg_ref[rows, :] = (xg_ref[rows, :].astype(F32) + a).astype(BF16)
                gs_ref[rows, :] = gs_ref[rows, :] + b
                return 0

            return lax.fori_loop(start + win, last + 1, add, 0)

        lax.fori_loop(0, chunks_per_step, extra, 0)


def _dispatch(off, lo, hi, cexp, h, rrow, n_blocks, rows_per_block, rcap, n_exp, sub):
    n, d = h.shape
    n_sub = rows_per_block // sub
    step_rows = _pick(rcap, DISPATCH_STEP_ROWS)
    cps = step_rows // DISPATCH_CHUNK
    cpb = rcap // DISPATCH_CHUNK
    grid_spec = pltpu.PrefetchScalarGridSpec(
        num_scalar_prefetch=4,
        grid=(n_blocks, rcap // step_rows),
        in_specs=[
            pl.BlockSpec((rows_per_block, d), lambda b, g, *_: (b, 0)),
            pl.BlockSpec((n_sub, ROUTE_FIELDS, sub), lambda b, g, *_: (b, 0, 0)),
        ],
        out_specs=[
            pl.BlockSpec((step_rows, d), lambda b, g, *_: (b * (rcap // step_rows) + g, 0)),
            pl.BlockSpec((step_rows, 1), lambda b, g, *_: (b * (rcap // step_rows) + g, 0)),
        ],
    )
    return pl.pallas_call(
        functools.partial(_dispatch_kernel, n_exp=n_exp, sub=sub, chunks_per_step=cps,
                          chunks_per_block=cpb),
        grid_spec=grid_spec,
        out_shape=[jax.ShapeDtypeStruct((n_blocks * rcap, d), BF16),
                   jax.ShapeDtypeStruct((n_blocks * rcap, 1), F32)],
        compiler_params=_cparams(("arbitrary", "arbitrary")),
        name="moe_dispatch",
    )(off, lo, hi, cexp, h, rrow)


def _expert_ffn_kernel(perm_ref, pexp_ref, pfill_ref, x_ref, gs_ref, wg_ref, wu_ref, wd_ref,
                       y_ref, *, tf):
    w = pl.program_id(0)
    fill = pfill_ref[w]
    ff = wg_ref.shape[2]

    def ffn(n_rows):
        x = x_ref[0:n_rows, :]
        gs = gs_ref[0:n_rows, :]
        acc = jnp.zeros((n_rows, y_ref.shape[1]), F32)
        proj = lambda f0: (_dot(x, wg_ref[0, :, f0:f0 + tf]), _dot(x, wu_ref[0, :, f0:f0 + tf]))
        nxt = proj(0)
        for f0 in range(0, ff, tf):
            gate, up = nxt
            if f0 + tf < ff:
                nxt = proj(f0 + tf)
            a = ((_silu(gate) * up) * gs).astype(BF16)
            acc = acc + _dot(a, wd_ref[0, f0:f0 + tf, :])
        return acc.astype(BF16)

    for level in range(PIECE // PIECE_QUANTUM + 1):
        n_rows = level * PIECE_QUANTUM

        @pl.when(fill == level)
        def _(n_rows=n_rows):
            if n_rows:
                y_ref[0:n_rows, :] = ffn(n_rows)
            if n_rows < PIECE:
                y_ref[n_rows:PIECE, :] = jnp.zeros((PIECE - n_rows, y_ref.shape[1]), BF16)


def _expert_ffn(perm, pexp, pfill, xg, gs, wg, wu, wd, tf):
    rows, d = xg.shape
    n_exp, _, ff = wg.shape
    grid_spec = pltpu.PrefetchScalarGridSpec(
        num_scalar_prefetch=3,
        grid=(rows // PIECE,),
        in_specs=[
            pl.BlockSpec((PIECE, d), lambda w, perm, pexp, pfill: (perm[w], 0)),
            pl.BlockSpec((PIECE, 1), lambda w, perm, pexp, pfill: (perm[w], 0)),
            pl.BlockSpec((1, d, ff), lambda w, perm, pexp, pfill: (pexp[w], 0, 0)),
            pl.BlockSpec((1, d, ff), lambda w, perm, pexp, pfill: (pexp[w], 0, 0)),
            pl.BlockSpec((1, ff, d), lambda w, perm, pexp, pfill: (pexp[w], 0, 0)),
        ],
        out_specs=pl.BlockSpec((PIECE, d), lambda w, perm, pexp, pfill: (perm[w], 0)),
    )
    return pl.pallas_call(
        functools.partial(_expert_ffn_kernel, tf=tf),
        grid_spec=grid_spec,
        out_shape=jax.ShapeDtypeStruct((rows, d), BF16),
        compiler_params=_cparams(("arbitrary",)),
        name="expert_ffn",
    )(perm, pexp, pfill, xg, gs, wg, wu, wd)


def _combine_kernel(off_ref, start_ref, end_ref, yg_ref, rcol_ref, x_ref, mod_ref, gfin_ref, o_ref,
                    acc_ref, *, n_exp, sub, subs_per_block):
    blk = pl.program_id(0)
    t = pl.program_id(1)
    tm, d = x_ref.shape
    col = lax.broadcasted_iota(jnp.int32, (1, COMBINE_WINDOW), 1)
    align = COMBINE_WINDOW // 2

    def token_info(s):
        rc = rcol_ref[s * sub:(s + 1) * sub, :]
        return tuple(jnp.broadcast_to(rc[:, k:k + 1], (sub, COMBINE_WINDOW)) for k in range(4))

    def gathered(info, e, w0):
        i1, i2, rank1, rank2 = info
        rank = jnp.where(i1 == e, rank1, jnp.where(i2 == e, rank2, -float(COMBINE_WINDOW)))
        in_region = (w0 - off_ref[blk * n_exp + e] + col).astype(F32)
        onehot = jnp.where(rank == in_region, 1.0, 0.0).astype(BF16)
        return _dot(onehot, yg_ref[pl.ds(pl.multiple_of(w0, align), COMBINE_WINDOW), :])

    def first_window(s, e):
        idx = (blk * subs_per_block + t * (tm // sub) + s) * n_exp + e
        w0 = lax.shift_right_logical(start_ref[idx], align.bit_length() - 1) * align
        return w0, end_ref[idx]

    for s in range(tm // sub):
        info = token_info(s)
        acc = jnp.zeros((sub, d), F32)
        for e in range(n_exp):
            acc += gathered(info, e, first_window(s, e)[0])
        acc_ref[s * sub:(s + 1) * sub, :] = acc

    for s in range(tm // sub):
        info = token_info(s)
        for e in range(n_exp):
            w0, end = first_window(s, e)
            n_more = lax.shift_right_logical(
                jnp.maximum(end - w0 - 1, 0), COMBINE_WINDOW.bit_length() - 1)

            def add(k, _, info=info, e=e, w0=w0, s=s):
                rows = slice(s * sub, (s + 1) * sub)
                acc_ref[rows, :] = acc_ref[rows, :] + gathered(info, e, w0 + k * COMBINE_WINDOW)
                return 0

            lax.fori_loop(1, n_more + 1, add, 0)

    x = x_ref[...] + mod_ref[0][5:6] * acc_ref[...]
    o_ref[...] = _rms_scale(x) * gfin_ref[...]


def _combine(off, row_start, row_end, yg, rcol, x2, mod, gfin, n_blocks, rows_per_block, rcap,
             n_exp, sub, tm):
    n, d = x2.shape
    tpb = rows_per_block // tm
    grid_spec = pltpu.PrefetchScalarGridSpec(
        num_scalar_prefetch=3,
        grid=(n_blocks, tpb),
        in_specs=[
            pl.BlockSpec((rcap, d), lambda b, t, *_: (b, 0), pipeline_mode=pl.Buffered(1)),
            pl.BlockSpec((tm, LANES), lambda b, t, *_: (b * tpb + t, 0)),
            pl.BlockSpec((tm, d), lambda b, t, *_: (b * tpb + t, 0)),
            pl.BlockSpec((1, 6, d), lambda b, t, *_: (b, 0, 0)),
            pl.BlockSpec((1, d), lambda b, t, *_: (0, 0)),
        ],
        out_specs=pl.BlockSpec((tm, d), lambda b, t, *_: (b * tpb + t, 0)),
        scratch_shapes=[pltpu.VMEM((tm, d), F32)],
    )
    return pl.pallas_call(
        functools.partial(_combine_kernel, n_exp=n_exp, sub=sub,
                          subs_per_block=rows_per_block // sub),
        grid_spec=grid_spec,
        out_shape=jax.ShapeDtypeStruct((n, d), F32),
        compiler_params=_cparams(("arbitrary", "arbitrary")),
        name="moe_combine",
    )(off, row_start, row_end, yg, rcol, x2, mod, gfin)


def _routing_tables(cnt, n_blocks, n_sub, n_exp, rcap):
    cum_incl = cnt.astype(jnp.int32)
    cum_excl = jnp.concatenate([jnp.zeros_like(cum_incl[:, :1]), cum_incl[:, :-1]], axis=1)
    total = cum_incl[:, -1]
    psz = (total + PIECE - 1) // PIECE * PIECE
    off = jnp.cumsum(psz, axis=1) - psz
    start = off[:, None, :] + cum_excl
    end = off[:, None, :] + cum_incl

    c0 = jnp.arange(rcap // DISPATCH_CHUNK, dtype=jnp.int32)[None, :, None, None] * DISPATCH_CHUNK
    hit = ((start[:, None] < c0 + DISPATCH_CHUNK) & (end[:, None] > c0)
           & (end[:, None] > start[:, None])).any(-1)
    sub_ids = jnp.arange(n_sub, dtype=jnp.int32)
    d_lo = jnp.min(jnp.where(hit, sub_ids, n_sub), axis=-1)
    d_hi = jnp.max(jnp.where(hit, sub_ids, -1), axis=-1)
    d_lo = jnp.where(d_hi < 0, 0, d_lo)

    ppb = rcap // PIECE
    prow = jnp.arange(ppb, dtype=jnp.int32)[None, :, None] * PIECE
    inside = (prow >= off[:, None, :]) & (prow < (off + psz)[:, None, :])
    p_exp = jnp.argmax(inside, axis=-1).astype(jnp.int32)
    used = inside.any(-1)
    left = jnp.take_along_axis(total + off, p_exp, axis=1) - prow[..., 0]
    p_fill = jnp.where(used, jnp.clip((left + PIECE_QUANTUM - 1) // PIECE_QUANTUM, 0,
                                      PIECE // PIECE_QUANTUM), 0).astype(jnp.int32)
    key = jnp.where(p_fill > 0, p_exp, n_exp).reshape(-1)
    perm = jnp.argsort(key, stable=True).astype(jnp.int32)
    p_exp_sorted = jnp.minimum(key[perm], n_exp - 1).astype(jnp.int32)
    d_exp = jnp.repeat(p_exp, PIECE // DISPATCH_CHUNK, axis=1).reshape(-1)
    return (off.reshape(-1).astype(jnp.int32), d_lo.reshape(-1), d_hi.reshape(-1), d_exp,
            start.reshape(-1).astype(jnp.int32), end.reshape(-1).astype(jnp.int32),
            perm, p_exp_sorted, p_fill.reshape(-1)[perm])


def _pick(n, pref):
    t = min(n, pref)
    while n % t:
        t //= 2
    return t


def kernel(x, c, positions, ada_w, ada_b, mix_norm_g, ffn_norm_g, pool_w_in, pool_w_grp,
           pool_scale, pool_w_out, kv_ada_w, kv_ada_b, kv_norm_g, w_dkv, ckv_norm_g, w_kr,
           w_uk, w_uv, w_dq, q_norm_g, w_uq, w_o, ffn_w_gate, ffn_w_up, ffn_w_down,
           router_w, moe_w_gate, moe_w_up, moe_w_down, final_norm_g):
    b, s, d = x.shape
    n = b * s
    depth = ada_w.shape[0]
    assert depth == 2 and pool_w_in.shape[0] == 1 and w_dq.shape[0] == 1
    assert b <= 8 and d % LANES == 0

    ts = _pick(s, 512)
    tp = _pick(s, 1024)
    ta = _pick(s, 1024)
    tc = _pick(s, 512)
    tm = _pick(s, 1024)
    tf = _pick(ffn_w_gate.shape[-1], 512)

    c_pad = jnp.zeros((8, d), F32).at[:b].set(c)
    mods = _adaln(c_pad, ada_w, ada_b[:, None, :], _pick(6 * d, 1536))
    mods = mods[:, :b].reshape(depth, b, 6, d)
    kvmod = _adaln(c_pad, kv_ada_w[None], kv_ada_b[None, None, :], _pick(2 * d, 1024))
    kvmod = kvmod[0, :b].reshape(b, 2, d)

    r = jnp.arange(min(tp, POOL_ROW_BLOCK))[:, None]
    cidx = jnp.arange(min(tp, POOL_ROW_BLOCK) + POOL_HALO)[None, :] - POOL_HALO
    band = jnp.stack([((cidx <= r) & (cidx > r - w)) for w in POOL_WINDOWS]).astype(BF16)
    inv_freq = ROPE_THETA ** (-jnp.arange(0, D_ROPE, 2, dtype=F32) / D_ROPE)
    x1, fwg, fwu, fwd, rope_cos, rope_sin, h1 = _pool_mixer(
        x, mods[0], mix_norm_g[0:1], pool_w_in[0].astype(BF16), band, pool_w_grp[0].astype(BF16),
        pool_scale[0:1], pool_w_out[0].astype(BF16), ffn_w_gate, ffn_w_up, ffn_w_down,
        positions.astype(F32)[:, None, :], inv_freq[:, None], ffn_norm_g[0:1], tp)

    x2 = _dense_ffn(x1.reshape(n, d), h1.reshape(n, d), mods[0], fwg, fwu, fwd, tm, tf, s)


    q_lora = w_dq.shape[-1]
    wuq = w_uq[0].reshape(q_lora, N_HEADS, D_NOPE + D_ROPE)
    wqn_t = wuq[:, :, :D_NOPE].reshape(q_lora, N_HEADS * D_NOPE).T.astype(BF16)
    wqr_t = wuq[:, :, D_NOPE:].reshape(q_lora, N_HEADS * D_ROPE).T.astype(BF16)
    qt, k, vt = _qkv_proj(
        x2.reshape(b, s, d), mods[1], kvmod, mix_norm_g[1:2], kv_norm_g[None, :],
        w_dq[0].astype(BF16), q_norm_g[0:1], wqn_t, wqr_t, w_dkv.astype(BF16),
        ckv_norm_g[None, :], jnp.pad(w_kr, ((0, 0), (0, LANES - D_ROPE))).astype(BF16),
        w_uk.astype(BF16), w_uv.T.astype(BF16), rope_cos, rope_sin, ts)
    o, ewg, ewu, ewd = _attention(qt, k, vt, moe_w_gate[0], moe_w_up[0], moe_w_down[0], ts)

    n_exp = router_w.shape[-1]
    sub = _pick(s, ROUTE_SUB)
    n_sub = s // sub
    rcap = 2 * s + n_exp * PIECE
    ltri = (jnp.arange(ta)[None, :] < jnp.arange(ta)[:, None]).astype(BF16)
    assert n_exp <= LANES
    wr = jnp.pad(router_w[0], ((0, 0), (0, LANES - n_exp)))
    wr_hi = wr.astype(BF16)
    wr_split = jnp.concatenate([wr_hi, (wr - wr_hi.astype(F32)).astype(BF16)], axis=1)
    x3, h3, rcol, rrow, cnt = _attn_out(o.reshape(n, N_HEADS * D_V), x2, mods[1],
                                        w_o[0].astype(BF16), ffn_norm_g[1:2], wr_split, n_exp,
                                        ltri, ta, sub, s)
    cnt = cnt[:, :ta // sub, :n_exp].reshape(b, n_sub, n_exp)
    (off, d_lo, d_hi, d_exp, row_start, row_end, perm, p_exp,
     p_fill) = _routing_tables(cnt, b, n_sub, n_exp, rcap)
    xg, gs = _dispatch(off, d_lo, d_hi, d_exp, h3, rrow, b, s, rcap, n_exp, sub)
    yg = _expert_ffn(perm, p_exp, p_fill, xg, gs, ewg, ewu, ewd, tf)
    out = _combine(off, row_start, row_end, yg, rcol, x3, mods[1], final_norm_g[None, :], b, s,
                   rcap, n_exp, sub, tc)
    return out.reshape(b, s, d)
```

```python
import functools
import math

import jax
import jax.numpy as jnp
from jax import lax
from jax.experimental import pallas as pl
from jax.experimental.pallas import tpu as pltpu

F32 = jnp.float32
BF16 = jnp.bfloat16

N_HEADS = 8
D_NOPE = 128
D_ROPE = 64
D_V = 128
HEAD_PAD = 256
ROPE_HALF = D_ROPE // 2
ROPE_THETA = 10000.0
ATTN_SCALE = 1.0 / math.sqrt(D_NOPE + D_ROPE)
Q_SCALE = ATTN_SCALE * math.log2(math.e)
ATTN_QUERY_TILE = 512
ATTN_KEY_BLOCK = 512
ATTN_LOOKAHEAD = 3
POOL_WINDOWS = (2, 4, 8, 16)
POOL_HALO = 16
POOL_ROW_BLOCK = 128
EPS = 1e-6

ROUTE_SUB = 256
DISPATCH_CHUNK = 128
DISPATCH_WINDOW = 3
DISPATCH_STEP_ROWS = 1024
COMBINE_WINDOW = 256
PIECE = 512
PIECE_QUANTUM = 128
ROUTE_FIELDS = 8

LANES = 128
VMEM_LIMIT_BYTES = 56 * 1024 * 1024


def _cparams(sem):
    return pltpu.CompilerParams(dimension_semantics=sem, vmem_limit_bytes=VMEM_LIMIT_BYTES)


def _dot(a, b):
    return jnp.dot(a, b, preferred_element_type=F32)


def _rms_scale(x):
    return x * lax.rsqrt(jnp.mean(x * x, axis=-1, keepdims=True) + EPS)


def _silu(x):
    return x * jax.nn.sigmoid(x)


def _adaln_kernel(c_ref, w_ref, b_ref, o_ref):
    sc = _silu(c_ref[...])
    o_ref[0] = jnp.dot(sc, w_ref[0], preferred_element_type=F32,
                       precision=lax.Precision.HIGHEST) + b_ref[0]


def _adaln(c_pad, w, b, tn):
    nl, d, m = w.shape
    return pl.pallas_call(
        _adaln_kernel,
        grid=(nl, m // tn),
        in_specs=[
            pl.BlockSpec((8, d), lambda l, j: (0, 0)),
            pl.BlockSpec((1, d, tn), lambda l, j: (l, 0, j)),
            pl.BlockSpec((1, 1, tn), lambda l, j: (l, 0, j)),
        ],
        out_specs=pl.BlockSpec((1, 8, tn), lambda l, j: (l, 0, j)),
        out_shape=jax.ShapeDtypeStruct((nl, 8, m), F32),
        compiler_params=_cparams(("arbitrary", "arbitrary")),
        name="adaln",
    )(c_pad, w, b)


def _rope_table_side_job(pos_ref, freq_ref, cos_ref, sin_ref):
    ang = freq_ref[...] * pos_ref[0]
    cos_ref[0, 0] = jnp.cos(ang)
    sin_ref[0, 0] = jnp.sin(ang)


def _rope_t(x, cos_t, sin_t):
    x1, x2 = x[:ROPE_HALF], x[ROPE_HALF:]
    return x1 * cos_t - x2 * sin_t, x2 * cos_t + x1 * sin_t


def _cast_specs(n_exp, d, ff, steps, step_of):
    assert steps >= n_exp, "too few grid steps to carry the weight cast"
    k = max(c for c in range(1, steps // n_exp + 1) if ff % (c * LANES) == 0)
    cw = ff // k

    def chunk(*idx):
        c = jnp.minimum(step_of(*idx), n_exp * k - 1)
        return c // k, c % k

    gate_up = pl.BlockSpec((1, d, cw), lambda *idx: (chunk(*idx)[0], 0, chunk(*idx)[1]))
    down = pl.BlockSpec((1, cw, d), lambda *idx: (chunk(*idx)[0], chunk(*idx)[1], 0))
    return gate_up, down


def _cast_side_job(src_refs, dst_refs):
    for src, dst in zip(src_refs, dst_refs):
        dst[...] = src[...].astype(BF16)


def _pool_kernel(x_ref, mod_ref, g_ref, win_ref, band_ref, wgrp_ref, scale_ref, wout_ref,
                 fg_ref, fu_ref, fd_ref, pos_ref, freq_ref, gffn_ref,
                 o_ref, fg_out_ref, fu_out_ref, fd_out_ref, cos_ref, sin_ref, h_out_ref, uext_ref,
                 *, ts, gw):
    s = pl.program_id(1)

    @pl.when(s == 0)
    def _():
        uext_ref[0:POOL_HALO, :] = jnp.zeros((POOL_HALO, uext_ref.shape[1]), BF16)

    x = x_ref[0]
    mod = mod_ref[0]
    h = (_rms_scale(x) * g_ref[...]) * (1.0 + mod[1:2]) + mod[0:1]
    u = _dot(h.astype(BF16), win_ref[...])
    uext_ref[POOL_HALO:POOL_HALO + ts, :] = u.astype(BF16)

    t = s * ts + lax.broadcasted_iota(jnp.int32, (ts, 1), 0)
    zs = []
    for g, w in enumerate(POOL_WINDOWS):
        cols = slice(g * gw, (g + 1) * gw)
        rb = band_ref.shape[1]
        wsum = jnp.concatenate(
            [_dot(band_ref[g], uext_ref[r0:r0 + rb + POOL_HALO, cols]) for r0 in range(0, ts, rb)],
            axis=0)
        cnt = jnp.minimum(t + 1, w).astype(F32)
        p = wsum / cnt - u[:, cols]
        zs.append(_dot(p.astype(BF16), wgrp_ref[g]))
    z = jnp.concatenate(zs, axis=-1) * scale_ref[...]
    y = _dot(z.astype(BF16), wout_ref[...])
    x_out = x + mod[2:3] * y
    o_ref[0] = x_out
    h_out_ref[0] = ((_rms_scale(x_out) * gffn_ref[...]) * (1.0 + mod[4:5]) + mod[3:4]).astype(BF16)

    uext_ref[0:POOL_HALO, :] = uext_ref[ts:ts + POOL_HALO, :]

    _cast_side_job((fg_ref, fu_ref, fd_ref), (fg_out_ref, fu_out_ref, fd_out_ref))
    _rope_table_side_job(pos_ref, freq_ref, cos_ref, sin_ref)


def _pool_mixer(x, mod, g, w_in, band, w_grp, scale, w_out, fg, fu, fd, pos_f, freq_row, g_ffn,
                ts):
    b, s, d = x.shape
    ng, gw, _ = w_grp.shape
    n_s = s // ts
    gate_up, down = _cast_specs(1, d, fg.shape[2], b * n_s, lambda i, j: i * n_s + j)
    const2 = lambda i, j: (0, 0)
    const3 = lambda i, j: (0, 0, 0)
    return pl.pallas_call(
        functools.partial(_pool_kernel, ts=ts, gw=gw),
        grid=(b, s // ts),
        in_specs=[
            pl.BlockSpec((1, ts, d), lambda i, j: (i, j, 0)),
            pl.BlockSpec((1, 6, d), lambda i, j: (i, 0, 0)),
            pl.BlockSpec((1, d), const2),
            pl.BlockSpec((d, d), const2),
            pl.BlockSpec(band.shape, const3),
            pl.BlockSpec((ng, gw, gw), const3),
            pl.BlockSpec((1, d), const2),
            pl.BlockSpec((d, d), const2),
            gate_up, gate_up, down,
            pl.BlockSpec((1, 1, ts), lambda i, j: (i, 0, j)),
            pl.BlockSpec((ROPE_HALF, 1), const2),
            pl.BlockSpec((1, d), const2),
        ],
        out_specs=[pl.BlockSpec((1, ts, d), lambda i, j: (i, j, 0)), gate_up, gate_up, down,
                   pl.BlockSpec((1, 1, ROPE_HALF, ts), lambda i, j: (i, j, 0, 0)),
                   pl.BlockSpec((1, 1, ROPE_HALF, ts), lambda i, j: (i, j, 0, 0)),
                   pl.BlockSpec((1, ts, d), lambda i, j: (i, j, 0))],
        out_shape=[jax.ShapeDtypeStruct((b, s, d), F32),
                   jax.ShapeDtypeStruct(fg.shape, BF16),
                   jax.ShapeDtypeStruct(fu.shape, BF16),
                   jax.ShapeDtypeStruct(fd.shape, BF16),
                   jax.ShapeDtypeStruct((b, s // ts, ROPE_HALF, ts), F32),
                   jax.ShapeDtypeStruct((b, s // ts, ROPE_HALF, ts), F32),
                   jax.ShapeDtypeStruct((b, s, d), BF16)],
        scratch_shapes=[pltpu.VMEM((POOL_HALO + ts, d), BF16)],
        compiler_params=_cparams(("arbitrary", "arbitrary")),
        name="pool_mixer",
    )(x, mod, g, w_in, band, w_grp, scale, w_out, fg, fu, fd, pos_f, freq_row, g_ffn)


def _ffn_kernel(x_ref, h_ref, mod_ref, wg_ref, wu_ref, wd_ref, o_ref, *, tf):
    h = h_ref[...]
    ff = wg_ref.shape[2]
    acc = jnp.zeros(x_ref.shape, F32)
    for f0 in range(0, ff, tf):
        gate = _dot(h, wg_ref[0, :, f0:f0 + tf])
        up = _dot(h, wu_ref[0, :, f0:f0 + tf])
        a = (_silu(gate) * up).astype(BF16)
        acc = acc + _dot(a, wd_ref[0, f0:f0 + tf, :])
    o_ref[...] = x_ref[...] + mod_ref[0][5:6] * acc


def _dense_ffn(x2, h2, mod, wg, wu, wd, tm, tf, rows_per_batch):
    n, d = x2.shape
    tpb = rows_per_batch // tm
    resident = lambda a: pl.BlockSpec(a.shape, lambda i: (0, 0, 0), pipeline_mode=pl.Buffered(1))
    return pl.pallas_call(
        functools.partial(_ffn_kernel, tf=tf),
        grid=(n // tm,),
        in_specs=[
            pl.BlockSpec((tm, d), lambda i: (i, 0)),
            pl.BlockSpec((tm, d), lambda i: (i, 0)),
            pl.BlockSpec((1, 6, d), lambda i: (i // tpb, 0, 0)),
            resident(wg), resident(wu), resident(wd),
        ],
        out_specs=pl.BlockSpec((tm, d), lambda i: (i, 0)),
        out_shape=jax.ShapeDtypeStruct((n, d), F32),
        compiler_params=_cparams(("arbitrary",)),
        name="dense_ffn",
    )(x2, h2, mod, wg, wu, wd)


def _dot_nt(a, b):
    return lax.dot_general(a, b, (((1,), (1,)), ((), ())), preferred_element_type=F32)


def _qkv_kernel(x_ref, mod_ref, kvmod_ref, gmix_ref, gkv_ref, wdq_ref, qg_ref, wqn_t_ref,
                wqr_t_ref, wdkv_ref, ckvg_ref, wkr_ref, wuk_ref, wuv_t_ref, cos_ref, sin_ref,
                qt_ref, k_ref, vt_ref):
    x = x_ref[0]
    xn = _rms_scale(x)
    mod = mod_ref[0]
    kvmod = kvmod_ref[0]
    hq = ((xn * gmix_ref[...]) * (1.0 + mod[1:2]) + mod[0:1]).astype(BF16)
    hk = ((xn * gkv_ref[...]) * (1.0 + kvmod[1:2]) + kvmod[0:1]).astype(BF16)

    cq_raw = _dot(hq, wdq_ref[...])
    ckv_raw = _dot(hk, wdkv_ref[...])
    kr_raw = _dot(hk, wkr_ref[...])

    cq = (_rms_scale(cq_raw) * qg_ref[...]).astype(BF16)
    ckv = (_rms_scale(ckv_raw) * ckvg_ref[...]).astype(BF16)
    qn_t = _dot_nt(wqn_t_ref[...], cq) * Q_SCALE
    qr_t = _dot_nt(wqr_t_ref[...], cq) * Q_SCALE
    cos_t = cos_ref[0, 0]
    sin_t = sin_ref[0, 0]
    zero_rows = jnp.zeros((HEAD_PAD - LANES - D_ROPE, x.shape[0]), BF16)

    kn = _dot(ckv, wuk_ref[...])
    vt_ref[0, 0] = _dot_nt(wuv_t_ref[...], ckv).astype(BF16)
    kr_raw_t = kr_raw.T
    k1, k2 = _rope_t(kr_raw_t[:D_ROPE], cos_t, sin_t)
    kr_t = jnp.concatenate([k1, k2, kr_raw_t[D_ROPE:]], axis=0)
    kr = kr_t.T.astype(BF16)

    for h in range(N_HEADS):
        lo = h * HEAD_PAD
        hs = slice(h * LANES, (h + 1) * LANES)
        qt_ref[0, 0, lo:lo + LANES, :] = qn_t[hs].astype(BF16)
        q1, q2 = _rope_t(qr_t[h * D_ROPE:(h + 1) * D_ROPE], cos_t, sin_t)
        ro = lo + LANES
        qt_ref[0, 0, ro:ro + ROPE_HALF, :] = q1.astype(BF16)
        qt_ref[0, 0, ro + ROPE_HALF:ro + D_ROPE, :] = q2.astype(BF16)
        qt_ref[0, 0, ro + D_ROPE:lo + HEAD_PAD, :] = zero_rows
        k_ref[0, :, lo:lo + LANES] = kn[:, hs].astype(BF16)
        k_ref[0, :, lo + LANES:lo + HEAD_PAD] = kr


def _qkv_proj(x, mod, kvmod, gmix, gkv, wdq, qg, wqn_t, wqr_t, wdkv, ckvg, wkr, wuk, wuv_t,
              cos_t, sin_t, ts):
    b, s, d = x.shape
    row = lambda i, j: (i, j, 0)
    tile = lambda i, j: (i, j, 0, 0)
    per_b = lambda i, j: (i, 0, 0)

    def full(a):
        return pl.BlockSpec(a.shape, lambda i, j: (0,) * a.ndim)

    weights = (gmix, gkv, wdq, qg, wqn_t, wqr_t, wdkv, ckvg, wkr, wuk, wuv_t)
    return pl.pallas_call(
        _qkv_kernel,
        grid=(b, s // ts),
        in_specs=[pl.BlockSpec((1, ts, d), row),
                  pl.BlockSpec((1, 6, d), per_b),
                  pl.BlockSpec((1, 2, d), per_b)]
                 + [full(a) for a in weights]
                 + [pl.BlockSpec((1, 1, ROPE_HALF, ts),
                                 lambda i, j: (i, j * ts // cos_t.shape[3], 0,
                                               j % (cos_t.shape[3] // ts)))] * 2,
        out_specs=[pl.BlockSpec((1, 1, N_HEADS * HEAD_PAD, ts), tile),
                   pl.BlockSpec((1, ts, N_HEADS * HEAD_PAD), row),
                   pl.BlockSpec((1, 1, N_HEADS * D_V, ts), tile)],
        out_shape=[jax.ShapeDtypeStruct((b, s // ts, N_HEADS * HEAD_PAD, ts), BF16),
                   jax.ShapeDtypeStruct((b, s, N_HEADS * HEAD_PAD), BF16),
                   jax.ShapeDtypeStruct((b, s // ts, N_HEADS * D_V, ts), BF16)],
        compiler_params=_cparams(("arbitrary", "arbitrary")),
        name="qkv_proj",
    )(x, mod, kvmod, *weights, cos_t, sin_t)


def _attn_kernel(qt_ref, k_ref, vt_ref, eg_ref, eu_ref, ed_ref,
                 o_ref, eg_out_ref, eu_out_ref, ed_out_ref, *, tq, tk):
    ts = qt_ref.shape[3]
    s = k_ref.shape[1]
    key = lax.broadcasted_iota(jnp.int32, (tk, tq), 0)
    qry = lax.broadcasted_iota(jnp.int32, (tk, tq), 1)
    last_block = lambda qi: ((qi + 1) * tq - 1) // tk
    blocks = [(qi, j) for qi in range(s // tq) for j in range(last_block(qi) + 1)]

    def tile(ref, start, size):
        return ref[0, start // ts, :, start % ts:start % ts + size]

    def scores_t(qi, j):
        st = _dot(k_ref[0, j * tk:(j + 1) * tk, :], tile(qt_ref, qi * tq, tq))
        if (j + 1) * tk - 1 > qi * tq:
            st = jnp.where(key + (j * tk - qi * tq) <= qry, st, -jnp.inf)
        return st

    ahead = [scores_t(*blk) for blk in blocks[:ATTN_LOOKAHEAD]]
    for n, (qi, j) in enumerate(blocks):
        st = ahead.pop(0)
        if n + ATTN_LOOKAHEAD < len(blocks):
            ahead.append(scores_t(*blocks[n + ATTN_LOOKAHEAD]))
        if j == 0:
            m = jnp.full((1, tq), -jnp.inf, F32)
            l = jnp.zeros((1, tq), F32)
            acc = jnp.zeros((D_V, tq), F32)
        m_new = jnp.maximum(m, jnp.max(st, axis=0, keepdims=True))
        alpha = jnp.exp2(m - m_new)
        pt = jnp.exp2(st - m_new)
        l = alpha * l + jnp.sum(pt, axis=0, keepdims=True)
        acc = alpha * acc + _dot(tile(vt_ref, j * tk, tk), pt.astype(BF16))
        m = m_new
        if j == last_block(qi):
            o_ref[0, qi * tq:(qi + 1) * tq, :] = (acc / l).T.astype(BF16)

    _cast_side_job((eg_ref, eu_ref, ed_ref), (eg_out_ref, eu_out_ref, ed_out_ref))


def _attention(qt, k, vt, eg, eu, ed, ts):
    b, s, _ = k.shape
    n_exp, d, eff = eg.shape
    gate_up, down = _cast_specs(n_exp, d, eff, b * N_HEADS, lambda i, h: i * N_HEADS + h)
    return pl.pallas_call(
        functools.partial(_attn_kernel, tq=min(ts, ATTN_QUERY_TILE), tk=min(ts, ATTN_KEY_BLOCK)),
        grid=(b, N_HEADS),
        in_specs=[
            pl.BlockSpec((1, s // ts, HEAD_PAD, ts), lambda i, h: (i, 0, h, 0)),
            pl.BlockSpec((1, s, HEAD_PAD), lambda i, h: (i, 0, h)),
            pl.BlockSpec((1, s // ts, D_V, ts), lambda i, h: (i, 0, h, 0)),
            gate_up, gate_up, down,
        ],
        out_specs=[pl.BlockSpec((1, s, D_V), lambda i, h: (i, 0, h)), gate_up, gate_up, down],
        out_shape=[jax.ShapeDtypeStruct((b, s, N_HEADS * D_V), BF16),
                   jax.ShapeDtypeStruct(eg.shape, BF16),
                   jax.ShapeDtypeStruct(eu.shape, BF16),
                   jax.ShapeDtypeStruct(ed.shape, BF16)],
        compiler_params=_cparams(("arbitrary", "arbitrary")),
        name="flash_attn",
    )(qt, k, vt, eg, eu, ed)


def _attn_out_kernel(o_ref, x_ref, mod_ref, wo_ref, g_ref, wr_ref, ltri_ref,
                     x_out_ref, h_out_ref, rcol_ref, rrow_ref, cnt_ref, carry_ref,
                     *, n_exp, tiles_per_block):
    i = pl.program_id(0)

    @pl.when(i % tiles_per_block == 0)
    def _():
        carry_ref[...] = jnp.zeros_like(carry_ref)

    mod = mod_ref[0]
    tm = x_ref.shape[0]
    n_streams = 2 if tm % (2 * LANES) == 0 else 1
    rows = [slice(k * tm // n_streams, (k + 1) * tm // n_streams) for k in range(n_streams)]
    proj = [_dot(o_ref[r, :], wo_ref[...]) for r in rows]
    hs, logit_parts = [], []
    for r, y in zip(rows, proj):
        x = x_ref[r, :] + mod[2:3] * y
        x_out_ref[r, :] = x
        h = (_rms_scale(x) * g_ref[...]) * (1.0 + mod[4:5]) + mod[3:4]
        h_hi = h.astype(BF16)
        h_out_ref[r, :] = h_hi
        hs.append((h_hi, (h - h_hi.astype(F32)).astype(BF16)))
    for h_hi, h_lo in hs:
        t = h_hi.shape[0]
        parts = _dot(jnp.concatenate([h_hi, h_lo], axis=0), wr_ref[...])
        logit_parts.append((parts[:t, :LANES] + parts[:t, LANES:])
                           + (parts[t:, :LANES] + parts[t:, LANES:]))
    logits = jnp.concatenate(logit_parts, axis=0)
    lane = lax.broadcasted_iota(jnp.int32, (tm, LANES), 1)
    lane_f = lane.astype(F32)

    def first_argmax(vals):
        m = jnp.max(vals, axis=-1, keepdims=True)
        idx = jnp.min(jnp.where(vals == m, lane_f, float(LANES)), axis=-1, keepdims=True)
        return m, idx.astype(jnp.int32)

    logits = jnp.where(lane < n_exp, logits, -jnp.inf)
    m1, i1 = first_argmax(logits)
    m2, i2 = first_argmax(jnp.where(lane == i1, -jnp.inf, logits))
    ex = jnp.exp(m2 - m1)
    w1 = 1.0 / (1.0 + ex)
    w2 = ex / (1.0 + ex)

    mask = jnp.where((lane == i1) | (lane == i2), 1.0, 0.0)
    ranks = _dot(ltri_ref[...], mask.astype(BF16)) + carry_ref[0:1]
    incl = ranks + mask
    rank1 = jnp.sum(jnp.where(lane == i1, ranks, 0.0), axis=-1, keepdims=True)
    rank2 = jnp.sum(jnp.where(lane == i2, ranks, 0.0), axis=-1, keepdims=True)
    carry_ref[0:1] = incl[tm - 1:tm]
    n_sub = tm // rrow_ref.shape[2]
    sub = rrow_ref.shape[2]
    cnt_ref[0] = jnp.zeros(cnt_ref.shape[1:], F32)
    for j in range(n_sub):
        cnt_ref[0, j:j + 1] = incl[(j + 1) * sub - 1:(j + 1) * sub]

    fields = (i1.astype(F32), i2.astype(F32), rank1, rank2, w1, w2)
    rcol = jnp.zeros((tm, LANES), F32)
    for k, val in enumerate(fields):
        rcol = jnp.where(lane == k, val, rcol)
    rcol_ref[...] = rcol
    rrow = rcol.T
    for j in range(n_sub):
        rrow_ref[j] = rrow[0:ROUTE_FIELDS, j * sub:(j + 1) * sub]


def _attn_out(o2, x2, mod, wo, g, wr, n_exp, ltri, tm, sub, rows_per_block):
    n, d = x2.shape
    tpb = rows_per_block // tm
    n_sub = tm // sub
    return pl.pallas_call(
        functools.partial(_attn_out_kernel, n_exp=n_exp, tiles_per_block=tpb),
        grid=(n // tm,),
        in_specs=[
            pl.BlockSpec((tm, o2.shape[1]), lambda i: (i, 0)),
            pl.BlockSpec((tm, d), lambda i: (i, 0)),
            pl.BlockSpec((1, 6, d), lambda i: (i // tpb, 0, 0)),
            pl.BlockSpec(wo.shape, lambda i: (0, 0)),
            pl.BlockSpec((1, d), lambda i: (0, 0)),
            pl.BlockSpec(wr.shape, lambda i: (0, 0)),
            pl.BlockSpec((tm, tm), lambda i: (0, 0)),
        ],
        out_specs=[pl.BlockSpec((tm, d), lambda i: (i, 0)),
                   pl.BlockSpec((tm, d), lambda i: (i, 0)),
                   pl.BlockSpec((tm, LANES), lambda i: (i, 0)),
                   pl.BlockSpec((n_sub, ROUTE_FIELDS, sub), lambda i: (i, 0, 0)),
                   pl.BlockSpec((1, 8, LANES), lambda i: (i, 0, 0))],
        out_shape=[jax.ShapeDtypeStruct((n, d), F32),
                   jax.ShapeDtypeStruct((n, d), BF16),
                   jax.ShapeDtypeStruct((n, LANES), F32),
                   jax.ShapeDtypeStruct((n // sub, ROUTE_FIELDS, sub), F32),
                   jax.ShapeDtypeStruct((n // tm, 8, LANES), F32)],
        scratch_shapes=[pltpu.VMEM((8, LANES), F32)],
        compiler_params=_cparams(("arbitrary",)),
        name="attn_out_router",
    )(o2, x2, mod, wo, g, wr, ltri)


def _dispatch_kernel(off_ref, lo_ref, hi_ref, cexp_ref, h_ref, rrow_ref, xg_ref, gs_ref,
                     *, n_exp, sub, chunks_per_step, chunks_per_block):
    blk = pl.program_id(0)
    g = pl.program_id(1)
    d = h_ref.shape[1]
    n_sub = rrow_ref.shape[0]
    win = min(DISPATCH_WINDOW, n_sub)
    iota = lax.broadcasted_iota(jnp.int32, (DISPATCH_CHUNK, 1), 0)

    def contribution(info, sb):
        row, exp_id, exp_off = info
        r = rrow_ref[sb]
        first = r[0:1] == exp_id
        second = r[1:2] == exp_id
        pos = jnp.where(first, r[2:3] + exp_off, jnp.where(second, r[3:4] + exp_off, -1.0))
        hit = pos == row
        onehot = jnp.where(hit, 1.0, 0.0).astype(BF16)
        tok = h_ref[pl.ds(pl.multiple_of(sb * sub, sub), sub), :]
        gate = jnp.sum(jnp.where(hit, jnp.where(first, r[4:5], r[5:6]), 0.0),
                       axis=-1, keepdims=True)
        return _dot(onehot, tok), gate

    def chunk_info(j):
        c = g * chunks_per_step + j
        t = blk * chunks_per_block + c
        row = (c * DISPATCH_CHUNK + iota).astype(F32)
        exp_id = cexp_ref[t]
        exp_off = off_ref[blk * n_exp + exp_id].astype(F32)
        start = jnp.minimum(lo_ref[t], n_sub - win)
        return (row, exp_id.astype(F32), exp_off), start, hi_ref[t]

    first_chunk = blk * chunks_per_block + g * chunks_per_step
    used = functools.reduce(jnp.logical_or,
                            [hi_ref[first_chunk + j] >= 0 for j in range(chunks_per_step)])

    @pl.when(jnp.logical_not(used))
    def _():
        xg_ref[...] = jnp.zeros(xg_ref.shape, BF16)
        gs_ref[...] = jnp.zeros(gs_ref.shape, F32)

    @pl.when(used)
    def _():
        for j in range(chunks_per_step):
            info, start, _ = chunk_info(j)
            acc = jnp.zeros((DISPATCH_CHUNK, d), F32)
            gate = jnp.zeros((DISPATCH_CHUNK, 1), F32)
            for k in range(win):
                a, b = contribution(info, start + k)
                acc += a
                gate += b
            rows = slice(j * DISPATCH_CHUNK, (j + 1) * DISPATCH_CHUNK)
            xg_ref[rows, :] = acc.astype(BF16)
            gs_ref[rows, :] = gate

        def extra(j, _):
            info, start, last = chunk_info(j)
            rows = pl.ds(pl.multiple_of(j * DISPATCH_CHUNK, DISPATCH_CHUNK), DISPATCH_CHUNK)

            def add(sb, _):
                a, b = contribution(info, sb)
                xg_ref[rows, :] = (xg_ref[rows, :].astype(F32) + a).astype(BF16)
                gs_ref[rows, :] = gs_ref[rows, :] + b
                return 0

            return lax.fori_loop(start + win, last + 1, add, 0)

        lax.fori_loop(0, chunks_per_step, extra, 0)


def _dispatch(off, lo, hi, cexp, h, rrow, n_blocks, rows_per_block, rcap, n_exp, sub):
    n, d = h.shape
    n_sub = rows_per_block // sub
    step_rows = _pick(rcap, DISPATCH_STEP_ROWS)
    cps = step_rows // DISPATCH_CHUNK
    cpb = rcap // DISPATCH_CHUNK
    grid_spec = pltpu.PrefetchScalarGridSpec(
        num_scalar_prefetch=4,
        grid=(n_blocks, rcap // step_rows),
        in_specs=[
            pl.BlockSpec((rows_per_block, d), lambda b, g, *_: (b, 0)),
            pl.BlockSpec((n_sub, ROUTE_FIELDS, sub), lambda b, g, *_: (b, 0, 0)),
        ],
        out_specs=[
            pl.BlockSpec((step_rows, d), lambda b, g, *_: (b * (rcap // step_rows) + g, 0)),
            pl.BlockSpec((step_rows, 1), lambda b, g, *_: (b * (rcap // step_rows) + g, 0)),
        ],
    )
    return pl.pallas_call(
        functools.partial(_dispatch_kernel, n_exp=n_exp, sub=sub, chunks_per_step=cps,
                          chunks_per_block=cpb),
        grid_spec=grid_spec,
        out_shape=[jax.ShapeDtypeStruct((n_blocks * rcap, d), BF16),
                   jax.ShapeDtypeStruct((n_blocks * rcap, 1), F32)],
        compiler_params=_cparams(("arbitrary", "arbitrary")),
        name="moe_dispatch",
    )(off, lo, hi, cexp, h, rrow)


def _expert_ffn_kernel(perm_ref, pexp_ref, pfill_ref, x_ref, gs_ref, wg_ref, wu_ref, wd_ref,
                       y_ref, *, tf):
    w = pl.program_id(0)
    fill = pfill_ref[w]
    ff = wg_ref.shape[2]

    def ffn(n_rows):
        x = x_ref[0:n_rows, :]
        gs = gs_ref[0:n_rows, :]
        acc = jnp.zeros((n_rows, y_ref.shape[1]), F32)
        proj = lambda f0: (_dot(x, wg_ref[0, :, f0:f0 + tf]), _dot(x, wu_ref[0, :, f0:f0 + tf]))
        nxt = proj(0)
        for f0 in range(0, ff, tf):
            gate, up = nxt
            if f0 + tf < ff:
                nxt = proj(f0 + tf)
            a = ((_silu(gate) * up) * gs).astype(BF16)
            acc = acc + _dot(a, wd_ref[0, f0:f0 + tf, :])
        return acc.astype(BF16)

    for level in range(PIECE // PIECE_QUANTUM + 1):
        n_rows = level * PIECE_QUANTUM

        @pl.when(fill == level)
        def _(n_rows=n_rows):
            if n_rows:
                y_ref[0:n_rows, :] = ffn(n_rows)
            if n_rows < PIECE:
                y_ref[n_rows:PIECE, :] = jnp.zeros((PIECE - n_rows, y_ref.shape[1]), BF16)


def _expert_ffn(perm, pexp, pfill, xg, gs, wg, wu, wd, tf):
    rows, d = xg.shape
    n_exp, _, ff = wg.shape
    grid_spec = pltpu.PrefetchScalarGridSpec(
        num_scalar_prefetch=3,
        grid=(rows // PIECE,),
        in_specs=[
            pl.BlockSpec((PIECE, d), lambda w, perm, pexp, pfill: (perm[w], 0)),
            pl.BlockSpec((PIECE, 1), lambda w, perm, pexp, pfill: (perm[w], 0)),
            pl.BlockSpec((1, d, ff), lambda w, perm, pexp, pfill: (pexp[w], 0, 0)),
            pl.BlockSpec((1, d, ff), lambda w, perm, pexp, pfill: (pexp[w], 0, 0)),
            pl.BlockSpec((1, ff, d), lambda w, perm, pexp, pfill: (pexp[w], 0, 0)),
        ],
        out_specs=pl.BlockSpec((PIECE, d), lambda w, perm, pexp, pfill: (perm[w], 0)),
    )
    return pl.pallas_call(
        functools.partial(_expert_ffn_kernel, tf=tf),
        grid_spec=grid_spec,
        out_shape=jax.ShapeDtypeStruct((rows, d), BF16),
        compiler_params=_cparams(("arbitrary",)),
        name="expert_ffn",
    )(perm, pexp, pfill, xg, gs, wg, wu, wd)


def _combine_kernel(off_ref, start_ref, end_ref, yg_ref, rcol_ref, x_ref, mod_ref, gfin_ref, o_ref,
                    acc_ref, *, n_exp, sub, subs_per_block):
    blk = pl.program_id(0)
    t = pl.program_id(1)
    tm, d = x_ref.shape
    col = lax.broadcasted_iota(jnp.int32, (1, COMBINE_WINDOW), 1)
    align = COMBINE_WINDOW // 2

    def token_info(s):
        rc = rcol_ref[s * sub:(s + 1) * sub, :]
        return tuple(jnp.broadcast_to(rc[:, k:k + 1], (sub, COMBINE_WINDOW)) for k in range(4))

    def gathered(info, e, w0):
        i1, i2, rank1, rank2 = info
        rank = jnp.where(i1 == e, rank1, jnp.where(i2 == e, rank2, -float(COMBINE_WINDOW)))
        in_region = (w0 - off_ref[blk * n_exp + e] + col).astype(F32)
        onehot = jnp.where(rank == in_region, 1.0, 0.0).astype(BF16)
        return _dot(onehot, yg_ref[pl.ds(pl.multiple_of(w0, align), COMBINE_WINDOW), :])

    def first_window(s, e):
        idx = (blk * subs_per_block + t * (tm // sub) + s) * n_exp + e
        w0 = lax.shift_right_logical(start_ref[idx], align.bit_length() - 1) * align
        return w0, end_ref[idx]

    for s in range(tm // sub):
        info = token_info(s)
        acc = jnp.zeros((sub, d), F32)
        for e in range(n_exp):
            acc += gathered(info, e, first_window(s, e)[0])
        acc_ref[s * sub:(s + 1) * sub, :] = acc

    for s in range(tm // sub):
        info = token_info(s)
        for e in range(n_exp):
            w0, end = first_window(s, e)
            n_more = lax.shift_right_logical(
                jnp.maximum(end - w0 - 1, 0), COMBINE_WINDOW.bit_length() - 1)

            def add(k, _, info=info, e=e, w0=w0, s=s):
                rows = slice(s * sub, (s + 1) * sub)
                acc_ref[rows, :] = acc_ref[rows, :] + gathered(info, e, w0 + k * COMBINE_WINDOW)
                return 0

            lax.fori_loop(1, n_more + 1, add, 0)

    x = x_ref[...] + mod_ref[0][5:6] * acc_ref[...]
    o_ref[...] = _rms_scale(x) * gfin_ref[...]


def _combine(off, row_start, row_end, yg, rcol, x2, mod, gfin, n_blocks, rows_per_block, rcap,
             n_exp, sub, tm):
    n, d = x2.shape
    tpb = rows_per_block // tm
    grid_spec = pltpu.PrefetchScalarGridSpec(
        num_scalar_prefetch=3,
        grid=(n_blocks, tpb),
        in_specs=[
            pl.BlockSpec((rcap, d), lambda b, t, *_: (b, 0), pipeline_mode=pl.Buffered(1)),
            pl.BlockSpec((tm, LANES), lambda b, t, *_: (b * tpb + t, 0)),
            pl.BlockSpec((tm, d), lambda b, t, *_: (b * tpb + t, 0)),
            pl.BlockSpec((1, 6, d), lambda b, t, *_: (b, 0, 0)),
            pl.BlockSpec((1, d), lambda b, t, *_: (0, 0)),
        ],
        out_specs=pl.BlockSpec((tm, d), lambda b, t, *_: (b * tpb + t, 0)),
        scratch_shapes=[pltpu.VMEM((tm, d), F32)],
    )
    return pl.pallas_call(
        functools.partial(_combine_kernel, n_exp=n_exp, sub=sub,
                          subs_per_block=rows_per_block // sub),
        grid_spec=grid_spec,
        out_shape=jax.ShapeDtypeStruct((n, d), F32),
        compiler_params=_cparams(("arbitrary", "arbitrary")),
        name="moe_combine",
    )(off, row_start, row_end, yg, rcol, x2, mod, gfin)


def _routing_tables(cnt, n_blocks, n_sub, n_exp, rcap):
    cum_incl = cnt.astype(jnp.int32)
    cum_excl = jnp.concatenate([jnp.zeros_like(cum_incl[:, :1]), cum_incl[:, :-1]], axis=1)
    total = cum_incl[:, -1]
    psz = (total + PIECE - 1) // PIECE * PIECE
    off = jnp.cumsum(psz, axis=1) - psz
    start = off[:, None, :] + cum_excl
    end = off[:, None, :] + cum_incl

    c0 = jnp.arange(rcap // DISPATCH_CHUNK, dtype=jnp.int32)[None, :, None, None] * DISPATCH_CHUNK
    hit = ((start[:, None] < c0 + DISPATCH_CHUNK) & (end[:, None] > c0)
           & (end[:, None] > start[:, None])).any(-1)
    sub_ids = jnp.arange(n_sub, dtype=jnp.int32)
    d_lo = jnp.min(jnp.where(hit, sub_ids, n_sub), axis=-1)
    d_hi = jnp.max(jnp.where(hit, sub_ids, -1), axis=-1)
    d_lo = jnp.where(d_hi < 0, 0, d_lo)

    ppb = rcap // PIECE
    prow = jnp.arange(ppb, dtype=jnp.int32)[None, :, None] * PIECE
    inside = (prow >= off[:, None, :]) & (prow < (off + psz)[:, None, :])
    p_exp = jnp.argmax(inside, axis=-1).astype(jnp.int32)
    used = inside.any(-1)
    left = jnp.take_along_axis(total + off, p_exp, axis=1) - prow[..., 0]
    p_fill = jnp.where(used, jnp.clip((left + PIECE_QUANTUM - 1) // PIECE_QUANTUM, 0,
                                      PIECE // PIECE_QUANTUM), 0).astype(jnp.int32)
    key = jnp.where(p_fill > 0, p_exp, n_exp).reshape(-1)
    perm = jnp.argsort(key, stable=True).astype(jnp.int32)
    p_exp_sorted = jnp.minimum(key[perm], n_exp - 1).astype(jnp.int32)
    d_exp = jnp.repeat(p_exp, PIECE // DISPATCH_CHUNK, axis=1).reshape(-1)
    return (off.reshape(-1).astype(jnp.int32), d_lo.reshape(-1), d_hi.reshape(-1), d_exp,
            start.reshape(-1).astype(jnp.int32), end.reshape(-1).astype(jnp.int32),
            perm, p_exp_sorted, p_fill.reshape(-1)[perm])


def _pick(n, pref):
    t = min(n, pref)
    while n % t:
        t //= 2
    return t


def kernel(x, c, positions, ada_w, ada_b, mix_norm_g, ffn_norm_g, pool_w_in, pool_w_grp,
           pool_scale, pool_w_out, kv_ada_w, kv_ada_b, kv_norm_g, w_dkv, ckv_norm_g, w_kr,
           w_uk, w_uv, w_dq, q_norm_g, w_uq, w_o, ffn_w_gate, ffn_w_up, ffn_w_down,
           router_w, moe_w_gate, moe_w_up, moe_w_down, final_norm_g):
    b, s, d = x.shape
    n = b * s
    depth = ada_w.shape[0]
    assert depth == 2 and pool_w_in.shape[0] == 1 and w_dq.shape[0] == 1
    assert b <= 8 and d % LANES == 0

    ts = _pick(s, 512)
    tp = _pick(s, 1024)
    ta = _pick(s, 1024)
    tc = _pick(s, 512)
    tm = _pick(s, 1024)
    tf = _pick(ffn_w_gate.shape[-1], 512)

    c_pad = jnp.zeros((8, d), F32).at[:b].set(c)
    mods = _adaln(c_pad, ada_w, ada_b[:, None, :], _pick(6 * d, 1536))
    mods = mods[:, :b].reshape(depth, b, 6, d)
    kvmod = _adaln(c_pad, kv_ada_w[None], kv_ada_b[None, None, :], _pick(2 * d, 1024))
    kvmod = kvmod[0, :b].reshape(b, 2, d)

    r = jnp.arange(min(tp, POOL_ROW_BLOCK))[:, None]
    cidx = jnp.arange(min(tp, POOL_ROW_BLOCK) + POOL_HALO)[None, :] - POOL_HALO
    band = jnp.stack([((cidx <= r) & (cidx > r - w)) for w in POOL_WINDOWS]).astype(BF16)
    inv_freq = ROPE_THETA ** (-jnp.arange(0, D_ROPE, 2, dtype=F32) / D_ROPE)
    x1, fwg, fwu, fwd, rope_cos, rope_sin, h1 = _pool_mixer(
        x, mods[0], mix_norm_g[0:1], pool_w_in[0].astype(BF16), band, pool_w_grp[0].astype(BF16),
        pool_scale[0:1], pool_w_out[0].astype(BF16), ffn_w_gate, ffn_w_up, ffn_w_down,
        positions.astype(F32)[:, None, :], inv_freq[:, None], ffn_norm_g[0:1], tp)

    x2 = _dense_ffn(x1.reshape(n, d), h1.reshape(n, d), mods[0], fwg, fwu, fwd, tm, tf, s)


    q_lora = w_dq.shape[-1]
    wuq = w_uq[0].reshape(q_lora, N_HEADS, D_NOPE + D_ROPE)
    wqn_t = wuq[:, :, :D_NOPE].reshape(q_lora, N_HEADS * D_NOPE).T.astype(BF16)
    wqr_t = wuq[:, :, D_NOPE:].reshape(q_lora, N_HEADS * D_ROPE).T.astype(BF16)
    qt, k, vt = _qkv_proj(
        x2.reshape(b, s, d), mods[1], kvmod, mix_norm_g[1:2], kv_norm_g[None, :],
        w_dq[0].astype(BF16), q_norm_g[0:1], wqn_t, wqr_t, w_dkv.astype(BF16),
        ckv_norm_g[None, :], jnp.pad(w_kr, ((0, 0), (0, LANES - D_ROPE))).astype(BF16),
        w_uk.astype(BF16), w_uv.T.astype(BF16), rope_cos, rope_sin, ts)
    o, ewg, ewu, ewd = _attention(qt, k, vt, moe_w_gate[0], moe_w_up[0], moe_w_down[0], ts)

    n_exp = router_w.shape[-1]
    sub = _pick(s, ROUTE_SUB)
    n_sub = s // sub
    rcap = 2 * s + n_exp * PIECE
    ltri = (jnp.arange(ta)[None, :] < jnp.arange(ta)[:, None]).astype(BF16)
    assert n_exp <= LANES
    wr = jnp.pad(router_w[0], ((0, 0), (0, LANES - n_exp)))
    wr_hi = wr.astype(BF16)
    wr_split = jnp.concatenate([wr_hi, (wr - wr_hi.astype(F32)).astype(BF16)], axis=1)
    x3, h3, rcol, rrow, cnt = _attn_out(o.reshape(n, N_HEADS * D_V), x2, mods[1],
                                        w_o[0].astype(BF16), ffn_norm_g[1:2], wr_split, n_exp,
                                        ltri, ta, sub, s)
    cnt = cnt[:, :ta // sub, :n_exp].reshape(b, n_sub, n_exp)
    (off, d_lo, d_hi, d_exp, row_start, row_end, perm, p_exp,
     p_fill) = _routing_tables(cnt, b, n_sub, n_exp, rcap)
    xg, gs = _dispatch(off, d_lo, d_hi, d_exp, h3, rrow, b, s, rcap, n_exp, sub)
    yg = _expert_ffn(perm, p_exp, p_fill, xg, gs, ewg, ewu, ewd, tf)
    out = _combine(off, row_start, row_end, yg, rcol, x3, mods[1], final_norm_g[None, :], b, s,
                   rcap, n_exp, sub, tc)
    return out.reshape(b, s, d)
```
